```python
import math
import jax, jax.numpy as jnp
from jax import lax
import numpy as np

D_MODEL = 2048
BATCH = 8
SEQ = 4096
DEPTH = 2

N_META = 16
EPS = 1e-6
D_CONV = D_MODEL // 2
CONV_WIDTH = 3
N_HEADS = D_MODEL // 128
QK_NOPE = 128
QK_ROPE = 64
QK_HEAD = QK_NOPE + QK_ROPE
V_HEAD = 128
Q_LORA = 512
KV_LORA = 512
ROPE_THETA = 10000.0
Q_BLOCK = 128
D_POOL = D_MODEL // 2
POOL_WINDOWS = (2, 4, 8, 16)
POOL_GROUP = D_POOL // len(POOL_WINDOWS)
N_BRANCH = 3
D_FF = 4 * D_MODEL
D_IN = 3 * D_CONV + Q_LORA + KV_LORA + QK_ROPE + D_POOL + N_BRANCH * D_MODEL
IN_SPLITS = (3 * D_CONV,
             3 * D_CONV + Q_LORA,
             3 * D_CONV + Q_LORA + KV_LORA,
             3 * D_CONV + Q_LORA + KV_LORA + QK_ROPE,
             3 * D_CONV + Q_LORA + KV_LORA + QK_ROPE + D_POOL)

kernel_name = 'hybrid_gated_conv_mla_pool_block'


def rms_norm(x, g):
    xf = x.astype(jnp.float32)
    y = xf * lax.rsqrt(jnp.mean(xf * xf, axis=-1, keepdims=True) + EPS)
    return (y * g.astype(jnp.float32)).astype(x.dtype)


def rope_tables(T, dtype):
    pos = jnp.arange(T, dtype=jnp.float32)
    inv = ROPE_THETA ** (-jnp.arange(0, QK_ROPE, 2, dtype=jnp.float32) / QK_ROPE)
    ang = pos[:, None] * inv[None, :]
    return jnp.cos(ang).astype(dtype), jnp.sin(ang).astype(dtype)


def apply_rope_tail(x, cos, sin):
    x_nope, x_rope = x[..., :QK_NOPE], x[..., QK_NOPE:]
    x1, x2 = jnp.split(x_rope, 2, axis=-1)
    c, s = cos[None, :, None, :], sin[None, :, None, :]
    return jnp.concatenate([x_nope, x1 * c - x2 * s, x2 * c + x1 * s], axis=-1)


def causal_short_conv(u, w):
    T = u.shape[1]
    up = jnp.pad(u, ((0, 0), (CONV_WIDTH - 1, 0), (0, 0)))
    return sum(w[j] * up[:, j:j + T] for j in range(CONV_WIDTH))


def causal_block_attention(q, k, v):
    B, T, H, dk = q.shape
    Tp = -(-T // Q_BLOCK) * Q_BLOCK
    pad = ((0, 0), (0, Tp - T), (0, 0), (0, 0))
    q, k, v = jnp.pad(q, pad), jnp.pad(k, pad), jnp.pad(v, pad)
    nb = Tp // Q_BLOCK
    q_blocks = q.reshape(B, nb, Q_BLOCK, H, dk).swapaxes(0, 1)
    k_pos = jnp.arange(Tp)
    scale = dk ** -0.5

    def one_block(args):
        qb, blk = args
        s = jnp.einsum('bqhd,bkhd->bhqk', qb, k).astype(jnp.float32) * scale
        q_pos = blk * Q_BLOCK + jnp.arange(Q_BLOCK)
        s = jnp.where(q_pos[:, None] >= k_pos[None, :], s, -jnp.inf)
        p = jax.nn.softmax(s, axis=-1).astype(v.dtype)
        return jnp.einsum('bhqk,bkhd->bqhd', p, v)

    out = lax.map(one_block, (q_blocks, jnp.arange(nb)))
    return out.swapaxes(0, 1).reshape(B, Tp, H, -1)[:, :T]


def multiscale_pool(u, pool_w, pool_scale):
    B, T, _ = u.shape
    uf = u.astype(jnp.float32)
    groups = jnp.split(uf, len(POOL_WINDOWS), axis=-1)
    seen = jnp.arange(1, T + 1, dtype=jnp.float32)
    outs = []
    for g, w in zip(groups, POOL_WINDOWS):
        cs = jnp.cumsum(g, axis=1)
        lagged = jnp.pad(cs, ((0, 0), (w, 0), (0, 0)))[:, :T]
        count = jnp.minimum(seen, float(w))[None, :, None]
        outs.append((cs - lagged) / count - g)
    pooled = jnp.stack(outs, axis=2).astype(u.dtype)
    mixed = jnp.einsum('btgc,gcd->btgd', pooled, pool_w).reshape(B, T, D_POOL)
    return mixed * pool_scale


def hybrid_layer(x, cos, sin, attn_norm, w_in, conv_w, q_lat_norm, kv_lat_norm, w_uq, w_ukv,
                 q_norm, k_norm, pool_w, pool_scale, w_branch_a, w_branch_b, w_branch_c, w_o,
                 mlp_norm, w_up, w_down):
    B, T, _ = x.shape
    h = rms_norm(x, attn_norm)
    proj = h @ w_in
    a_in, q_lat, kv_lat, k_rope, pool_in, gate_logits = jnp.split(proj, IN_SPLITS, axis=-1)

    u_a, b_a, c_a = jnp.split(a_in, 3, axis=-1)
    y_a = b_a * causal_short_conv(c_a * u_a, conv_w)

    q = (rms_norm(q_lat, q_lat_norm) @ w_uq).reshape(B, T, N_HEADS, QK_HEAD)
    kv = (rms_norm(kv_lat, kv_lat_norm) @ w_ukv).reshape(B, T, N_HEADS, QK_NOPE + V_HEAD)
    k_nope, v = kv[..., :QK_NOPE], kv[..., QK_NOPE:]
    k = jnp.concatenate(
        [k_nope, jnp.broadcast_to(k_rope[:, :, None, :], (B, T, N_HEADS, QK_ROPE))], axis=-1)
    q = apply_rope_tail(rms_norm(q, q_norm), cos, sin)
    k = apply_rope_tail(rms_norm(k, k_norm), cos, sin)
    y_b = causal_block_attention(q, k, v).reshape(B, T, N_HEADS * V_HEAD)

    y_c = multiscale_pool(pool_in, pool_w, pool_scale)

    gates = jax.nn.sigmoid(gate_logits).reshape(B, T, N_BRANCH, D_MODEL)
    merged = (gates[:, :, 0] * (y_a @ w_branch_a)
              + gates[:, :, 1] * (y_b @ w_branch_b)
              + gates[:, :, 2] * (y_c @ w_branch_c))
    x = x + merged @ w_o

    h2 = rms_norm(x, mlp_norm)
    return x + jnp.square(jax.nn.relu(h2 @ w_up)) @ w_down


def _fwd_setup_inputs(seed: int = 0) -> dict:
    key = jax.random.key(seed)
    ks = jax.random.split(key, 24)
    f32 = jnp.float32

    def w(k, shape, fan_in):
        return jax.random.normal(k, shape, f32) * fan_in ** -0.5

    def gain(k, shape):
        return 1.0 + 0.05 * jax.random.normal(k, shape, f32)

    L = DEPTH
    return {
        'x': jax.random.normal(ks[0], (BATCH, SEQ, D_MODEL), f32),
        'meta_tokens': jax.random.normal(ks[1], (N_META, D_MODEL), f32),
        'attn_norm': gain(ks[2], (L, D_MODEL)),
        'w_in': w(ks[3], (L, D_MODEL, D_IN), D_MODEL),
        'conv_w': w(ks[4], (L, CONV_WIDTH, D_CONV), CONV_WIDTH),
        'q_lat_norm': gain(ks[5], (L, Q_LORA)),
        'kv_lat_norm': gain(ks[6], (L, KV_LORA)),
        'w_uq': w(ks[7], (L, Q_LORA, N_HEADS * QK_HEAD), Q_LORA),
        'w_ukv': w(ks[8], (L, KV_LORA, N_HEADS * (QK_NOPE + V_HEAD)), KV_LORA),
        'q_norm': gain(ks[9], (L, QK_HEAD)),
        'k_norm': gain(ks[10], (L, QK_HEAD)),
        'pool_w': w(ks[11], (L, len(POOL_WINDOWS), POOL_GROUP, POOL_GROUP), POOL_GROUP),
        'pool_scale': gain(ks[12], (L, D_POOL)),
        'w_branch_a': w(ks[13], (L, D_CONV, D_MODEL), D_CONV),
        'w_branch_b': w(ks[14], (L, N_HEADS * V_HEAD, D_MODEL), N_HEADS * V_HEAD),
        'w_branch_c': w(ks[15], (L, D_POOL, D_MODEL), D_POOL),
        'w_o': w(ks[16], (L, D_MODEL, D_MODEL), D_MODEL),
        'mlp_norm': gain(ks[17], (L, D_MODEL)),
        'w_up': w(ks[18], (L, D_MODEL, D_FF), D_MODEL),
        'w_down': w(ks[19], (L, D_FF, D_MODEL), D_FF),
    }


def _fwd_reference(x, meta_tokens, attn_norm, w_in, conv_w, q_lat_norm, kv_lat_norm, w_uq, w_ukv,
              q_norm, k_norm, pool_w, pool_scale, w_branch_a, w_branch_b, w_branch_c, w_o,
              mlp_norm, w_up, w_down):
    B = x.shape[0]
    meta = jnp.broadcast_to(meta_tokens[None].astype(x.dtype), (B, N_META, D_MODEL))
    h = jnp.concatenate([meta, x], axis=1)
    cos, sin = rope_tables(h.shape[1], h.dtype)
    for l in range(DEPTH):
        h = hybrid_layer(h, cos, sin, attn_norm[l], w_in[l], conv_w[l], q_lat_norm[l],
                         kv_lat_norm[l], w_uq[l], w_ukv[l], q_norm[l], k_norm[l], pool_w[l],
                         pool_scale[l], w_branch_a[l], w_branch_b[l], w_branch_c[l], w_o[l],
                         mlp_norm[l], w_up[l], w_down[l])
    return h[:, N_META:]


import jax as _jax
import jax.numpy as _jnp

TWIN_FORMAT = 'train_step'
FWD_PARAMS = ['x', 'meta_tokens', 'attn_norm', 'w_in', 'conv_w', 'q_lat_norm', 'kv_lat_norm', 'w_uq', 'w_ukv', 'q_norm', 'k_norm', 'pool_w', 'pool_scale', 'w_branch_a', 'w_branch_b', 'w_branch_c', 'w_o', 'mlp_norm', 'w_up', 'w_down']
TWIN_WEIGHTS = ['meta_tokens', 'attn_norm', 'w_in', 'conv_w', 'q_lat_norm', 'kv_lat_norm', 'w_uq', 'w_ukv', 'q_norm', 'k_norm', 'pool_w', 'pool_scale', 'w_branch_a', 'w_branch_b', 'w_branch_c', 'w_o', 'mlp_norm', 'w_up', 'w_down']
TWIN_DIFF_INPUT = 'x'
TWIN_INPUTS = ['x', 'meta_tokens', 'attn_norm', 'w_in', 'conv_w', 'q_lat_norm', 'kv_lat_norm', 'w_uq', 'w_ukv', 'q_norm', 'k_norm', 'pool_w', 'pool_scale', 'w_branch_a', 'w_branch_b', 'w_branch_c', 'w_o', 'mlp_norm', 'w_up', 'w_down', 'loss_target', 'm_meta_tokens', 'm_attn_norm', 'm_w_in', 'm_conv_w', 'm_q_lat_norm', 'm_kv_lat_norm', 'm_w_uq', 'm_w_ukv', 'm_q_norm', 'm_k_norm', 'm_pool_w', 'm_pool_scale', 'm_w_branch_a', 'm_w_branch_b', 'm_w_branch_c', 'm_w_o', 'm_mlp_norm', 'm_w_up', 'm_w_down', 'v_meta_tokens', 'v_attn_norm', 'v_w_in', 'v_conv_w', 'v_q_lat_norm', 'v_kv_lat_norm', 'v_w_uq', 'v_w_ukv', 'v_q_norm', 'v_k_norm', 'v_pool_w', 'v_pool_scale', 'v_w_branch_a', 'v_w_branch_b', 'v_w_branch_c', 'v_w_o', 'v_mlp_norm', 'v_w_up', 'v_w_down']
TWIN_OUTPUTS = ['loss', 'grad_x', 'grad_meta_tokens', 'grad_attn_norm', 'grad_w_in', 'grad_conv_w', 'grad_q_lat_norm', 'grad_kv_lat_norm', 'grad_w_uq', 'grad_w_ukv', 'grad_q_norm', 'grad_k_norm', 'grad_pool_w', 'grad_pool_scale', 'grad_w_branch_a', 'grad_w_branch_b', 'grad_w_branch_c', 'grad_w_o', 'grad_mlp_norm', 'grad_w_up', 'grad_w_down', 'delta_meta_tokens', 'delta_attn_norm', 'delta_w_in', 'delta_conv_w', 'delta_q_lat_norm', 'delta_kv_lat_norm', 'delta_w_uq', 'delta_w_ukv', 'delta_q_norm', 'delta_k_norm', 'delta_pool_w', 'delta_pool_scale', 'delta_w_branch_a', 'delta_w_branch_b', 'delta_w_branch_c', 'delta_w_o', 'delta_mlp_norm', 'delta_w_up', 'delta_w_down', 'new_m_meta_tokens', 'new_m_attn_norm', 'new_m_w_in', 'new_m_conv_w', 'new_m_q_lat_norm', 'new_m_kv_lat_norm', 'new_m_w_uq', 'new_m_w_ukv', 'new_m_q_norm', 'new_m_k_norm', 'new_m_pool_w', 'new_m_pool_scale', 'new_m_w_branch_a', 'new_m_w_branch_b', 'new_m_w_branch_c', 'new_m_w_o', 'new_m_mlp_norm', 'new_m_w_up', 'new_m_w_down', 'new_v_meta_tokens', 'new_v_attn_norm', 'new_v_w_in', 'new_v_conv_w', 'new_v_q_lat_norm', 'new_v_kv_lat_norm', 'new_v_w_uq', 'new_v_w_ukv', 'new_v_q_norm', 'new_v_k_norm', 'new_v_pool_w', 'new_v_pool_scale', 'new_v_w_branch_a', 'new_v_w_branch_b', 'new_v_w_branch_c', 'new_v_w_o', 'new_v_mlp_norm', 'new_v_w_up', 'new_v_w_down']
TWIN_LEAF_KINDS = {'loss': 'loss', 'grad_x': 'grad_x', 'grad_meta_tokens': 'grad_w', 'grad_attn_norm': 'grad_w', 'grad_w_in': 'grad_w', 'grad_conv_w': 'grad_w', 'grad_q_lat_norm': 'grad_w', 'grad_kv_lat_norm': 'grad_w', 'grad_w_uq': 'grad_w', 'grad_w_ukv': 'grad_w', 'grad_q_norm': 'grad_w', 'grad_k_norm': 'grad_w', 'grad_pool_w': 'grad_w', 'grad_pool_scale': 'grad_w', 'grad_w_branch_a': 'grad_w', 'grad_w_branch_b': 'grad_w', 'grad_w_branch_c': 'grad_w', 'grad_w_o': 'grad_w', 'grad_mlp_norm': 'grad_w', 'grad_w_up': 'grad_w', 'grad_w_down': 'grad_w', 'delta_meta_tokens': 'delta_w', 'delta_attn_norm': 'delta_w', 'delta_w_in': 'delta_w', 'delta_conv_w': 'delta_w', 'delta_q_lat_norm': 'delta_w', 'delta_kv_lat_norm': 'delta_w', 'delta_w_uq': 'delta_w', 'delta_w_ukv': 'delta_w', 'delta_q_norm': 'delta_w', 'delta_k_norm': 'delta_w', 'delta_pool_w': 'delta_w', 'delta_pool_scale': 'delta_w', 'delta_w_branch_a': 'delta_w', 'delta_w_branch_b': 'delta_w', 'delta_w_branch_c': 'delta_w', 'delta_w_o': 'delta_w', 'delta_mlp_norm': 'delta_w', 'delta_w_up': 'delta_w', 'delta_w_down': 'delta_w', 'new_m_meta_tokens': 'new_m', 'new_m_attn_norm': 'new_m', 'new_m_w_in': 'new_m', 'new_m_conv_w': 'new_m', 'new_m_q_lat_norm': 'new_m', 'new_m_kv_lat_norm': 'new_m', 'new_m_w_uq': 'new_m', 'new_m_w_ukv': 'new_m', 'new_m_q_norm': 'new_m', 'new_m_k_norm': 'new_m', 'new_m_pool_w': 'new_m', 'new_m_pool_scale': 'new_m', 'new_m_w_branch_a': 'new_m', 'new_m_w_branch_b': 'new_m', 'new_m_w_branch_c': 'new_m', 'new_m_w_o': 'new_m', 'new_m_mlp_norm': 'new_m', 'new_m_w_up': 'new_m', 'new_m_w_down': 'new_m', 'new_v_meta_tokens': 'new_v', 'new_v_attn_norm': 'new_v', 'new_v_w_in': 'new_v', 'new_v_conv_w': 'new_v', 'new_v_q_lat_norm': 'new_v', 'new_v_kv_lat_norm': 'new_v', 'new_v_w_uq': 'new_v', 'new_v_w_ukv': 'new_v', 'new_v_q_norm': 'new_v', 'new_v_k_norm': 'new_v', 'new_v_pool_w': 'new_v', 'new_v_pool_scale': 'new_v', 'new_v_w_branch_a': 'new_v', 'new_v_w_branch_b': 'new_v', 'new_v_w_branch_c': 'new_v', 'new_v_w_o': 'new_v', 'new_v_mlp_norm': 'new_v', 'new_v_w_up': 'new_v', 'new_v_w_down': 'new_v'}


def _forward(args):
    return _fwd_reference(*[args[k] for k in FWD_PARAMS])


def _output_shape():
    def fwd():
        inp = _fwd_setup_inputs(0)
        return _fwd_reference(*[inp[k] for k in FWD_PARAMS])
    out = _jax.eval_shape(fwd)
    return out.shape, out.dtype

N_MICROBATCH = 1
ADAM_LR = 0.001
ADAM_B1 = 0.9
ADAM_B2 = 0.999
ADAM_EPS = 1e-08
ADAM_WD = 0.01
ADAM_STEP = 10
PER_EXAMPLE_BATCH_AXIS = {'x': 0, 'loss_target': 0}
SHARED_INPUTS = []
_WEIGHT_DTYPES = {'meta_tokens': _jnp.float32, 'attn_norm': _jnp.float32, 'w_in': _jnp.float32, 'conv_w': _jnp.float32, 'q_lat_norm': _jnp.float32, 'kv_lat_norm': _jnp.float32, 'w_uq': _jnp.float32, 'w_ukv': _jnp.float32, 'q_norm': _jnp.float32, 'k_norm': _jnp.float32, 'pool_w': _jnp.float32, 'pool_scale': _jnp.float32, 'w_branch_a': _jnp.float32, 'w_branch_b': _jnp.float32, 'w_branch_c': _jnp.float32, 'w_o': _jnp.float32, 'mlp_norm': _jnp.float32, 'w_up': _jnp.float32, 'w_down': _jnp.float32}
MOMENT_SCALE = {'meta_tokens': 4.889289e-02, 'attn_norm': 1.877739e+01, 'w_in': 7.294104e-01, 'conv_w': 5.474219e+00, 'q_lat_norm': 9.980481e-02, 'kv_lat_norm': 3.304506e+00, 'w_uq': 4.037638e-02, 'w_ukv': 1.038755e+00, 'q_norm': 3.035157e-01, 'k_norm': 3.032183e-01, 'pool_w': 6.950621e-01, 'pool_scale': 7.186242e+00, 'w_branch_a': 3.856024e-01, 'w_branch_b': 1.463492e+00, 'w_branch_c': 4.385934e-01, 'w_o': 1.432097e+00, 'mlp_norm': 4.778747e+01, 'w_up': 1.621665e+00, 'w_down': 7.148456e+00}


def _to_microbatches(a, axis):
    t = _jnp.moveaxis(a, axis, 0)
    t = t.reshape((N_MICROBATCH, t.shape[0] // N_MICROBATCH) + t.shape[1:])
    return _jnp.moveaxis(t, 1, axis + 1)


def setup_inputs(seed: int = 0) -> dict:
    inp = _fwd_setup_inputs(seed)
    key = _jax.random.fold_in(_jax.random.key(seed), 7919)
    shape, _ = _output_shape()
    out = dict(inp)
    out["loss_target"] = _jax.random.normal(_jax.random.fold_in(key, 0), shape, _jnp.float32)
    for i, name in enumerate(TWIN_WEIGHTS):
        w = inp[name].astype(_jnp.float32)
        if MOMENT_SCALE is None:
            s = _jnp.sqrt(_jnp.mean(_jnp.square(w)) + 1e-30)
        else:
            s = MOMENT_SCALE[name]
        km, kv = _jax.random.split(_jax.random.fold_in(key, i + 1))
        out[name] = w
        out["m_" + name] = s * _jax.random.normal(km, w.shape, _jnp.float32)
        out["v_" + name] = (s * s) * _jax.random.uniform(kv, w.shape, _jnp.float32, 0.5, 1.5)
    if N_MICROBATCH > 1:
        for name, axis in PER_EXAMPLE_BATCH_AXIS.items():
            out[name] = _to_microbatches(out[name], axis)
    return {'x': out['x'], 'meta_tokens': out['meta_tokens'], 'attn_norm': out['attn_norm'], 'w_in': out['w_in'], 'conv_w': out['conv_w'], 'q_lat_norm': out['q_lat_norm'], 'kv_lat_norm': out['kv_lat_norm'], 'w_uq': out['w_uq'], 'w_ukv': out['w_ukv'], 'q_norm': out['q_norm'], 'k_norm': out['k_norm'], 'pool_w': out['pool_w'], 'pool_scale': out['pool_scale'], 'w_branch_a': out['w_branch_a'], 'w_branch_b': out['w_branch_b'], 'w_branch_c': out['w_branch_c'], 'w_o': out['w_o'], 'mlp_norm': out['mlp_norm'], 'w_up': out['w_up'], 'w_down': out['w_down'], 'loss_target': out['loss_target'], 'm_meta_tokens': out['m_meta_tokens'], 'm_attn_norm': out['m_attn_norm'], 'm_w_in': out['m_w_in'], 'm_conv_w': out['m_conv_w'], 'm_q_lat_norm': out['m_q_lat_norm'], 'm_kv_lat_norm': out['m_kv_lat_norm'], 'm_w_uq': out['m_w_uq'], 'm_w_ukv': out['m_w_ukv'], 'm_q_norm': out['m_q_norm'], 'm_k_norm': out['m_k_norm'], 'm_pool_w': out['m_pool_w'], 'm_pool_scale': out['m_pool_scale'], 'm_w_branch_a': out['m_w_branch_a'], 'm_w_branch_b': out['m_w_branch_b'], 'm_w_branch_c': out['m_w_branch_c'], 'm_w_o': out['m_w_o'], 'm_mlp_norm': out['m_mlp_norm'], 'm_w_up': out['m_w_up'], 'm_w_down': out['m_w_down'], 'v_meta_tokens': out['v_meta_tokens'], 'v_attn_norm': out['v_attn_norm'], 'v_w_in': out['v_w_in'], 'v_conv_w': out['v_conv_w'], 'v_q_lat_norm': out['v_q_lat_norm'], 'v_kv_lat_norm': out['v_kv_lat_norm'], 'v_w_uq': out['v_w_uq'], 'v_w_ukv': out['v_w_ukv'], 'v_q_norm': out['v_q_norm'], 'v_k_norm': out['v_k_norm'], 'v_pool_w': out['v_pool_w'], 'v_pool_scale': out['v_pool_scale'], 'v_w_branch_a': out['v_w_branch_a'], 'v_w_branch_b': out['v_w_branch_b'], 'v_w_branch_c': out['v_w_branch_c'], 'v_w_o': out['v_w_o'], 'v_mlp_norm': out['v_mlp_norm'], 'v_w_up': out['v_w_up'], 'v_w_down': out['v_w_down']}


def _loss(weights, diff, rest, loss_target):
    with _jax.named_scope("forward"):
        args = {**rest, TWIN_DIFF_INPUT: diff, **{k: w.astype(_WEIGHT_DTYPES[k]) for k, w in weights.items()}}
        y = _forward(args)
    with _jax.named_scope("loss_head"):
        err = _jnp.square(y.astype(_jnp.float32) - loss_target)
        return 0.5 * _jnp.sum(_jnp.mean(err, axis=-1)) if err.ndim else 0.5 * err


def _adamw(w, g, m, v):
    m = ADAM_B1 * m + (1.0 - ADAM_B1) * g
    v = ADAM_B2 * v + (1.0 - ADAM_B2) * _jnp.square(g)
    m_hat = m / (1.0 - ADAM_B1 ** ADAM_STEP)
    v_hat = v / (1.0 - ADAM_B2 ** ADAM_STEP)
    delta = -ADAM_LR * (m_hat / (_jnp.sqrt(v_hat) + ADAM_EPS) + ADAM_WD * w)
    return delta, m, v


def reference(x, meta_tokens, attn_norm, w_in, conv_w, q_lat_norm, kv_lat_norm, w_uq, w_ukv, q_norm, k_norm, pool_w, pool_scale, w_branch_a, w_branch_b, w_branch_c, w_o, mlp_norm, w_up, w_down, loss_target, m_meta_tokens, m_attn_norm, m_w_in, m_conv_w, m_q_lat_norm, m_kv_lat_norm, m_w_uq, m_w_ukv, m_q_norm, m_k_norm, m_pool_w, m_pool_scale, m_w_branch_a, m_w_branch_b, m_w_branch_c, m_w_o, m_mlp_norm, m_w_up, m_w_down, v_meta_tokens, v_attn_norm, v_w_in, v_conv_w, v_q_lat_norm, v_kv_lat_norm, v_w_uq, v_w_ukv, v_q_norm, v_k_norm, v_pool_w, v_pool_scale, v_w_branch_a, v_w_branch_b, v_w_branch_c, v_w_o, v_mlp_norm, v_w_up, v_w_down):
    given = dict(x=x, meta_tokens=meta_tokens, attn_norm=attn_norm, w_in=w_in, conv_w=conv_w, q_lat_norm=q_lat_norm, kv_lat_norm=kv_lat_norm, w_uq=w_uq, w_ukv=w_ukv, q_norm=q_norm, k_norm=k_norm, pool_w=pool_w, pool_scale=pool_scale, w_branch_a=w_branch_a, w_branch_b=w_branch_b, w_branch_c=w_branch_c, w_o=w_o, mlp_norm=mlp_norm, w_up=w_up, w_down=w_down, loss_target=loss_target, m_meta_tokens=m_meta_tokens, m_attn_norm=m_attn_norm, m_w_in=m_w_in, m_conv_w=m_conv_w, m_q_lat_norm=m_q_lat_norm, m_kv_lat_norm=m_kv_lat_norm, m_w_uq=m_w_uq, m_w_ukv=m_w_ukv, m_q_norm=m_q_norm, m_k_norm=m_k_norm, m_pool_w=m_pool_w, m_pool_scale=m_pool_scale, m_w_branch_a=m_w_branch_a, m_w_branch_b=m_w_branch_b, m_w_branch_c=m_w_branch_c, m_w_o=m_w_o, m_mlp_norm=m_mlp_norm, m_w_up=m_w_up, m_w_down=m_w_down, v_meta_tokens=v_meta_tokens, v_attn_norm=v_attn_norm, v_w_in=v_w_in, v_conv_w=v_conv_w, v_q_lat_norm=v_q_lat_norm, v_kv_lat_norm=v_kv_lat_norm, v_w_uq=v_w_uq, v_w_ukv=v_w_ukv, v_q_norm=v_q_norm, v_k_norm=v_k_norm, v_pool_w=v_pool_w, v_pool_scale=v_pool_scale, v_w_branch_a=v_w_branch_a, v_w_branch_b=v_w_branch_b, v_w_branch_c=v_w_branch_c, v_w_o=v_w_o, v_mlp_norm=v_mlp_norm, v_w_up=v_w_up, v_w_down=v_w_down)
    weights = {n: given[n] for n in TWIN_WEIGHTS}
    shared = {n: given[n] for n in SHARED_INPUTS}
    per_example = {n: given[n] for n in ['x']}
    grad_fn = _jax.value_and_grad(_loss, argnums=(0, 1))

    def one_microbatch(ex, loss_target):
        ex = dict(ex)
        diff = ex.pop(TWIN_DIFF_INPUT)
        return grad_fn(weights, diff, {**shared, **ex}, loss_target)

    if N_MICROBATCH == 1:
        loss, (grad_w, grad_x) = one_microbatch(per_example, given["loss_target"])
    else:
        def body(carry, xs):
            loss_sum, grad_sum = carry
            l_k, (gw_k, gx_k) = one_microbatch(xs[0], xs[1])
            with _jax.named_scope("update"):
                return (loss_sum + l_k, _jax.tree.map(_jnp.add, grad_sum, gw_k)), gx_k

        init = (_jnp.zeros((), _jnp.float32), _jax.tree.map(_jnp.zeros_like, weights))
        (loss, grad_w), grad_x = _jax.lax.scan(body, init, (per_example, given["loss_target"]))
    with _jax.named_scope("update"):
        delta_w, new_m, new_v = {}, {}, {}
        for n in TWIN_WEIGHTS:
            delta_w[n], new_m[n], new_v[n] = _adamw(weights[n], grad_w[n], given["m_" + n], given["v_" + n])
    return (loss, grad_x, *[grad_w[n] for n in TWIN_WEIGHTS], *[delta_w[n] for n in TWIN_WEIGHTS],
            *[new_m[n] for n in TWIN_WEIGHTS], *[new_v[n] for n in TWIN_WEIGHTS])
```

```python
import functools
import math

import jax
import jax.numpy as jnp
from jax import lax
from jax.experimental import pallas as pl
from jax.experimental.pallas import tpu as pltpu

F32 = jnp.float32
BF16 = jnp.bfloat16

VMEM_LIMIT_BYTES = 56 * 1024 * 1024
LANES = 128
EPS = 1e-6
HALO = 16

ADAM_LR = 0.001
ADAM_B1 = 0.9
ADAM_B2 = 0.999
ADAM_EPS = 1e-08
ADAM_WD = 0.01
ADAM_STEP = 10


def _params(sem):
    return pltpu.CompilerParams(dimension_semantics=sem, vmem_limit_bytes=VMEM_LIMIT_BYTES)


def _pick(n, prefs):
    for p in prefs:
        if p <= n and n % p == 0:
            return p
    return n


def matmul(a, b, mode, out_dtypes, name, extras=(), epi=None, tm=None, tn=None, tk=None):
    if mode == "nn":
        (M, K), (K2, N) = a.shape, b.shape
    elif mode == "nt":
        (M, K), (N, K2) = a.shape, b.shape
    else:
        (K, M), (K2, N) = a.shape, b.shape
    assert K == K2, (a.shape, b.shape, mode)
    tm = tm or _pick(M, (1408, 1056, 1024, 768, 512, 384, 256, 128))
    tn = tn or _pick(N, (1280, 1024, 768, 512, 384, 256, 128))
    tk = tk or _pick(K, (1056, 1024, 768, 512, 384, 256, 128))
    nk = K // tk
    dims = {"nn": (((1,), (0,)), ((), ())), "nt": (((1,), (1,)), ((), ())), "tn": (((0,), (0,)), ((), ()))}[mode]
    n_extra, n_out = len(extras), len(out_dtypes)

    def body(*refs):
        a_ref, b_ref = refs[0], refs[1]
        extra_refs = refs[2:2 + n_extra]
        out_refs = refs[2 + n_extra:2 + n_extra + n_out]
        acc_ref = refs[-1]
        k = pl.program_id(2)

        @pl.when(k == 0)
        def _():
            acc_ref[...] = jnp.zeros_like(acc_ref)

        acc_ref[...] += lax.dot_general(a_ref[...], b_ref[...], dims, preferred_element_type=F32)

        @pl.when(k == nk - 1)
        def _():
            acc = acc_ref[...]
            outs = (acc,) if epi is None else epi(acc, *[r[...] for r in extra_refs])
            for o_ref, o in zip(out_refs, outs):
                o_ref[...] = o.astype(o_ref.dtype)

    a_spec = {"nn": pl.BlockSpec((tm, tk), lambda i, j, k: (i, k)),
              "nt": pl.BlockSpec((tm, tk), lambda i, j, k: (i, k)),
              "tn": pl.BlockSpec((tk, tm), lambda i, j, k: (k, i))}[mode]
    b_spec = {"nn": pl.BlockSpec((tk, tn), lambda i, j, k: (k, j)),
              "nt": pl.BlockSpec((tn, tk), lambda i, j, k: (j, k)),
              "tn": pl.BlockSpec((tk, tn), lambda i, j, k: (k, j))}[mode]
    o_spec = pl.BlockSpec((tm, tn), lambda i, j, k: (i, j))
    outs = pl.pallas_call(
        body,
        name=name,
        grid=(M // tm, N // tn, nk),
        in_specs=[a_spec, b_spec] + [o_spec] * n_extra,
        out_specs=[o_spec] * n_out,
        out_shape=[jax.ShapeDtypeStruct((M, N), d) for d in out_dtypes],
        scratch_shapes=[pltpu.VMEM((tm, tn), F32)],
        compiler_params=_params(("parallel", "parallel", "arbitrary")),
    )(a, b, *extras)
    return outs[0] if n_out == 1 else outs


class Dims:
    def __init__(self, d_model, seq, n_meta):
        self.D = d_model
        self.n_meta = n_meta
        self.T_real = seq + n_meta
        self.T = -(-self.T_real // LANES) * LANES
        self.H = d_model // 128
        self.DC = d_model // 2
        self.DP = d_model // 2
        self.PG = self.DP // 4
        self.QL = 512
        self.KL = 512
        self.ROPE = 64
        self.NOPE = 128
        self.QKH = 192
        self.HP = 256
        self.DFF = 4 * d_model
        self.o_gate = 0
        self.o_u = 3 * d_model
        self.o_b = self.o_u + self.DC
        self.o_c = self.o_b + self.DC
        self.o_ql = self.o_c + self.DC
        self.o_kl = self.o_ql + self.QL
        self.o_pool = self.o_kl + self.KL
        self.o_rope = self.o_pool + self.DP
        self.NIN = self.o_rope + 256
        self.tr = _pick(self.T, (384, 256, 128))
        self.tq = _pick(self.T, (384, 256, 128))


def _row_ids(i, tr):
    return i * tr + lax.broadcasted_iota(jnp.int32, (tr, 1), 0)


def rms_fwd(x, col_block, width, g, dm, name):
    tr = dm.tr

    def body(x_ref, g_ref, y_ref):
        xv = x_ref[...]
        r = lax.rsqrt(jnp.mean(xv * xv, axis=-1, keepdims=True) + EPS)
        y_ref[...] = (xv * r * g_ref[...]).astype(y_ref.dtype)

    return pl.pallas_call(
        body, name=name, grid=(dm.T // tr,),
        in_specs=[pl.BlockSpec((tr, width), lambda i: (i, col_block)), pl.BlockSpec((1, width), lambda i: (0, 0))],
        out_specs=pl.BlockSpec((tr, width), lambda i: (i, 0)),
        out_shape=jax.ShapeDtypeStruct((dm.T, width), BF16),
        compiler_params=_params(("parallel",)),
    )(x, g.reshape(1, width))


def rms_bwd(x, col_block, width, g, dy, dres, dm, name):
    tr = dm.tr
    has_res = dres is not None

    def body(*refs):
        if has_res:
            x_ref, g_ref, dy_ref, dres_ref, dx_ref, dxb_ref, dg_ref = refs
        else:
            x_ref, g_ref, dy_ref, dx_ref, dxb_ref, dg_ref = refs
        xv, dyv = x_ref[...], dy_ref[...]
        r = lax.rsqrt(jnp.mean(xv * xv, axis=-1, keepdims=True) + EPS)
        gdy = dyv * g_ref[...]
        dx = r * gdy - xv * (r * r * r) * jnp.mean(xv * gdy, axis=-1, keepdims=True)
        if has_res:
            dx = dx + dres_ref[...]
        dx_ref[...] = dx
        dxb_ref[...] = dx.astype(BF16)

        @pl.when(pl.program_id(0) == 0)
        def _():
            dg_ref[...] = jnp.zeros_like(dg_ref)

        dg_ref[...] += jnp.sum(dyv * xv * r, axis=0, keepdims=True)

    row = pl.BlockSpec((tr, width), lambda i: (i, 0))
    in_specs = [pl.BlockSpec((tr, width), lambda i: (i, col_block)), pl.BlockSpec((1, width), lambda i: (0, 0)), row]
    args = [x, g.reshape(1, width), dy]
    if has_res:
        in_specs.append(row)
        args.append(dres)
    return pl.pallas_call(
        body, name=name, grid=(dm.T // tr,),
        in_specs=in_specs,
        out_specs=[row, row, pl.BlockSpec((1, width), lambda i: (0, 0))],
        out_shape=[jax.ShapeDtypeStruct((dm.T, width), F32), jax.ShapeDtypeStruct((dm.T, width), BF16),
                   jax.ShapeDtypeStruct((1, width), F32)],
        compiler_params=_params(("arbitrary",)),
    )(*args)


def _fill_halo_buf(buf, src_fn, T, R, width):
    zeros = jnp.zeros((HALO, width), F32)
    buf[pl.ds(0, HALO), :] = zeros
    buf[pl.ds(HALO + T, HALO), :] = zeros

    def fill(r, c):
        r0 = pl.multiple_of(r * R, 8)
        buf[pl.ds(r0 + HALO, R), :] = src_fn(r0)
        return c

    lax.fori_loop(0, T // R, fill, 0)


def _back(win, sh):
    return pltpu.roll(win, sh, 0)


def _fwd(win, sh):
    return pltpu.roll(win, win.shape[0] - sh, 0)


def mixer_a_fwd(proj, conv_w, dm, name):
    T, cw = dm.T, 128
    R = dm.tr
    nb = dm.DC // cw

    def body(u_ref, b_ref, c_ref, w_ref, ya_ref, buf):
        _fill_halo_buf(buf, lambda r0: c_ref[pl.ds(r0, R), :] * u_ref[pl.ds(r0, R), :], T, R, cw)
        w0, w1, w2 = w_ref[0:1, :], w_ref[1:2, :], w_ref[2:3, :]

        def chunk(r, c):
            r0 = pl.multiple_of(r * R, 8)
            win = buf[pl.ds(r0, R + HALO), :]
            cv = w2 * win + w1 * _back(win, 1) + w0 * _back(win, 2)
            ya_ref[pl.ds(r0, R), :] = (b_ref[pl.ds(r0, R), :] * cv[HALO:, :]).astype(BF16)
            return c

        lax.fori_loop(0, T // R, chunk, 0)

    col = lambda off: pl.BlockSpec((T, cw), lambda j: (0, off // cw + j))
    return pl.pallas_call(
        body, name=name, grid=(nb,),
        in_specs=[col(dm.o_u), col(dm.o_b), col(dm.o_c), pl.BlockSpec((3, cw), lambda j: (0, j))],
        out_specs=pl.BlockSpec((T, cw), lambda j: (0, j)),
        out_shape=jax.ShapeDtypeStruct((T, dm.DC), BF16),
        scratch_shapes=[pltpu.VMEM((T + 2 * HALO, cw), F32)],
        compiler_params=_params(("parallel",)),
    )(proj, proj, proj, conv_w)


def mixer_a_bwd(proj, conv_w, dya, dm, name):
    T, cw = dm.T, 128
    R = dm.tr
    nb = dm.DC // cw

    def body(u_ref, b_ref, c_ref, w_ref, dya_ref, du_ref, db_ref, dc_ref, dw_ref, sbuf, gbuf):
        _fill_halo_buf(sbuf, lambda r0: c_ref[pl.ds(r0, R), :] * u_ref[pl.ds(r0, R), :], T, R, cw)
        _fill_halo_buf(gbuf, lambda r0: dya_ref[pl.ds(r0, R), :] * b_ref[pl.ds(r0, R), :], T, R, cw)
        w0, w1, w2 = w_ref[0:1, :], w_ref[1:2, :], w_ref[2:3, :]

        def chunk(r, acc):
            a0, a1, a2 = acc
            r0 = pl.multiple_of(r * R, 8)
            swin = sbuf[pl.ds(r0, R + HALO), :]
            s0, s1, s2 = swin[HALO:, :], _back(swin, 1)[HALO:, :], _back(swin, 2)[HALO:, :]
            gwin = gbuf[pl.ds(r0 + HALO, R + HALO), :]
            g0, g1, g2 = gwin[:R, :], _fwd(gwin, 1)[:R, :], _fwd(gwin, 2)[:R, :]
            cv = w2 * s0 + w1 * s1 + w0 * s2
            ds = w2 * g0 + w1 * g1 + w0 * g2
            db_ref[pl.ds(r0, R), :] = (dya_ref[pl.ds(r0, R), :] * cv).astype(BF16)
            du_ref[pl.ds(r0, R), :] = (ds * c_ref[pl.ds(r0, R), :]).astype(BF16)
            dc_ref[pl.ds(r0, R), :] = (ds * u_ref[pl.ds(r0, R), :]).astype(BF16)
            a2 = a2 + jnp.sum(g0 * s0, axis=0, keepdims=True)
            a1 = a1 + jnp.sum(g0 * s1, axis=0, keepdims=True)
            a0 = a0 + jnp.sum(g0 * s2, axis=0, keepdims=True)
            return a0, a1, a2

        z = jnp.zeros((1, cw), F32)
        a0, a1, a2 = lax.fori_loop(0, T // R, chunk, (z, z, z))
        dw_ref[0:1, :] = a0
        dw_ref[1:2, :] = a1
        dw_ref[2:3, :] = a2

    col = lambda off: pl.BlockSpec((T, cw), lambda j: (0, off // cw + j))
    own = pl.BlockSpec((T, cw), lambda j: (0, j))
    o = jax.ShapeDtypeStruct((T, dm.DC), BF16)
    return pl.pallas_call(
        body, name=name, grid=(nb,),
        in_specs=[col(dm.o_u), col(dm.o_b), col(dm.o_c), pl.BlockSpec((3, cw), lambda j: (0, j)), own],
        out_specs=[own, own, own, pl.BlockSpec((3, cw), lambda j: (0, j))],
        out_shape=[o, o, o, jax.ShapeDtypeStruct((3, dm.DC), F32)],
        scratch_shapes=[pltpu.VMEM((T + 2 * HALO, cw), F32), pltpu.VMEM((T + 2 * HALO, cw), F32)],
        compiler_params=_params(("parallel",)),
    )(proj, proj, proj, conv_w, dya)


def _rope(x, C, S):
    return x * C + (pltpu.roll(x, 32, 1) - pltpu.roll(x, 96, 1)) * S


def _rope_t(dy, C, S):
    return dy * C + (pltpu.roll(dy, 96, 1) - pltpu.roll(dy, 32, 1)) * S


def qk_prep_fwd(q0, kv0, proj, qn, kn, C, S, dm, name):
    T, H, tr = dm.T, dm.H, dm.tr
    inv = 1.0 / dm.QKH

    def body(q0_ref, kv_ref, kr_ref, qn_ref, kn_ref, c_ref, s_ref, q_ref, k_ref, v_ref):
        Cv, Sv = c_ref[...], s_ref[...]
        qa, qb = q0_ref[:, :128], q0_ref[:, 128:]
        r = lax.rsqrt((jnp.sum(qa * qa, -1, keepdims=True) + jnp.sum(qb * qb, -1, keepdims=True)) * inv + EPS)
        q_ref[:, :128] = (qa * r * qn_ref[:, :128]).astype(BF16)
        q_ref[:, 128:] = _rope(qb * r * qn_ref[:, 128:], Cv, Sv).astype(BF16)
        ka, kb = kv_ref[:, :128], kr_ref[...]
        r = lax.rsqrt((jnp.sum(ka * ka, -1, keepdims=True) + jnp.sum(kb * kb, -1, keepdims=True)) * inv + EPS)
        k_ref[:, :128] = (ka * r * kn_ref[:, :128]).astype(BF16)
        k_ref[:, 128:] = _rope(kb * r * kn_ref[:, 128:], Cv, Sv).astype(BF16)
        v_ref[...] = kv_ref[:, 128:].astype(BF16)

    head = pl.BlockSpec((tr, 256), lambda i, h: (i, h))
    gain = pl.BlockSpec((1, 256), lambda i, h: (0, 0))
    tab = pl.BlockSpec((tr, 128), lambda i, h: (i, 0))
    return pl.pallas_call(
        body, name=name, grid=(T // tr, H),
        in_specs=[head, head, pl.BlockSpec((tr, 128), lambda i, h: (i, dm.o_rope // 128)), gain, gain, tab, tab],
        out_specs=[head, head, pl.BlockSpec((tr, 128), lambda i, h: (i, h))],
        out_shape=[jax.ShapeDtypeStruct((T, H * 256), BF16), jax.ShapeDtypeStruct((T, H * 256), BF16),
                   jax.ShapeDtypeStruct((T, H * 128), BF16)],
        compiler_params=_params(("parallel", "parallel")),
    )(q0, kv0, proj, qn, kn, C, S)


def qk_prep_bwd(q0, kv0, proj, qn, kn, C, S, dq, dk, dv, dm, name):
    T, H, tr = dm.T, dm.H, dm.tr
    inv = 1.0 / dm.QKH

    def body(q0_ref, kv_ref, kr_ref, qn_ref, kn_ref, c_ref, s_ref, dq_ref, dk_ref, dv_ref,
             dq0_ref, dkv_ref, dkr_ref, dqn_ref, dkn_ref):
        i, h = pl.program_id(0), pl.program_id(1)
        Cv, Sv = c_ref[...], s_ref[...]

        def norm_bwd(xa, xb, ga, gb, dya, dyb):
            r = lax.rsqrt((jnp.sum(xa * xa, -1, keepdims=True) + jnp.sum(xb * xb, -1, keepdims=True)) * inv + EPS)
            dzb = _rope_t(dyb, Cv, Sv)
            gda, gdb = ga * dya, gb * dzb
            dot = (jnp.sum(xa * gda, -1, keepdims=True) + jnp.sum(xb * gdb, -1, keepdims=True)) * inv
            r3 = r * r * r
            dxa = r * gda - xa * r3 * dot
            dxb = r * gdb - xb * r3 * dot
            dga = jnp.sum(dya * xa * r, axis=0, keepdims=True)
            dgb = jnp.sum(dzb * xb * r, axis=0, keepdims=True)
            return dxa, dxb, dga, dgb

        @pl.when((i == 0) & (h == 0))
        def _():
            dqn_ref[...] = jnp.zeros_like(dqn_ref)
            dkn_ref[...] = jnp.zeros_like(dkn_ref)

        dxa, dxb, dga, dgb = norm_bwd(q0_ref[:, :128], q0_ref[:, 128:], qn_ref[:, :128], qn_ref[:, 128:],
                                      dq_ref[:, :128], dq_ref[:, 128:])
        dq0_ref[:, :128] = dxa.astype(BF16)
        dq0_ref[:, 128:] = dxb.astype(BF16)
        dqn_ref[:, :128] += dga
        dqn_ref[:, 128:] += dgb
        dxa, dxb, dga, dgb = norm_bwd(kv_ref[:, :128], kr_ref[...], kn_ref[:, :128], kn_ref[:, 128:],
                                      dk_ref[:, :128], dk_ref[:, 128:])
        dkv_ref[:, :128] = dxa.astype(BF16)
        dkv_ref[:, 128:] = dv_ref[...].astype(BF16)
        dkn_ref[:, :128] += dga
        dkn_ref[:, 128:] += dgb

        @pl.when(h == 0)
        def _():
            dkr_ref[...] = jnp.zeros_like(dkr_ref)

        dkr_ref[...] += dxb

    head = pl.BlockSpec((tr, 256), lambda i, h: (i, h))
    gain = pl.BlockSpec((1, 256), lambda i, h: (0, 0))
    tab = pl.BlockSpec((tr, 128), lambda i, h: (i, 0))
    return pl.pallas_call(
        body, name=name, grid=(T // tr, H),
        in_specs=[head, head, pl.BlockSpec((tr, 128), lambda i, h: (i, dm.o_rope // 128)), gain, gain, tab, tab,
                  head, head, pl.BlockSpec((tr, 128), lambda i, h: (i, h))],
        out_specs=[head, head, tab, gain, gain],
        out_shape=[jax.ShapeDtypeStruct((T, H * 256), BF16), jax.ShapeDtypeStruct((T, H * 256), BF16),
                   jax.ShapeDtypeStruct((T, 128), F32), jax.ShapeDtypeStruct((1, 256), F32),
                   jax.ShapeDtypeStruct((1, 256), F32)],
        compiler_params=_params(("arbitrary", "arbitrary")),
    )(q0, kv0, proj, qn, kn, C, S, dq, dk, dv)


_NT = (((1,), (1,)), ((), ()))


def _causal_mask(t):
    return lax.broadcasted_iota(jnp.int32, (t, t), 0) >= lax.broadcasted_iota(jnp.int32, (t, t), 1)


def _causal_mask_t(t):
    return lax.broadcasted_iota(jnp.int32, (t, t), 0) <= lax.broadcasted_iota(jnp.int32, (t, t), 1)


def attn_fwd(q, k, v, dm, name):
    T, H, tq = dm.T, dm.H, dm.tq
    scale = dm.QKH ** -0.5

    def body(q_ref, k_ref, v_ref, o_ref, ob_ref, lse_ref):
        qi = pl.program_id(1)
        qv = q_ref[...]

        def step(j, carry, masked):
            m, l, acc = carry
            j0 = pl.multiple_of(j * tq, tq)
            s = lax.dot_general(qv, k_ref[pl.ds(j0, tq), :], _NT, preferred_element_type=F32) * scale
            if masked:
                s = jnp.where(_causal_mask(tq), s, -jnp.inf)
            m_new = jnp.maximum(m, jnp.max(s, -1, keepdims=True))
            alpha = jnp.exp(m - m_new)
            p = jnp.exp(s - m_new)
            l = alpha * l + jnp.sum(p, -1, keepdims=True)
            acc = alpha * acc + jnp.dot(p.astype(BF16), v_ref[pl.ds(j0, tq), :], preferred_element_type=F32)
            return m_new, l, acc

        carry = (jnp.full((tq, 1), -jnp.inf, F32), jnp.zeros((tq, 1), F32), jnp.zeros((tq, 128), F32))
        carry = lax.fori_loop(0, qi, lambda j, c: step(j, c, False), carry)
        m, l, acc = step(qi, carry, True)
        o = acc / l
        o_ref[...] = o
        ob_ref[...] = o.astype(BF16)
        lse_ref[...] = m + jnp.log(l)

    return pl.pallas_call(
        body, name=name, grid=(H, T // tq),
        in_specs=[pl.BlockSpec((tq, 256), lambda h, i: (i, h)), pl.BlockSpec((T, 256), lambda h, i: (0, h)),
                  pl.BlockSpec((T, 128), lambda h, i: (0, h))],
        out_specs=[pl.BlockSpec((tq, 128), lambda h, i: (i, h)), pl.BlockSpec((tq, 128), lambda h, i: (i, h)),
                   pl.BlockSpec((None, tq, 1), lambda h, i: (h, i, 0))],
        out_shape=[jax.ShapeDtypeStruct((T, H * 128), F32), jax.ShapeDtypeStruct((T, H * 128), BF16),
                   jax.ShapeDtypeStruct((H, T, 1), F32)],
        compiler_params=_params(("parallel", "parallel")),
    )(q, k, v)


def attn_delta(do, o, dm, name):
    T, H, tr = dm.T, dm.H, dm.tr

    def body(do_ref, o_ref, delta_ref, dob_ref):
        d = do_ref[...]
        delta_ref[...] = jnp.sum(d * o_ref[...], -1, keepdims=True)
        dob_ref[...] = d.astype(BF16)

    blk = pl.BlockSpec((tr, 128), lambda i, h: (i, h))
    return pl.pallas_call(
        body, name=name, grid=(T // tr, H),
        in_specs=[blk, blk],
        out_specs=[pl.BlockSpec((None, tr, 1), lambda i, h: (h, i, 0)), blk],
        out_shape=[jax.ShapeDtypeStruct((H, T, 1), F32), jax.ShapeDtypeStruct((T, H * 128), BF16)],
        compiler_params=_params(("parallel", "parallel")),
    )(do, o)


def attn_bwd_dq(q, k, v, do, lse, delta, dm, name):
    T, H, tq = dm.T, dm.H, dm.tq
    scale = dm.QKH ** -0.5

    def body(q_ref, k_ref, v_ref, do_ref, lse_ref, delta_ref, dq_ref):
        qi = pl.program_id(1)
        qv, dov, lsev, dlt = q_ref[...], do_ref[...], lse_ref[...], delta_ref[...]

        def step(j, dq, masked):
            j0 = pl.multiple_of(j * tq, tq)
            kt = k_ref[pl.ds(j0, tq), :]
            s = lax.dot_general(qv, kt, _NT, preferred_element_type=F32) * scale
            p = jnp.exp(s - lsev)
            if masked:
                p = jnp.where(_causal_mask(tq), p, 0.0)
            dp = lax.dot_general(dov, v_ref[pl.ds(j0, tq), :], _NT, preferred_element_type=F32)
            ds = p * (dp - dlt) * scale
            return dq + jnp.dot(ds.astype(BF16), kt, preferred_element_type=F32)

        dq = lax.fori_loop(0, qi, lambda j, c: step(j, c, False), jnp.zeros((tq, 256), F32))
        dq_ref[...] = step(qi, dq, True)

    stat = pl.BlockSpec((None, tq, 1), lambda h, i: (h, i, 0))
    return pl.pallas_call(
        body, name=name, grid=(H, T // tq),
        in_specs=[pl.BlockSpec((tq, 256), lambda h, i: (i, h)), pl.BlockSpec((T, 256), lambda h, i: (0, h)),
                  pl.BlockSpec((T, 128), lambda h, i: (0, h)), pl.BlockSpec((tq, 128), lambda h, i: (i, h)), stat, stat],
        out_specs=pl.BlockSpec((tq, 256), lambda h, i: (i, h)),
        out_shape=jax.ShapeDtypeStruct((T, H * 256), F32),
        compiler_params=_params(("parallel", "parallel")),
    )(q, k, v, do, lse, delta)


def attn_bwd_dkv(q, k, v, do, lse_rows, delta_rows, dm, name):
    T, H, tq = dm.T, dm.H, dm.tq
    nq = T // tq
    scale = dm.QKH ** -0.5

    def body(q_ref, k_ref, v_ref, do_ref, lse_ref, delta_ref, dk_ref, dv_ref):
        kj = pl.program_id(1)
        kt, vt = k_ref[...], v_ref[...]

        def step(i, carry, masked):
            dk, dv = carry
            i0 = pl.multiple_of(i * tq, tq)
            qt, dot = q_ref[pl.ds(i0, tq), :], do_ref[pl.ds(i0, tq), :]
            st = lax.dot_general(kt, qt, _NT, preferred_element_type=F32) * scale
            pt = jnp.exp(st - lse_ref[i])
            if masked:
                pt = jnp.where(_causal_mask_t(tq), pt, 0.0)
            dv = dv + jnp.dot(pt.astype(BF16), dot, preferred_element_type=F32)
            dpt = lax.dot_general(vt, dot, _NT, preferred_element_type=F32)
            dst = pt * (dpt - delta_ref[i]) * scale
            dk = dk + jnp.dot(dst.astype(BF16), qt, preferred_element_type=F32)
            return dk, dv

        carry = step(kj, (jnp.zeros((tq, 256), F32), jnp.zeros((tq, 128), F32)), True)
        dk, dv = lax.fori_loop(kj + 1, nq, lambda i, c: step(i, c, False), carry)
        dk_ref[...] = dk
        dv_ref[...] = dv

    rows = pl.BlockSpec((None, nq, 1, tq), lambda h, j: (h, 0, 0, 0))
    return pl.pallas_call(
        body, name=name, grid=(H, nq),
        in_specs=[pl.BlockSpec((T, 256), lambda h, j: (0, h)), pl.BlockSpec((tq, 256), lambda h, j: (j, h)),
                  pl.BlockSpec((tq, 128), lambda h, j: (j, h)), pl.BlockSpec((T, 128), lambda h, j: (0, h)), rows, rows],
        out_specs=[pl.BlockSpec((tq, 256), lambda h, j: (j, h)), pl.BlockSpec((tq, 128), lambda h, j: (j, h))],
        out_shape=[jax.ShapeDtypeStruct((T, H * 256), F32), jax.ShapeDtypeStruct((T, H * 128), F32)],
        compiler_params=_params(("parallel", "parallel")),
    )(q, k, v, do, lse_rows, delta_rows)


def _window_sum(win, g, shift):
    s1 = win + shift(win, 1)
    s2 = s1 + shift(s1, 2)
    s3 = s2 + shift(s2, 4)
    s4 = s3 + shift(s3, 8)
    return jnp.where(g == 0, s1, jnp.where(g == 1, s2, jnp.where(g == 2, s3, s4)))


def _count(r0, R, g, T_unused=None):
    t = r0 + lax.broadcasted_iota(jnp.int32, (R, 1), 0)
    return jnp.minimum(t + 1, jnp.left_shift(2, g)).astype(F32)


def pool_fwd(proj, pw, ps, dm, name):
    T, PG, R = dm.T, dm.PG, dm.tr

    def body(x_ref, pw_ref, ps_ref, pooled_ref, mixed_ref, yc_ref, buf):
        g = pl.program_id(0)
        _fill_halo_buf(buf, lambda r0: x_ref[pl.ds(r0, R), :], T, R, PG)

        def chunk(r, c):
            r0 = pl.multiple_of(r * R, 8)
            win = buf[pl.ds(r0, R + HALO), :]
            ws = _window_sum(win, g, _back)[HALO:, :]
            pooled = (ws / _count(r0, R, g) - win[HALO:, :]).astype(BF16)
            pooled_ref[pl.ds(r0, R), :] = pooled
            mixed = jnp.dot(pooled, pw_ref[...], preferred_element_type=F32)
            mixed_ref[pl.ds(r0, R), :] = mixed
            yc_ref[pl.ds(r0, R), :] = (mixed * ps_ref[...]).astype(BF16)
            return c

        lax.fori_loop(0, T // R, chunk, 0)

    own = pl.BlockSpec((T, PG), lambda g: (0, g))
    return pl.pallas_call(
        body, name=name, grid=(4,),
        in_specs=[pl.BlockSpec((T, PG), lambda g: (0, dm.o_pool // PG + g)), pl.BlockSpec((None, PG, PG), lambda g: (g, 0, 0)),
                  pl.BlockSpec((1, PG), lambda g: (0, g))],
        out_specs=[own, own, own],
        out_shape=[jax.ShapeDtypeStruct((T, dm.DP), BF16), jax.ShapeDtypeStruct((T, dm.DP), F32),
                   jax.ShapeDtypeStruct((T, dm.DP), BF16)],
        scratch_shapes=[pltpu.VMEM((T + 2 * HALO, PG), F32)],
        compiler_params=_params(("parallel",)),
    )(proj, pw, ps)


def pool_bwd(dyc, mixed, pooled, pw, ps, dm, name):
    T, PG, R = dm.T, dm.PG, dm.tr
    _TN = (((0,), (0,)), ((), ()))

    def body(dyc_ref, mixed_ref, pooled_ref, pw_ref, ps_ref, dx_ref, dpw_ref, dps_ref, qbuf, dpbuf):
        g = pl.program_id(0)
        zeros = jnp.zeros((HALO, PG), F32)
        qbuf[pl.ds(0, HALO), :] = zeros
        qbuf[pl.ds(HALO + T, HALO), :] = zeros
        dpw_ref[...] = jnp.zeros_like(dpw_ref)

        def first(r, dps):
            r0 = pl.multiple_of(r * R, 8)
            dyc = dyc_ref[pl.ds(r0, R), :]
            dps = dps + jnp.sum(dyc * mixed_ref[pl.ds(r0, R), :], axis=0, keepdims=True)
            dmb = (dyc * ps_ref[...]).astype(BF16)
            dpw_ref[...] += lax.dot_general(pooled_ref[pl.ds(r0, R), :], dmb, _TN, preferred_element_type=F32)
            dp = lax.dot_general(dmb, pw_ref[...], _NT, preferred_element_type=F32)
            dpbuf[pl.ds(r0, R), :] = dp
            qbuf[pl.ds(r0 + HALO, R), :] = dp / _count(r0, R, g)
            return dps

        dps_ref[...] = lax.fori_loop(0, T // R, first, jnp.zeros((1, PG), F32))

        def second(r, c):
            r0 = pl.multiple_of(r * R, 8)
            win = qbuf[pl.ds(r0 + HALO, R + HALO), :]
            ws = _window_sum(win, g, _fwd)[:R, :]
            dx_ref[pl.ds(r0, R), :] = (ws - dpbuf[pl.ds(r0, R), :]).astype(BF16)
            return c

        lax.fori_loop(0, T // R, second, 0)

    own = pl.BlockSpec((T, PG), lambda g: (0, g))
    return pl.pallas_call(
        body, name=name, grid=(4,),
        in_specs=[own, own, own, pl.BlockSpec((None, PG, PG), lambda g: (g, 0, 0)), pl.BlockSpec((1, PG), lambda g: (0, g))],
        out_specs=[own, pl.BlockSpec((None, PG, PG), lambda g: (g, 0, 0)), pl.BlockSpec((1, PG), lambda g: (0, g))],
        out_shape=[jax.ShapeDtypeStruct((T, dm.DP), BF16), jax.ShapeDtypeStruct((4, PG, PG), F32),
                   jax.ShapeDtypeStruct((1, dm.DP), F32)],
        scratch_shapes=[pltpu.VMEM((T + 2 * HALO, PG), F32), pltpu.VMEM((T, PG), F32)],
        compiler_params=_params(("parallel",)),
    )(dyc, mixed, pooled, pw, ps)


def _sigmoid(x):
    return 1.0 / (1.0 + jnp.exp(-x))


def merge_fwd(proj, A, B, C, dm, name):
    T, D, tr, tc = dm.T, dm.D, dm.tr, 512
    nc = D // tc

    def body(g0, g1, g2, a, b, c, out):
        out[...] = (_sigmoid(g0[...]) * a[...] + _sigmoid(g1[...]) * b[...] + _sigmoid(g2[...]) * c[...]).astype(BF16)

    gate = lambda k: pl.BlockSpec((tr, tc), lambda i, j: (i, k * nc + j))
    own = pl.BlockSpec((tr, tc), lambda i, j: (i, j))
    return pl.pallas_call(
        body, name=name, grid=(T // tr, nc),
        in_specs=[gate(0), gate(1), gate(2), own, own, own],
        out_specs=own,
        out_shape=jax.ShapeDtypeStruct((T, D), BF16),
        compiler_params=_params(("parallel", "parallel")),
    )(proj, proj, proj, A, B, C)


def merge_bwd(proj, A, B, C, dmerged, dm, name):
    T, D, tr, tc = dm.T, dm.D, dm.tr, 512
    nc = D // tc

    def body(g0, g1, g2, a, b, c, dmr, da, db, dc, dl0, dl1, dl2):
        d = dmr[...]
        for g_ref, y_ref, dy_ref, dl_ref in ((g0, a, da, dl0), (g1, b, db, dl1), (g2, c, dc, dl2)):
            s = _sigmoid(g_ref[...])
            dy_ref[...] = (d * s).astype(BF16)
            dl_ref[...] = (d * y_ref[...] * s * (1.0 - s)).astype(BF16)

    gate = lambda k: pl.BlockSpec((tr, tc), lambda i, j: (i, k * nc + j))
    own = pl.BlockSpec((tr, tc), lambda i, j: (i, j))
    o = jax.ShapeDtypeStruct((T, D), BF16)
    return pl.pallas_call(
        body, name=name, grid=(T // tr, nc),
        in_specs=[gate(0), gate(1), gate(2), own, own, own, own],
        out_specs=[own] * 6,
        out_shape=[o] * 6,
        compiler_params=_params(("parallel", "parallel")),
    )(proj, proj, proj, A, B, C, dmerged)


def loss_head(y, target, dm, name):
    T, D, tr = dm.T, dm.D, dm.tr

    def body(y_ref, t_ref, dy_ref, dyb_ref, loss_ref):
        i = pl.program_id(0)
        t = _row_ids(i, tr)
        real = (t >= dm.n_meta) & (t < dm.T_real)
        err = jnp.where(real, y_ref[...] - t_ref[...], 0.0)
        dy = err * (1.0 / D)
        dy_ref[...] = dy
        dyb_ref[...] = dy.astype(BF16)

        @pl.when(i == 0)
        def _():
            loss_ref[...] = jnp.zeros_like(loss_ref)

        loss_ref[...] += 0.5 * jnp.sum(jnp.sum(err * err, axis=-1, keepdims=True) * (1.0 / D))

    row = pl.BlockSpec((tr, D), lambda i: (i, 0))
    return pl.pallas_call(
        body, name=name, grid=(T // tr,),
        in_specs=[row, row],
        out_specs=[row, row, pl.BlockSpec((8, LANES), lambda i: (0, 0))],
        out_shape=[jax.ShapeDtypeStruct((T, D), F32), jax.ShapeDtypeStruct((T, D), BF16),
                   jax.ShapeDtypeStruct((8, LANES), F32)],
        compiler_params=_params(("arbitrary",)),
    )(y, target)


def adamw(w, m, v, parts, name):
    R, C = w.shape
    P = parts.shape[0]
    br = R
    for cand in (512, 256, 128, 64, 32, 16, 8):
        if R % cand == 0 and cand * C * 4 <= (1 << 20):
            br = cand
            break
    if R * C * 4 <= (1 << 20):
        br = R

    def body(w_ref, m_ref, v_ref, p_ref, g_ref, d_ref, nm_ref, nv_ref):
        g = p_ref[0].astype(F32)
        for k in range(1, P):
            g = g + p_ref[k].astype(F32)
        mm = ADAM_B1 * m_ref[...] + (1.0 - ADAM_B1) * g
        vv = ADAM_B2 * v_ref[...] + (1.0 - ADAM_B2) * (g * g)
        m_hat = mm / (1.0 - ADAM_B1 ** ADAM_STEP)
        v_hat = vv / (1.0 - ADAM_B2 ** ADAM_STEP)
        g_ref[...] = g
        d_ref[...] = -ADAM_LR * (m_hat / (jnp.sqrt(v_hat) + ADAM_EPS) + ADAM_WD * w_ref[...])
        nm_ref[...] = mm
        nv_ref[...] = vv

    blk = pl.BlockSpec((br, C), lambda i: (i, 0))
    o = jax.ShapeDtypeStruct((R, C), F32)
    return pl.pallas_call(
        body, name=name, grid=(R // br,),
        in_specs=[blk, blk, blk, pl.BlockSpec((P, br, C), lambda i: (0, i, 0))],
        out_specs=[blk] * 4,
        out_shape=[o] * 4,
        compiler_params=_params(("parallel",)),
    )(w, m, v, parts)


def sum_parts(parts, name):
    P, R, C = parts.shape

    def body(p_ref, o_ref):
        acc = p_ref[0]
        for k in range(1, P):
            acc = acc + p_ref[k]
        o_ref[...] = acc

    return pl.pallas_call(
        body, name=name, grid=(1,),
        in_specs=[pl.BlockSpec((P, R, C), lambda i: (0, 0, 0))],
        out_specs=pl.BlockSpec((R, C), lambda i: (0, 0)),
        out_shape=jax.ShapeDtypeStruct((R, C), F32),
        compiler_params=_params(("arbitrary",)),
    )(parts)


def add_bf16(a, b, name):
    R, C = a.shape
    br = _pick(R, (1024, 512, 256, 128, 64, 32, 16))

    def body(a_ref, b_ref, o_ref):
        o_ref[...] = (a_ref[...].astype(F32) + b_ref[...].astype(F32)).astype(BF16)

    blk = pl.BlockSpec((br, C), lambda i: (i, 0))
    return pl.pallas_call(
        body, name=name, grid=(R // br,), in_specs=[blk, blk], out_specs=blk,
        out_shape=jax.ShapeDtypeStruct((R, C), BF16), compiler_params=_params(("parallel",)),
    )(a, b)


_MESH = pl.DeviceIdType.MESH
_HBM = pl.BlockSpec(memory_space=pltpu.HBM)


def _place():
    return lax.axis_index("x"), lax.axis_index("y"), lax.axis_index("c")


def all_gather(arrs, name):
    n = len(arrs)

    def body(*refs):
        ins, outs = refs[:n], refs[n:2 * n]
        send_sems, recv_sems, local_sems = refs[2 * n:]
        x, y, c = _place()
        me, sibling = (x, y, c), (x, y, 1 - c)
        chips = [(1 - x, y), (x, 1 - y), (1 - x, 1 - y)]

        def copy(a, k, block, to, src=None):
            px, py, pc = block
            dst = outs[a].at[4 * px + 2 * py + pc]
            return pltpu.make_async_remote_copy(
                src_ref=dst if src is None else src, dst_ref=dst,
                send_sem=send_sems.at[7 * a + k], recv_sem=recv_sems.at[7 * a + k],
                device_id=to, device_id_type=_MESH)

        started = []
        for a in range(n):
            mine = pltpu.make_async_copy(ins[a], outs[a].at[4 * x + 2 * y + c], local_sems.at[a])
            mine.start()
            started.append(mine)
        sends = []
        for a in range(n):
            sends.append(copy(a, 0, me, sibling, src=ins[a]))
            for j, chip in enumerate(chips):
                sends.append(copy(a, 1 + j, me, (*chip, c), src=ins[a]))
        for cp in sends:
            cp.start()
        for j, chip in enumerate(chips):
            for a in range(n):
                copy(a, 1 + j, (*chip, c), me).wait_recv()
                fwd = copy(a, 4 + j, (*chip, c), sibling)
                fwd.start()
                sends.append(fwd)
        for a in range(n):
            copy(a, 0, sibling, me).wait_recv()
            for j, chip in enumerate(chips):
                copy(a, 4 + j, (*chip, 1 - c), me).wait_recv()
        for cp in sends:
            cp.wait_send()
        for cp in started:
            cp.wait()

    outs = pl.pallas_call(
        body, name=name,
        in_specs=[_HBM] * n, out_specs=[_HBM] * n,
        out_shape=[jax.ShapeDtypeStruct((8,) + a.shape, a.dtype) for a in arrs],
        scratch_shapes=[pltpu.SemaphoreType.DMA((7 * n,)), pltpu.SemaphoreType.DMA((7 * n,)), pltpu.SemaphoreType.DMA((n,))],
    )(*arrs)
    return list(outs)


def sibling_exchange(arrs, name):
    n = len(arrs)

    def body(*refs):
        ins, mine, got = refs[:n], refs[n:2 * n], refs[2 * n:3 * n]
        send_sems, recv_sems, local_sems = refs[3 * n:]
        x, y, c = _place()
        work = []
        for a in range(n):
            for ch in range(4):
                loc = pltpu.make_async_copy(ins[a].at[ch, c], mine[a].at[ch], local_sems.at[4 * a + ch])
                loc.start()
                work.append(loc)
        for a in range(n):
            for ch in range(4):
                cp = pltpu.make_async_remote_copy(
                    src_ref=ins[a].at[ch, 1 - c], dst_ref=got[a].at[ch],
                    send_sem=send_sems.at[4 * a + ch], recv_sem=recv_sems.at[4 * a + ch],
                    device_id=(x, y, 1 - c), device_id_type=_MESH)
                cp.start()
                work.append(cp)
        for cp in work:
            cp.wait()

    shapes = [jax.ShapeDtypeStruct((4,) + a.shape[2:], a.dtype) for a in arrs]
    outs = pl.pallas_call(
        body, name=name,
        in_specs=[_HBM] * n, out_specs=[_HBM] * (2 * n), out_shape=shapes + shapes,
        scratch_shapes=[pltpu.SemaphoreType.DMA((4 * n,)), pltpu.SemaphoreType.DMA((4 * n,)), pltpu.SemaphoreType.DMA((4 * n,))],
    )(*arrs)
    return list(outs[:n]), list(outs[n:])


def chip_exchange(arrs, name):
    n = len(arrs)

    def body(*refs):
        ins, outs = refs[:n], refs[n:2 * n]
        send_sems, recv_sems, local_sems = refs[2 * n:]
        x, y, c = _place()
        my_chip = 2 * x + y
        chips = [(1 - x, y), (x, 1 - y), (1 - x, 1 - y)]
        work = []
        for a in range(n):
            loc = pltpu.make_async_copy(ins[a].at[my_chip], outs[a].at[my_chip], local_sems.at[a])
            loc.start()
            work.append(loc)
        sends = []
        for j, (px, py) in enumerate(chips):
            for a in range(n):
                cp = pltpu.make_async_remote_copy(
                    src_ref=ins[a].at[2 * px + py], dst_ref=outs[a].at[my_chip],
                    send_sem=send_sems.at[3 * a + j], recv_sem=recv_sems.at[3 * a + j],
                    device_id=(px, py, c), device_id_type=_MESH)
                cp.start()
                sends.append(cp)
        for j, (px, py) in enumerate(chips):
            for a in range(n):
                pltpu.make_async_remote_copy(
                    src_ref=ins[a].at[my_chip], dst_ref=outs[a].at[2 * px + py],
                    send_sem=send_sems.at[3 * a + j], recv_sem=recv_sems.at[3 * a + j],
                    device_id=(px, py, c), device_id_type=_MESH).wait_recv()
        for cp in sends:
            cp.wait_send()
        for cp in work:
            cp.wait()

    outs = pl.pallas_call(
        body, name=name,
        in_specs=[_HBM] * n, out_specs=[_HBM] * n,
        out_shape=[jax.ShapeDtypeStruct(a.shape, a.dtype) for a in arrs],
        scratch_shapes=[pltpu.SemaphoreType.DMA((3 * n,)), pltpu.SemaphoreType.DMA((3 * n,)), pltpu.SemaphoreType.DMA((n,))],
    )(*arrs)
    return list(outs)


COL_SHARDED = ("w_in", "w_uq", "w_ukv", "w_branch_a", "w_branch_c", "w_up")
ROW_SHARDED = ("w_branch_b", "w_o", "w_down")
BIG = COL_SHARDED + ROW_SHARDED


def _full_from_stacked(name, st):
    if name in COL_SHARDED:
        _, L, K, n = st.shape
        return st.transpose(1, 2, 0, 3).reshape(L, K, 8 * n)
    if name in ROW_SHARDED:
        _, L, k, N = st.shape
        return st.transpose(1, 0, 2, 3).reshape(L, 8 * k, N)
    if name == "pool_w":
        _, L, G, pk, PG = st.shape
        return st.transpose(1, 2, 0, 3, 4).reshape(L, G, 8 * pk, PG)
    if name == "meta_tokens":
        _, M, n = st.shape
        return st.transpose(1, 0, 2).reshape(M, 8 * n)
    if name == "conv_w":
        _, L, W, n = st.shape
        return st.transpose(1, 2, 0, 3).reshape(L, W, 8 * n)
    raise ValueError(name)


def _shards_from_full(name, g):
    if name in COL_SHARDED:
        L, K, N = g.shape
        s = g.reshape(L, K, 8, N // 8).transpose(2, 0, 1, 3)
    else:
        L, K, N = g.shape
        s = g.reshape(L, 8, K // 8, N).transpose(1, 0, 2, 3)
    return s.reshape((4, 2) + s.shape[1:])


def _w_in_to_padded(w, dm):
    o3 = 3 * dm.DC + dm.QL + dm.KL
    o4 = o3 + dm.ROPE
    o5 = o4 + dm.DP
    pad = jnp.zeros(w.shape[:-1] + (256 - dm.ROPE,), w.dtype)
    return jnp.concatenate([w[..., o5:], w[..., :o3], w[..., o4:o5], w[..., o3:o4], pad], axis=-1)


def _w_in_from_padded(g, dm):
    o3 = 3 * dm.DC + dm.QL + dm.KL
    a = 3 * dm.D
    return jnp.concatenate([g[..., a:a + o3], g[..., dm.o_rope:dm.o_rope + dm.ROPE], g[..., dm.o_pool:dm.o_pool + dm.DP],
                            g[..., :a]], axis=-1)


def _pad_heads(w, dm):
    w = w.reshape(w.shape[:-1] + (dm.H, dm.QKH))
    w = jnp.pad(w, [(0, 0)] * (w.ndim - 1) + [(0, dm.HP - dm.QKH)])
    return w.reshape(w.shape[:-2] + (dm.H * dm.HP,))


def _unpad_heads(g, dm):
    g = g.reshape(g.shape[:-1] + (dm.H, dm.HP))[..., :dm.QKH]
    return g.reshape(g.shape[:-2] + (dm.H * dm.QKH,))


def _layer_fwd(xin, W, tabs, dm, l):
    nm = lambda s: f"l{l}_{s}"
    D = dm.D
    h = rms_fwd(xin, 0, D, W["attn_norm"], dm, nm("rms1"))
    proj = matmul(h, W["w_in"], "nn", (F32,), nm("proj"))
    ya = mixer_a_fwd(proj, W["conv_w"], dm, nm("mixa"))
    ql = rms_fwd(proj, dm.o_ql // dm.QL, dm.QL, W["q_lat_norm"], dm, nm("rms_q"))
    kl = rms_fwd(proj, dm.o_kl // dm.KL, dm.KL, W["kv_lat_norm"], dm, nm("rms_kv"))
    q0 = matmul(ql, W["w_uq"], "nn", (F32,), nm("uq"))
    kv0 = matmul(kl, W["w_ukv"], "nn", (F32,), nm("ukv"))
    q_s, k_s, v_s = qk_prep_fwd(q0, kv0, proj, W["q_norm"], W["k_norm"], tabs[0], tabs[1], dm, nm("qkprep"))
    o, ob, lse = attn_fwd(q_s, k_s, v_s, dm, nm("attn"))
    pooled, mixed, yc = pool_fwd(proj, W["pool_w"], W["pool_scale"], dm, nm("pool"))
    A = matmul(ya, W["w_branch_a"], "nn", (F32,), nm("br_a"))
    B = matmul(ob, W["w_branch_b"], "nn", (F32,), nm("br_b"))
    C = matmul(yc, W["w_branch_c"], "nn", (F32,), nm("br_c"))
    merged = merge_fwd(proj, A, B, C, dm, nm("merge"))
    x1 = matmul(merged, W["w_o"], "nn", (F32,), nm("wo"), extras=(xin,), epi=lambda acc, r: (acc + r,))
    h2 = rms_fwd(x1, 0, D, W["mlp_norm"], dm, nm("rms2"))
    up, act = matmul(h2, W["w_up"], "nn", (F32, BF16), nm("up"),
                     epi=lambda acc: (acc, jnp.square(jnp.maximum(acc, 0.0))))
    x2 = matmul(act, W["w_down"], "nn", (F32,), nm("down"), extras=(x1,), epi=lambda acc, r: (acc + r,))
    saved = dict(xin=xin, h=h, proj=proj, ya=ya, ql=ql, kl=kl, q0=q0, kv0=kv0, q_s=q_s, k_s=k_s, v_s=v_s, o=o, ob=ob,
                 lse=lse, pooled=pooled, mixed=mixed, yc=yc, A=A, B=B, C=C, merged=merged, x1=x1, h2=h2, up=up, act=act)
    return x2, saved


def _layer_bwd(dx2, dx2b, S, W, tabs, dm, l):
    nm = lambda s: f"l{l}_b_{s}"
    D, T = dm.D, dm.T
    g = {}
    d_up = matmul(dx2b, W["w_down"], "nt", (BF16,), nm("d_act"), extras=(S["up"],),
                  epi=lambda acc, up: (acc * (2.0 * jnp.maximum(up, 0.0)),))
    g["w_down"] = matmul(S["act"], dx2b, "tn", (BF16,), nm("g_down"))
    g["w_up"] = matmul(S["h2"], d_up, "tn", (BF16,), nm("g_up"))
    dh2 = matmul(d_up, W["w_up"], "nt", (F32,), nm("d_h2"))
    dx1, dx1b, g["mlp_norm"] = rms_bwd(S["x1"], 0, D, W["mlp_norm"], dh2, dx2, dm, nm("rms2"))
    dmerged = matmul(dx1b, W["w_o"], "nt", (F32,), nm("d_merged"))
    g["w_o"] = matmul(S["merged"], dx1b, "tn", (BF16,), nm("g_o"))
    dA, dB, dC, dl0, dl1, dl2 = merge_bwd(S["proj"], S["A"], S["B"], S["C"], dmerged, dm, nm("merge"))
    dya = matmul(dA, W["w_branch_a"], "nt", (F32,), nm("d_ya"))
    g["w_branch_a"] = matmul(S["ya"], dA, "tn", (BF16,), nm("g_a"))
    dyb = matmul(dB, W["w_branch_b"], "nt", (F32,), nm("d_yb"))
    g["w_branch_b"] = matmul(S["ob"], dB, "tn", (BF16,), nm("g_b"))
    dyc = matmul(dC, W["w_branch_c"], "nt", (F32,), nm("d_yc"))
    g["w_branch_c"] = matmul(S["yc"], dC, "tn", (BF16,), nm("g_c"))
    du, db, dc, g["conv_w"] = mixer_a_bwd(S["proj"], W["conv_w"], dya, dm, nm("mixa"))
    dpool, g["pool_w"], g["pool_scale"] = pool_bwd(dyc, S["mixed"], S["pooled"], W["pool_w"], W["pool_scale"], dm, nm("pool"))
    delta, dob = attn_delta(dyb, S["o"], dm, nm("delta"))
    nq = T // dm.tq
    dq = attn_bwd_dq(S["q_s"], S["k_s"], S["v_s"], dob, S["lse"], delta, dm, nm("attn_dq"))
    dk, dv = attn_bwd_dkv(S["q_s"], S["k_s"], S["v_s"], dob, S["lse"].reshape(dm.H, nq, 1, dm.tq),
                          delta.reshape(dm.H, nq, 1, dm.tq), dm, nm("attn_dkv"))
    dq0, dkv0, dkr, g["q_norm"], g["k_norm"] = qk_prep_bwd(S["q0"], S["kv0"], S["proj"], W["q_norm"], W["k_norm"],
                                                            tabs[0], tabs[1], dq, dk, dv, dm, nm("qkprep"))
    dql = matmul(dq0, W["w_uq"], "nt", (F32,), nm("d_ql"))
    g["w_uq"] = matmul(S["ql"], dq0, "tn", (BF16,), nm("g_uq"))
    dkl = matmul(dkv0, W["w_ukv"], "nt", (F32,), nm("d_kl"))
    g["w_ukv"] = matmul(S["kl"], dkv0, "tn", (BF16,), nm("g_ukv"))
    _, dqlat, g["q_lat_norm"] = rms_bwd(S["proj"], dm.o_ql // dm.QL, dm.QL, W["q_lat_norm"], dql, None, dm, nm("rms_q"))
    _, dkvlat, g["kv_lat_norm"] = rms_bwd(S["proj"], dm.o_kl // dm.KL, dm.KL, W["kv_lat_norm"], dkl, None, dm, nm("rms_kv"))
    dproj = jnp.concatenate([dl0, dl1, dl2, du, db, dc, dqlat, dkvlat, dpool, dkr.astype(BF16),
                             jnp.zeros((T, 128), BF16)], axis=1)
    dh = matmul(dproj, W["w_in"], "nt", (F32,), nm("d_h"))
    g["w_in"] = matmul(S["h"], dproj, "tn", (BF16,), nm("g_in"))
    dx, dxb, g["attn_norm"] = rms_bwd(S["xin"], 0, D, W["attn_norm"], dh, dx1, dm, nm("rms1"))
    return dx, dxb, g


WEIGHTS = ("meta_tokens", "attn_norm", "w_in", "conv_w", "q_lat_norm", "kv_lat_norm", "w_uq", "w_ukv", "q_norm", "k_norm",
           "pool_w", "pool_scale", "w_branch_a", "w_branch_b", "w_branch_c", "w_o", "mlp_norm", "w_up", "w_down")
SMALL = tuple(n for n in WEIGHTS if n not in BIG)


def kernel(x, meta_tokens, attn_norm, w_in, conv_w, q_lat_norm, kv_lat_norm, w_uq, w_ukv, q_norm, k_norm, pool_w, pool_scale, w_branch_a, w_branch_b, w_branch_c, w_o, mlp_norm, w_up, w_down, loss_target, m_meta_tokens, m_attn_norm, m_w_in, m_conv_w, m_q_lat_norm, m_kv_lat_norm, m_w_uq, m_w_ukv, m_q_norm, m_k_norm, m_pool_w, m_pool_scale, m_w_branch_a, m_w_branch_b, m_w_branch_c, m_w_o, m_mlp_norm, m_w_up, m_w_down, v_meta_tokens, v_attn_norm, v_w_in, v_conv_w, v_q_lat_norm, v_kv_lat_norm, v_w_uq, v_w_ukv, v_q_norm, v_k_norm, v_pool_w, v_pool_scale, v_w_branch_a, v_w_branch_b, v_w_branch_c, v_w_o, v_mlp_norm, v_w_up, v_w_down):
    w = dict(meta_tokens=meta_tokens, attn_norm=attn_norm, w_in=w_in, conv_w=conv_w, q_lat_norm=q_lat_norm,
             kv_lat_norm=kv_lat_norm, w_uq=w_uq, w_ukv=w_ukv, q_norm=q_norm, k_norm=k_norm, pool_w=pool_w,
             pool_scale=pool_scale, w_branch_a=w_branch_a, w_branch_b=w_branch_b, w_branch_c=w_branch_c, w_o=w_o,
             mlp_norm=mlp_norm, w_up=w_up, w_down=w_down)
    m = dict(meta_tokens=m_meta_tokens, attn_norm=m_attn_norm, w_in=m_w_in, conv_w=m_conv_w, q_lat_norm=m_q_lat_norm,
             kv_lat_norm=m_kv_lat_norm, w_uq=m_w_uq, w_ukv=m_w_ukv, q_norm=m_q_norm, k_norm=m_k_norm, pool_w=m_pool_w,
             pool_scale=m_pool_scale, w_branch_a=m_w_branch_a, w_branch_b=m_w_branch_b, w_branch_c=m_w_branch_c, w_o=m_w_o,
             mlp_norm=m_mlp_norm, w_up=m_w_up, w_down=m_w_down)
    v = dict(meta_tokens=v_meta_tokens, attn_norm=v_attn_norm, w_in=v_w_in, conv_w=v_conv_w, q_lat_norm=v_q_lat_norm,
             kv_lat_norm=v_kv_lat_norm, w_uq=v_w_uq, w_ukv=v_w_ukv, q_norm=v_q_norm, k_norm=v_k_norm, pool_w=v_pool_w,
             pool_scale=v_pool_scale, w_branch_a=v_w_branch_a, w_branch_b=v_w_branch_b, w_branch_c=v_w_branch_c, w_o=v_w_o,
             mlp_norm=v_mlp_norm, w_up=v_w_up, w_down=v_w_down)
    L = attn_norm.shape[0]
    seq, D = x.shape[1], x.shape[2]
    n_meta = meta_tokens.shape[0]
    dm = Dims(D, seq, n_meta)
    T = dm.T
    me = 4 * lax.axis_index("x") + 2 * lax.axis_index("y") + lax.axis_index("c")

    gather_bf16 = BIG + ("pool_w",)
    stacked = all_gather([w[n].astype(BF16) for n in gather_bf16], "gather_weights")
    full = {n: _full_from_stacked(n, s) for n, s in zip(gather_bf16, stacked)}
    st_small = all_gather([w["meta_tokens"], w["conv_w"]], "gather_small")
    full["meta_tokens"] = _full_from_stacked("meta_tokens", st_small[0])
    full["conv_w"] = _full_from_stacked("conv_w", st_small[1])
    full["w_in"] = _w_in_to_padded(full["w_in"], dm)
    full["w_uq"] = _pad_heads(full["w_uq"], dm)
    pad_gain = lambda gn: jnp.pad(gn, ((0, 0), (0, dm.HP - dm.QKH)))
    for n in ("attn_norm", "q_lat_norm", "kv_lat_norm", "pool_scale", "mlp_norm"):
        full[n] = w[n]
    full["q_norm"], full["k_norm"] = pad_gain(w["q_norm"]), pad_gain(w["k_norm"])

    def layer_weights(l):
        W = {n: full[n][l] for n in BIG + ("pool_w", "conv_w")}
        for n in ("attn_norm", "q_lat_norm", "kv_lat_norm", "pool_scale", "mlp_norm", "q_norm", "k_norm"):
            W[n] = full[n][l][None, :]
        return W

    pos = jnp.arange(dm.T_real, dtype=F32)
    inv = 10000.0 ** (-jnp.arange(0, dm.ROPE, 2, dtype=F32) / dm.ROPE)
    ang = pos[:, None] * inv[None, :]
    zpad = jnp.zeros((dm.T_real, LANES - dm.ROPE), F32)
    rows = ((0, T - dm.T_real), (0, 0))
    tabs = (jnp.pad(jnp.concatenate([jnp.cos(ang), jnp.cos(ang), zpad], 1), rows),
            jnp.pad(jnp.concatenate([jnp.sin(ang), jnp.sin(ang), zpad], 1), rows))

    xs = jnp.concatenate([full["meta_tokens"], x[0], jnp.zeros((T - dm.T_real, D), F32)], axis=0)
    target = jnp.pad(loss_target[0], ((n_meta, T - dm.T_real), (0, 0)))
    saved = []
    Ws = [layer_weights(l) for l in range(L)]
    for l in range(L):
        xs, S = _layer_fwd(xs, Ws[l], tabs, dm, l)
        saved.append(S)
    dx, dxb, loss_acc = loss_head(xs, target, dm, "loss_head")

    grads = [None] * L
    for l in reversed(range(L)):
        dx, dxb, grads[l] = _layer_bwd(dx, dxb, saved[l], Ws[l], tabs, dm, l)
    grad_x = dx[n_meta:dm.T_real][None]

    gfull = {n: jnp.stack([grads[l][n] for l in range(L)]) for n in BIG}
    gfull["w_in"] = _w_in_from_padded(gfull["w_in"], dm)
    gfull["w_uq"] = _unpad_heads(gfull["w_uq"], dm)
    parts = [_shards_from_full(n, gfull[n]) for n in BIG]
    mine, got = sibling_exchange(parts, "reduce_sibling")
    chip_parts = []
    for n, a, b in zip(BIG, mine, got):
        C = a.shape[-1]
        chip_parts.append(add_bf16(a.reshape(-1, C), b.reshape(-1, C), f"reduce_add_{n}").reshape(a.shape))
    summed = chip_exchange(chip_parts, "reduce_chips")

    out_g, out_d, out_m, out_v = {}, {}, {}, {}

    def update(n, parts3):
        shp = w[n].shape
        C = shp[-1]
        res = adamw(w[n].reshape(-1, C), m[n].reshape(-1, C), v[n].reshape(-1, C), parts3, f"adamw_{n}")
        out_g[n], out_d[n], out_m[n], out_v[n] = [r.reshape(shp) for r in res]

    for n, q in zip(BIG, summed):
        update(n, q.reshape(4, -1, q.shape[-1]))

    small_full = {
        "meta_tokens": dx[:n_meta],
        "conv_w": jnp.stack([grads[l]["conv_w"] for l in range(L)]),
        "pool_w": jnp.stack([grads[l]["pool_w"] for l in range(L)]),
        "q_norm": jnp.stack([grads[l]["q_norm"][0, :dm.QKH] for l in range(L)]),
        "k_norm": jnp.stack([grads[l]["k_norm"][0, :dm.QKH] for l in range(L)]),
    }
    for n in ("attn_norm", "q_lat_norm", "kv_lat_norm", "pool_scale", "mlp_norm"):
        small_full[n] = jnp.stack([grads[l][n][0] for l in range(L)])
    flat = jnp.concatenate([small_full[n].reshape(-1) for n in SMALL] + [loss_acc[0, :1]])
    n_flat = flat.shape[0]
    rows_small = -(-n_flat // (8 * LANES)) * 8
    flat = jnp.pad(flat, (0, rows_small * LANES - n_flat)).reshape(rows_small, LANES)
    total = sum_parts(all_gather([flat], "gather_small_grads")[0], "sum_small").reshape(-1)
    off = 0
    for n in SMALL:
        size = math.prod(small_full[n].shape)
        gsum = total[off:off + size].reshape(small_full[n].shape)
        off += size
        if n in ("meta_tokens", "conv_w"):
            blk = w[n].shape[-1]
            gsum = lax.dynamic_slice_in_dim(gsum, me * blk, blk, axis=gsum.ndim - 1)
        elif n == "pool_w":
            blk = w[n].shape[2]
            gsum = lax.dynamic_slice_in_dim(gsum, me * blk, blk, axis=2)
        update(n, gsum.reshape(1, -1, gsum.shape[-1]))
    loss = total[off]

    return (loss, grad_x, *[out_g[n] for n in WEIGHTS], *[out_d[n] for n in WEIGHTS],
            *[out_m[n] for n in WEIGHTS], *[out_v[n] for n in WEIGHTS])
```

```python
import functools
import math

import jax
import jax.numpy as jnp
from jax import lax
from jax.experimental import pallas as pl
from jax.experimental.pallas import tpu as pltpu

F32 = jnp.float32
BF16 = jnp.bfloat16

VMEM_LIMIT_BYTES = 56 * 1024 * 1024
LANES = 128
EPS = 1e-6
HALO = 16

ADAM_LR = 0.001
ADAM_B1 = 0.9
ADAM_B2 = 0.999
ADAM_EPS = 1e-08
ADAM_WD = 0.01
ADAM_STEP = 10


def _params(sem):
    return pltpu.CompilerParams(dimension_semantics=sem, vmem_limit_bytes=VMEM_LIMIT_BYTES)


def _pick(n, prefs):
    for p in prefs:
        if p <= n and n % p == 0:
            return p
    return n


def matmul(a, b, mode, out_dtypes, name, extras=(), epi=None, tm=None, tn=None, tk=None):
    if mode == "nn":
        (M, K), (K2, N) = a.shape, b.shape
    elif mode == "nt":
        (M, K), (N, K2) = a.shape, b.shape
    else:
        (K, M), (K2, N) = a.shape, b.shape
    assert K == K2, (a.shape, b.shape, mode)
    tm = tm or _pick(M, (1408, 1056, 1024, 768, 512, 384, 256, 128))
    tn = tn or _pick(N, (1280, 1024, 768, 512, 384, 256, 128))
    tk = tk or _pick(K, (1056, 1024, 768, 512, 384, 256, 128))
    nk = K // tk
    dims = {"nn": (((1,), (0,)), ((), ())), "nt": (((1,), (1,)), ((), ())), "tn": (((0,), (0,)), ((), ()))}[mode]
    n_extra, n_out = len(extras), len(out_dtypes)

    def body(*refs):
        a_ref, b_ref = refs[0], refs[1]
        extra_refs = refs[2:2 + n_extra]
        out_refs = refs[2 + n_extra:2 + n_extra + n_out]
        acc_ref = refs[-1]
        k = pl.program_id(2)

        @pl.when(k == 0)
        def _():
            acc_ref[...] = jnp.zeros_like(acc_ref)

        acc_ref[...] += lax.dot_general(a_ref[...], b_ref[...], dims, preferred_element_type=F32)

        @pl.when(k == nk - 1)
        def _():
            acc = acc_ref[...]
            outs = (acc,) if epi is None else epi(acc, *[r[...] for r in extra_refs])
            for o_ref, o in zip(out_refs, outs):
                o_ref[...] = o.astype(o_ref.dtype)

    a_spec = {"nn": pl.BlockSpec((tm, tk), lambda i, j, k: (i, k)),
              "nt": pl.BlockSpec((tm, tk), lambda i, j, k: (i, k)),
              "tn": pl.BlockSpec((tk, tm), lambda i, j, k: (k, i))}[mode]
    b_spec = {"nn": pl.BlockSpec((tk, tn), lambda i, j, k: (k, j)),
              "nt": pl.BlockSpec((tn, tk), lambda i, j, k: (j, k)),
              "tn": pl.BlockSpec((tk, tn), lambda i, j, k: (k, j))}[mode]
    o_spec = pl.BlockSpec((tm, tn), lambda i, j, k: (i, j))
    outs = pl.pallas_call(
        body,
        name=name,
        grid=(M // tm, N // tn, nk),
        in_specs=[a_spec, b_spec] + [o_spec] * n_extra,
        out_specs=[o_spec] * n_out,
        out_shape=[jax.ShapeDtypeStruct((M, N), d) for d in out_dtypes],
        scratch_shapes=[pltpu.VMEM((tm, tn), F32)],
        compiler_params=_params(("parallel", "parallel", "arbitrary")),
    )(a, b, *extras)
    return outs[0] if n_out == 1 else outs


class Dims:
    def __init__(self, d_model, seq, n_meta):
        self.D = d_model
        self.n_meta = n_meta
        self.T_real = seq + n_meta
        self.T = -(-self.T_real // LANES) * LANES
        self.H = d_model // 128
        self.DC = d_model // 2
        self.DP = d_model // 2
        self.PG = self.DP // 4
        self.QL = 512
        self.KL = 512
        self.ROPE = 64
        self.NOPE = 128
        self.QKH = 192
        self.HP = 256
        self.DFF = 4 * d_model
        self.o_gate = 0
        self.o_u = 3 * d_model
        self.o_b = self.o_u + self.DC
        self.o_c = self.o_b + self.DC
        self.o_ql = self.o_c + self.DC
        self.o_kl = self.o_ql + self.QL
        self.o_pool = self.o_kl + self.KL
        self.o_rope = self.o_pool + self.DP
        self.NIN = self.o_rope + 256
        self.tr = _pick(self.T, (384, 256, 128))
        self.tq = _pick(self.T, (384, 256, 128))


def _row_ids(i, tr):
    return i * tr + lax.broadcasted_iota(jnp.int32, (tr, 1), 0)


def rms_fwd(x, col_block, width, g, dm, name):
    tr = dm.tr

    def body(x_ref, g_ref, y_ref):
        xv = x_ref[...]
        r = lax.rsqrt(jnp.mean(xv * xv, axis=-1, keepdims=True) + EPS)
        y_ref[...] = (xv * r * g_ref[...]).astype(y_ref.dtype)

    return pl.pallas_call(
        body, name=name, grid=(dm.T // tr,),
        in_specs=[pl.BlockSpec((tr, width), lambda i: (i, col_block)), pl.BlockSpec((1, width), lambda i: (0, 0))],
        out_specs=pl.BlockSpec((tr, width), lambda i: (i, 0)),
        out_shape=jax.ShapeDtypeStruct((dm.T, width), BF16),
        compiler_params=_params(("parallel",)),
    )(x, g.reshape(1, width))


def rms_bwd(x, col_block, width, g, dy, dres, dm, name):
    tr = dm.tr
    has_res = dres is not None

    def body(*refs):
        if has_res:
            x_ref, g_ref, dy_ref, dres_ref, dx_ref, dxb_ref, dg_ref = refs
        else:
            x_ref, g_ref, dy_ref, dx_ref, dxb_ref, dg_ref = refs
        xv, dyv = x_ref[...], dy_ref[...]
        r = lax.rsqrt(jnp.mean(xv * xv, axis=-1, keepdims=True) + EPS)
        gdy = dyv * g_ref[...]
        dx = r * gdy - xv * (r * r * r) * jnp.mean(xv * gdy, axis=-1, keepdims=True)
        if has_res:
            dx = dx + dres_ref[...]
        dx_ref[...] = dx
        dxb_ref[...] = dx.astype(BF16)

        @pl.when(pl.program_id(0) == 0)
        def _():
            dg_ref[...] = jnp.zeros_like(dg_ref)

        dg_ref[...] += jnp.sum(dyv * xv * r, axis=0, keepdims=True)

    row = pl.BlockSpec((tr, width), lambda i: (i, 0))
    in_specs = [pl.BlockSpec((tr, width), lambda i: (i, col_block)), pl.BlockSpec((1, width), lambda i: (0, 0)), row]
    args = [x, g.reshape(1, width), dy]
    if has_res:
        in_specs.append(row)
        args.append(dres)
    return pl.pallas_call(
        body, name=name, grid=(dm.T // tr,),
        in_specs=in_specs,
        out_specs=[row, row, pl.BlockSpec((1, width), lambda i: (0, 0))],
        out_shape=[jax.ShapeDtypeStruct((dm.T, width), F32), jax.ShapeDtypeStruct((dm.T, width), BF16),
                   jax.ShapeDtypeStruct((1, width), F32)],
        compiler_params=_params(("arbitrary",)),
    )(*args)


def _fill_halo_buf(buf, src_fn, T, R, width):
    zeros = jnp.zeros((HALO, width), F32)
    buf[pl.ds(0, HALO), :] = zeros
    buf[pl.ds(HALO + T, HALO), :] = zeros

    def fill(r, c):
        r0 = pl.multiple_of(r * R, 8)
        buf[pl.ds(r0 + HALO, R), :] = src_fn(r0)
        return c

    lax.fori_loop(0, T // R, fill, 0)


def _back(win, sh):
    return pltpu.roll(win, sh, 0)


def _fwd(win, sh):
    return pltpu.roll(win, win.shape[0] - sh, 0)


def mixer_a_fwd(proj, conv_w, dm, name):
    T, cw = dm.T, 128
    R = dm.tr
    nb = dm.DC // cw

    def body(u_ref, b_ref, c_ref, w_ref, ya_ref, buf):
        _fill_halo_buf(buf, lambda r0: c_ref[pl.ds(r0, R), :] * u_ref[pl.ds(r0, R), :], T, R, cw)
        w0, w1, w2 = w_ref[0:1, :], w_ref[1:2, :], w_ref[2:3, :]

        def chunk(r, c):
            r0 = pl.multiple_of(r * R, 8)
            win = buf[pl.ds(r0, R + HALO), :]
            cv = w2 * win + w1 * _back(win, 1) + w0 * _back(win, 2)
            ya_ref[pl.ds(r0, R), :] = (b_ref[pl.ds(r0, R), :] * cv[HALO:, :]).astype(BF16)
            return c

        lax.fori_loop(0, T // R, chunk, 0)

    col = lambda off: pl.BlockSpec((T, cw), lambda j: (0, off // cw + j))
    return pl.pallas_call(
        body, name=name, grid=(nb,),
        in_specs=[col(dm.o_u), col(dm.o_b), col(dm.o_c), pl.BlockSpec((3, cw), lambda j: (0, j))],
        out_specs=pl.BlockSpec((T, cw), lambda j: (0, j)),
        out_shape=jax.ShapeDtypeStruct((T, dm.DC), BF16),
        scratch_shapes=[pltpu.VMEM((T + 2 * HALO, cw), F32)],
        compiler_params=_params(("parallel",)),
    )(proj, proj, proj, conv_w)


def mixer_a_bwd(proj, conv_w, dya, dm, name):
    T, cw = dm.T, 128
    R = dm.tr
    nb = dm.DC // cw

    def body(u_ref, b_ref, c_ref, w_ref, dya_ref, du_ref, db_ref, dc_ref, dw_ref, sbuf, gbuf):
        _fill_halo_buf(sbuf, lambda r0: c_ref[pl.ds(r0, R), :] * u_ref[pl.ds(r0, R), :], T, R, cw)
        _fill_halo_buf(gbuf, lambda r0: dya_ref[pl.ds(r0, R), :] * b_ref[pl.ds(r0, R), :], T, R, cw)
        w0, w1, w2 = w_ref[0:1, :], w_ref[1:2, :], w_ref[2:3, :]

        def chunk(r, acc):
            a0, a1, a2 = acc
            r0 = pl.multiple_of(r * R, 8)
            swin = sbuf[pl.ds(r0, R + HALO), :]
            s0, s1, s2 = swin[HALO:, :], _back(swin, 1)[HALO:, :], _back(swin, 2)[HALO:, :]
            gwin = gbuf[pl.ds(r0 + HALO, R + HALO), :]
            g0, g1, g2 = gwin[:R, :], _fwd(gwin, 1)[:R, :], _fwd(gwin, 2)[:R, :]
            cv = w2 * s0 + w1 * s1 + w0 * s2
            ds = w2 * g0 + w1 * g1 + w0 * g2
            db_ref[pl.ds(r0, R), :] = (dya_ref[pl.ds(r0, R), :] * cv).astype(BF16)
            du_ref[pl.ds(r0, R), :] = (ds * c_ref[pl.ds(r0, R), :]).astype(BF16)
            dc_ref[pl.ds(r0, R), :] = (ds * u_ref[pl.ds(r0, R), :]).astype(BF16)
            a2 = a2 + jnp.sum(g0 * s0, axis=0, keepdims=True)
            a1 = a1 + jnp.sum(g0 * s1, axis=0, keepdims=True)
            a0 = a0 + jnp.sum(g0 * s2, axis=0, keepdims=True)
            return a0, a1, a2

        z = jnp.zeros((1, cw), F32)
        a0, a1, a2 = lax.fori_loop(0, T // R, chunk, (z, z, z))
        dw_ref[0:1, :] = a0
        dw_ref[1:2, :] = a1
        dw_ref[2:3, :] = a2

    col = lambda off: pl.BlockSpec((T, cw), lambda j: (0, off // cw + j))
    own = pl.BlockSpec((T, cw), lambda j: (0, j))
    o = jax.ShapeDtypeStruct((T, dm.DC), BF16)
    return pl.pallas_call(
        body, name=name, grid=(nb,),
        in_specs=[col(dm.o_u), col(dm.o_b), col(dm.o_c), pl.BlockSpec((3, cw), lambda j: (0, j)), own],
        out_specs=[own, own, own, pl.BlockSpec((3, cw), lambda j: (0, j))],
        out_shape=[o, o, o, jax.ShapeDtypeStruct((3, dm.DC), F32)],
        scratch_shapes=[pltpu.VMEM((T + 2 * HALO, cw), F32), pltpu.VMEM((T + 2 * HALO, cw), F32)],
        compiler_params=_params(("parallel",)),
    )(proj, proj, proj, conv_w, dya)


def _rope(x, C, S):
    return x * C + (pltpu.roll(x, 32, 1) - pltpu.roll(x, 96, 1)) * S


def _rope_t(dy, C, S):
    return dy * C + (pltpu.roll(dy, 96, 1) - pltpu.roll(dy, 32, 1)) * S


def qk_prep_fwd(q0, kv0, proj, qn, kn, C, S, dm, name):
    T, H, tr = dm.T, dm.H, dm.tr
    inv = 1.0 / dm.QKH

    def body(q0_ref, kv_ref, kr_ref, qn_ref, kn_ref, c_ref, s_ref, q_ref, k_ref, v_ref):
        Cv, Sv = c_ref[...], s_ref[...]
        qa, qb = q0_ref[:, :128], q0_ref[:, 128:]
        r = lax.rsqrt((jnp.sum(qa * qa, -1, keepdims=True) + jnp.sum(qb * qb, -1, keepdims=True)) * inv + EPS)
        q_ref[:, :128] = (qa * r * qn_ref[:, :128]).astype(BF16)
        q_ref[:, 128:] = _rope(qb * r * qn_ref[:, 128:], Cv, Sv).astype(BF16)
        ka, kb = kv_ref[:, :128], kr_ref[...]
        r = lax.rsqrt((jnp.sum(ka * ka, -1, keepdims=True) + jnp.sum(kb * kb, -1, keepdims=True)) * inv + EPS)
        k_ref[:, :128] = (ka * r * kn_ref[:, :128]).astype(BF16)
        k_ref[:, 128:] = _rope(kb * r * kn_ref[:, 128:], Cv, Sv).astype(BF16)
        v_ref[...] = kv_ref[:, 128:].astype(BF16)

    head = pl.BlockSpec((tr, 256), lambda i, h: (i, h))
    gain = pl.BlockSpec((1, 256), lambda i, h: (0, 0))
    tab = pl.BlockSpec((tr, 128), lambda i, h: (i, 0))
    return pl.pallas_call(
        body, name=name, grid=(T // tr, H),
        in_specs=[head, head, pl.BlockSpec((tr, 128), lambda i, h: (i, dm.o_rope // 128)), gain, gain, tab, tab],
        out_specs=[head, head, pl.BlockSpec((tr, 128), lambda i, h: (i, h))],
        out_shape=[jax.ShapeDtypeStruct((T, H * 256), BF16), jax.ShapeDtypeStruct((T, H * 256), BF16),
                   jax.ShapeDtypeStruct((T, H * 128), BF16)],
        compiler_params=_params(("parallel", "parallel")),
    )(q0, kv0, proj, qn, kn, C, S)


def qk_prep_bwd(q0, kv0, proj, qn, kn, C, S, dq, dk, dv, dm, name):
    T, H, tr = dm.T, dm.H, dm.tr
    inv = 1.0 / dm.QKH

    def body(q0_ref, kv_ref, kr_ref, qn_ref, kn_ref, c_ref, s_ref, dq_ref, dk_ref, dv_ref,
             dq0_ref, dkv_ref, dkr_ref, dqn_ref, dkn_ref):
        i, h = pl.program_id(0), pl.program_id(1)
        Cv, Sv = c_ref[...], s_ref[...]

        def norm_bwd(xa, xb, ga, gb, dya, dyb):
            r = lax.rsqrt((jnp.sum(xa * xa, -1, keepdims=True) + jnp.sum(xb * xb, -1, keepdims=True)) * inv + EPS)
            dzb = _rope_t(dyb, Cv, Sv)
            gda, gdb = ga * dya, gb * dzb
            dot = (jnp.sum(xa * gda, -1, keepdims=True) + jnp.sum(xb * gdb, -1, keepdims=True)) * inv
            r3 = r * r * r
            dxa = r * gda - xa * r3 * dot
            dxb = r * gdb - xb * r3 * dot
            dga = jnp.sum(dya * xa * r, axis=0, keepdims=True)
            dgb = jnp.sum(dzb * xb * r, axis=0, keepdims=True)
            return dxa, dxb, dga, dgb

        @pl.when((i == 0) & (h == 0))
        def _():
            dqn_ref[...] = jnp.zeros_like(dqn_ref)
            dkn_ref[...] = jnp.zeros_like(dkn_ref)

        dxa, dxb, dga, dgb = norm_bwd(q0_ref[:, :128], q0_ref[:, 128:], qn_ref[:, :128], qn_ref[:, 128:],
                                      dq_ref[:, :128], dq_ref[:, 128:])
        dq0_ref[:, :128] = dxa.astype(BF16)
        dq0_ref[:, 128:] = dxb.astype(BF16)
        dqn_ref[:, :128] += dga
        dqn_ref[:, 128:] += dgb
        dxa, dxb, dga, dgb = norm_bwd(kv_ref[:, :128], kr_ref[...], kn_ref[:, :128], kn_ref[:, 128:],
                                      dk_ref[:, :128], dk_ref[:, 128:])
        dkv_ref[:, :128] = dxa.astype(BF16)
        dkv_ref[:, 128:] = dv_ref[...].astype(BF16)
        dkn_ref[:, :128] += dga
        dkn_ref[:, 128:] += dgb

        @pl.when(h == 0)
        def _():
            dkr_ref[...] = jnp.zeros_like(dkr_ref)

        dkr_ref[...] += dxb

    head = pl.BlockSpec((tr, 256), lambda i, h: (i, h))
    gain = pl.BlockSpec((1, 256), lambda i, h: (0, 0))
    tab = pl.BlockSpec((tr, 128), lambda i, h: (i, 0))
    return pl.pallas_call(
        body, name=name, grid=(T // tr, H),
        in_specs=[head, head, pl.BlockSpec((tr, 128), lambda i, h: (i, dm.o_rope // 128)), gain, gain, tab, tab,
                  head, head, pl.BlockSpec((tr, 128), lambda i, h: (i, h))],
        out_specs=[head, head, tab, gain, gain],
        out_shape=[jax.ShapeDtypeStruct((T, H * 256), BF16), jax.ShapeDtypeStruct((T, H * 256), BF16),
                   jax.ShapeDtypeStruct((T, 128), F32), jax.ShapeDtypeStruct((1, 256), F32),
                   jax.ShapeDtypeStruct((1, 256), F32)],
        compiler_params=_params(("arbitrary", "arbitrary")),
    )(q0, kv0, proj, qn, kn, C, S, dq, dk, dv)


_NT = (((1,), (1,)), ((), ()))


def _causal_mask(t):
    return lax.broadcasted_iota(jnp.int32, (t, t), 0) >= lax.broadcasted_iota(jnp.int32, (t, t), 1)


def _causal_mask_t(t):
    return lax.broadcasted_iota(jnp.int32, (t, t), 0) <= lax.broadcasted_iota(jnp.int32, (t, t), 1)


def attn_fwd(q, k, v, dm, name):
    T, H, tq = dm.T, dm.H, dm.tq
    scale = dm.QKH ** -0.5

    def body(q_ref, k_ref, v_ref, o_ref, ob_ref, lse_ref):
        qi = pl.program_id(1)
        qv = q_ref[...]

        def step(j, carry, masked):
            m, l, acc = carry
            j0 = pl.multiple_of(j * tq, tq)
            s = lax.dot_general(qv, k_ref[pl.ds(j0, tq), :], _NT, preferred_element_type=F32) * scale
            if masked:
                s = jnp.where(_causal_mask(tq), s, -jnp.inf)
            m_new = jnp.maximum(m, jnp.max(s, -1, keepdims=True))
            alpha = jnp.exp(m - m_new)
            p = jnp.exp(s - m_new)
            l = alpha * l + jnp.sum(p, -1, keepdims=True)
            acc = alpha * acc + jnp.dot(p.astype(BF16), v_ref[pl.ds(j0, tq), :], preferred_element_type=F32)
            return m_new, l, acc

        carry = (jnp.full((tq, 1), -jnp.inf, F32), jnp.zeros((tq, 1), F32), jnp.zeros((tq, 128), F32))
        carry = lax.fori_loop(0, qi, lambda j, c: step(j, c, False), carry)
        m, l, acc = step(qi, carry, True)
        o = acc / l
        o_ref[...] = o
        ob_ref[...] = o.astype(BF16)
        lse_ref[...] = m + jnp.log(l)

    return pl.pallas_call(
        body, name=name, grid=(H, T // tq),
        in_specs=[pl.BlockSpec((tq, 256), lambda h, i: (i, h)), pl.BlockSpec((T, 256), lambda h, i: (0, h)),
                  pl.BlockSpec((T, 128), lambda h, i: (0, h))],
        out_specs=[pl.BlockSpec((tq, 128), lambda h, i: (i, h)), pl.BlockSpec((tq, 128), lambda h, i: (i, h)),
                   pl.BlockSpec((None, tq, 1), lambda h, i: (h, i, 0))],
        out_shape=[jax.ShapeDtypeStruct((T, H * 128), F32), jax.ShapeDtypeStruct((T, H * 128), BF16),
                   jax.ShapeDtypeStruct((H, T, 1), F32)],
        compiler_params=_params(("parallel", "parallel")),
    )(q, k, v)


def attn_delta(do, o, dm, name):
    T, H, tr = dm.T, dm.H, dm.tr

    def body(do_ref, o_ref, delta_ref, dob_ref):
        d = do_ref[...]
        delta_ref[...] = jnp.sum(d * o_ref[...], -1, keepdims=True)
        dob_ref[...] = d.astype(BF16)

    blk = pl.BlockSpec((tr, 128), lambda i, h: (i, h))
    return pl.pallas_call(
        body, name=name, grid=(T // tr, H),
        in_specs=[blk, blk],
        out_specs=[pl.BlockSpec((None, tr, 1), lambda i, h: (h, i, 0)), blk],
        out_shape=[jax.ShapeDtypeStruct((H, T, 1), F32), jax.ShapeDtypeStruct((T, H * 128), BF16)],
        compiler_params=_params(("parallel", "parallel")),
    )(do, o)


def attn_bwd_dq(q, k, v, do, lse, delta, dm, name):
    T, H, tq = dm.T, dm.H, dm.tq
    scale = dm.QKH ** -0.5

    def body(q_ref, k_ref, v_ref, do_ref, lse_ref, delta_ref, dq_ref):
        qi = pl.program_id(1)
        qv, dov, lsev, dlt = q_ref[...], do_ref[...], lse_ref[...], delta_ref[...]

        def step(j, dq, masked):
            j0 = pl.multiple_of(j * tq, tq)
            kt = k_ref[pl.ds(j0, tq), :]
            s = lax.dot_general(qv, kt, _NT, preferred_element_type=F32) * scale
            p = jnp.exp(s - lsev)
            if masked:
                p = jnp.where(_causal_mask(tq), p, 0.0)
            dp = lax.dot_general(dov, v_ref[pl.ds(j0, tq), :], _NT, preferred_element_type=F32)
            ds = p * (dp - dlt) * scale
            return dq + jnp.dot(ds.astype(BF16), kt, preferred_element_type=F32)

        dq = lax.fori_loop(0, qi, lambda j, c: step(j, c, False), jnp.zeros((tq, 256), F32))
        dq_ref[...] = step(qi, dq, True)

    stat = pl.BlockSpec((None, tq, 1), lambda h, i: (h, i, 0))
    return pl.pallas_call(
        body, name=name, grid=(H, T // tq),
        in_specs=[pl.BlockSpec((tq, 256), lambda h, i: (i, h)), pl.BlockSpec((T, 256), lambda h, i: (0, h)),
                  pl.BlockSpec((T, 128), lambda h, i: (0, h)), pl.BlockSpec((tq, 128), lambda h, i: (i, h)), stat, stat],
        out_specs=pl.BlockSpec((tq, 256), lambda h, i: (i, h)),
        out_shape=jax.ShapeDtypeStruct((T, H * 256), F32),
        compiler_params=_params(("parallel", "parallel")),
    )(q, k, v, do, lse, delta)


def attn_bwd_dkv(q, k, v, do, lse_rows, delta_rows, dm, name):
    T, H, tq = dm.T, dm.H, dm.tq
    nq = T // tq
    scale = dm.QKH ** -0.5

    def body(q_ref, k_ref, v_ref, do_ref, lse_ref, delta_ref, dk_ref, dv_ref):
        kj = pl.program_id(1)
        kt, vt = k_ref[...], v_ref[...]

        def step(i, carry, masked):
            dk, dv = carry
            i0 = pl.multiple_of(i * tq, tq)
            qt, dot = q_ref[pl.ds(i0, tq), :], do_ref[pl.ds(i0, tq), :]
            st = lax.dot_general(kt, qt, _NT, preferred_element_type=F32) * scale
            pt = jnp.exp(st - lse_ref[i])
            if masked:
                pt = jnp.where(_causal_mask_t(tq), pt, 0.0)
            dv = dv + jnp.dot(pt.astype(BF16), dot, preferred_element_type=F32)
            dpt = lax.dot_general(vt, dot, _NT, preferred_element_type=F32)
            dst = pt * (dpt - delta_ref[i]) * scale
            dk = dk + jnp.dot(dst.astype(BF16), qt, preferred_element_type=F32)
            return dk, dv

        carry = step(kj, (jnp.zeros((tq, 256), F32), jnp.zeros((tq, 128), F32)), True)
        dk, dv = lax.fori_loop(kj + 1, nq, lambda i, c: step(i, c, False), carry)
        dk_ref[...] = dk
        dv_ref[...] = dv

    rows = pl.BlockSpec((None, nq, 1, tq), lambda h, j: (h, 0, 0, 0))
    return pl.pallas_call(
        body, name=name, grid=(H, nq),
        in_specs=[pl.BlockSpec((T, 256), lambda h, j: (0, h)), pl.BlockSpec((tq, 256), lambda h, j: (j, h)),
                  pl.BlockSpec((tq, 128), lambda h, j: (j, h)), pl.BlockSpec((T, 128), lambda h, j: (0, h)), rows, rows],
        out_specs=[pl.BlockSpec((tq, 256), lambda h, j: (j, h)), pl.BlockSpec((tq, 128), lambda h, j: (j, h))],
        out_shape=[jax.ShapeDtypeStruct((T, H * 256), F32), jax.ShapeDtypeStruct((T, H * 128), F32)],
        compiler_params=_params(("parallel", "parallel")),
    )(q, k, v, do, lse_rows, delta_rows)


def _window_sum(win, g, shift):
    s1 = win + shift(win, 1)
    s2 = s1 + shift(s1, 2)
    s3 = s2 + shift(s2, 4)
    s4 = s3 + shift(s3, 8)
    return jnp.where(g == 0, s1, jnp.where(g == 1, s2, jnp.where(g == 2, s3, s4)))


def _count(r0, R, g, T_unused=None):
    t = r0 + lax.broadcasted_iota(jnp.int32, (R, 1), 0)
    return jnp.minimum(t + 1, jnp.left_shift(2, g)).astype(F32)


def pool_fwd(proj, pw, ps, dm, name):
    T, PG, R = dm.T, dm.PG, dm.tr

    def body(x_ref, pw_ref, ps_ref, pooled_ref, mixed_ref, yc_ref, buf):
        g = pl.program_id(0)
        _fill_halo_buf(buf, lambda r0: x_ref[pl.ds(r0, R), :], T, R, PG)

        def chunk(r, c):
            r0 = pl.multiple_of(r * R, 8)
            win = buf[pl.ds(r0, R + HALO), :]
            ws = _window_sum(win, g, _back)[HALO:, :]
            pooled = (ws / _count(r0, R, g) - win[HALO:, :]).astype(BF16)
            pooled_ref[pl.ds(r0, R), :] = pooled
            mixed = jnp.dot(pooled, pw_ref[...], preferred_element_type=F32)
            mixed_ref[pl.ds(r0, R), :] = mixed
            yc_ref[pl.ds(r0, R), :] = (mixed * ps_ref[...]).astype(BF16)
            return c

        lax.fori_loop(0, T // R, chunk, 0)

    own = pl.BlockSpec((T, PG), lambda g: (0, g))
    return pl.pallas_call(
        body, name=name, grid=(4,),
        in_specs=[pl.BlockSpec((T, PG), lambda g: (0, dm.o_pool // PG + g)), pl.BlockSpec((None, PG, PG), lambda g: (g, 0, 0)),
                  pl.BlockSpec((1, PG), lambda g: (0, g))],
        out_specs=[own, own, own],
        out_shape=[jax.ShapeDtypeStruct((T, dm.DP), BF16), jax.ShapeDtypeStruct((T, dm.DP), F32),
                   jax.ShapeDtypeStruct((T, dm.DP), BF16)],
        scratch_shapes=[pltpu.VMEM((T + 2 * HALO, PG), F32)],
        compiler_params=_params(("parallel",)),
    )(proj, pw, ps)


def pool_bwd(dyc, mixed, pooled, pw, ps, dm, name):
    T, PG, R = dm.T, dm.PG, dm.tr
    _TN = (((0,), (0,)), ((), ()))

    def body(dyc_ref, mixed_ref, pooled_ref, pw_ref, ps_ref, dx_ref, dpw_ref, dps_ref, qbuf, dpbuf):
        g = pl.program_id(0)
        zeros = jnp.zeros((HALO, PG), F32)
        qbuf[pl.ds(0, HALO), :] = zeros
        qbuf[pl.ds(HALO + T, HALO), :] = zeros
        dpw_ref[...] = jnp.zeros_like(dpw_ref)

        def first(r, dps):
            r0 = pl.multiple_of(r * R, 8)
            dyc = dyc_ref[pl.ds(r0, R), :]
            dps = dps + jnp.sum(dyc * mixed_ref[pl.ds(r0, R), :], axis=0, keepdims=True)
            dmb = (dyc * ps_ref[...]).astype(BF16)
            dpw_ref[...] += lax.dot_general(pooled_ref[pl.ds(r0, R), :], dmb, _TN, preferred_element_type=F32)
            dp = lax.dot_general(dmb, pw_ref[...], _NT, preferred_element_type=F32)
            dpbuf[pl.ds(r0, R), :] = dp
            qbuf[pl.ds(r0 + HALO, R), :] = dp / _count(r0, R, g)
            return dps

        dps_ref[...] = lax.fori_loop(0, T // R, first, jnp.zeros((1, PG), F32))

        def second(r, c):
            r0 = pl.multiple_of(r * R, 8)
            win = qbuf[pl.ds(r0 + HALO, R + HALO), :]
            ws = _window_sum(win, g, _fwd)[:R, :]
            dx_ref[pl.ds(r0, R), :] = (ws - dpbuf[pl.ds(r0, R), :]).astype(BF16)
            return c

        lax.fori_loop(0, T // R, second, 0)

    own = pl.BlockSpec((T, PG), lambda g: (0, g))
    return pl.pallas_call(
        body, name=name, grid=(4,),
        in_specs=[own, own, own, pl.BlockSpec((None, PG, PG), lambda g: (g, 0, 0)), pl.BlockSpec((1, PG), lambda g: (0, g))],
        out_specs=[own, pl.BlockSpec((None, PG, PG), lambda g: (g, 0, 0)), pl.BlockSpec((1, PG), lambda g: (0, g))],
        out_shape=[jax.ShapeDtypeStruct((T, dm.DP), BF16), jax.ShapeDtypeStruct((4, PG, PG), F32),
                   jax.ShapeDtypeStruct((1, dm.DP), F32)],
        scratch_shapes=[pltpu.VMEM((T + 2 * HALO, PG), F32), pltpu.VMEM((T, PG), F32)],
        compiler_params=_params(("parallel",)),
    )(dyc, mixed, pooled, pw, ps)


def _sigmoid(x):
    return 1.0 / (1.0 + jnp.exp(-x))


def merge_fwd(proj, A, B, C, dm, name):
    T, D, tr, tc = dm.T, dm.D, dm.tr, 512
    nc = D // tc

    def body(g0, g1, g2, a, b, c, out):
        out[...] = (_sigmoid(g0[...]) * a[...] + _sigmoid(g1[...]) * b[...] + _sigmoid(g2[...]) * c[...]).astype(BF16)

    gate = lambda k: pl.BlockSpec((tr, tc), lambda i, j: (i, k * nc + j))
    own = pl.BlockSpec((tr, tc), lambda i, j: (i, j))
    return pl.pallas_call(
        body, name=name, grid=(T // tr, nc),
        in_specs=[gate(0), gate(1), gate(2), own, own, own],
        out_specs=own,
        out_shape=jax.ShapeDtypeStruct((T, D), BF16),
        compiler_params=_params(("parallel", "parallel")),
    )(proj, proj, proj, A, B, C)


def merge_bwd(proj, A, B, C, dmerged, dm, name):
    T, D, tr, tc = dm.T, dm.D, dm.tr, 512
    nc = D // tc

    def body(g0, g1, g2, a, b, c, dmr, da, db, dc, dl0, dl1, dl2):
        d = dmr[...]
        for g_ref, y_ref, dy_ref, dl_ref in ((g0, a, da, dl0), (g1, b, db, dl1), (g2, c, dc, dl2)):
            s = _sigmoid(g_ref[...])
            dy_ref[...] = (d * s).astype(BF16)
            dl_ref[...] = (d * y_ref[...] * s * (1.0 - s)).astype(BF16)

    gate = lambda k: pl.BlockSpec((tr, tc), lambda i, j: (i, k * nc + j))
    own = pl.BlockSpec((tr, tc), lambda i, j: (i, j))
    o = jax.ShapeDtypeStruct((T, D), BF16)
    return pl.pallas_call(
        body, name=name, grid=(T // tr, nc),
        in_specs=[gate(0), gate(1), gate(2), own, own, own, own],
        out_specs=[own] * 6,
        out_shape=[o] * 6,
        compiler_params=_params(("parallel", "parallel")),
    )(proj, proj, proj, A, B, C, dmerged)


def loss_head(y, target, dm, name):
    T, D, tr = dm.T, dm.D, dm.tr

    def body(y_ref, t_ref, dy_ref, dyb_ref, loss_ref):
        i = pl.program_id(0)
        t = _row_ids(i, tr)
        real = (t >= dm.n_meta) & (t < dm.T_real)
        err = jnp.where(real, y_ref[...] - t_ref[...], 0.0)
        dy = err * (1.0 / D)
        dy_ref[...] = dy
        dyb_ref[...] = dy.astype(BF16)

        @pl.when(i == 0)
        def _():
            loss_ref[...] = jnp.zeros_like(loss_ref)

        loss_ref[...] += 0.5 * jnp.sum(jnp.sum(err * err, axis=-1, keepdims=True) * (1.0 / D))

    row = pl.BlockSpec((tr, D), lambda i: (i, 0))
    return pl.pallas_call(
        body, name=name, grid=(T // tr,),
        in_specs=[row, row],
        out_specs=[row, row, pl.BlockSpec((8, LANES), lambda i: (0, 0))],
        out_shape=[jax.ShapeDtypeStruct((T, D), F32), jax.ShapeDtypeStruct((T, D), BF16),
                   jax.ShapeDtypeStruct((8, LANES), F32)],
        compiler_params=_params(("arbitrary",)),
    )(y, target)


def adamw(w, m, v, parts, name):
    R, C = w.shape
    P = parts.shape[0]
    br = R
    for cand in (512, 256, 128, 64, 32, 16, 8):
        if R % cand == 0 and cand * C * 4 <= (1 << 20):
            br = cand
            break
    if R * C * 4 <= (1 << 20):
        br = R

    def body(w_ref, m_ref, v_ref, p_ref, g_ref, d_ref, nm_ref, nv_ref):
        g = p_ref[0].astype(F32)
        for k in range(1, P):
            g = g + p_ref[k].astype(F32)
        mm = ADAM_B1 * m_ref[...] + (1.0 - ADAM_B1) * g
        vv = ADAM_B2 * v_ref[...] + (1.0 - ADAM_B2) * (g * g)
        m_hat = mm / (1.0 - ADAM_B1 ** ADAM_STEP)
        v_hat = vv / (1.0 - ADAM_B2 ** ADAM_STEP)
        g_ref[...] = g
        d_ref[...] = -ADAM_LR * (m_hat / (jnp.sqrt(v_hat) + ADAM_EPS) + ADAM_WD * w_ref[...])
        nm_ref[...] = mm
        nv_ref[...] = vv

    blk = pl.BlockSpec((br, C), lambda i: (i, 0))
    o = jax.ShapeDtypeStruct((R, C), F32)
    return pl.pallas_call(
        body, name=name, grid=(R // br,),
        in_specs=[blk, blk, blk, pl.BlockSpec((P, br, C), lambda i: (0, i, 0))],
        out_specs=[blk] * 4,
        out_shape=[o] * 4,
        compiler_params=_params(("parallel",)),
    )(w, m, v, parts)


def sum_parts(parts, name):
    P, R, C = parts.shape

    def body(p_ref, o_ref):
        acc = p_ref[0]
        for k in range(1, P):
            acc = acc + p_ref[k]
        o_ref[...] = acc

    return pl.pallas_call(
        body, name=name, grid=(1,),
        in_specs=[pl.BlockSpec((P, R, C), lambda i: (0, 0, 0))],
        out_specs=pl.BlockSpec((R, C), lambda i: (0, 0)),
        out_shape=jax.ShapeDtypeStruct((R, C), F32),
        compiler_params=_params(("arbitrary",)),
    )(parts)


def add_sibling(parts, got, core, name):
    _, _, R, C = parts.shape
    br = _pick(R, (1024, 512, 256, 128, 64, 32, 16))

    def body(c_ref, a_ref, b_ref, o_ref):
        o_ref[...] = (a_ref[...].astype(F32) + b_ref[...].astype(F32)).astype(BF16)

    blk = pl.BlockSpec((None, br, C), lambda ch, i, c: (ch, i, 0))
    return pl.pallas_call(
        body, name=name,
        grid_spec=pltpu.PrefetchScalarGridSpec(
            num_scalar_prefetch=1, grid=(4, R // br),
            in_specs=[pl.BlockSpec((None, None, br, C), lambda ch, i, c: (ch, c[0], i, 0)), blk],
            out_specs=blk),
        out_shape=jax.ShapeDtypeStruct((4, R, C), BF16), compiler_params=_params(("parallel", "parallel")),
    )(core, parts, got)


_MESH = pl.DeviceIdType.MESH
_HBM = pl.BlockSpec(memory_space=pltpu.HBM)


def _place():
    return lax.axis_index("x"), lax.axis_index("y"), lax.axis_index("c")


def all_gather(arrs, name):
    n = len(arrs)

    def body(*refs):
        ins, outs = refs[:n], refs[n:2 * n]
        send_sems, recv_sems, local_sems = refs[2 * n:]
        x, y, c = _place()
        me, sibling = (x, y, c), (x, y, 1 - c)
        chips = [(1 - x, y), (x, 1 - y), (1 - x, 1 - y)]

        def copy(a, k, block, to, src=None):
            px, py, pc = block
            dst = outs[a].at[4 * px + 2 * py + pc]
            return pltpu.make_async_remote_copy(
                src_ref=dst if src is None else src, dst_ref=dst,
                send_sem=send_sems.at[7 * a + k], recv_sem=recv_sems.at[7 * a + k],
                device_id=to, device_id_type=_MESH)

        started = []
        for a in range(n):
            mine = pltpu.make_async_copy(ins[a], outs[a].at[4 * x + 2 * y + c], local_sems.at[a])
            mine.start()
            started.append(mine)
        sends = []
        for a in range(n):
            sends.append(copy(a, 0, me, sibling, src=ins[a]))
            for j, chip in enumerate(chips):
                sends.append(copy(a, 1 + j, me, (*chip, c), src=ins[a]))
        for cp in sends:
            cp.start()
        for j, chip in enumerate(chips):
            for a in range(n):
                copy(a, 1 + j, (*chip, c), me).wait_recv()
                fwd = copy(a, 4 + j, (*chip, c), sibling)
                fwd.start()
                sends.append(fwd)
        for a in range(n):
            copy(a, 0, sibling, me).wait_recv()
            for j, chip in enumerate(chips):
                copy(a, 4 + j, (*chip, 1 - c), me).wait_recv()
        for cp in sends:
            cp.wait_send()
        for cp in started:
            cp.wait()

    outs = pl.pallas_call(
        body, name=name,
        in_specs=[_HBM] * n, out_specs=[_HBM] * n,
        out_shape=[jax.ShapeDtypeStruct((8,) + a.shape, a.dtype) for a in arrs],
        scratch_shapes=[pltpu.SemaphoreType.DMA((7 * n,)), pltpu.SemaphoreType.DMA((7 * n,)), pltpu.SemaphoreType.DMA((n,))],
    )(*arrs)
    return list(outs)


def sibling_exchange(arrs, name):
    n = len(arrs)

    def body(*refs):
        ins, got = refs[:n], refs[n:2 * n]
        send_sems, recv_sems = refs[2 * n:]
        x, y, c = _place()
        work = []
        for a in range(n):
            for ch in range(4):
                cp = pltpu.make_async_remote_copy(
                    src_ref=ins[a].at[ch, 1 - c], dst_ref=got[a].at[ch],
                    send_sem=send_sems.at[4 * a + ch], recv_sem=recv_sems.at[4 * a + ch],
                    device_id=(x, y, 1 - c), device_id_type=_MESH)
                cp.start()
                work.append(cp)
        for cp in work:
            cp.wait()

    outs = pl.pallas_call(
        body, name=name,
        in_specs=[_HBM] * n, out_specs=[_HBM] * n,
        out_shape=[jax.ShapeDtypeStruct((4,) + a.shape[2:], a.dtype) for a in arrs],
        scratch_shapes=[pltpu.SemaphoreType.DMA((4 * n,)), pltpu.SemaphoreType.DMA((4 * n,))],
    )(*arrs)
    return list(outs)


def chip_exchange(arrs, name):
    n = len(arrs)

    def body(*refs):
        ins, outs = refs[:n], refs[n:2 * n]
        send_sems, recv_sems, local_sems = refs[2 * n:]
        x, y, c = _place()
        my_chip = 2 * x + y
        chips = [(1 - x, y), (x, 1 - y), (1 - x, 1 - y)]
        work = []
        for a in range(n):
            loc = pltpu.make_async_copy(ins[a].at[my_chip], outs[a].at[my_chip], local_sems.at[a])
            loc.start()
            work.append(loc)
        sends = []
        for j, (px, py) in enumerate(chips):
            for a in range(n):
                cp = pltpu.make_async_remote_copy(
                    src_ref=ins[a].at[2 * px + py], dst_ref=outs[a].at[my_chip],
                    send_sem=send_sems.at[3 * a + j], recv_sem=recv_sems.at[3 * a + j],
                    device_id=(px, py, c), device_id_type=_MESH)
                cp.start()
                sends.append(cp)
        for j, (px, py) in enumerate(chips):
            for a in range(n):
                pltpu.make_async_remote_copy(
                    src_ref=ins[a].at[my_chip], dst_ref=outs[a].at[2 * px + py],
                    send_sem=send_sems.at[3 * a + j], recv_sem=recv_sems.at[3 * a + j],
                    device_id=(px, py, c), device_id_type=_MESH).wait_recv()
        for cp in sends:
            cp.wait_send()
        for cp in work:
            cp.wait()

    outs = pl.pallas_call(
        body, name=name,
        in_specs=[_HBM] * n, out_specs=[_HBM] * n,
        out_shape=[jax.ShapeDtypeStruct(a.shape, a.dtype) for a in arrs],
        scratch_shapes=[pltpu.SemaphoreType.DMA((3 * n,)), pltpu.SemaphoreType.DMA((3 * n,)), pltpu.SemaphoreType.DMA((n,))],
    )(*arrs)
    return list(outs)


COL_SHARDED = ("w_in", "w_uq", "w_ukv", "w_branch_a", "w_branch_c", "w_up")
ROW_SHARDED = ("w_branch_b", "w_o", "w_down")
BIG = COL_SHARDED + ROW_SHARDED


def _full_from_stacked(name, st):
    if name in COL_SHARDED:
        _, L, K, n = st.shape
        return st.transpose(1, 2, 0, 3).reshape(L, K, 8 * n)
    if name in ROW_SHARDED:
        _, L, k, N = st.shape
        return st.transpose(1, 0, 2, 3).reshape(L, 8 * k, N)
    if name == "pool_w":
        _, L, G, pk, PG = st.shape
        return st.transpose(1, 2, 0, 3, 4).reshape(L, G, 8 * pk, PG)
    if name == "meta_tokens":
        _, M, n = st.shape
        return st.transpose(1, 0, 2).reshape(M, 8 * n)
    if name == "conv_w":
        _, L, W, n = st.shape
        return st.transpose(1, 2, 0, 3).reshape(L, W, 8 * n)
    raise ValueError(name)


def _shards_from_full(name, g):
    if name in COL_SHARDED:
        L, K, N = g.shape
        s = g.reshape(L, K, 8, N // 8).transpose(2, 0, 1, 3)
    else:
        L, K, N = g.shape
        s = g.reshape(L, 8, K // 8, N).transpose(1, 0, 2, 3)
    return s.reshape((4, 2) + s.shape[1:])


def _w_in_to_padded(w, dm):
    o3 = 3 * dm.DC + dm.QL + dm.KL
    o4 = o3 + dm.ROPE
    o5 = o4 + dm.DP
    pad = jnp.zeros(w.shape[:-1] + (256 - dm.ROPE,), w.dtype)
    return jnp.concatenate([w[..., o5:], w[..., :o3], w[..., o4:o5], w[..., o3:o4], pad], axis=-1)


def _w_in_from_padded(g, dm):
    o3 = 3 * dm.DC + dm.QL + dm.KL
    a = 3 * dm.D
    return jnp.concatenate([g[..., a:a + o3], g[..., dm.o_rope:dm.o_rope + dm.ROPE], g[..., dm.o_pool:dm.o_pool + dm.DP],
                            g[..., :a]], axis=-1)


def _pad_heads(w, dm):
    w = w.reshape(w.shape[:-1] + (dm.H, dm.QKH))
    w = jnp.pad(w, [(0, 0)] * (w.ndim - 1) + [(0, dm.HP - dm.QKH)])
    return w.reshape(w.shape[:-2] + (dm.H * dm.HP,))


def _unpad_heads(g, dm):
    g = g.reshape(g.shape[:-1] + (dm.H, dm.HP))[..., :dm.QKH]
    return g.reshape(g.shape[:-2] + (dm.H * dm.QKH,))


def _layer_fwd(xin, W, tabs, dm, l):
    nm = lambda s: f"l{l}_{s}"
    D = dm.D
    h = rms_fwd(xin, 0, D, W["attn_norm"], dm, nm("rms1"))
    proj = matmul(h, W["w_in"], "nn", (F32,), nm("proj"))
    ya = mixer_a_fwd(proj, W["conv_w"], dm, nm("mixa"))
    ql = rms_fwd(proj, dm.o_ql // dm.QL, dm.QL, W["q_lat_norm"], dm, nm("rms_q"))
    kl = rms_fwd(proj, dm.o_kl // dm.KL, dm.KL, W["kv_lat_norm"], dm, nm("rms_kv"))
    q0 = matmul(ql, W["w_uq"], "nn", (F32,), nm("uq"))
    kv0 = matmul(kl, W["w_ukv"], "nn", (F32,), nm("ukv"))
    q_s, k_s, v_s = qk_prep_fwd(q0, kv0, proj, W["q_norm"], W["k_norm"], tabs[0], tabs[1], dm, nm("qkprep"))
    o, ob, lse = attn_fwd(q_s, k_s, v_s, dm, nm("attn"))
    pooled, mixed, yc = pool_fwd(proj, W["pool_w"], W["pool_scale"], dm, nm("pool"))
    A = matmul(ya, W["w_branch_a"], "nn", (F32,), nm("br_a"))
    B = matmul(ob, W["w_branch_b"], "nn", (F32,), nm("br_b"))
    C = matmul(yc, W["w_branch_c"], "nn", (F32,), nm("br_c"))
    merged = merge_fwd(proj, A, B, C, dm, nm("merge"))
    x1 = matmul(merged, W["w_o"], "nn", (F32,), nm("wo"), extras=(xin,), epi=lambda acc, r: (acc + r,))
    h2 = rms_fwd(x1, 0, D, W["mlp_norm"], dm, nm("rms2"))
    up, act = matmul(h2, W["w_up"], "nn", (F32, BF16), nm("up"),
                     epi=lambda acc: (acc, jnp.square(jnp.maximum(acc, 0.0))))
    x2 = matmul(act, W["w_down"], "nn", (F32,), nm("down"), extras=(x1,), epi=lambda acc, r: (acc + r,))
    saved = dict(xin=xin, h=h, proj=proj, ya=ya, ql=ql, kl=kl, q0=q0, kv0=kv0, q_s=q_s, k_s=k_s, v_s=v_s, o=o, ob=ob,
                 lse=lse, pooled=pooled, mixed=mixed, yc=yc, A=A, B=B, C=C, merged=merged, x1=x1, h2=h2, up=up, act=act)
    return x2, saved


def _layer_bwd(dx2, dx2b, S, W, tabs, dm, l):
    nm = lambda s: f"l{l}_b_{s}"
    D, T = dm.D, dm.T
    g = {}
    d_up = matmul(dx2b, W["w_down"], "nt", (BF16,), nm("d_act"), extras=(S["up"],),
                  epi=lambda acc, up: (acc * (2.0 * jnp.maximum(up, 0.0)),))
    g["w_down"] = matmul(S["act"], dx2b, "tn", (BF16,), nm("g_down"))
    g["w_up"] = matmul(S["h2"], d_up, "tn", (BF16,), nm("g_up"))
    dh2 = matmul(d_up, W["w_up"], "nt", (F32,), nm("d_h2"))
    dx1, dx1b, g["mlp_norm"] = rms_bwd(S["x1"], 0, D, W["mlp_norm"], dh2, dx2, dm, nm("rms2"))
    dmerged = matmul(dx1b, W["w_o"], "nt", (F32,), nm("d_merged"))
    g["w_o"] = matmul(S["merged"], dx1b, "tn", (BF16,), nm("g_o"))
    dA, dB, dC, dl0, dl1, dl2 = merge_bwd(S["proj"], S["A"], S["B"], S["C"], dmerged, dm, nm("merge"))
    dya = matmul(dA, W["w_branch_a"], "nt", (F32,), nm("d_ya"))
    g["w_branch_a"] = matmul(S["ya"], dA, "tn", (BF16,), nm("g_a"))
    dyb = matmul(dB, W["w_branch_b"], "nt", (F32,), nm("d_yb"))
    g["w_branch_b"] = matmul(S["ob"], dB, "tn", (BF16,), nm("g_b"))
    dyc = matmul(dC, W["w_branch_c"], "nt", (F32,), nm("d_yc"))
    g["w_branch_c"] = matmul(S["yc"], dC, "tn", (BF16,), nm("g_c"))
    du, db, dc, g["conv_w"] = mixer_a_bwd(S["proj"], W["conv_w"], dya, dm, nm("mixa"))
    dpool, g["pool_w"], g["pool_scale"] = pool_bwd(dyc, S["mixed"], S["pooled"], W["pool_w"], W["pool_scale"], dm, nm("pool"))
    delta, dob = attn_delta(dyb, S["o"], dm, nm("delta"))
    nq = T // dm.tq
    dq = attn_bwd_dq(S["q_s"], S["k_s"], S["v_s"], dob, S["lse"], delta, dm, nm("attn_dq"))
    dk, dv = attn_bwd_dkv(S["q_s"], S["k_s"], S["v_s"], dob, S["lse"].reshape(dm.H, nq, 1, dm.tq),
                          delta.reshape(dm.H, nq, 1, dm.tq), dm, nm("attn_dkv"))
    dq0, dkv0, dkr, g["q_norm"], g["k_norm"] = qk_prep_bwd(S["q0"], S["kv0"], S["proj"], W["q_norm"], W["k_norm"],
                                                            tabs[0], tabs[1], dq, dk, dv, dm, nm("qkprep"))
    dql = matmul(dq0, W["w_uq"], "nt", (F32,), nm("d_ql"))
    g["w_uq"] = matmul(S["ql"], dq0, "tn", (BF16,), nm("g_uq"))
    dkl = matmul(dkv0, W["w_ukv"], "nt", (F32,), nm("d_kl"))
    g["w_ukv"] = matmul(S["kl"], dkv0, "tn", (BF16,), nm("g_ukv"))
    _, dqlat, g["q_lat_norm"] = rms_bwd(S["proj"], dm.o_ql // dm.QL, dm.QL, W["q_lat_norm"], dql, None, dm, nm("rms_q"))
    _, dkvlat, g["kv_lat_norm"] = rms_bwd(S["proj"], dm.o_kl // dm.KL, dm.KL, W["kv_lat_norm"], dkl, None, dm, nm("rms_kv"))
    dproj = jnp.concatenate([dl0, dl1, dl2, du, db, dc, dqlat, dkvlat, dpool, dkr.astype(BF16),
                             jnp.zeros((T, 128), BF16)], axis=1)
    dh = matmul(dproj, W["w_in"], "nt", (F32,), nm("d_h"))
    g["w_in"] = matmul(S["h"], dproj, "tn", (BF16,), nm("g_in"))
    dx, dxb, g["attn_norm"] = rms_bwd(S["xin"], 0, D, W["attn_norm"], dh, dx1, dm, nm("rms1"))
    return dx, dxb, g


WEIGHTS = ("meta_tokens", "attn_norm", "w_in", "conv_w", "q_lat_norm", "kv_lat_norm", "w_uq", "w_ukv", "q_norm", "k_norm",
           "pool_w", "pool_scale", "w_branch_a", "w_branch_b", "w_branch_c", "w_o", "mlp_norm", "w_up", "w_down")
SMALL = tuple(n for n in WEIGHTS if n not in BIG)


def kernel(x, meta_tokens, attn_norm, w_in, conv_w, q_lat_norm, kv_lat_norm, w_uq, w_ukv, q_norm, k_norm, pool_w, pool_scale, w_branch_a, w_branch_b, w_branch_c, w_o, mlp_norm, w_up, w_down, loss_target, m_meta_tokens, m_attn_norm, m_w_in, m_conv_w, m_q_lat_norm, m_kv_lat_norm, m_w_uq, m_w_ukv, m_q_norm, m_k_norm, m_pool_w, m_pool_scale, m_w_branch_a, m_w_branch_b, m_w_branch_c, m_w_o, m_mlp_norm, m_w_up, m_w_down, v_meta_tokens, v_attn_norm, v_w_in, v_conv_w, v_q_lat_norm, v_kv_lat_norm, v_w_uq, v_w_ukv, v_q_norm, v_k_norm, v_pool_w, v_pool_scale, v_w_branch_a, v_w_branch_b, v_w_branch_c, v_w_o, v_mlp_norm, v_w_up, v_w_down):
    w = dict(meta_tokens=meta_tokens, attn_norm=attn_norm, w_in=w_in, conv_w=conv_w, q_lat_norm=q_lat_norm,
             kv_lat_norm=kv_lat_norm, w_uq=w_uq, w_ukv=w_ukv, q_norm=q_norm, k_norm=k_norm, pool_w=pool_w,
             pool_scale=pool_scale, w_branch_a=w_branch_a, w_branch_b=w_branch_b, w_branch_c=w_branch_c, w_o=w_o,
             mlp_norm=mlp_norm, w_up=w_up, w_down=w_down)
    m = dict(meta_tokens=m_meta_tokens, attn_norm=m_attn_norm, w_in=m_w_in, conv_w=m_conv_w, q_lat_norm=m_q_lat_norm,
             kv_lat_norm=m_kv_lat_norm, w_uq=m_w_uq, w_ukv=m_w_ukv, q_norm=m_q_norm, k_norm=m_k_norm, pool_w=m_pool_w,
             pool_scale=m_pool_scale, w_branch_a=m_w_branch_a, w_branch_b=m_w_branch_b, w_branch_c=m_w_branch_c, w_o=m_w_o,
             mlp_norm=m_mlp_norm, w_up=m_w_up, w_down=m_w_down)
    v = dict(meta_tokens=v_meta_tokens, attn_norm=v_attn_norm, w_in=v_w_in, conv_w=v_conv_w, q_lat_norm=v_q_lat_norm,
             kv_lat_norm=v_kv_lat_norm, w_uq=v_w_uq, w_ukv=v_w_ukv, q_norm=v_q_norm, k_norm=v_k_norm, pool_w=v_pool_w,
             pool_scale=v_pool_scale, w_branch_a=v_w_branch_a, w_branch_b=v_w_branch_b, w_branch_c=v_w_branch_c, w_o=v_w_o,
             mlp_norm=v_mlp_norm, w_up=v_w_up, w_down=v_w_down)
    L = attn_norm.shape[0]
    seq, D = x.shape[1], x.shape[2]
    n_meta = meta_tokens.shape[0]
    dm = Dims(D, seq, n_meta)
    T = dm.T
    me = 4 * lax.axis_index("x") + 2 * lax.axis_index("y") + lax.axis_index("c")

    gather_bf16 = BIG + ("pool_w",)
    stacked = all_gather([w[n].astype(BF16) for n in gather_bf16], "gather_weights")
    full = {n: _full_from_stacked(n, s) for n, s in zip(gather_bf16, stacked)}
    st_small = all_gather([w["meta_tokens"], w["conv_w"]], "gather_small")
    full["meta_tokens"] = _full_from_stacked("meta_tokens", st_small[0])
    full["conv_w"] = _full_from_stacked("conv_w", st_small[1])
    full["w_in"] = _w_in_to_padded(full["w_in"], dm)
    full["w_uq"] = _pad_heads(full["w_uq"], dm)
    pad_gain = lambda gn: jnp.pad(gn, ((0, 0), (0, dm.HP - dm.QKH)))
    for n in ("attn_norm", "q_lat_norm", "kv_lat_norm", "pool_scale", "mlp_norm"):
        full[n] = w[n]
    full["q_norm"], full["k_norm"] = pad_gain(w["q_norm"]), pad_gain(w["k_norm"])

    def layer_weights(l):
        W = {n: full[n][l] for n in BIG + ("pool_w", "conv_w")}
        for n in ("attn_norm", "q_lat_norm", "kv_lat_norm", "pool_scale", "mlp_norm", "q_norm", "k_norm"):
            W[n] = full[n][l][None, :]
        return W

    pos = jnp.arange(dm.T_real, dtype=F32)
    inv = 10000.0 ** (-jnp.arange(0, dm.ROPE, 2, dtype=F32) / dm.ROPE)
    ang = pos[:, None] * inv[None, :]
    zpad = jnp.zeros((dm.T_real, LANES - dm.ROPE), F32)
    rows = ((0, T - dm.T_real), (0, 0))
    tabs = (jnp.pad(jnp.concatenate([jnp.cos(ang), jnp.cos(ang), zpad], 1), rows),
            jnp.pad(jnp.concatenate([jnp.sin(ang), jnp.sin(ang), zpad], 1), rows))

    xs = jnp.concatenate([full["meta_tokens"], x[0], jnp.zeros((T - dm.T_real, D), F32)], axis=0)
    target = jnp.pad(loss_target[0], ((n_meta, T - dm.T_real), (0, 0)))
    saved = []
    Ws = [layer_weights(l) for l in range(L)]
    for l in range(L):
        xs, S = _layer_fwd(xs, Ws[l], tabs, dm, l)
        saved.append(S)
    dx, dxb, loss_acc = loss_head(xs, target, dm, "loss_head")

    grads = [None] * L
    for l in reversed(range(L)):
        dx, dxb, grads[l] = _layer_bwd(dx, dxb, saved[l], Ws[l], tabs, dm, l)
    grad_x = dx[n_meta:dm.T_real][None]

    gfull = {n: jnp.stack([grads[l][n] for l in range(L)]) for n in BIG}
    gfull["w_in"] = _w_in_from_padded(gfull["w_in"], dm)
    gfull["w_uq"] = _unpad_heads(gfull["w_uq"], dm)
    parts = [_shards_from_full(n, gfull[n]) for n in BIG]
    got = sibling_exchange(parts, "reduce_sibling")
    core = lax.axis_index("c").astype(jnp.int32).reshape(1)
    chip_parts = []
    for n, a, b in zip(BIG, parts, got):
        C = a.shape[-1]
        chip_parts.append(add_sibling(a.reshape(4, 2, -1, C), b.reshape(4, -1, C), core, f"reduce_add_{n}").reshape(b.shape))
    summed = chip_exchange(chip_parts, "reduce_chips")

    out_g, out_d, out_m, out_v = {}, {}, {}, {}

    def update(n, parts3):
        shp = w[n].shape
        C = shp[-1]
        res = adamw(w[n].reshape(-1, C), m[n].reshape(-1, C), v[n].reshape(-1, C), parts3, f"adamw_{n}")
        out_g[n], out_d[n], out_m[n], out_v[n] = [r.reshape(shp) for r in res]

    for n, q in zip(BIG, summed):
        update(n, q.reshape(4, -1, q.shape[-1]))

    small_full = {
        "meta_tokens": dx[:n_meta],
        "conv_w": jnp.stack([grads[l]["conv_w"] for l in range(L)]),
        "pool_w": jnp.stack([grads[l]["pool_w"] for l in range(L)]),
        "q_norm": jnp.stack([grads[l]["q_norm"][0, :dm.QKH] for l in range(L)]),
        "k_norm": jnp.stack([grads[l]["k_norm"][0, :dm.QKH] for l in range(L)]),
    }
    for n in ("attn_norm", "q_lat_norm", "kv_lat_norm", "pool_scale", "mlp_norm"):
        small_full[n] = jnp.stack([grads[l][n][0] for l in range(L)])
    flat = jnp.concatenate([small_full[n].reshape(-1) for n in SMALL] + [loss_acc[0, :1]])
    n_flat = flat.shape[0]
    rows_small = -(-n_flat // (8 * LANES)) * 8
    flat = jnp.pad(flat, (0, rows_small * LANES - n_flat)).reshape(rows_small, LANES)
    total = sum_parts(all_gather([flat], "gather_small_grads")[0], "sum_small").reshape(-1)
    off = 0
    for n in SMALL:
        size = math.prod(small_full[n].shape)
        gsum = total[off:off + size].reshape(small_full[n].shape)
        off += size
        if n in ("meta_tokens", "conv_w"):
            blk = w[n].shape[-1]
            gsum = lax.dynamic_slice_in_dim(gsum, me * blk, blk, axis=gsum.ndim - 1)
        elif n == "pool_w":
            blk = w[n].shape[2]
            gsum = lax.dynamic_slice_in_dim(gsum, me * blk, blk, axis=2)
        update(n, gsum.reshape(1, -1, gsum.shape[-1]))
    loss = total[off]

    return (loss, grad_x, *[out_g[n] for n in WEIGHTS], *[out_d[n] for n in WEIGHTS],
            *[out_m[n] for n in WEIGHTS], *[out_v[n] for n in WEIGHTS])
```

```python
import functools
import math

import jax
import jax.numpy as jnp
from jax import lax
from jax.experimental import pallas as pl
from jax.experimental.pallas import tpu as pltpu

F32 = jnp.float32
BF16 = jnp.bfloat16

VMEM_LIMIT_BYTES = 56 * 1024 * 1024
LANES = 128
EPS = 1e-6
HALO = 16

ADAM_LR = 0.001
ADAM_B1 = 0.9
ADAM_B2 = 0.999
ADAM_EPS = 1e-08
ADAM_WD = 0.01
ADAM_STEP = 10


def _params(sem):
    return pltpu.CompilerParams(dimension_semantics=sem, vmem_limit_bytes=VMEM_LIMIT_BYTES)


def _pick(n, prefs):
    for p in prefs:
        if p <= n and n % p == 0:
            return p
    return n


_MESH = pl.DeviceIdType.MESH
_HBM = pl.BlockSpec(memory_space=pltpu.HBM)


def _place():
    return lax.axis_index("x"), lax.axis_index("y"), lax.axis_index("c")


class Hosted:
    def __init__(self, ins, out_shapes, sems, start, finish):
        self.ins, self.out_shapes, self.sems, self.start, self.finish = ins, out_shapes, sems, start, finish


def _hosted_call(body, *, name, grid, in_specs, out_specs, out_shape, scratch_shapes, semantics, args, comm):
    n_in, n_out, n_scr = len(in_specs), len(out_specs), len(scratch_shapes)
    if comm is None:
        outs = pl.pallas_call(body, name=name, grid=grid, in_specs=in_specs, out_specs=out_specs, out_shape=out_shape,
                              scratch_shapes=scratch_shapes, compiler_params=_params(semantics))(*args)
        return list(outs), []
    ci, co = len(comm.ins), len(comm.out_shapes)

    def hosting(*refs):
        ins, cins = refs[:n_in], refs[n_in:n_in + ci]
        outs = refs[n_in + ci:n_in + ci + n_out]
        couts = refs[n_in + ci + n_out:n_in + ci + n_out + co]
        scr = refs[n_in + ci + n_out + co:n_in + ci + n_out + co + n_scr]
        csems = refs[n_in + ci + n_out + co + n_scr:]
        ids = [pl.program_id(d) for d in range(len(grid))]
        first = functools.reduce(jnp.logical_and, [i == 0 for i in ids])
        last = functools.reduce(jnp.logical_and, [i == g - 1 for i, g in zip(ids, grid)])

        @pl.when(first)
        def _():
            comm.start(cins, couts, csems)

        body(*ins, *outs, *scr)

        @pl.when(last)
        def _():
            comm.finish(cins, couts, csems)

    outs = pl.pallas_call(
        hosting, name=name, grid=grid,
        in_specs=list(in_specs) + [_HBM] * ci, out_specs=list(out_specs) + [_HBM] * co,
        out_shape=list(out_shape) + list(comm.out_shapes),
        scratch_shapes=list(scratch_shapes) + list(comm.sems),
        compiler_params=_params(("arbitrary",) * len(grid)),
    )(*args, *comm.ins)
    return list(outs[:n_out]), list(outs[n_out:])


def matmul(a, b, mode, out_dtypes, name, extras=(), epi=None, tm=None, tn=None, tk=None, comm=None):
    if mode == "nn":
        (M, K), (K2, N) = a.shape, b.shape
    elif mode == "nt":
        (M, K), (N, K2) = a.shape, b.shape
    else:
        (K, M), (K2, N) = a.shape, b.shape
    assert K == K2, (a.shape, b.shape, mode)
    tm = tm or _pick(M, (1408, 1056, 1024, 768, 512, 384, 256, 128))
    tn = tn or _pick(N, (1280, 1024, 768, 512, 384, 256, 128))
    tk = tk or _pick(K, (1056, 1024, 768, 512, 384, 256, 128))
    nk = K // tk
    dims = {"nn": (((1,), (0,)), ((), ())), "nt": (((1,), (1,)), ((), ())), "tn": (((0,), (0,)), ((), ()))}[mode]
    n_extra, n_out = len(extras), len(out_dtypes)

    def body(*refs):
        a_ref, b_ref = refs[0], refs[1]
        extra_refs = refs[2:2 + n_extra]
        out_refs = refs[2 + n_extra:2 + n_extra + n_out]
        acc_ref = refs[-1]
        k = pl.program_id(2)

        @pl.when(k == 0)
        def _():
            acc_ref[...] = jnp.zeros_like(acc_ref)

        acc_ref[...] += lax.dot_general(a_ref[...], b_ref[...], dims, preferred_element_type=F32)

        @pl.when(k == nk - 1)
        def _():
            acc = acc_ref[...]
            outs = (acc,) if epi is None else epi(acc, *[r[...] for r in extra_refs])
            for o_ref, o in zip(out_refs, outs):
                o_ref[...] = o.astype(o_ref.dtype)

    a_spec = {"nn": pl.BlockSpec((tm, tk), lambda i, j, k: (i, k)),
              "nt": pl.BlockSpec((tm, tk), lambda i, j, k: (i, k)),
              "tn": pl.BlockSpec((tk, tm), lambda i, j, k: (k, i))}[mode]
    b_spec = {"nn": pl.BlockSpec((tk, tn), lambda i, j, k: (k, j)),
              "nt": pl.BlockSpec((tn, tk), lambda i, j, k: (j, k)),
              "tn": pl.BlockSpec((tk, tn), lambda i, j, k: (k, j))}[mode]
    o_spec = pl.BlockSpec((tm, tn), lambda i, j, k: (i, j))
    outs, couts = _hosted_call(
        body, name=name, grid=(M // tm, N // tn, nk),
        in_specs=[a_spec, b_spec] + [o_spec] * n_extra,
        out_specs=[o_spec] * n_out,
        out_shape=[jax.ShapeDtypeStruct((M, N), d) for d in out_dtypes],
        scratch_shapes=[pltpu.VMEM((tm, tn), F32)],
        semantics=("parallel", "parallel", "arbitrary"), args=(a, b, *extras), comm=comm)
    res = outs[0] if n_out == 1 else outs
    return res if comm is None else (res, couts)


class Dims:
    def __init__(self, d_model, seq, n_meta):
        self.D = d_model
        self.n_meta = n_meta
        self.T_real = seq + n_meta
        self.T = -(-self.T_real // LANES) * LANES
        self.H = d_model // 128
        self.DC = d_model // 2
        self.DP = d_model // 2
        self.PG = self.DP // 4
        self.QL = 512
        self.KL = 512
        self.ROPE = 64
        self.NOPE = 128
        self.QKH = 192
        self.HP = 256
        self.DFF = 4 * d_model
        self.o_gate = 0
        self.o_u = 3 * d_model
        self.o_b = self.o_u + self.DC
        self.o_c = self.o_b + self.DC
        self.o_ql = self.o_c + self.DC
        self.o_kl = self.o_ql + self.QL
        self.o_pool = self.o_kl + self.KL
        self.o_rope = self.o_pool + self.DP
        self.NIN = self.o_rope + 256
        self.tr = _pick(self.T, (384, 256, 128))
        self.tq = _pick(self.T, (384, 256, 128))


def _row_ids(i, tr):
    return i * tr + lax.broadcasted_iota(jnp.int32, (tr, 1), 0)


def rms_fwd(x, col_block, width, g, dm, name):
    tr = dm.tr

    def body(x_ref, g_ref, y_ref):
        xv = x_ref[...]
        r = lax.rsqrt(jnp.mean(xv * xv, axis=-1, keepdims=True) + EPS)
        y_ref[...] = (xv * r * g_ref[...]).astype(y_ref.dtype)

    return pl.pallas_call(
        body, name=name, grid=(dm.T // tr,),
        in_specs=[pl.BlockSpec((tr, width), lambda i: (i, col_block)), pl.BlockSpec((1, width), lambda i: (0, 0))],
        out_specs=pl.BlockSpec((tr, width), lambda i: (i, 0)),
        out_shape=jax.ShapeDtypeStruct((dm.T, width), BF16),
        compiler_params=_params(("parallel",)),
    )(x, g.reshape(1, width))


def rms_bwd(x, col_block, width, g, dy, dres, dm, name):
    tr = dm.tr
    has_res = dres is not None

    def body(*refs):
        if has_res:
            x_ref, g_ref, dy_ref, dres_ref, dx_ref, dxb_ref, dg_ref = refs
        else:
            x_ref, g_ref, dy_ref, dx_ref, dxb_ref, dg_ref = refs
        xv, dyv = x_ref[...], dy_ref[...]
        r = lax.rsqrt(jnp.mean(xv * xv, axis=-1, keepdims=True) + EPS)
        gdy = dyv * g_ref[...]
        dx = r * gdy - xv * (r * r * r) * jnp.mean(xv * gdy, axis=-1, keepdims=True)
        if has_res:
            dx = dx + dres_ref[...]
        dx_ref[...] = dx
        dxb_ref[...] = dx.astype(BF16)

        @pl.when(pl.program_id(0) == 0)
        def _():
            dg_ref[...] = jnp.zeros_like(dg_ref)

        dg_ref[...] += jnp.sum(dyv * xv * r, axis=0, keepdims=True)

    row = pl.BlockSpec((tr, width), lambda i: (i, 0))
    in_specs = [pl.BlockSpec((tr, width), lambda i: (i, col_block)), pl.BlockSpec((1, width), lambda i: (0, 0)), row]
    args = [x, g.reshape(1, width), dy]
    if has_res:
        in_specs.append(row)
        args.append(dres)
    return pl.pallas_call(
        body, name=name, grid=(dm.T // tr,),
        in_specs=in_specs,
        out_specs=[row, row, pl.BlockSpec((1, width), lambda i: (0, 0))],
        out_shape=[jax.ShapeDtypeStruct((dm.T, width), F32), jax.ShapeDtypeStruct((dm.T, width), BF16),
                   jax.ShapeDtypeStruct((1, width), F32)],
        compiler_params=_params(("arbitrary",)),
    )(*args)


def _fill_halo_buf(buf, src_fn, T, R, width):
    zeros = jnp.zeros((HALO, width), F32)
    buf[pl.ds(0, HALO), :] = zeros
    buf[pl.ds(HALO + T, HALO), :] = zeros

    def fill(r, c):
        r0 = pl.multiple_of(r * R, 8)
        buf[pl.ds(r0 + HALO, R), :] = src_fn(r0)
        return c

    lax.fori_loop(0, T // R, fill, 0)


def _back(win, sh):
    return pltpu.roll(win, sh, 0)


def _fwd(win, sh):
    return pltpu.roll(win, win.shape[0] - sh, 0)


def mixer_a_fwd(proj, conv_w, dm, name):
    T, cw = dm.T, 128
    R = dm.tr
    nb = dm.DC // cw

    def body(u_ref, b_ref, c_ref, w_ref, ya_ref, buf):
        _fill_halo_buf(buf, lambda r0: c_ref[pl.ds(r0, R), :] * u_ref[pl.ds(r0, R), :], T, R, cw)
        w0, w1, w2 = w_ref[0:1, :], w_ref[1:2, :], w_ref[2:3, :]

        def chunk(r, c):
            r0 = pl.multiple_of(r * R, 8)
            win = buf[pl.ds(r0, R + HALO), :]
            cv = w2 * win + w1 * _back(win, 1) + w0 * _back(win, 2)
            ya_ref[pl.ds(r0, R), :] = (b_ref[pl.ds(r0, R), :] * cv[HALO:, :]).astype(BF16)
            return c

        lax.fori_loop(0, T // R, chunk, 0)

    col = lambda off: pl.BlockSpec((T, cw), lambda j: (0, off // cw + j))
    return pl.pallas_call(
        body, name=name, grid=(nb,),
        in_specs=[col(dm.o_u), col(dm.o_b), col(dm.o_c), pl.BlockSpec((3, cw), lambda j: (0, j))],
        out_specs=pl.BlockSpec((T, cw), lambda j: (0, j)),
        out_shape=jax.ShapeDtypeStruct((T, dm.DC), BF16),
        scratch_shapes=[pltpu.VMEM((T + 2 * HALO, cw), F32)],
        compiler_params=_params(("parallel",)),
    )(proj, proj, proj, conv_w)


def mixer_a_bwd(proj, conv_w, dya, dm, name):
    T, cw = dm.T, 128
    R = dm.tr
    nb = dm.DC // cw

    def body(u_ref, b_ref, c_ref, w_ref, dya_ref, du_ref, db_ref, dc_ref, dw_ref, sbuf, gbuf):
        _fill_halo_buf(sbuf, lambda r0: c_ref[pl.ds(r0, R), :] * u_ref[pl.ds(r0, R), :], T, R, cw)
        _fill_halo_buf(gbuf, lambda r0: dya_ref[pl.ds(r0, R), :] * b_ref[pl.ds(r0, R), :], T, R, cw)
        w0, w1, w2 = w_ref[0:1, :], w_ref[1:2, :], w_ref[2:3, :]

        def chunk(r, acc):
            a0, a1, a2 = acc
            r0 = pl.multiple_of(r * R, 8)
            swin = sbuf[pl.ds(r0, R + HALO), :]
            s0, s1, s2 = swin[HALO:, :], _back(swin, 1)[HALO:, :], _back(swin, 2)[HALO:, :]
            gwin = gbuf[pl.ds(r0 + HALO, R + HALO), :]
            g0, g1, g2 = gwin[:R, :], _fwd(gwin, 1)[:R, :], _fwd(gwin, 2)[:R, :]
            cv = w2 * s0 + w1 * s1 + w0 * s2
            ds = w2 * g0 + w1 * g1 + w0 * g2
            db_ref[pl.ds(r0, R), :] = (dya_ref[pl.ds(r0, R), :] * cv).astype(BF16)
            du_ref[pl.ds(r0, R), :] = (ds * c_ref[pl.ds(r0, R), :]).astype(BF16)
            dc_ref[pl.ds(r0, R), :] = (ds * u_ref[pl.ds(r0, R), :]).astype(BF16)
            a2 = a2 + jnp.sum(g0 * s0, axis=0, keepdims=True)
            a1 = a1 + jnp.sum(g0 * s1, axis=0, keepdims=True)
            a0 = a0 + jnp.sum(g0 * s2, axis=0, keepdims=True)
            return a0, a1, a2

        z = jnp.zeros((1, cw), F32)
        a0, a1, a2 = lax.fori_loop(0, T // R, chunk, (z, z, z))
        dw_ref[0:1, :] = a0
        dw_ref[1:2, :] = a1
        dw_ref[2:3, :] = a2

    col = lambda off: pl.BlockSpec((T, cw), lambda j: (0, off // cw + j))
    own = pl.BlockSpec((T, cw), lambda j: (0, j))
    o = jax.ShapeDtypeStruct((T, dm.DC), BF16)
    return pl.pallas_call(
        body, name=name, grid=(nb,),
        in_specs=[col(dm.o_u), col(dm.o_b), col(dm.o_c), pl.BlockSpec((3, cw), lambda j: (0, j)), own],
        out_specs=[own, own, own, pl.BlockSpec((3, cw), lambda j: (0, j))],
        out_shape=[o, o, o, jax.ShapeDtypeStruct((3, dm.DC), F32)],
        scratch_shapes=[pltpu.VMEM((T + 2 * HALO, cw), F32), pltpu.VMEM((T + 2 * HALO, cw), F32)],
        compiler_params=_params(("parallel",)),
    )(proj, proj, proj, conv_w, dya)


def _rope(x, C, S):
    return x * C + (pltpu.roll(x, 32, 1) - pltpu.roll(x, 96, 1)) * S


def _rope_t(dy, C, S):
    return dy * C + (pltpu.roll(dy, 96, 1) - pltpu.roll(dy, 32, 1)) * S


def qk_prep_fwd(q0, kv0, proj, qn, kn, C, S, dm, name):
    T, H, tr = dm.T, dm.H, dm.tr
    inv = 1.0 / dm.QKH

    def body(q0_ref, kv_ref, kr_ref, qn_ref, kn_ref, c_ref, s_ref, q_ref, k_ref, v_ref):
        Cv, Sv = c_ref[...], s_ref[...]
        qa, qb = q0_ref[:, :128], q0_ref[:, 128:]
        r = lax.rsqrt((jnp.sum(qa * qa, -1, keepdims=True) + jnp.sum(qb * qb, -1, keepdims=True)) * inv + EPS)
        q_ref[:, :128] = (qa * r * qn_ref[:, :128]).astype(BF16)
        q_ref[:, 128:] = _rope(qb * r * qn_ref[:, 128:], Cv, Sv).astype(BF16)
        ka, kb = kv_ref[:, :128], kr_ref[...]
        r = lax.rsqrt((jnp.sum(ka * ka, -1, keepdims=True) + jnp.sum(kb * kb, -1, keepdims=True)) * inv + EPS)
        k_ref[:, :128] = (ka * r * kn_ref[:, :128]).astype(BF16)
        k_ref[:, 128:] = _rope(kb * r * kn_ref[:, 128:], Cv, Sv).astype(BF16)
        v_ref[...] = kv_ref[:, 128:].astype(BF16)

    head = pl.BlockSpec((tr, 256), lambda i, h: (i, h))
    gain = pl.BlockSpec((1, 256), lambda i, h: (0, 0))
    tab = pl.BlockSpec((tr, 128), lambda i, h: (i, 0))
    return pl.pallas_call(
        body, name=name, grid=(T // tr, H),
        in_specs=[head, head, pl.BlockSpec((tr, 128), lambda i, h: (i, dm.o_rope // 128)), gain, gain, tab, tab],
        out_specs=[head, head, pl.BlockSpec((tr, 128), lambda i, h: (i, h))],
        out_shape=[jax.ShapeDtypeStruct((T, H * 256), BF16), jax.ShapeDtypeStruct((T, H * 256), BF16),
                   jax.ShapeDtypeStruct((T, H * 128), BF16)],
        compiler_params=_params(("parallel", "parallel")),
    )(q0, kv0, proj, qn, kn, C, S)


def qk_prep_bwd(q0, kv0, proj, qn, kn, C, S, dq, dk, dv, dm, name):
    T, H, tr = dm.T, dm.H, dm.tr
    inv = 1.0 / dm.QKH

    def body(q0_ref, kv_ref, kr_ref, qn_ref, kn_ref, c_ref, s_ref, dq_ref, dk_ref, dv_ref,
             dq0_ref, dkv_ref, dkr_ref, dqn_ref, dkn_ref):
        i, h = pl.program_id(0), pl.program_id(1)
        Cv, Sv = c_ref[...], s_ref[...]

        def norm_bwd(xa, xb, ga, gb, dya, dyb):
            r = lax.rsqrt((jnp.sum(xa * xa, -1, keepdims=True) + jnp.sum(xb * xb, -1, keepdims=True)) * inv + EPS)
            dzb = _rope_t(dyb, Cv, Sv)
            gda, gdb = ga * dya, gb * dzb
            dot = (jnp.sum(xa * gda, -1, keepdims=True) + jnp.sum(xb * gdb, -1, keepdims=True)) * inv
            r3 = r * r * r
            dxa = r * gda - xa * r3 * dot
            dxb = r * gdb - xb * r3 * dot
            dga = jnp.sum(dya * xa * r, axis=0, keepdims=True)
            dgb = jnp.sum(dzb * xb * r, axis=0, keepdims=True)
            return dxa, dxb, dga, dgb

        @pl.when((i == 0) & (h == 0))
        def _():
            dqn_ref[...] = jnp.zeros_like(dqn_ref)
            dkn_ref[...] = jnp.zeros_like(dkn_ref)

        dxa, dxb, dga, dgb = norm_bwd(q0_ref[:, :128], q0_ref[:, 128:], qn_ref[:, :128], qn_ref[:, 128:],
                                      dq_ref[:, :128], dq_ref[:, 128:])
        dq0_ref[:, :128] = dxa.astype(BF16)
        dq0_ref[:, 128:] = dxb.astype(BF16)
        dqn_ref[:, :128] += dga
        dqn_ref[:, 128:] += dgb
        dxa, dxb, dga, dgb = norm_bwd(kv_ref[:, :128], kr_ref[...], kn_ref[:, :128], kn_ref[:, 128:],
                                      dk_ref[:, :128], dk_ref[:, 128:])
        dkv_ref[:, :128] = dxa.astype(BF16)
        dkv_ref[:, 128:] = dv_ref[...].astype(BF16)
        dkn_ref[:, :128] += dga
        dkn_ref[:, 128:] += dgb

        @pl.when(h == 0)
        def _():
            dkr_ref[...] = jnp.zeros_like(dkr_ref)

        dkr_ref[...] += dxb

    head = pl.BlockSpec((tr, 256), lambda i, h: (i, h))
    gain = pl.BlockSpec((1, 256), lambda i, h: (0, 0))
    tab = pl.BlockSpec((tr, 128), lambda i, h: (i, 0))
    return pl.pallas_call(
        body, name=name, grid=(T // tr, H),
        in_specs=[head, head, pl.BlockSpec((tr, 128), lambda i, h: (i, dm.o_rope // 128)), gain, gain, tab, tab,
                  head, head, pl.BlockSpec((tr, 128), lambda i, h: (i, h))],
        out_specs=[head, head, tab, gain, gain],
        out_shape=[jax.ShapeDtypeStruct((T, H * 256), BF16), jax.ShapeDtypeStruct((T, H * 256), BF16),
                   jax.ShapeDtypeStruct((T, 128), F32), jax.ShapeDtypeStruct((1, 256), F32),
                   jax.ShapeDtypeStruct((1, 256), F32)],
        compiler_params=_params(("arbitrary", "arbitrary")),
    )(q0, kv0, proj, qn, kn, C, S, dq, dk, dv)


_NT = (((1,), (1,)), ((), ()))


def _causal_mask(t):
    return lax.broadcasted_iota(jnp.int32, (t, t), 0) >= lax.broadcasted_iota(jnp.int32, (t, t), 1)


def _causal_mask_t(t):
    return lax.broadcasted_iota(jnp.int32, (t, t), 0) <= lax.broadcasted_iota(jnp.int32, (t, t), 1)


def attn_fwd(q, k, v, dm, name, comm=None):
    T, H, tq = dm.T, dm.H, dm.tq
    scale = dm.QKH ** -0.5

    def body(q_ref, k_ref, v_ref, o_ref, ob_ref, lse_ref):
        qi = pl.program_id(1)
        qv = q_ref[...]

        def step(j, carry, masked):
            m, l, acc = carry
            j0 = pl.multiple_of(j * tq, tq)
            s = lax.dot_general(qv, k_ref[pl.ds(j0, tq), :], _NT, preferred_element_type=F32) * scale
            if masked:
                s = jnp.where(_causal_mask(tq), s, -jnp.inf)
            m_new = jnp.maximum(m, jnp.max(s, -1, keepdims=True))
            alpha = jnp.exp(m - m_new)
            p = jnp.exp(s - m_new)
            l = alpha * l + jnp.sum(p, -1, keepdims=True)
            acc = alpha * acc + jnp.dot(p.astype(BF16), v_ref[pl.ds(j0, tq), :], preferred_element_type=F32)
            return m_new, l, acc

        carry = (jnp.full((tq, 1), -jnp.inf, F32), jnp.zeros((tq, 1), F32), jnp.zeros((tq, 128), F32))
        carry = lax.fori_loop(0, qi, lambda j, c: step(j, c, False), carry)
        m, l, acc = step(qi, carry, True)
        o = acc / l
        o_ref[...] = o
        ob_ref[...] = o.astype(BF16)
        lse_ref[...] = m + jnp.log(l)

    outs, couts = _hosted_call(
        body, name=name, grid=(H, T // tq),
        in_specs=[pl.BlockSpec((tq, 256), lambda h, i: (i, h)), pl.BlockSpec((T, 256), lambda h, i: (0, h)),
                  pl.BlockSpec((T, 128), lambda h, i: (0, h))],
        out_specs=[pl.BlockSpec((tq, 128), lambda h, i: (i, h)), pl.BlockSpec((tq, 128), lambda h, i: (i, h)),
                   pl.BlockSpec((None, tq, 1), lambda h, i: (h, i, 0))],
        out_shape=[jax.ShapeDtypeStruct((T, H * 128), F32), jax.ShapeDtypeStruct((T, H * 128), BF16),
                   jax.ShapeDtypeStruct((H, T, 1), F32)],
        scratch_shapes=[], semantics=("parallel", "parallel"), args=(q, k, v), comm=comm)
    return outs if comm is None else (outs, couts)


def attn_delta(do, o, dm, name):
    T, H, tr = dm.T, dm.H, dm.tr

    def body(do_ref, o_ref, delta_ref, dob_ref):
        d = do_ref[...]
        delta_ref[...] = jnp.sum(d * o_ref[...], -1, keepdims=True)
        dob_ref[...] = d.astype(BF16)

    blk = pl.BlockSpec((tr, 128), lambda i, h: (i, h))
    return pl.pallas_call(
        body, name=name, grid=(T // tr, H),
        in_specs=[blk, blk],
        out_specs=[pl.BlockSpec((None, tr, 1), lambda i, h: (h, i, 0)), blk],
        out_shape=[jax.ShapeDtypeStruct((H, T, 1), F32), jax.ShapeDtypeStruct((T, H * 128), BF16)],
        compiler_params=_params(("parallel", "parallel")),
    )(do, o)


def attn_bwd_dq(q, k, v, do, lse, delta, dm, name, comm=None):
    T, H, tq = dm.T, dm.H, dm.tq
    scale = dm.QKH ** -0.5

    def body(q_ref, k_ref, v_ref, do_ref, lse_ref, delta_ref, dq_ref):
        qi = pl.program_id(1)
        qv, dov, lsev, dlt = q_ref[...], do_ref[...], lse_ref[...], delta_ref[...]

        def step(j, dq, masked):
            j0 = pl.multiple_of(j * tq, tq)
            kt = k_ref[pl.ds(j0, tq), :]
            s = lax.dot_general(qv, kt, _NT, preferred_element_type=F32) * scale
            p = jnp.exp(s - lsev)
            if masked:
                p = jnp.where(_causal_mask(tq), p, 0.0)
            dp = lax.dot_general(dov, v_ref[pl.ds(j0, tq), :], _NT, preferred_element_type=F32)
            ds = p * (dp - dlt) * scale
            return dq + jnp.dot(ds.astype(BF16), kt, preferred_element_type=F32)

        dq = lax.fori_loop(0, qi, lambda j, c: step(j, c, False), jnp.zeros((tq, 256), F32))
        dq_ref[...] = step(qi, dq, True)

    stat = pl.BlockSpec((None, tq, 1), lambda h, i: (h, i, 0))
    outs, couts = _hosted_call(
        body, name=name, grid=(H, T // tq),
        in_specs=[pl.BlockSpec((tq, 256), lambda h, i: (i, h)), pl.BlockSpec((T, 256), lambda h, i: (0, h)),
                  pl.BlockSpec((T, 128), lambda h, i: (0, h)), pl.BlockSpec((tq, 128), lambda h, i: (i, h)), stat, stat],
        out_specs=[pl.BlockSpec((tq, 256), lambda h, i: (i, h))],
        out_shape=[jax.ShapeDtypeStruct((T, H * 256), F32)],
        scratch_shapes=[], semantics=("parallel", "parallel"), args=(q, k, v, do, lse, delta), comm=comm)
    return outs[0] if comm is None else (outs[0], couts)


def attn_bwd_dkv(q, k, v, do, lse_rows, delta_rows, dm, name, comm=None):
    T, H, tq = dm.T, dm.H, dm.tq
    nq = T // tq
    scale = dm.QKH ** -0.5

    def body(q_ref, k_ref, v_ref, do_ref, lse_ref, delta_ref, dk_ref, dv_ref):
        kj = pl.program_id(1)
        kt, vt = k_ref[...], v_ref[...]

        def step(i, carry, masked):
            dk, dv = carry
            i0 = pl.multiple_of(i * tq, tq)
            qt, dot = q_ref[pl.ds(i0, tq), :], do_ref[pl.ds(i0, tq), :]
            st = lax.dot_general(kt, qt, _NT, preferred_element_type=F32) * scale
            pt = jnp.exp(st - lse_ref[i])
            if masked:
                pt = jnp.where(_causal_mask_t(tq), pt, 0.0)
            dv = dv + jnp.dot(pt.astype(BF16), dot, preferred_element_type=F32)
            dpt = lax.dot_general(vt, dot, _NT, preferred_element_type=F32)
            dst = pt * (dpt - delta_ref[i]) * scale
            dk = dk + jnp.dot(dst.astype(BF16), qt, preferred_element_type=F32)
            return dk, dv

        carry = step(kj, (jnp.zeros((tq, 256), F32), jnp.zeros((tq, 128), F32)), True)
        dk, dv = lax.fori_loop(kj + 1, nq, lambda i, c: step(i, c, False), carry)
        dk_ref[...] = dk
        dv_ref[...] = dv

    rows = pl.BlockSpec((None, nq, 1, tq), lambda h, j: (h, 0, 0, 0))
    outs, couts = _hosted_call(
        body, name=name, grid=(H, nq),
        in_specs=[pl.BlockSpec((T, 256), lambda h, j: (0, h)), pl.BlockSpec((tq, 256), lambda h, j: (j, h)),
                  pl.BlockSpec((tq, 128), lambda h, j: (j, h)), pl.BlockSpec((T, 128), lambda h, j: (0, h)), rows, rows],
        out_specs=[pl.BlockSpec((tq, 256), lambda h, j: (j, h)), pl.BlockSpec((tq, 128), lambda h, j: (j, h))],
        out_shape=[jax.ShapeDtypeStruct((T, H * 256), F32), jax.ShapeDtypeStruct((T, H * 128), F32)],
        scratch_shapes=[], semantics=("parallel", "parallel"), args=(q, k, v, do, lse_rows, delta_rows), comm=comm)
    return outs if comm is None else (outs, couts)


def _window_sum(win, g, shift):
    s1 = win + shift(win, 1)
    s2 = s1 + shift(s1, 2)
    s3 = s2 + shift(s2, 4)
    s4 = s3 + shift(s3, 8)
    return jnp.where(g == 0, s1, jnp.where(g == 1, s2, jnp.where(g == 2, s3, s4)))


def _count(r0, R, g, T_unused=None):
    t = r0 + lax.broadcasted_iota(jnp.int32, (R, 1), 0)
    return jnp.minimum(t + 1, jnp.left_shift(2, g)).astype(F32)


def pool_fwd(proj, pw, ps, dm, name):
    T, PG, R = dm.T, dm.PG, dm.tr

    def body(x_ref, pw_ref, ps_ref, pooled_ref, mixed_ref, yc_ref, buf):
        g = pl.program_id(0)
        _fill_halo_buf(buf, lambda r0: x_ref[pl.ds(r0, R), :], T, R, PG)

        def chunk(r, c):
            r0 = pl.multiple_of(r * R, 8)
            win = buf[pl.ds(r0, R + HALO), :]
            ws = _window_sum(win, g, _back)[HALO:, :]
            pooled = (ws / _count(r0, R, g) - win[HALO:, :]).astype(BF16)
            pooled_ref[pl.ds(r0, R), :] = pooled
            mixed = jnp.dot(pooled, pw_ref[...], preferred_element_type=F32)
            mixed_ref[pl.ds(r0, R), :] = mixed
            yc_ref[pl.ds(r0, R), :] = (mixed * ps_ref[...]).astype(BF16)
            return c

        lax.fori_loop(0, T // R, chunk, 0)

    own = pl.BlockSpec((T, PG), lambda g: (0, g))
    return pl.pallas_call(
        body, name=name, grid=(4,),
        in_specs=[pl.BlockSpec((T, PG), lambda g: (0, dm.o_pool // PG + g)), pl.BlockSpec((None, PG, PG), lambda g: (g, 0, 0)),
                  pl.BlockSpec((1, PG), lambda g: (0, g))],
        out_specs=[own, own, own],
        out_shape=[jax.ShapeDtypeStruct((T, dm.DP), BF16), jax.ShapeDtypeStruct((T, dm.DP), F32),
                   jax.ShapeDtypeStruct((T, dm.DP), BF16)],
        scratch_shapes=[pltpu.VMEM((T + 2 * HALO, PG), F32)],
        compiler_params=_params(("parallel",)),
    )(proj, pw, ps)


def pool_bwd(dyc, mixed, pooled, pw, ps, dm, name):
    T, PG, R = dm.T, dm.PG, dm.tr
    _TN = (((0,), (0,)), ((), ()))

    def body(dyc_ref, mixed_ref, pooled_ref, pw_ref, ps_ref, dx_ref, dpw_ref, dps_ref, qbuf, dpbuf):
        g = pl.program_id(0)
        zeros = jnp.zeros((HALO, PG), F32)
        qbuf[pl.ds(0, HALO), :] = zeros
        qbuf[pl.ds(HALO + T, HALO), :] = zeros
        dpw_ref[...] = jnp.zeros_like(dpw_ref)

        def first(r, dps):
            r0 = pl.multiple_of(r * R, 8)
            dyc = dyc_ref[pl.ds(r0, R), :]
            dps = dps + jnp.sum(dyc * mixed_ref[pl.ds(r0, R), :], axis=0, keepdims=True)
            dmb = (dyc * ps_ref[...]).astype(BF16)
            dpw_ref[...] += lax.dot_general(pooled_ref[pl.ds(r0, R), :], dmb, _TN, preferred_element_type=F32)
            dp = lax.dot_general(dmb, pw_ref[...], _NT, preferred_element_type=F32)
            dpbuf[pl.ds(r0, R), :] = dp
            qbuf[pl.ds(r0 + HALO, R), :] = dp / _count(r0, R, g)
            return dps

        dps_ref[...] = lax.fori_loop(0, T // R, first, jnp.zeros((1, PG), F32))

        def second(r, c):
            r0 = pl.multiple_of(r * R, 8)
            win = qbuf[pl.ds(r0 + HALO, R + HALO), :]
            ws = _window_sum(win, g, _fwd)[:R, :]
            dx_ref[pl.ds(r0, R), :] = (ws - dpbuf[pl.ds(r0, R), :]).astype(BF16)
            return c

        lax.fori_loop(0, T // R, second, 0)

    own = pl.BlockSpec((T, PG), lambda g: (0, g))
    return pl.pallas_call(
        body, name=name, grid=(4,),
        in_specs=[own, own, own, pl.BlockSpec((None, PG, PG), lambda g: (g, 0, 0)), pl.BlockSpec((1, PG), lambda g: (0, g))],
        out_specs=[own, pl.BlockSpec((None, PG, PG), lambda g: (g, 0, 0)), pl.BlockSpec((1, PG), lambda g: (0, g))],
        out_shape=[jax.ShapeDtypeStruct((T, dm.DP), BF16), jax.ShapeDtypeStruct((4, PG, PG), F32),
                   jax.ShapeDtypeStruct((1, dm.DP), F32)],
        scratch_shapes=[pltpu.VMEM((T + 2 * HALO, PG), F32), pltpu.VMEM((T, PG), F32)],
        compiler_params=_params(("parallel",)),
    )(dyc, mixed, pooled, pw, ps)


def _sigmoid(x):
    return 1.0 / (1.0 + jnp.exp(-x))


def merge_fwd(proj, A, B, C, dm, name):
    T, D, tr, tc = dm.T, dm.D, dm.tr, 512
    nc = D // tc

    def body(g0, g1, g2, a, b, c, out):
        out[...] = (_sigmoid(g0[...]) * a[...] + _sigmoid(g1[...]) * b[...] + _sigmoid(g2[...]) * c[...]).astype(BF16)

    gate = lambda k: pl.BlockSpec((tr, tc), lambda i, j: (i, k * nc + j))
    own = pl.BlockSpec((tr, tc), lambda i, j: (i, j))
    return pl.pallas_call(
        body, name=name, grid=(T // tr, nc),
        in_specs=[gate(0), gate(1), gate(2), own, own, own],
        out_specs=own,
        out_shape=jax.ShapeDtypeStruct((T, D), BF16),
        compiler_params=_params(("parallel", "parallel")),
    )(proj, proj, proj, A, B, C)


def merge_bwd(proj, A, B, C, dmerged, dm, name):
    T, D, tr, tc = dm.T, dm.D, dm.tr, 512
    nc = D // tc

    def body(g0, g1, g2, a, b, c, dmr, da, db, dc, dl0, dl1, dl2):
        d = dmr[...]
        for g_ref, y_ref, dy_ref, dl_ref in ((g0, a, da, dl0), (g1, b, db, dl1), (g2, c, dc, dl2)):
            s = _sigmoid(g_ref[...])
            dy_ref[...] = (d * s).astype(BF16)
            dl_ref[...] = (d * y_ref[...] * s * (1.0 - s)).astype(BF16)

    gate = lambda k: pl.BlockSpec((tr, tc), lambda i, j: (i, k * nc + j))
    own = pl.BlockSpec((tr, tc), lambda i, j: (i, j))
    o = jax.ShapeDtypeStruct((T, D), BF16)
    return pl.pallas_call(
        body, name=name, grid=(T // tr, nc),
        in_specs=[gate(0), gate(1), gate(2), own, own, own, own],
        out_specs=[own] * 6,
        out_shape=[o] * 6,
        compiler_params=_params(("parallel", "parallel")),
    )(proj, proj, proj, A, B, C, dmerged)


def loss_head(y, target, dm, name):
    T, D, tr = dm.T, dm.D, dm.tr

    def body(y_ref, t_ref, dy_ref, dyb_ref, loss_ref):
        i = pl.program_id(0)
        t = _row_ids(i, tr)
        real = (t >= dm.n_meta) & (t < dm.T_real)
        err = jnp.where(real, y_ref[...] - t_ref[...], 0.0)
        dy = err * (1.0 / D)
        dy_ref[...] = dy
        dyb_ref[...] = dy.astype(BF16)

        @pl.when(i == 0)
        def _():
            loss_ref[...] = jnp.zeros_like(loss_ref)

        loss_ref[...] += 0.5 * jnp.sum(jnp.sum(err * err, axis=-1, keepdims=True) * (1.0 / D))

    row = pl.BlockSpec((tr, D), lambda i: (i, 0))
    return pl.pallas_call(
        body, name=name, grid=(T // tr,),
        in_specs=[row, row],
        out_specs=[row, row, pl.BlockSpec((8, LANES), lambda i: (0, 0))],
        out_shape=[jax.ShapeDtypeStruct((T, D), F32), jax.ShapeDtypeStruct((T, D), BF16),
                   jax.ShapeDtypeStruct((8, LANES), F32)],
        compiler_params=_params(("arbitrary",)),
    )(y, target)


def adamw(w, m, v, parts, name):
    R, C = w.shape
    P = parts.shape[0]
    br = R
    for cand in (512, 256, 128, 64, 32, 16, 8):
        if R % cand == 0 and cand * C * 4 <= (1 << 20):
            br = cand
            break
    if R * C * 4 <= (1 << 20):
        br = R

    def body(w_ref, m_ref, v_ref, p_ref, g_ref, d_ref, nm_ref, nv_ref):
        g = p_ref[0].astype(F32)
        for k in range(1, P):
            g = g + p_ref[k].astype(F32)
        mm = ADAM_B1 * m_ref[...] + (1.0 - ADAM_B1) * g
        vv = ADAM_B2 * v_ref[...] + (1.0 - ADAM_B2) * (g * g)
        m_hat = mm / (1.0 - ADAM_B1 ** ADAM_STEP)
        v_hat = vv / (1.0 - ADAM_B2 ** ADAM_STEP)
        g_ref[...] = g
        d_ref[...] = -ADAM_LR * (m_hat / (jnp.sqrt(v_hat) + ADAM_EPS) + ADAM_WD * w_ref[...])
        nm_ref[...] = mm
        nv_ref[...] = vv

    blk = pl.BlockSpec((br, C), lambda i: (i, 0))
    o = jax.ShapeDtypeStruct((R, C), F32)
    return pl.pallas_call(
        body, name=name, grid=(R // br,),
        in_specs=[blk, blk, blk, pl.BlockSpec((P, br, C), lambda i: (0, i, 0))],
        out_specs=[blk] * 4,
        out_shape=[o] * 4,
        compiler_params=_params(("parallel",)),
    )(w, m, v, parts)


def sum_parts(parts, name):
    P, R, C = parts.shape

    def body(p_ref, o_ref):
        acc = p_ref[0]
        for k in range(1, P):
            acc = acc + p_ref[k]
        o_ref[...] = acc

    return pl.pallas_call(
        body, name=name, grid=(1,),
        in_specs=[pl.BlockSpec((P, R, C), lambda i: (0, 0, 0))],
        out_specs=pl.BlockSpec((R, C), lambda i: (0, 0)),
        out_shape=jax.ShapeDtypeStruct((R, C), F32),
        compiler_params=_params(("arbitrary",)),
    )(parts)


def add_sibling(parts, got, core, name):
    _, _, R, C = parts.shape
    br = _pick(R, (1024, 512, 256, 128, 64, 32, 16))

    def body(c_ref, a_ref, b_ref, o_ref):
        o_ref[...] = (a_ref[...].astype(F32) + b_ref[...].astype(F32)).astype(BF16)

    blk = pl.BlockSpec((None, br, C), lambda ch, i, c: (ch, i, 0))
    return pl.pallas_call(
        body, name=name,
        grid_spec=pltpu.PrefetchScalarGridSpec(
            num_scalar_prefetch=1, grid=(4, R // br),
            in_specs=[pl.BlockSpec((None, None, br, C), lambda ch, i, c: (ch, c[0], i, 0)), blk],
            out_specs=blk),
        out_shape=jax.ShapeDtypeStruct((4, R, C), BF16), compiler_params=_params(("parallel", "parallel")),
    )(core, parts, got)


def all_gather(arrs, name):
    n = len(arrs)

    def body(*refs):
        ins, outs = refs[:n], refs[n:2 * n]
        send_sems, recv_sems, local_sems = refs[2 * n:]
        x, y, c = _place()
        me, sibling = (x, y, c), (x, y, 1 - c)
        chips = [(1 - x, y), (x, 1 - y), (1 - x, 1 - y)]

        def copy(a, k, block, to, src=None):
            px, py, pc = block
            dst = outs[a].at[4 * px + 2 * py + pc]
            return pltpu.make_async_remote_copy(
                src_ref=dst if src is None else src, dst_ref=dst,
                send_sem=send_sems.at[7 * a + k], recv_sem=recv_sems.at[7 * a + k],
                device_id=to, device_id_type=_MESH)

        started = []
        for a in range(n):
            mine = pltpu.make_async_copy(ins[a], outs[a].at[4 * x + 2 * y + c], local_sems.at[a])
            mine.start()
            started.append(mine)
        sends = []
        for a in range(n):
            sends.append(copy(a, 0, me, sibling, src=ins[a]))
            for j, chip in enumerate(chips):
                sends.append(copy(a, 1 + j, me, (*chip, c), src=ins[a]))
        for cp in sends:
            cp.start()
        for j, chip in enumerate(chips):
            for a in range(n):
                copy(a, 1 + j, (*chip, c), me).wait_recv()
                fwd = copy(a, 4 + j, (*chip, c), sibling)
                fwd.start()
                sends.append(fwd)
        for a in range(n):
            copy(a, 0, sibling, me).wait_recv()
            for j, chip in enumerate(chips):
                copy(a, 4 + j, (*chip, 1 - c), me).wait_recv()
        for cp in sends:
            cp.wait_send()
        for cp in started:
            cp.wait()

    outs = pl.pallas_call(
        body, name=name,
        in_specs=[_HBM] * n, out_specs=[_HBM] * n,
        out_shape=[jax.ShapeDtypeStruct((8,) + a.shape, a.dtype) for a in arrs],
        scratch_shapes=[pltpu.SemaphoreType.DMA((7 * n,)), pltpu.SemaphoreType.DMA((7 * n,)), pltpu.SemaphoreType.DMA((n,))],
    )(*arrs)
    return list(outs)


def sibling_exchange(arrs, name):
    n = len(arrs)

    def body(*refs):
        ins, got = refs[:n], refs[n:2 * n]
        send_sems, recv_sems = refs[2 * n:]
        x, y, c = _place()
        work = []
        for a in range(n):
            for ch in range(4):
                cp = pltpu.make_async_remote_copy(
                    src_ref=ins[a].at[ch, 1 - c], dst_ref=got[a].at[ch],
                    send_sem=send_sems.at[4 * a + ch], recv_sem=recv_sems.at[4 * a + ch],
                    device_id=(x, y, 1 - c), device_id_type=_MESH)
                cp.start()
                work.append(cp)
        for cp in work:
            cp.wait()

    outs = pl.pallas_call(
        body, name=name,
        in_specs=[_HBM] * n, out_specs=[_HBM] * n,
        out_shape=[jax.ShapeDtypeStruct((4,) + a.shape[2:], a.dtype) for a in arrs],
        scratch_shapes=[pltpu.SemaphoreType.DMA((4 * n,)), pltpu.SemaphoreType.DMA((4 * n,))],
    )(*arrs)
    return list(outs)


def chip_exchange(arrs, name):
    n = len(arrs)

    def body(*refs):
        ins, outs = refs[:n], refs[n:2 * n]
        send_sems, recv_sems, local_sems = refs[2 * n:]
        x, y, c = _place()
        my_chip = 2 * x + y
        chips = [(1 - x, y), (x, 1 - y), (1 - x, 1 - y)]
        work = []
        for a in range(n):
            loc = pltpu.make_async_copy(ins[a].at[my_chip], outs[a].at[my_chip], local_sems.at[a])
            loc.start()
            work.append(loc)
        sends = []
        for j, (px, py) in enumerate(chips):
            for a in range(n):
                cp = pltpu.make_async_remote_copy(
                    src_ref=ins[a].at[2 * px + py], dst_ref=outs[a].at[my_chip],
                    send_sem=send_sems.at[3 * a + j], recv_sem=recv_sems.at[3 * a + j],
                    device_id=(px, py, c), device_id_type=_MESH)
                cp.start()
                sends.append(cp)
        for j, (px, py) in enumerate(chips):
            for a in range(n):
                pltpu.make_async_remote_copy(
                    src_ref=ins[a].at[my_chip], dst_ref=outs[a].at[2 * px + py],
                    send_sem=send_sems.at[3 * a + j], recv_sem=recv_sems.at[3 * a + j],
                    device_id=(px, py, c), device_id_type=_MESH).wait_recv()
        for cp in sends:
            cp.wait_send()
        for cp in work:
            cp.wait()

    outs = pl.pallas_call(
        body, name=name,
        in_specs=[_HBM] * n, out_specs=[_HBM] * n,
        out_shape=[jax.ShapeDtypeStruct(a.shape, a.dtype) for a in arrs],
        scratch_shapes=[pltpu.SemaphoreType.DMA((3 * n,)), pltpu.SemaphoreType.DMA((3 * n,)), pltpu.SemaphoreType.DMA((n,))],
    )(*arrs)
    return list(outs)


def _remote(src, dst, send_sem, recv_sem, to):
    return pltpu.make_async_remote_copy(src_ref=src, dst_ref=dst, send_sem=send_sem, recv_sem=recv_sem,
                                        device_id=to, device_id_type=_MESH)


def gather_ici(blocks):
    n = len(blocks)

    def copies(cins, couts, sems):
        send_sems, recv_sems, local_sems = sems
        x, y, c = _place()
        mine = 4 * x + 2 * y + c
        local, sends, recvs = [], [], []
        for a in range(n):
            local.append(pltpu.make_async_copy(cins[a], couts[a].at[mine], local_sems.at[a]))
            for j, (px, py) in enumerate([(1 - x, y), (x, 1 - y), (1 - x, 1 - y)]):
                k = 3 * a + j
                sends.append(_remote(cins[a], couts[a].at[mine], send_sems.at[k], recv_sems.at[k], (px, py, c)))
                recvs.append(_remote(cins[a], couts[a].at[4 * px + 2 * py + c], send_sems.at[k], recv_sems.at[k], (px, py, c)))
        return local, sends, recvs

    def start(cins, couts, sems):
        local, sends, _ = copies(cins, couts, sems)
        for cp in local + sends:
            cp.start()

    def finish(cins, couts, sems):
        local, sends, recvs = copies(cins, couts, sems)
        for cp in sends:
            cp.wait_send()
        for cp in recvs:
            cp.wait_recv()
        for cp in local:
            cp.wait()

    return Hosted(list(blocks), [jax.ShapeDtypeStruct((8,) + b.shape, b.dtype) for b in blocks],
                  [pltpu.SemaphoreType.DMA((3 * n,)), pltpu.SemaphoreType.DMA((3 * n,)), pltpu.SemaphoreType.DMA((n,))],
                  start, finish)


def fill_sibling(stks, name):
    n = len(stks)

    def body(*refs):
        ins, outs = refs[:n], refs[n:2 * n]
        send_sems, recv_sems = refs[2 * n:]
        x, y, c = _place()
        sends, recvs = [], []
        for a in range(n):
            for ch in range(4):
                k = 4 * a + ch
                sends.append(_remote(ins[a].at[2 * ch + c], outs[a].at[2 * ch + c], send_sems.at[k], recv_sems.at[k], (x, y, 1 - c)))
                recvs.append(_remote(ins[a].at[2 * ch + c], outs[a].at[2 * ch + 1 - c], send_sems.at[k], recv_sems.at[k], (x, y, 1 - c)))
        for cp in sends:
            cp.start()
        for cp in sends:
            cp.wait_send()
        for cp in recvs:
            cp.wait_recv()

    outs = pl.pallas_call(
        body, name=name,
        in_specs=[_HBM] * n, out_specs=[_HBM] * n,
        out_shape=[jax.ShapeDtypeStruct(s.shape, s.dtype) for s in stks],
        scratch_shapes=[pltpu.SemaphoreType.DMA((4 * n,)), pltpu.SemaphoreType.DMA((4 * n,))],
        input_output_aliases={a: a for a in range(n)},
    )(*stks)
    return list(outs)


def reduce_ici(arrs):
    n = len(arrs)

    def copies(cins, couts, sems):
        send_sems, recv_sems, local_sems = sems
        x, y, c = _place()
        my_chip = 2 * x + y
        local, sends, recvs = [], [], []
        for a in range(n):
            local.append(pltpu.make_async_copy(cins[a].at[my_chip], couts[a].at[my_chip], local_sems.at[a]))
            for j, (px, py) in enumerate([(1 - x, y), (x, 1 - y), (1 - x, 1 - y)]):
                k = 3 * a + j
                sends.append(_remote(cins[a].at[2 * px + py], couts[a].at[my_chip], send_sems.at[k], recv_sems.at[k], (px, py, c)))
                recvs.append(_remote(cins[a].at[my_chip], couts[a].at[2 * px + py], send_sems.at[k], recv_sems.at[k], (px, py, c)))
        return local, sends, recvs

    def start(cins, couts, sems):
        local, sends, _ = copies(cins, couts, sems)
        for cp in local + sends:
            cp.start()

    def finish(cins, couts, sems):
        local, sends, recvs = copies(cins, couts, sems)
        for cp in sends:
            cp.wait_send()
        for cp in recvs:
            cp.wait_recv()
        for cp in local:
            cp.wait()

    return Hosted(list(arrs), [jax.ShapeDtypeStruct(a.shape, a.dtype) for a in arrs],
                  [pltpu.SemaphoreType.DMA((3 * n,)), pltpu.SemaphoreType.DMA((3 * n,)), pltpu.SemaphoreType.DMA((n,))],
                  start, finish)


COL_SHARDED = ("w_in", "w_uq", "w_ukv", "w_branch_a", "w_branch_c", "w_up")
ROW_SHARDED = ("w_branch_b", "w_o", "w_down")
BIG = COL_SHARDED + ROW_SHARDED


def _full_from_stacked(name, st):
    if name in COL_SHARDED:
        _, L, K, n = st.shape
        return st.transpose(1, 2, 0, 3).reshape(L, K, 8 * n)
    if name in ROW_SHARDED:
        _, L, k, N = st.shape
        return st.transpose(1, 0, 2, 3).reshape(L, 8 * k, N)
    if name == "pool_w":
        _, L, G, pk, PG = st.shape
        return st.transpose(1, 2, 0, 3, 4).reshape(L, G, 8 * pk, PG)
    if name == "meta_tokens":
        _, M, n = st.shape
        return st.transpose(1, 0, 2).reshape(M, 8 * n)
    if name == "conv_w":
        _, L, W, n = st.shape
        return st.transpose(1, 2, 0, 3).reshape(L, W, 8 * n)
    raise ValueError(name)


def _shards_from_full(name, g):
    if name in COL_SHARDED:
        L, K, N = g.shape
        s = g.reshape(L, K, 8, N // 8).transpose(2, 0, 1, 3)
    else:
        L, K, N = g.shape
        s = g.reshape(L, 8, K // 8, N).transpose(1, 0, 2, 3)
    return s.reshape((4, 2) + s.shape[1:])


def _w_in_to_padded(w, dm):
    o3 = 3 * dm.DC + dm.QL + dm.KL
    o4 = o3 + dm.ROPE
    o5 = o4 + dm.DP
    pad = jnp.zeros(w.shape[:-1] + (256 - dm.ROPE,), w.dtype)
    return jnp.concatenate([w[..., o5:], w[..., :o3], w[..., o4:o5], w[..., o3:o4], pad], axis=-1)


def _w_in_from_padded(g, dm):
    o3 = 3 * dm.DC + dm.QL + dm.KL
    a = 3 * dm.D
    return jnp.concatenate([g[..., a:a + o3], g[..., dm.o_rope:dm.o_rope + dm.ROPE], g[..., dm.o_pool:dm.o_pool + dm.DP],
                            g[..., :a]], axis=-1)


def _pad_heads(w, dm):
    w = w.reshape(w.shape[:-1] + (dm.H, dm.QKH))
    w = jnp.pad(w, [(0, 0)] * (w.ndim - 1) + [(0, dm.HP - dm.QKH)])
    return w.reshape(w.shape[:-2] + (dm.H * dm.HP,))


def _unpad_heads(g, dm):
    g = g.reshape(g.shape[:-1] + (dm.H, dm.HP))[..., :dm.QKH]
    return g.reshape(g.shape[:-2] + (dm.H * dm.QKH,))


class _Weights:
    def __init__(self, w, dm, plan):
        self.w, self.dm, self.plan, self.full = w, dm, plan, {}

    def blocks(self, items):
        return [self.w[n][l:l + 1].astype(BF16) for n, l in items]

    def put(self, items, stacked):
        for (n, l), st in zip(items, stacked):
            f = _full_from_stacked(n, st)[0]
            if n == "w_in":
                f = _w_in_to_padded(f, self.dm)
            if n == "w_uq":
                f = _pad_heads(f, self.dm)
            self.full[(n, l)] = f

    def comm(self, tag):
        items = self.plan.get(tag)
        return gather_ici(self.blocks(items)) if items else None

    def arrived(self, tag, couts):
        self.put(self.plan[tag], fill_sibling(couts, f"fill_{tag}"))

    def __call__(self, n, l):
        return self.full[(n, l)]


class _Reducer:
    def __init__(self, dm, core):
        self.dm, self.core, self.q = dm, core, {}

    def prepare(self, tag, items, g):
        parts = []
        for n, l in items:
            f = g[n]
            if n == "w_in":
                f = _w_in_from_padded(f, self.dm)
            if n == "w_uq":
                f = _unpad_heads(f, self.dm)
            parts.append(_shards_from_full(n, f[None]))
        got = sibling_exchange(parts, f"reduce_sibling_{tag}")
        out = []
        for (n, l), a, b in zip(items, parts, got):
            C = a.shape[-1]
            out.append(add_sibling(a.reshape(4, 2, -1, C), b.reshape(4, -1, C), self.core,
                                   f"reduce_add_{n}_{l}").reshape(b.shape))
        return out

    def put(self, items, summed):
        for key, q in zip(items, summed):
            self.q[key] = q


def _layer_fwd(xin, l, ws, G, tabs, dm):
    nm = lambda s: f"l{l}_{s}"
    D = dm.D

    def mm(tag, *args, **kw):
        comm = ws.comm(nm(tag))
        if comm is None:
            return matmul(*args, nm(tag), **kw)
        res, couts = matmul(*args, nm(tag), comm=comm, **kw)
        ws.arrived(nm(tag), couts)
        return res

    h = rms_fwd(xin, 0, D, G["attn_norm"], dm, nm("rms1"))
    proj = mm("proj", h, ws("w_in", l), "nn", (F32,))
    ya = mixer_a_fwd(proj, G["conv_w"], dm, nm("mixa"))
    ql = rms_fwd(proj, dm.o_ql // dm.QL, dm.QL, G["q_lat_norm"], dm, nm("rms_q"))
    kl = rms_fwd(proj, dm.o_kl // dm.KL, dm.KL, G["kv_lat_norm"], dm, nm("rms_kv"))
    q0 = mm("uq", ql, ws("w_uq", l), "nn", (F32,))
    kv0 = mm("ukv", kl, ws("w_ukv", l), "nn", (F32,))
    q_s, k_s, v_s = qk_prep_fwd(q0, kv0, proj, G["q_norm"], G["k_norm"], tabs[0], tabs[1], dm, nm("qkprep"))
    comm = ws.comm(nm("attn"))
    if comm is None:
        o, ob, lse = attn_fwd(q_s, k_s, v_s, dm, nm("attn"))
    else:
        (o, ob, lse), couts = attn_fwd(q_s, k_s, v_s, dm, nm("attn"), comm=comm)
        ws.arrived(nm("attn"), couts)
    pooled, mixed, yc = pool_fwd(proj, ws("pool_w", l), G["pool_scale"], dm, nm("pool"))
    A = mm("br_a", ya, ws("w_branch_a", l), "nn", (F32,))
    B = mm("br_b", ob, ws("w_branch_b", l), "nn", (F32,))
    C = mm("br_c", yc, ws("w_branch_c", l), "nn", (F32,))
    merged = merge_fwd(proj, A, B, C, dm, nm("merge"))
    x1 = mm("wo", merged, ws("w_o", l), "nn", (F32,), extras=(xin,), epi=lambda acc, r: (acc + r,))
    h2 = rms_fwd(x1, 0, D, G["mlp_norm"], dm, nm("rms2"))
    up, act = mm("up", h2, ws("w_up", l), "nn", (F32, BF16), epi=lambda acc: (acc, jnp.square(jnp.maximum(acc, 0.0))))
    x2 = mm("down", act, ws("w_down", l), "nn", (F32,), extras=(x1,), epi=lambda acc, r: (acc + r,))
    saved = dict(xin=xin, h=h, proj=proj, ya=ya, ql=ql, kl=kl, q0=q0, kv0=kv0, q_s=q_s, k_s=k_s, v_s=v_s, o=o, ob=ob,
                 lse=lse, pooled=pooled, mixed=mixed, yc=yc, A=A, B=B, C=C, merged=merged, x1=x1, h2=h2, up=up, act=act)
    return x2, saved


def _layer_bwd(dx2, dx2b, S, l, ws, G, tabs, dm, pre_attn):
    nm = lambda s: f"l{l}_b_{s}"
    D, T = dm.D, dm.T
    g = {}
    d_up = matmul(dx2b, ws("w_down", l), "nt", (BF16,), nm("d_act"), extras=(S["up"],),
                  epi=lambda acc, up: (acc * (2.0 * jnp.maximum(up, 0.0)),))
    g["w_down"] = matmul(S["act"], dx2b, "tn", (BF16,), nm("g_down"))
    g["w_up"] = matmul(S["h2"], d_up, "tn", (BF16,), nm("g_up"))
    dh2 = matmul(d_up, ws("w_up", l), "nt", (F32,), nm("d_h2"))
    dx1, dx1b, g["mlp_norm"] = rms_bwd(S["x1"], 0, D, G["mlp_norm"], dh2, dx2, dm, nm("rms2"))
    dmerged = matmul(dx1b, ws("w_o", l), "nt", (F32,), nm("d_merged"))
    g["w_o"] = matmul(S["merged"], dx1b, "tn", (BF16,), nm("g_o"))
    dA, dB, dC, dl0, dl1, dl2 = merge_bwd(S["proj"], S["A"], S["B"], S["C"], dmerged, dm, nm("merge"))
    dya = matmul(dA, ws("w_branch_a", l), "nt", (F32,), nm("d_ya"))
    g["w_branch_a"] = matmul(S["ya"], dA, "tn", (BF16,), nm("g_a"))
    dyb = matmul(dB, ws("w_branch_b", l), "nt", (F32,), nm("d_yb"))
    g["w_branch_b"] = matmul(S["ob"], dB, "tn", (BF16,), nm("g_b"))
    dyc = matmul(dC, ws("w_branch_c", l), "nt", (F32,), nm("d_yc"))
    g["w_branch_c"] = matmul(S["yc"], dC, "tn", (BF16,), nm("g_c"))
    du, db, dc, g["conv_w"] = mixer_a_bwd(S["proj"], G["conv_w"], dya, dm, nm("mixa"))
    dpool, g["pool_w"], g["pool_scale"] = pool_bwd(dyc, S["mixed"], S["pooled"], ws("pool_w", l), G["pool_scale"], dm, nm("pool"))
    delta, dob = attn_delta(dyb, S["o"], dm, nm("delta"))
    nq = T // dm.tq
    comm_dq, done_dq, comm_dkv, done_dkv = pre_attn(g)
    dq = attn_bwd_dq(S["q_s"], S["k_s"], S["v_s"], dob, S["lse"], delta, dm, nm("attn_dq"), comm=comm_dq)
    if comm_dq is not None:
        dq, couts = dq
        done_dq(couts)
    dkv = attn_bwd_dkv(S["q_s"], S["k_s"], S["v_s"], dob, S["lse"].reshape(dm.H, nq, 1, dm.tq),
                       delta.reshape(dm.H, nq, 1, dm.tq), dm, nm("attn_dkv"), comm=comm_dkv)
    if comm_dkv is not None:
        dkv, couts = dkv
        done_dkv(couts)
    dk, dv = dkv
    dq0, dkv0, dkr, g["q_norm"], g["k_norm"] = qk_prep_bwd(S["q0"], S["kv0"], S["proj"], G["q_norm"], G["k_norm"],
                                                            tabs[0], tabs[1], dq, dk, dv, dm, nm("qkprep"))
    dql = matmul(dq0, ws("w_uq", l), "nt", (F32,), nm("d_ql"))
    g["w_uq"] = matmul(S["ql"], dq0, "tn", (BF16,), nm("g_uq"))
    dkl = matmul(dkv0, ws("w_ukv", l), "nt", (F32,), nm("d_kl"))
    g["w_ukv"] = matmul(S["kl"], dkv0, "tn", (BF16,), nm("g_ukv"))
    _, dqlat, g["q_lat_norm"] = rms_bwd(S["proj"], dm.o_ql // dm.QL, dm.QL, G["q_lat_norm"], dql, None, dm, nm("rms_q"))
    _, dkvlat, g["kv_lat_norm"] = rms_bwd(S["proj"], dm.o_kl // dm.KL, dm.KL, G["kv_lat_norm"], dkl, None, dm, nm("rms_kv"))
    dproj = jnp.concatenate([dl0, dl1, dl2, du, db, dc, dqlat, dkvlat, dpool, dkr.astype(BF16),
                             jnp.zeros((T, 128), BF16)], axis=1)
    dh = matmul(dproj, ws("w_in", l), "nt", (F32,), nm("d_h"))
    g["w_in"] = matmul(S["h"], dproj, "tn", (BF16,), nm("g_in"))
    dx, dxb, g["attn_norm"] = rms_bwd(S["xin"], 0, D, G["attn_norm"], dh, dx1, dm, nm("rms1"))
    return dx, dxb, g


WEIGHTS = ("meta_tokens", "attn_norm", "w_in", "conv_w", "q_lat_norm", "kv_lat_norm", "w_uq", "w_ukv", "q_norm", "k_norm",
           "pool_w", "pool_scale", "w_branch_a", "w_branch_b", "w_branch_c", "w_o", "mlp_norm", "w_up", "w_down")
SMALL = tuple(n for n in WEIGHTS if n not in BIG)


def kernel(x, meta_tokens, attn_norm, w_in, conv_w, q_lat_norm, kv_lat_norm, w_uq, w_ukv, q_norm, k_norm, pool_w, pool_scale, w_branch_a, w_branch_b, w_branch_c, w_o, mlp_norm, w_up, w_down, loss_target, m_meta_tokens, m_attn_norm, m_w_in, m_conv_w, m_q_lat_norm, m_kv_lat_norm, m_w_uq, m_w_ukv, m_q_norm, m_k_norm, m_pool_w, m_pool_scale, m_w_branch_a, m_w_branch_b, m_w_branch_c, m_w_o, m_mlp_norm, m_w_up, m_w_down, v_meta_tokens, v_attn_norm, v_w_in, v_conv_w, v_q_lat_norm, v_kv_lat_norm, v_w_uq, v_w_ukv, v_q_norm, v_k_norm, v_pool_w, v_pool_scale, v_w_branch_a, v_w_branch_b, v_w_branch_c, v_w_o, v_mlp_norm, v_w_up, v_w_down):
    w = dict(meta_tokens=meta_tokens, attn_norm=attn_norm, w_in=w_in, conv_w=conv_w, q_lat_norm=q_lat_norm,
             kv_lat_norm=kv_lat_norm, w_uq=w_uq, w_ukv=w_ukv, q_norm=q_norm, k_norm=k_norm, pool_w=pool_w,
             pool_scale=pool_scale, w_branch_a=w_branch_a, w_branch_b=w_branch_b, w_branch_c=w_branch_c, w_o=w_o,
             mlp_norm=mlp_norm, w_up=w_up, w_down=w_down)
    m = dict(meta_tokens=m_meta_tokens, attn_norm=m_attn_norm, w_in=m_w_in, conv_w=m_conv_w, q_lat_norm=m_q_lat_norm,
             kv_lat_norm=m_kv_lat_norm, w_uq=m_w_uq, w_ukv=m_w_ukv, q_norm=m_q_norm, k_norm=m_k_norm, pool_w=m_pool_w,
             pool_scale=m_pool_scale, w_branch_a=m_w_branch_a, w_branch_b=m_w_branch_b, w_branch_c=m_w_branch_c, w_o=m_w_o,
             mlp_norm=m_mlp_norm, w_up=m_w_up, w_down=m_w_down)
    v = dict(meta_tokens=v_meta_tokens, attn_norm=v_attn_norm, w_in=v_w_in, conv_w=v_conv_w, q_lat_norm=v_q_lat_norm,
             kv_lat_norm=v_kv_lat_norm, w_uq=v_w_uq, w_ukv=v_w_ukv, q_norm=v_q_norm, k_norm=v_k_norm, pool_w=v_pool_w,
             pool_scale=v_pool_scale, w_branch_a=v_w_branch_a, w_branch_b=v_w_branch_b, w_branch_c=v_w_branch_c, w_o=v_w_o,
             mlp_norm=v_mlp_norm, w_up=v_w_up, w_down=v_w_down)
    L = attn_norm.shape[0]
    assert L == 2, "the gather / reduce schedule below is written for two layers"
    seq, D = x.shape[1], x.shape[2]
    n_meta = meta_tokens.shape[0]
    dm = Dims(D, seq, n_meta)
    T = dm.T
    me = 4 * lax.axis_index("x") + 2 * lax.axis_index("y") + lax.axis_index("c")
    core = lax.axis_index("c").astype(jnp.int32).reshape(1)

    plan = {
        "l0_proj": [("w_branch_a", 0), ("w_branch_b", 0), ("w_branch_c", 0), ("w_o", 0), ("w_up", 0)],
        "l0_attn": [("w_down", 0), ("w_in", 1), ("w_uq", 1), ("w_ukv", 1), ("w_branch_a", 1), ("w_branch_b", 1),
                    ("w_branch_c", 1), ("w_o", 1)],
        "l0_up": [("w_up", 1)],
        "l0_down": [("w_down", 1)],
    }
    ws = _Weights(w, dm, plan)
    first = [("w_in", 0), ("w_uq", 0), ("w_ukv", 0), ("pool_w", 0), ("pool_w", 1)]
    ws.put(first, all_gather(ws.blocks(first), "gather_first"))
    st_small = all_gather([w["meta_tokens"], w["conv_w"]], "gather_small")
    meta_full = _full_from_stacked("meta_tokens", st_small[0])
    conv_full = _full_from_stacked("conv_w", st_small[1])
    pad_gain = lambda gn: jnp.pad(gn, (0, dm.HP - dm.QKH))

    def gains(l):
        G = {n: w[n][l][None, :] for n in ("attn_norm", "q_lat_norm", "kv_lat_norm", "pool_scale", "mlp_norm")}
        G["q_norm"], G["k_norm"] = pad_gain(w["q_norm"][l])[None, :], pad_gain(w["k_norm"][l])[None, :]
        G["conv_w"] = conv_full[l]
        return G

    Gs = [gains(l) for l in range(L)]

    pos = jnp.arange(dm.T_real, dtype=F32)
    inv = 10000.0 ** (-jnp.arange(0, dm.ROPE, 2, dtype=F32) / dm.ROPE)
    ang = pos[:, None] * inv[None, :]
    zpad = jnp.zeros((dm.T_real, LANES - dm.ROPE), F32)
    rows = ((0, T - dm.T_real), (0, 0))
    tabs = (jnp.pad(jnp.concatenate([jnp.cos(ang), jnp.cos(ang), zpad], 1), rows),
            jnp.pad(jnp.concatenate([jnp.sin(ang), jnp.sin(ang), zpad], 1), rows))

    xs = jnp.concatenate([meta_full, x[0], jnp.zeros((T - dm.T_real, D), F32)], axis=0)
    target = jnp.pad(loss_target[0], ((n_meta, T - dm.T_real), (0, 0)))
    saved = []
    for l in range(L):
        xs, S = _layer_fwd(xs, l, ws, Gs[l], tabs, dm)
        saved.append(S)
    dx, dxb, loss_acc = loss_head(xs, target, dm, "loss_head")

    red = _Reducer(dm, core)
    early = lambda l: [(n, l) for n in ("w_down", "w_up", "w_o", "w_branch_a", "w_branch_b", "w_branch_c")]
    late = lambda l: [(n, l) for n in ("w_uq", "w_ukv", "w_in")]
    grads = [None] * L

    def pre_attn_1(g):
        return (reduce_ici(red.prepare("e1", early(1), g)), lambda couts: red.put(early(1), couts), None, None)

    dx, dxb, grads[1] = _layer_bwd(dx, dxb, saved[1], 1, ws, Gs[1], tabs, dm, pre_attn_1)
    late1 = red.prepare("l1", late(1), grads[1])

    def pre_attn_0(g):
        return (reduce_ici(late1), lambda couts: red.put(late(1), couts),
                reduce_ici(red.prepare("e0", early(0), g)), lambda couts: red.put(early(0), couts))

    dx, dxb, grads[0] = _layer_bwd(dx, dxb, saved[0], 0, ws, Gs[0], tabs, dm, pre_attn_0)
    red.put(late(0), chip_exchange(red.prepare("l0", late(0), grads[0]), "reduce_chips_last"))
    grad_x = dx[n_meta:dm.T_real][None]
    summed = [jnp.concatenate([red.q[(n, l)] for l in range(L)], axis=1) for n in BIG]

    out_g, out_d, out_m, out_v = {}, {}, {}, {}

    def update(n, parts3):
        shp = w[n].shape
        C = shp[-1]
        res = adamw(w[n].reshape(-1, C), m[n].reshape(-1, C), v[n].reshape(-1, C), parts3, f"adamw_{n}")
        out_g[n], out_d[n], out_m[n], out_v[n] = [r.reshape(shp) for r in res]

    for n, q in zip(BIG, summed):
        update(n, q.reshape(4, -1, q.shape[-1]))

    small_full = {
        "meta_tokens": dx[:n_meta],
        "conv_w": jnp.stack([grads[l]["conv_w"] for l in range(L)]),
        "pool_w": jnp.stack([grads[l]["pool_w"] for l in range(L)]),
        "q_norm": jnp.stack([grads[l]["q_norm"][0, :dm.QKH] for l in range(L)]),
        "k_norm": jnp.stack([grads[l]["k_norm"][0, :dm.QKH] for l in range(L)]),
    }
    for n in ("attn_norm", "q_lat_norm", "kv_lat_norm", "pool_scale", "mlp_norm"):
        small_full[n] = jnp.stack([grads[l][n][0] for l in range(L)])
    flat = jnp.concatenate([small_full[n].reshape(-1) for n in SMALL] + [loss_acc[0, :1]])
    n_flat = flat.shape[0]
    rows_small = -(-n_flat // (8 * LANES)) * 8
    flat = jnp.pad(flat, (0, rows_small * LANES - n_flat)).reshape(rows_small, LANES)
    total = sum_parts(all_gather([flat], "gather_small_grads")[0], "sum_small").reshape(-1)
    off = 0
    for n in SMALL:
        size = math.prod(small_full[n].shape)
        gsum = total[off:off + size].reshape(small_full[n].shape)
        off += size
        if n in ("meta_tokens", "conv_w"):
            blk = w[n].shape[-1]
            gsum = lax.dynamic_slice_in_dim(gsum, me * blk, blk, axis=gsum.ndim - 1)
        elif n == "pool_w":
            blk = w[n].shape[2]
            gsum = lax.dynamic_slice_in_dim(gsum, me * blk, blk, axis=2)
        update(n, gsum.reshape(1, -1, gsum.shape[-1]))
    loss = total[off]

    return (loss, grad_x, *[out_g[n] for n in WEIGHTS], *[out_d[n] for n in WEIGHTS],
            *[out_m[n] for n in WEIGHTS], *[out_v[n] for n in WEIGHTS])
```

```python
import functools
import math

import jax
import jax.numpy as jnp
from jax import lax
from jax.experimental import pallas as pl
from jax.experimental.pallas import tpu as pltpu

F32 = jnp.float32
BF16 = jnp.bfloat16

VMEM_LIMIT_BYTES = 56 * 1024 * 1024
LANES = 128
EPS = 1e-6
HALO = 16

ADAM_LR = 0.001
ADAM_B1 = 0.9
ADAM_B2 = 0.999
ADAM_EPS = 1e-08
ADAM_WD = 0.01
ADAM_STEP = 10


def _params(sem):
    return pltpu.CompilerParams(dimension_semantics=sem, vmem_limit_bytes=VMEM_LIMIT_BYTES)


def _pick(n, prefs):
    for p in prefs:
        if p <= n and n % p == 0:
            return p
    return n


_MESH = pl.DeviceIdType.MESH
_HBM = pl.BlockSpec(memory_space=pltpu.HBM)


def _place():
    return lax.axis_index("x"), lax.axis_index("y"), lax.axis_index("c")


class Hosted:
    def __init__(self, ins, out_shapes, sems, start, finish):
        self.ins, self.out_shapes, self.sems, self.start, self.finish = ins, out_shapes, sems, start, finish


def _hosted_call(body, *, name, grid, in_specs, out_specs, out_shape, scratch_shapes, semantics, args, comm):
    n_in, n_out, n_scr = len(in_specs), len(out_specs), len(scratch_shapes)
    if comm is None:
        outs = pl.pallas_call(body, name=name, grid=grid, in_specs=in_specs, out_specs=out_specs, out_shape=out_shape,
                              scratch_shapes=scratch_shapes, compiler_params=_params(semantics))(*args)
        return list(outs), []
    ci, co = len(comm.ins), len(comm.out_shapes)

    def hosting(*refs):
        ins, cins = refs[:n_in], refs[n_in:n_in + ci]
        outs = refs[n_in + ci:n_in + ci + n_out]
        couts = refs[n_in + ci + n_out:n_in + ci + n_out + co]
        scr = refs[n_in + ci + n_out + co:n_in + ci + n_out + co + n_scr]
        csems = refs[n_in + ci + n_out + co + n_scr:]
        ids = [pl.program_id(d) for d in range(len(grid))]
        first = functools.reduce(jnp.logical_and, [i == 0 for i in ids])
        last = functools.reduce(jnp.logical_and, [i == g - 1 for i, g in zip(ids, grid)])

        @pl.when(first)
        def _():
            comm.start(cins, couts, csems)

        body(*ins, *outs, *scr)

        @pl.when(last)
        def _():
            comm.finish(cins, couts, csems)

    outs = pl.pallas_call(
        hosting, name=name, grid=grid,
        in_specs=list(in_specs) + [_HBM] * ci, out_specs=list(out_specs) + [_HBM] * co,
        out_shape=list(out_shape) + list(comm.out_shapes),
        scratch_shapes=list(scratch_shapes) + list(comm.sems),
        compiler_params=_params(("arbitrary",) * len(grid)),
    )(*args, *comm.ins)
    return list(outs[:n_out]), list(outs[n_out:])


def matmul(a, b, mode, out_dtypes, name, extras=(), epi=None, tm=None, tn=None, tk=None, comm=None):
    if mode == "nn":
        (M, K), (K2, N) = a.shape, b.shape
    elif mode == "nt":
        (M, K), (N, K2) = a.shape, b.shape
    else:
        (K, M), (K2, N) = a.shape, b.shape
    assert K == K2, (a.shape, b.shape, mode)
    tm = tm or _pick(M, (1408, 1056, 1024, 768, 512, 384, 256, 128))
    tn = tn or _pick(N, (1280, 1024, 768, 512, 384, 256, 128))
    tk = tk or _pick(K, (1056, 1024, 768, 512, 384, 256, 128))
    nk = K // tk
    dims = {"nn": (((1,), (0,)), ((), ())), "nt": (((1,), (1,)), ((), ())), "tn": (((0,), (0,)), ((), ()))}[mode]
    n_extra, n_out = len(extras), len(out_dtypes)

    def body(*refs):
        a_ref, b_ref = refs[0], refs[1]
        extra_refs = refs[2:2 + n_extra]
        out_refs = refs[2 + n_extra:2 + n_extra + n_out]
        acc_ref = refs[-1]
        k = pl.program_id(2)

        @pl.when(k == 0)
        def _():
            acc_ref[...] = jnp.zeros_like(acc_ref)

        acc_ref[...] += lax.dot_general(a_ref[...], b_ref[...], dims, preferred_element_type=F32)

        @pl.when(k == nk - 1)
        def _():
            acc = acc_ref[...]
            outs = (acc,) if epi is None else epi(acc, *[r[...] for r in extra_refs])
            for o_ref, o in zip(out_refs, outs):
                o_ref[...] = o.astype(o_ref.dtype)

    a_spec = {"nn": pl.BlockSpec((tm, tk), lambda i, j, k: (i, k)),
              "nt": pl.BlockSpec((tm, tk), lambda i, j, k: (i, k)),
              "tn": pl.BlockSpec((tk, tm), lambda i, j, k: (k, i))}[mode]
    b_spec = {"nn": pl.BlockSpec((tk, tn), lambda i, j, k: (k, j)),
              "nt": pl.BlockSpec((tn, tk), lambda i, j, k: (j, k)),
              "tn": pl.BlockSpec((tk, tn), lambda i, j, k: (k, j))}[mode]
    o_spec = pl.BlockSpec((tm, tn), lambda i, j, k: (i, j))
    outs, couts = _hosted_call(
        body, name=name, grid=(M // tm, N // tn, nk),
        in_specs=[a_spec, b_spec] + [o_spec] * n_extra,
        out_specs=[o_spec] * n_out,
        out_shape=[jax.ShapeDtypeStruct((M, N), d) for d in out_dtypes],
        scratch_shapes=[pltpu.VMEM((tm, tn), F32)],
        semantics=("parallel", "parallel", "arbitrary"), args=(a, b, *extras), comm=comm)
    res = outs[0] if n_out == 1 else outs
    return res if comm is None else (res, couts)


class Dims:
    def __init__(self, d_model, seq, n_meta):
        self.D = d_model
        self.n_meta = n_meta
        self.T_real = seq + n_meta
        self.T = -(-self.T_real // LANES) * LANES
        self.H = d_model // 128
        self.DC = d_model // 2
        self.DP = d_model // 2
        self.PG = self.DP // 4
        self.QL = 512
        self.KL = 512
        self.ROPE = 64
        self.NOPE = 128
        self.QKH = 192
        self.HP = 256
        self.DFF = 4 * d_model
        self.o_gate = 0
        self.o_u = 3 * d_model
        self.o_b = self.o_u + self.DC
        self.o_c = self.o_b + self.DC
        self.o_ql = self.o_c + self.DC
        self.o_kl = self.o_ql + self.QL
        self.o_pool = self.o_kl + self.KL
        self.o_rope = self.o_pool + self.DP
        self.NIN = self.o_rope + 256
        self.tr = _pick(self.T, (384, 256, 128))
        self.tq = _pick(self.T, (384, 256, 128))


def _row_ids(i, tr):
    return i * tr + lax.broadcasted_iota(jnp.int32, (tr, 1), 0)


def rms_fwd(x, col_block, width, g, dm, name):
    tr = dm.tr

    def body(x_ref, g_ref, y_ref):
        xv = x_ref[...]
        r = lax.rsqrt(jnp.mean(xv * xv, axis=-1, keepdims=True) + EPS)
        y_ref[...] = (xv * r * g_ref[...]).astype(y_ref.dtype)

    return pl.pallas_call(
        body, name=name, grid=(dm.T // tr,),
        in_specs=[pl.BlockSpec((tr, width), lambda i: (i, col_block)), pl.BlockSpec((1, width), lambda i: (0, 0))],
        out_specs=pl.BlockSpec((tr, width), lambda i: (i, 0)),
        out_shape=jax.ShapeDtypeStruct((dm.T, width), BF16),
        compiler_params=_params(("parallel",)),
    )(x, g.reshape(1, width))


def rms_bwd(x, col_block, width, g, dy, dres, dm, name):
    tr = dm.tr
    has_res = dres is not None

    def body(*refs):
        if has_res:
            x_ref, g_ref, dy_ref, dres_ref, dx_ref, dxb_ref, dg_ref = refs
        else:
            x_ref, g_ref, dy_ref, dx_ref, dxb_ref, dg_ref = refs
        xv, dyv = x_ref[...], dy_ref[...]
        r = lax.rsqrt(jnp.mean(xv * xv, axis=-1, keepdims=True) + EPS)
        gdy = dyv * g_ref[...]
        dx = r * gdy - xv * (r * r * r) * jnp.mean(xv * gdy, axis=-1, keepdims=True)
        if has_res:
            dx = dx + dres_ref[...]
        dx_ref[...] = dx
        dxb_ref[...] = dx.astype(BF16)

        @pl.when(pl.program_id(0) == 0)
        def _():
            dg_ref[...] = jnp.zeros_like(dg_ref)

        dg_ref[...] += jnp.sum(dyv * xv * r, axis=0, keepdims=True)

    row = pl.BlockSpec((tr, width), lambda i: (i, 0))
    in_specs = [pl.BlockSpec((tr, width), lambda i: (i, col_block)), pl.BlockSpec((1, width), lambda i: (0, 0)), row]
    args = [x, g.reshape(1, width), dy]
    if has_res:
        in_specs.append(row)
        args.append(dres)
    return pl.pallas_call(
        body, name=name, grid=(dm.T // tr,),
        in_specs=in_specs,
        out_specs=[row, row, pl.BlockSpec((1, width), lambda i: (0, 0))],
        out_shape=[jax.ShapeDtypeStruct((dm.T, width), F32), jax.ShapeDtypeStruct((dm.T, width), BF16),
                   jax.ShapeDtypeStruct((1, width), F32)],
        compiler_params=_params(("arbitrary",)),
    )(*args)


def _fill_halo_buf(buf, src_fn, T, R, width):
    zeros = jnp.zeros((HALO, width), F32)
    buf[pl.ds(0, HALO), :] = zeros
    buf[pl.ds(HALO + T, HALO), :] = zeros

    def fill(r, c):
        r0 = pl.multiple_of(r * R, 8)
        buf[pl.ds(r0 + HALO, R), :] = src_fn(r0)
        return c

    lax.fori_loop(0, T // R, fill, 0)


def _back(win, sh):
    return pltpu.roll(win, sh, 0)


def _fwd(win, sh):
    return pltpu.roll(win, win.shape[0] - sh, 0)


def mixer_a_fwd(proj, conv_w, dm, name):
    T, cw = dm.T, 128
    R = dm.tr
    nb = dm.DC // cw

    def body(u_ref, b_ref, c_ref, w_ref, ya_ref, buf):
        _fill_halo_buf(buf, lambda r0: c_ref[pl.ds(r0, R), :] * u_ref[pl.ds(r0, R), :], T, R, cw)
        w0, w1, w2 = w_ref[0:1, :], w_ref[1:2, :], w_ref[2:3, :]

        def chunk(r, c):
            r0 = pl.multiple_of(r * R, 8)
            win = buf[pl.ds(r0, R + HALO), :]
            cv = w2 * win + w1 * _back(win, 1) + w0 * _back(win, 2)
            ya_ref[pl.ds(r0, R), :] = (b_ref[pl.ds(r0, R), :] * cv[HALO:, :]).astype(BF16)
            return c

        lax.fori_loop(0, T // R, chunk, 0)

    col = lambda off: pl.BlockSpec((T, cw), lambda j: (0, off // cw + j))
    return pl.pallas_call(
        body, name=name, grid=(nb,),
        in_specs=[col(dm.o_u), col(dm.o_b), col(dm.o_c), pl.BlockSpec((3, cw), lambda j: (0, j))],
        out_specs=pl.BlockSpec((T, cw), lambda j: (0, j)),
        out_shape=jax.ShapeDtypeStruct((T, dm.DC), BF16),
        scratch_shapes=[pltpu.VMEM((T + 2 * HALO, cw), F32)],
        compiler_params=_params(("parallel",)),
    )(proj, proj, proj, conv_w)


def mixer_a_bwd(proj, conv_w, dya, dm, name):
    T, cw = dm.T, 128
    R = dm.tr
    nb = dm.DC // cw

    def body(u_ref, b_ref, c_ref, w_ref, dya_ref, du_ref, db_ref, dc_ref, dw_ref, sbuf, gbuf):
        _fill_halo_buf(sbuf, lambda r0: c_ref[pl.ds(r0, R), :] * u_ref[pl.ds(r0, R), :], T, R, cw)
        _fill_halo_buf(gbuf, lambda r0: dya_ref[pl.ds(r0, R), :] * b_ref[pl.ds(r0, R), :], T, R, cw)
        w0, w1, w2 = w_ref[0:1, :], w_ref[1:2, :], w_ref[2:3, :]

        def chunk(r, acc):
            a0, a1, a2 = acc
            r0 = pl.multiple_of(r * R, 8)
            swin = sbuf[pl.ds(r0, R + HALO), :]
            s0, s1, s2 = swin[HALO:, :], _back(swin, 1)[HALO:, :], _back(swin, 2)[HALO:, :]
            gwin = gbuf[pl.ds(r0 + HALO, R + HALO), :]
            g0, g1, g2 = gwin[:R, :], _fwd(gwin, 1)[:R, :], _fwd(gwin, 2)[:R, :]
            cv = w2 * s0 + w1 * s1 + w0 * s2
            ds = w2 * g0 + w1 * g1 + w0 * g2
            db_ref[pl.ds(r0, R), :] = (dya_ref[pl.ds(r0, R), :] * cv).astype(BF16)
            du_ref[pl.ds(r0, R), :] = (ds * c_ref[pl.ds(r0, R), :]).astype(BF16)
            dc_ref[pl.ds(r0, R), :] = (ds * u_ref[pl.ds(r0, R), :]).astype(BF16)
            a2 = a2 + jnp.sum(g0 * s0, axis=0, keepdims=True)
            a1 = a1 + jnp.sum(g0 * s1, axis=0, keepdims=True)
            a0 = a0 + jnp.sum(g0 * s2, axis=0, keepdims=True)
            return a0, a1, a2

        z = jnp.zeros((1, cw), F32)
        a0, a1, a2 = lax.fori_loop(0, T // R, chunk, (z, z, z))
        dw_ref[0:1, :] = a0
        dw_ref[1:2, :] = a1
        dw_ref[2:3, :] = a2

    col = lambda off: pl.BlockSpec((T, cw), lambda j: (0, off // cw + j))
    own = pl.BlockSpec((T, cw), lambda j: (0, j))
    o = jax.ShapeDtypeStruct((T, dm.DC), BF16)
    return pl.pallas_call(
        body, name=name, grid=(nb,),
        in_specs=[col(dm.o_u), col(dm.o_b), col(dm.o_c), pl.BlockSpec((3, cw), lambda j: (0, j)), own],
        out_specs=[own, own, own, pl.BlockSpec((3, cw), lambda j: (0, j))],
        out_shape=[o, o, o, jax.ShapeDtypeStruct((3, dm.DC), F32)],
        scratch_shapes=[pltpu.VMEM((T + 2 * HALO, cw), F32), pltpu.VMEM((T + 2 * HALO, cw), F32)],
        compiler_params=_params(("parallel",)),
    )(proj, proj, proj, conv_w, dya)


def _rope(x, C, S):
    return x * C + (pltpu.roll(x, 32, 1) - pltpu.roll(x, 96, 1)) * S


def _rope_t(dy, C, S):
    return dy * C + (pltpu.roll(dy, 96, 1) - pltpu.roll(dy, 32, 1)) * S


def qk_prep_fwd(q0, kv0, proj, qn, kn, C, S, dm, name):
    T, H, tr = dm.T, dm.H, dm.tr
    inv = 1.0 / dm.QKH

    def body(q0_ref, kv_ref, kr_ref, qn_ref, kn_ref, c_ref, s_ref, q_ref, k_ref, v_ref):
        Cv, Sv = c_ref[...], s_ref[...]
        qa, qb = q0_ref[:, :128], q0_ref[:, 128:]
        r = lax.rsqrt((jnp.sum(qa * qa, -1, keepdims=True) + jnp.sum(qb * qb, -1, keepdims=True)) * inv + EPS)
        q_ref[:, :128] = (qa * r * qn_ref[:, :128]).astype(BF16)
        q_ref[:, 128:] = _rope(qb * r * qn_ref[:, 128:], Cv, Sv).astype(BF16)
        ka, kb = kv_ref[:, :128], kr_ref[...]
        r = lax.rsqrt((jnp.sum(ka * ka, -1, keepdims=True) + jnp.sum(kb * kb, -1, keepdims=True)) * inv + EPS)
        k_ref[:, :128] = (ka * r * kn_ref[:, :128]).astype(BF16)
        k_ref[:, 128:] = _rope(kb * r * kn_ref[:, 128:], Cv, Sv).astype(BF16)
        v_ref[...] = kv_ref[:, 128:].astype(BF16)

    head = pl.BlockSpec((tr, 256), lambda i, h: (i, h))
    gain = pl.BlockSpec((1, 256), lambda i, h: (0, 0))
    tab = pl.BlockSpec((tr, 128), lambda i, h: (i, 0))
    return pl.pallas_call(
        body, name=name, grid=(T // tr, H),
        in_specs=[head, head, pl.BlockSpec((tr, 128), lambda i, h: (i, dm.o_rope // 128)), gain, gain, tab, tab],
        out_specs=[head, head, pl.BlockSpec((tr, 128), lambda i, h: (i, h))],
        out_shape=[jax.ShapeDtypeStruct((T, H * 256), BF16), jax.ShapeDtypeStruct((T, H * 256), BF16),
                   jax.ShapeDtypeStruct((T, H * 128), BF16)],
        compiler_params=_params(("parallel", "parallel")),
    )(q0, kv0, proj, qn, kn, C, S)


def qk_prep_bwd(q0, kv0, proj, qn, kn, C, S, dq, dk, dv, dm, name):
    T, H, tr = dm.T, dm.H, dm.tr
    inv = 1.0 / dm.QKH

    def body(q0_ref, kv_ref, kr_ref, qn_ref, kn_ref, c_ref, s_ref, dq_ref, dk_ref, dv_ref,
             dq0_ref, dkv_ref, dkr_ref, dqn_ref, dkn_ref):
        i, h = pl.program_id(0), pl.program_id(1)
        Cv, Sv = c_ref[...], s_ref[...]

        def norm_bwd(xa, xb, ga, gb, dya, dyb):
            r = lax.rsqrt((jnp.sum(xa * xa, -1, keepdims=True) + jnp.sum(xb * xb, -1, keepdims=True)) * inv + EPS)
            dzb = _rope_t(dyb, Cv, Sv)
            gda, gdb = ga * dya, gb * dzb
            dot = (jnp.sum(xa * gda, -1, keepdims=True) + jnp.sum(xb * gdb, -1, keepdims=True)) * inv
            r3 = r * r * r
            dxa = r * gda - xa * r3 * dot
            dxb = r * gdb - xb * r3 * dot
            dga = jnp.sum(dya * xa * r, axis=0, keepdims=True)
            dgb = jnp.sum(dzb * xb * r, axis=0, keepdims=True)
            return dxa, dxb, dga, dgb

        @pl.when((i == 0) & (h == 0))
        def _():
            dqn_ref[...] = jnp.zeros_like(dqn_ref)
            dkn_ref[...] = jnp.zeros_like(dkn_ref)

        dxa, dxb, dga, dgb = norm_bwd(q0_ref[:, :128], q0_ref[:, 128:], qn_ref[:, :128], qn_ref[:, 128:],
                                      dq_ref[:, :128], dq_ref[:, 128:])
        dq0_ref[:, :128] = dxa.astype(BF16)
        dq0_ref[:, 128:] = dxb.astype(BF16)
        dqn_ref[:, :128] += dga
        dqn_ref[:, 128:] += dgb
        dxa, dxb, dga, dgb = norm_bwd(kv_ref[:, :128], kr_ref[...], kn_ref[:, :128], kn_ref[:, 128:],
                                      dk_ref[:, :128], dk_ref[:, 128:])
        dkv_ref[:, :128] = dxa.astype(BF16)
        dkv_ref[:, 128:] = dv_ref[...].astype(BF16)
        dkn_ref[:, :128] += dga
        dkn_ref[:, 128:] += dgb

        @pl.when(h == 0)
        def _():
            dkr_ref[...] = jnp.zeros_like(dkr_ref)

        dkr_ref[...] += dxb

    head = pl.BlockSpec((tr, 256), lambda i, h: (i, h))
    gain = pl.BlockSpec((1, 256), lambda i, h: (0, 0))
    tab = pl.BlockSpec((tr, 128), lambda i, h: (i, 0))
    return pl.pallas_call(
        body, name=name, grid=(T // tr, H),
        in_specs=[head, head, pl.BlockSpec((tr, 128), lambda i, h: (i, dm.o_rope // 128)), gain, gain, tab, tab,
                  head, head, pl.BlockSpec((tr, 128), lambda i, h: (i, h))],
        out_specs=[head, head, tab, gain, gain],
        out_shape=[jax.ShapeDtypeStruct((T, H * 256), BF16), jax.ShapeDtypeStruct((T, H * 256), BF16),
                   jax.ShapeDtypeStruct((T, 128), F32), jax.ShapeDtypeStruct((1, 256), F32),
                   jax.ShapeDtypeStruct((1, 256), F32)],
        compiler_params=_params(("arbitrary", "arbitrary")),
    )(q0, kv0, proj, qn, kn, C, S, dq, dk, dv)


_NT = (((1,), (1,)), ((), ()))


def _causal_mask(t):
    return lax.broadcasted_iota(jnp.int32, (t, t), 0) >= lax.broadcasted_iota(jnp.int32, (t, t), 1)


def _causal_mask_t(t):
    return lax.broadcasted_iota(jnp.int32, (t, t), 0) <= lax.broadcasted_iota(jnp.int32, (t, t), 1)


HB = 2


def attn_fwd(q, k, v, dm, name, comm=None):
    T, H, tq = dm.T, dm.H, dm.tq
    scale = dm.QKH ** -0.5

    def body(q_ref, k_ref, v_ref, o_ref, ob_ref, lse_ref):
        qi = pl.program_id(1)

        def step(j, carry, masked):
            j0 = pl.multiple_of(j * tq, tq)
            ss = [lax.dot_general(q_ref[:, 256 * a:256 * (a + 1)], k_ref[pl.ds(j0, tq), 256 * a:256 * (a + 1)], _NT,
                                  preferred_element_type=F32) for a in range(HB)]
            out = []
            for a in range(HB):
                m, l, acc = carry[a]
                s = ss[a] * scale
                if masked:
                    s = jnp.where(_causal_mask(tq), s, -jnp.inf)
                m_new = jnp.maximum(m, jnp.max(s, -1, keepdims=True))
                alpha = jnp.exp(m - m_new)
                p = jnp.exp(s - m_new)
                l = alpha * l + jnp.sum(p, -1, keepdims=True)
                acc = alpha * acc + jnp.dot(p.astype(BF16), v_ref[pl.ds(j0, tq), 128 * a:128 * (a + 1)],
                                            preferred_element_type=F32)
                out.append((m_new, l, acc))
            return tuple(out)

        one = (jnp.full((tq, 1), -jnp.inf, F32), jnp.zeros((tq, 1), F32), jnp.zeros((tq, 128), F32))
        carry = lax.fori_loop(0, qi, lambda j, c: step(j, c, False), (one,) * HB)
        carry = step(qi, carry, True)
        for a in range(HB):
            m, l, acc = carry[a]
            o = acc / l
            o_ref[:, 128 * a:128 * (a + 1)] = o
            ob_ref[:, 128 * a:128 * (a + 1)] = o.astype(BF16)
            lse_ref[a] = m + jnp.log(l)

    outs, couts = _hosted_call(
        body, name=name, grid=(H // HB, T // tq),
        in_specs=[pl.BlockSpec((tq, 256 * HB), lambda h, i: (i, h)), pl.BlockSpec((T, 256 * HB), lambda h, i: (0, h)),
                  pl.BlockSpec((T, 128 * HB), lambda h, i: (0, h))],
        out_specs=[pl.BlockSpec((tq, 128 * HB), lambda h, i: (i, h)), pl.BlockSpec((tq, 128 * HB), lambda h, i: (i, h)),
                   pl.BlockSpec((HB, tq, 1), lambda h, i: (h, i, 0))],
        out_shape=[jax.ShapeDtypeStruct((T, H * 128), F32), jax.ShapeDtypeStruct((T, H * 128), BF16),
                   jax.ShapeDtypeStruct((H, T, 1), F32)],
        scratch_shapes=[], semantics=("parallel", "parallel"), args=(q, k, v), comm=comm)
    return outs if comm is None else (outs, couts)


def attn_delta(do, o, dm, name):
    T, H, tr = dm.T, dm.H, dm.tr

    def body(do_ref, o_ref, delta_ref, dob_ref):
        d = do_ref[...]
        delta_ref[...] = jnp.sum(d * o_ref[...], -1, keepdims=True)
        dob_ref[...] = d.astype(BF16)

    blk = pl.BlockSpec((tr, 128), lambda i, h: (i, h))
    return pl.pallas_call(
        body, name=name, grid=(T // tr, H),
        in_specs=[blk, blk],
        out_specs=[pl.BlockSpec((None, tr, 1), lambda i, h: (h, i, 0)), blk],
        out_shape=[jax.ShapeDtypeStruct((H, T, 1), F32), jax.ShapeDtypeStruct((T, H * 128), BF16)],
        compiler_params=_params(("parallel", "parallel")),
    )(do, o)


def attn_bwd_dq(q, k, v, do, lse, delta, dm, name, comm=None):
    T, H, tq = dm.T, dm.H, dm.tq
    scale = dm.QKH ** -0.5

    def body(q_ref, k_ref, v_ref, do_ref, lse_ref, delta_ref, dq_ref):
        qi = pl.program_id(1)

        def step(j, dqs, masked):
            j0 = pl.multiple_of(j * tq, tq)
            hk = lambda a: slice(256 * a, 256 * (a + 1))
            hv = lambda a: slice(128 * a, 128 * (a + 1))
            ss = [lax.dot_general(q_ref[:, hk(a)], k_ref[pl.ds(j0, tq), hk(a)], _NT, preferred_element_type=F32)
                  for a in range(HB)]
            dps = [lax.dot_general(do_ref[:, hv(a)], v_ref[pl.ds(j0, tq), hv(a)], _NT, preferred_element_type=F32)
                   for a in range(HB)]
            out = []
            for a in range(HB):
                p = jnp.exp(ss[a] * scale - lse_ref[a])
                if masked:
                    p = jnp.where(_causal_mask(tq), p, 0.0)
                ds = p * (dps[a] - delta_ref[a]) * scale
                out.append(dqs[a] + jnp.dot(ds.astype(BF16), k_ref[pl.ds(j0, tq), hk(a)], preferred_element_type=F32))
            return tuple(out)

        dqs = lax.fori_loop(0, qi, lambda j, c: step(j, c, False), (jnp.zeros((tq, 256), F32),) * HB)
        dqs = step(qi, dqs, True)
        for a in range(HB):
            dq_ref[:, 256 * a:256 * (a + 1)] = dqs[a]

    stat = pl.BlockSpec((HB, tq, 1), lambda h, i: (h, i, 0))
    outs, couts = _hosted_call(
        body, name=name, grid=(H // HB, T // tq),
        in_specs=[pl.BlockSpec((tq, 256 * HB), lambda h, i: (i, h)), pl.BlockSpec((T, 256 * HB), lambda h, i: (0, h)),
                  pl.BlockSpec((T, 128 * HB), lambda h, i: (0, h)), pl.BlockSpec((tq, 128 * HB), lambda h, i: (i, h)),
                  stat, stat],
        out_specs=[pl.BlockSpec((tq, 256 * HB), lambda h, i: (i, h))],
        out_shape=[jax.ShapeDtypeStruct((T, H * 256), F32)],
        scratch_shapes=[], semantics=("parallel", "parallel"), args=(q, k, v, do, lse, delta), comm=comm)
    return outs[0] if comm is None else (outs[0], couts)


def attn_bwd_dkv(q, k, v, do, lse_rows, delta_rows, dm, name, comm=None):
    T, H, tq = dm.T, dm.H, dm.tq
    nq = T // tq
    scale = dm.QKH ** -0.5

    def body(q_ref, k_ref, v_ref, do_ref, lse_ref, delta_ref, dk_ref, dv_ref):
        kj = pl.program_id(1)

        def step(i, carry, masked):
            i0 = pl.multiple_of(i * tq, tq)
            hk = lambda a: slice(256 * a, 256 * (a + 1))
            hv = lambda a: slice(128 * a, 128 * (a + 1))
            sts = [lax.dot_general(k_ref[:, hk(a)], q_ref[pl.ds(i0, tq), hk(a)], _NT, preferred_element_type=F32)
                   for a in range(HB)]
            dpts = [lax.dot_general(v_ref[:, hv(a)], do_ref[pl.ds(i0, tq), hv(a)], _NT, preferred_element_type=F32)
                    for a in range(HB)]
            out = []
            for a in range(HB):
                dk, dv = carry[a]
                pt = jnp.exp(sts[a] * scale - lse_ref[a, i])
                if masked:
                    pt = jnp.where(_causal_mask_t(tq), pt, 0.0)
                dv = dv + jnp.dot(pt.astype(BF16), do_ref[pl.ds(i0, tq), hv(a)], preferred_element_type=F32)
                dst = pt * (dpts[a] - delta_ref[a, i]) * scale
                dk = dk + jnp.dot(dst.astype(BF16), q_ref[pl.ds(i0, tq), hk(a)], preferred_element_type=F32)
                out.append((dk, dv))
            return tuple(out)

        carry = step(kj, ((jnp.zeros((tq, 256), F32), jnp.zeros((tq, 128), F32)),) * HB, True)
        carry = lax.fori_loop(kj + 1, nq, lambda i, c: step(i, c, False), carry)
        for a in range(HB):
            dk_ref[:, 256 * a:256 * (a + 1)] = carry[a][0]
            dv_ref[:, 128 * a:128 * (a + 1)] = carry[a][1]

    rows = pl.BlockSpec((HB, nq, 1, tq), lambda h, j: (h, 0, 0, 0))
    outs, couts = _hosted_call(
        body, name=name, grid=(H // HB, nq),
        in_specs=[pl.BlockSpec((T, 256 * HB), lambda h, j: (0, h)), pl.BlockSpec((tq, 256 * HB), lambda h, j: (j, h)),
                  pl.BlockSpec((tq, 128 * HB), lambda h, j: (j, h)), pl.BlockSpec((T, 128 * HB), lambda h, j: (0, h)),
                  rows, rows],
        out_specs=[pl.BlockSpec((tq, 256 * HB), lambda h, j: (j, h)), pl.BlockSpec((tq, 128 * HB), lambda h, j: (j, h))],
        out_shape=[jax.ShapeDtypeStruct((T, H * 256), F32), jax.ShapeDtypeStruct((T, H * 128), F32)],
        scratch_shapes=[], semantics=("parallel", "parallel"), args=(q, k, v, do, lse_rows, delta_rows), comm=comm)
    return outs if comm is None else (outs, couts)


def _window_sum(win, g, shift):
    s1 = win + shift(win, 1)
    s2 = s1 + shift(s1, 2)
    s3 = s2 + shift(s2, 4)
    s4 = s3 + shift(s3, 8)
    return jnp.where(g == 0, s1, jnp.where(g == 1, s2, jnp.where(g == 2, s3, s4)))


def _count(r0, R, g, T_unused=None):
    t = r0 + lax.broadcasted_iota(jnp.int32, (R, 1), 0)
    return jnp.minimum(t + 1, jnp.left_shift(2, g)).astype(F32)


def pool_fwd(proj, pw, ps, dm, name):
    T, PG, R = dm.T, dm.PG, dm.tr

    def body(x_ref, pw_ref, ps_ref, pooled_ref, mixed_ref, yc_ref, buf):
        g = pl.program_id(0)
        _fill_halo_buf(buf, lambda r0: x_ref[pl.ds(r0, R), :], T, R, PG)

        def chunk(r, c):
            r0 = pl.multiple_of(r * R, 8)
            win = buf[pl.ds(r0, R + HALO), :]
            ws = _window_sum(win, g, _back)[HALO:, :]
            pooled = (ws / _count(r0, R, g) - win[HALO:, :]).astype(BF16)
            pooled_ref[pl.ds(r0, R), :] = pooled
            mixed = jnp.dot(pooled, pw_ref[...], preferred_element_type=F32)
            mixed_ref[pl.ds(r0, R), :] = mixed
            yc_ref[pl.ds(r0, R), :] = (mixed * ps_ref[...]).astype(BF16)
            return c

        lax.fori_loop(0, T // R, chunk, 0)

    own = pl.BlockSpec((T, PG), lambda g: (0, g))
    return pl.pallas_call(
        body, name=name, grid=(4,),
        in_specs=[pl.BlockSpec((T, PG), lambda g: (0, dm.o_pool // PG + g)), pl.BlockSpec((None, PG, PG), lambda g: (g, 0, 0)),
                  pl.BlockSpec((1, PG), lambda g: (0, g))],
        out_specs=[own, own, own],
        out_shape=[jax.ShapeDtypeStruct((T, dm.DP), BF16), jax.ShapeDtypeStruct((T, dm.DP), F32),
                   jax.ShapeDtypeStruct((T, dm.DP), BF16)],
        scratch_shapes=[pltpu.VMEM((T + 2 * HALO, PG), F32)],
        compiler_params=_params(("parallel",)),
    )(proj, pw, ps)


def pool_bwd(dyc, mixed, pooled, pw, ps, dm, name):
    T, PG, R = dm.T, dm.PG, dm.tr
    _TN = (((0,), (0,)), ((), ()))

    def body(dyc_ref, mixed_ref, pooled_ref, pw_ref, ps_ref, dx_ref, dpw_ref, dps_ref, qbuf, dpbuf):
        g = pl.program_id(0)
        zeros = jnp.zeros((HALO, PG), F32)
        qbuf[pl.ds(0, HALO), :] = zeros
        qbuf[pl.ds(HALO + T, HALO), :] = zeros
        dpw_ref[...] = jnp.zeros_like(dpw_ref)

        def first(r, dps):
            r0 = pl.multiple_of(r * R, 8)
            dyc = dyc_ref[pl.ds(r0, R), :]
            dps = dps + jnp.sum(dyc * mixed_ref[pl.ds(r0, R), :], axis=0, keepdims=True)
            dmb = (dyc * ps_ref[...]).astype(BF16)
            dpw_ref[...] += lax.dot_general(pooled_ref[pl.ds(r0, R), :], dmb, _TN, preferred_element_type=F32)
            dp = lax.dot_general(dmb, pw_ref[...], _NT, preferred_element_type=F32)
            dpbuf[pl.ds(r0, R), :] = dp
            qbuf[pl.ds(r0 + HALO, R), :] = dp / _count(r0, R, g)
            return dps

        dps_ref[...] = lax.fori_loop(0, T // R, first, jnp.zeros((1, PG), F32))

        def second(r, c):
            r0 = pl.multiple_of(r * R, 8)
            win = qbuf[pl.ds(r0 + HALO, R + HALO), :]
            ws = _window_sum(win, g, _fwd)[:R, :]
            dx_ref[pl.ds(r0, R), :] = (ws - dpbuf[pl.ds(r0, R), :]).astype(BF16)
            return c

        lax.fori_loop(0, T // R, second, 0)

    own = pl.BlockSpec((T, PG), lambda g: (0, g))
    return pl.pallas_call(
        body, name=name, grid=(4,),
        in_specs=[own, own, own, pl.BlockSpec((None, PG, PG), lambda g: (g, 0, 0)), pl.BlockSpec((1, PG), lambda g: (0, g))],
        out_specs=[own, pl.BlockSpec((None, PG, PG), lambda g: (g, 0, 0)), pl.BlockSpec((1, PG), lambda g: (0, g))],
        out_shape=[jax.ShapeDtypeStruct((T, dm.DP), BF16), jax.ShapeDtypeStruct((4, PG, PG), F32),
                   jax.ShapeDtypeStruct((1, dm.DP), F32)],
        scratch_shapes=[pltpu.VMEM((T + 2 * HALO, PG), F32), pltpu.VMEM((T, PG), F32)],
        compiler_params=_params(("parallel",)),
    )(dyc, mixed, pooled, pw, ps)


def _sigmoid(x):
    return 1.0 / (1.0 + jnp.exp(-x))


def merge_fwd(proj, A, B, C, dm, name):
    T, D, tr, tc = dm.T, dm.D, dm.tr, 512
    nc = D // tc

    def body(g0, g1, g2, a, b, c, out):
        out[...] = (_sigmoid(g0[...]) * a[...] + _sigmoid(g1[...]) * b[...] + _sigmoid(g2[...]) * c[...]).astype(BF16)

    gate = lambda k: pl.BlockSpec((tr, tc), lambda i, j: (i, k * nc + j))
    own = pl.BlockSpec((tr, tc), lambda i, j: (i, j))
    return pl.pallas_call(
        body, name=name, grid=(T // tr, nc),
        in_specs=[gate(0), gate(1), gate(2), own, own, own],
        out_specs=own,
        out_shape=jax.ShapeDtypeStruct((T, D), BF16),
        compiler_params=_params(("parallel", "parallel")),
    )(proj, proj, proj, A, B, C)


def merge_bwd(proj, A, B, C, dmerged, dm, name):
    T, D, tr, tc = dm.T, dm.D, dm.tr, 512
    nc = D // tc

    def body(g0, g1, g2, a, b, c, dmr, da, db, dc, dl0, dl1, dl2):
        d = dmr[...]
        for g_ref, y_ref, dy_ref, dl_ref in ((g0, a, da, dl0), (g1, b, db, dl1), (g2, c, dc, dl2)):
            s = _sigmoid(g_ref[...])
            dy_ref[...] = (d * s).astype(BF16)
            dl_ref[...] = (d * y_ref[...] * s * (1.0 - s)).astype(BF16)

    gate = lambda k: pl.BlockSpec((tr, tc), lambda i, j: (i, k * nc + j))
    own = pl.BlockSpec((tr, tc), lambda i, j: (i, j))
    o = jax.ShapeDtypeStruct((T, D), BF16)
    return pl.pallas_call(
        body, name=name, grid=(T // tr, nc),
        in_specs=[gate(0), gate(1), gate(2), own, own, own, own],
        out_specs=[own] * 6,
        out_shape=[o] * 6,
        compiler_params=_params(("parallel", "parallel")),
    )(proj, proj, proj, A, B, C, dmerged)


def loss_head(y, target, dm, name):
    T, D, tr = dm.T, dm.D, dm.tr

    def body(y_ref, t_ref, dy_ref, dyb_ref, loss_ref):
        i = pl.program_id(0)
        t = _row_ids(i, tr)
        real = (t >= dm.n_meta) & (t < dm.T_real)
        err = jnp.where(real, y_ref[...] - t_ref[...], 0.0)
        dy = err * (1.0 / D)
        dy_ref[...] = dy
        dyb_ref[...] = dy.astype(BF16)

        @pl.when(i == 0)
        def _():
            loss_ref[...] = jnp.zeros_like(loss_ref)

        loss_ref[...] += 0.5 * jnp.sum(jnp.sum(err * err, axis=-1, keepdims=True) * (1.0 / D))

    row = pl.BlockSpec((tr, D), lambda i: (i, 0))
    return pl.pallas_call(
        body, name=name, grid=(T // tr,),
        in_specs=[row, row],
        out_specs=[row, row, pl.BlockSpec((8, LANES), lambda i: (0, 0))],
        out_shape=[jax.ShapeDtypeStruct((T, D), F32), jax.ShapeDtypeStruct((T, D), BF16),
                   jax.ShapeDtypeStruct((8, LANES), F32)],
        compiler_params=_params(("arbitrary",)),
    )(y, target)


def adamw(w, m, v, parts, name):
    R, C = w.shape
    P = parts.shape[0]
    br = R
    for cand in (512, 256, 128, 64, 32, 16, 8):
        if R % cand == 0 and cand * C * 4 <= (1 << 20):
            br = cand
            break
    if R * C * 4 <= (1 << 20):
        br = R

    def body(w_ref, m_ref, v_ref, p_ref, g_ref, d_ref, nm_ref, nv_ref):
        g = p_ref[0].astype(F32)
        for k in range(1, P):
            g = g + p_ref[k].astype(F32)
        mm = ADAM_B1 * m_ref[...] + (1.0 - ADAM_B1) * g
        vv = ADAM_B2 * v_ref[...] + (1.0 - ADAM_B2) * (g * g)
        m_hat = mm / (1.0 - ADAM_B1 ** ADAM_STEP)
        v_hat = vv / (1.0 - ADAM_B2 ** ADAM_STEP)
        g_ref[...] = g
        d_ref[...] = -ADAM_LR * (m_hat / (jnp.sqrt(v_hat) + ADAM_EPS) + ADAM_WD * w_ref[...])
        nm_ref[...] = mm
        nv_ref[...] = vv

    blk = pl.BlockSpec((br, C), lambda i: (i, 0))
    o = jax.ShapeDtypeStruct((R, C), F32)
    return pl.pallas_call(
        body, name=name, grid=(R // br,),
        in_specs=[blk, blk, blk, pl.BlockSpec((P, br, C), lambda i: (0, i, 0))],
        out_specs=[blk] * 4,
        out_shape=[o] * 4,
        compiler_params=_params(("parallel",)),
    )(w, m, v, parts)


def sum_parts(parts, name):
    P, R, C = parts.shape

    def body(p_ref, o_ref):
        acc = p_ref[0]
        for k in range(1, P):
            acc = acc + p_ref[k]
        o_ref[...] = acc

    return pl.pallas_call(
        body, name=name, grid=(1,),
        in_specs=[pl.BlockSpec((P, R, C), lambda i: (0, 0, 0))],
        out_specs=pl.BlockSpec((R, C), lambda i: (0, 0)),
        out_shape=jax.ShapeDtypeStruct((R, C), F32),
        compiler_params=_params(("arbitrary",)),
    )(parts)


def add_sibling(parts, got, core, name):
    _, _, R, C = parts.shape
    br = _pick(R, (1024, 512, 256, 128, 64, 32, 16))

    def body(c_ref, a_ref, b_ref, o_ref):
        o_ref[...] = (a_ref[...].astype(F32) + b_ref[...].astype(F32)).astype(BF16)

    blk = pl.BlockSpec((None, br, C), lambda ch, i, c: (ch, i, 0))
    return pl.pallas_call(
        body, name=name,
        grid_spec=pltpu.PrefetchScalarGridSpec(
            num_scalar_prefetch=1, grid=(4, R // br),
            in_specs=[pl.BlockSpec((None, None, br, C), lambda ch, i, c: (ch, c[0], i, 0)), blk],
            out_specs=blk),
        out_shape=jax.ShapeDtypeStruct((4, R, C), BF16), compiler_params=_params(("parallel", "parallel")),
    )(core, parts, got)


def all_gather(arrs, name):
    n = len(arrs)

    def body(*refs):
        ins, outs = refs[:n], refs[n:2 * n]
        send_sems, recv_sems, local_sems = refs[2 * n:]
        x, y, c = _place()
        me, sibling = (x, y, c), (x, y, 1 - c)
        chips = [(1 - x, y), (x, 1 - y), (1 - x, 1 - y)]

        def copy(a, k, block, to, src=None):
            px, py, pc = block
            dst = outs[a].at[4 * px + 2 * py + pc]
            return pltpu.make_async_remote_copy(
                src_ref=dst if src is None else src, dst_ref=dst,
                send_sem=send_sems.at[7 * a + k], recv_sem=recv_sems.at[7 * a + k],
                device_id=to, device_id_type=_MESH)

        started = []
        for a in range(n):
            mine = pltpu.make_async_copy(ins[a], outs[a].at[4 * x + 2 * y + c], local_sems.at[a])
            mine.start()
            started.append(mine)
        sends = []
        for a in range(n):
            sends.append(copy(a, 0, me, sibling, src=ins[a]))
            for j, chip in enumerate(chips):
                sends.append(copy(a, 1 + j, me, (*chip, c), src=ins[a]))
        for cp in sends:
            cp.start()
        for j, chip in enumerate(chips):
            for a in range(n):
                copy(a, 1 + j, (*chip, c), me).wait_recv()
                fwd = copy(a, 4 + j, (*chip, c), sibling)
                fwd.start()
                sends.append(fwd)
        for a in range(n):
            copy(a, 0, sibling, me).wait_recv()
            for j, chip in enumerate(chips):
                copy(a, 4 + j, (*chip, 1 - c), me).wait_recv()
        for cp in sends:
            cp.wait_send()
        for cp in started:
            cp.wait()

    outs = pl.pallas_call(
        body, name=name,
        in_specs=[_HBM] * n, out_specs=[_HBM] * n,
        out_shape=[jax.ShapeDtypeStruct((8,) + a.shape, a.dtype) for a in arrs],
        scratch_shapes=[pltpu.SemaphoreType.DMA((7 * n,)), pltpu.SemaphoreType.DMA((7 * n,)), pltpu.SemaphoreType.DMA((n,))],
    )(*arrs)
    return list(outs)


def sibling_exchange(arrs, name):
    n = len(arrs)

    def body(*refs):
        ins, got = refs[:n], refs[n:2 * n]
        send_sems, recv_sems = refs[2 * n:]
        x, y, c = _place()
        work = []
        for a in range(n):
            for ch in range(4):
                cp = pltpu.make_async_remote_copy(
                    src_ref=ins[a].at[ch, 1 - c], dst_ref=got[a].at[ch],
                    send_sem=send_sems.at[4 * a + ch], recv_sem=recv_sems.at[4 * a + ch],
                    device_id=(x, y, 1 - c), device_id_type=_MESH)
                cp.start()
                work.append(cp)
        for cp in work:
            cp.wait()

    outs = pl.pallas_call(
        body, name=name,
        in_specs=[_HBM] * n, out_specs=[_HBM] * n,
        out_shape=[jax.ShapeDtypeStruct((4,) + a.shape[2:], a.dtype) for a in arrs],
        scratch_shapes=[pltpu.SemaphoreType.DMA((4 * n,)), pltpu.SemaphoreType.DMA((4 * n,))],
    )(*arrs)
    return list(outs)


def chip_exchange(arrs, name):
    n = len(arrs)

    def body(*refs):
        ins, outs = refs[:n], refs[n:2 * n]
        send_sems, recv_sems, local_sems = refs[2 * n:]
        x, y, c = _place()
        my_chip = 2 * x + y
        chips = [(1 - x, y), (x, 1 - y), (1 - x, 1 - y)]
        work = []
        for a in range(n):
            loc = pltpu.make_async_copy(ins[a].at[my_chip], outs[a].at[my_chip], local_sems.at[a])
            loc.start()
            work.append(loc)
        sends = []
        for j, (px, py) in enumerate(chips):
            for a in range(n):
                cp = pltpu.make_async_remote_copy(
                    src_ref=ins[a].at[2 * px + py], dst_ref=outs[a].at[my_chip],
                    send_sem=send_sems.at[3 * a + j], recv_sem=recv_sems.at[3 * a + j],
                    device_id=(px, py, c), device_id_type=_MESH)
                cp.start()
                sends.append(cp)
        for j, (px, py) in enumerate(chips):
            for a in range(n):
                pltpu.make_async_remote_copy(
                    src_ref=ins[a].at[my_chip], dst_ref=outs[a].at[2 * px + py],
                    send_sem=send_sems.at[3 * a + j], recv_sem=recv_sems.at[3 * a + j],
                    device_id=(px, py, c), device_id_type=_MESH).wait_recv()
        for cp in sends:
            cp.wait_send()
        for cp in work:
            cp.wait()

    outs = pl.pallas_call(
        body, name=name,
        in_specs=[_HBM] * n, out_specs=[_HBM] * n,
        out_shape=[jax.ShapeDtypeStruct(a.shape, a.dtype) for a in arrs],
        scratch_shapes=[pltpu.SemaphoreType.DMA((3 * n,)), pltpu.SemaphoreType.DMA((3 * n,)), pltpu.SemaphoreType.DMA((n,))],
    )(*arrs)
    return list(outs)


def _remote(src, dst, send_sem, recv_sem, to):
    return pltpu.make_async_remote_copy(src_ref=src, dst_ref=dst, send_sem=send_sem, recv_sem=recv_sem,
                                        device_id=to, device_id_type=_MESH)


def gather_ici(blocks):
    n = len(blocks)

    def copies(cins, couts, sems):
        send_sems, recv_sems, local_sems = sems
        x, y, c = _place()
        mine = 4 * x + 2 * y + c
        local, sends, recvs = [], [], []
        for a in range(n):
            local.append(pltpu.make_async_copy(cins[a], couts[a].at[mine], local_sems.at[a]))
            for j, (px, py) in enumerate([(1 - x, y), (x, 1 - y), (1 - x, 1 - y)]):
                k = 3 * a + j
                sends.append(_remote(cins[a], couts[a].at[mine], send_sems.at[k], recv_sems.at[k], (px, py, c)))
                recvs.append(_remote(cins[a], couts[a].at[4 * px + 2 * py + c], send_sems.at[k], recv_sems.at[k], (px, py, c)))
        return local, sends, recvs

    def start(cins, couts, sems):
        local, sends, _ = copies(cins, couts, sems)
        for cp in local + sends:
            cp.start()

    def finish(cins, couts, sems):
        local, sends, recvs = copies(cins, couts, sems)
        for cp in sends:
            cp.wait_send()
        for cp in recvs:
            cp.wait_recv()
        for cp in local:
            cp.wait()

    return Hosted(list(blocks), [jax.ShapeDtypeStruct((8,) + b.shape, b.dtype) for b in blocks],
                  [pltpu.SemaphoreType.DMA((3 * n,)), pltpu.SemaphoreType.DMA((3 * n,)), pltpu.SemaphoreType.DMA((n,))],
                  start, finish)


def fill_sibling(stks, name):
    n = len(stks)

    def body(*refs):
        ins, outs = refs[:n], refs[n:2 * n]
        send_sems, recv_sems = refs[2 * n:]
        x, y, c = _place()
        sends, recvs = [], []
        for a in range(n):
            for ch in range(4):
                k = 4 * a + ch
                sends.append(_remote(ins[a].at[2 * ch + c], outs[a].at[2 * ch + c], send_sems.at[k], recv_sems.at[k], (x, y, 1 - c)))
                recvs.append(_remote(ins[a].at[2 * ch + c], outs[a].at[2 * ch + 1 - c], send_sems.at[k], recv_sems.at[k], (x, y, 1 - c)))
        for cp in sends:
            cp.start()
        for cp in sends:
            cp.wait_send()
        for cp in recvs:
            cp.wait_recv()

    outs = pl.pallas_call(
        body, name=name,
        in_specs=[_HBM] * n, out_specs=[_HBM] * n,
        out_shape=[jax.ShapeDtypeStruct(s.shape, s.dtype) for s in stks],
        scratch_shapes=[pltpu.SemaphoreType.DMA((4 * n,)), pltpu.SemaphoreType.DMA((4 * n,))],
        input_output_aliases={a: a for a in range(n)},
    )(*stks)
    return list(outs)


def reduce_ici(arrs):
    n = len(arrs)

    def copies(cins, couts, sems):
        send_sems, recv_sems, local_sems = sems
        x, y, c = _place()
        my_chip = 2 * x + y
        local, sends, recvs = [], [], []
        for a in range(n):
            local.append(pltpu.make_async_copy(cins[a].at[my_chip], couts[a].at[my_chip], local_sems.at[a]))
            for j, (px, py) in enumerate([(1 - x, y), (x, 1 - y), (1 - x, 1 - y)]):
                k = 3 * a + j
                sends.append(_remote(cins[a].at[2 * px + py], couts[a].at[my_chip], send_sems.at[k], recv_sems.at[k], (px, py, c)))
                recvs.append(_remote(cins[a].at[my_chip], couts[a].at[2 * px + py], send_sems.at[k], recv_sems.at[k], (px, py, c)))
        return local, sends, recvs

    def start(cins, couts, sems):
        local, sends, _ = copies(cins, couts, sems)
        for cp in local + sends:
            cp.start()

    def finish(cins, couts, sems):
        local, sends, recvs = copies(cins, couts, sems)
        for cp in sends:
            cp.wait_send()
        for cp in recvs:
            cp.wait_recv()
        for cp in local:
            cp.wait()

    return Hosted(list(arrs), [jax.ShapeDtypeStruct(a.shape, a.dtype) for a in arrs],
                  [pltpu.SemaphoreType.DMA((3 * n,)), pltpu.SemaphoreType.DMA((3 * n,)), pltpu.SemaphoreType.DMA((n,))],
                  start, finish)


COL_SHARDED = ("w_in", "w_uq", "w_ukv", "w_branch_a", "w_branch_c", "w_up")
ROW_SHARDED = ("w_branch_b", "w_o", "w_down")
BIG = COL_SHARDED + ROW_SHARDED


def _full_from_stacked(name, st):
    if name in COL_SHARDED:
        _, L, K, n = st.shape
        return st.transpose(1, 2, 0, 3).reshape(L, K, 8 * n)
    if name in ROW_SHARDED:
        _, L, k, N = st.shape
        return st.transpose(1, 0, 2, 3).reshape(L, 8 * k, N)
    if name == "pool_w":
        _, L, G, pk, PG = st.shape
        return st.transpose(1, 2, 0, 3, 4).reshape(L, G, 8 * pk, PG)
    if name == "meta_tokens":
        _, M, n = st.shape
        return st.transpose(1, 0, 2).reshape(M, 8 * n)
    if name == "conv_w":
        _, L, W, n = st.shape
        return st.transpose(1, 2, 0, 3).reshape(L, W, 8 * n)
    raise ValueError(name)


def _shards_from_full(name, g):
    if name in COL_SHARDED:
        L, K, N = g.shape
        s = g.reshape(L, K, 8, N // 8).transpose(2, 0, 1, 3)
    else:
        L, K, N = g.shape
        s = g.reshape(L, 8, K // 8, N).transpose(1, 0, 2, 3)
    return s.reshape((4, 2) + s.shape[1:])


def _w_in_to_padded(w, dm):
    o3 = 3 * dm.DC + dm.QL + dm.KL
    o4 = o3 + dm.ROPE
    o5 = o4 + dm.DP
    pad = jnp.zeros(w.shape[:-1] + (256 - dm.ROPE,), w.dtype)
    return jnp.concatenate([w[..., o5:], w[..., :o3], w[..., o4:o5], w[..., o3:o4], pad], axis=-1)


def _w_in_from_padded(g, dm):
    o3 = 3 * dm.DC + dm.QL + dm.KL
    a = 3 * dm.D
    return jnp.concatenate([g[..., a:a + o3], g[..., dm.o_rope:dm.o_rope + dm.ROPE], g[..., dm.o_pool:dm.o_pool + dm.DP],
                            g[..., :a]], axis=-1)


def _pad_heads(w, dm):
    w = w.reshape(w.shape[:-1] + (dm.H, dm.QKH))
    w = jnp.pad(w, [(0, 0)] * (w.ndim - 1) + [(0, dm.HP - dm.QKH)])
    return w.reshape(w.shape[:-2] + (dm.H * dm.HP,))


def _unpad_heads(g, dm):
    g = g.reshape(g.shape[:-1] + (dm.H, dm.HP))[..., :dm.QKH]
    return g.reshape(g.shape[:-2] + (dm.H * dm.QKH,))


class _Weights:
    def __init__(self, w, dm, plan):
        self.w, self.dm, self.plan, self.full = w, dm, plan, {}

    def blocks(self, items):
        return [self.w[n][l:l + 1].astype(BF16) for n, l in items]

    def put(self, items, stacked):
        for (n, l), st in zip(items, stacked):
            f = _full_from_stacked(n, st)[0]
            if n == "w_in":
                f = _w_in_to_padded(f, self.dm)
            if n == "w_uq":
                f = _pad_heads(f, self.dm)
            self.full[(n, l)] = f

    def comm(self, tag):
        items = self.plan.get(tag)
        return gather_ici(self.blocks(items)) if items else None

    def arrived(self, tag, couts):
        self.put(self.plan[tag], fill_sibling(couts, f"fill_{tag}"))

    def __call__(self, n, l):
        return self.full[(n, l)]


class _Reducer:
    def __init__(self, dm, core):
        self.dm, self.core, self.q = dm, core, {}

    def prepare(self, tag, items, g):
        parts = []
        for n, l in items:
            f = g[n]
            if n == "w_in":
                f = _w_in_from_padded(f, self.dm)
            if n == "w_uq":
                f = _unpad_heads(f, self.dm)
            parts.append(_shards_from_full(n, f[None]))
        got = sibling_exchange(parts, f"reduce_sibling_{tag}")
        out = []
        for (n, l), a, b in zip(items, parts, got):
            C = a.shape[-1]
            out.append(add_sibling(a.reshape(4, 2, -1, C), b.reshape(4, -1, C), self.core,
                                   f"reduce_add_{n}_{l}").reshape(b.shape))
        return out

    def put(self, items, summed):
        for key, q in zip(items, summed):
            self.q[key] = q


def _layer_fwd(xin, l, ws, G, tabs, dm):
    nm = lambda s: f"l{l}_{s}"
    D = dm.D

    def mm(tag, *args, **kw):
        comm = ws.comm(nm(tag))
        if comm is None:
            return matmul(*args, nm(tag), **kw)
        res, couts = matmul(*args, nm(tag), comm=comm, **kw)
        ws.arrived(nm(tag), couts)
        return res

    h = rms_fwd(xin, 0, D, G["attn_norm"], dm, nm("rms1"))
    proj = mm("proj", h, ws("w_in", l), "nn", (F32,))
    ya = mixer_a_fwd(proj, G["conv_w"], dm, nm("mixa"))
    ql = rms_fwd(proj, dm.o_ql // dm.QL, dm.QL, G["q_lat_norm"], dm, nm("rms_q"))
    kl = rms_fwd(proj, dm.o_kl // dm.KL, dm.KL, G["kv_lat_norm"], dm, nm("rms_kv"))
    q0 = mm("uq", ql, ws("w_uq", l), "nn", (F32,))
    kv0 = mm("ukv", kl, ws("w_ukv", l), "nn", (F32,))
    q_s, k_s, v_s = qk_prep_fwd(q0, kv0, proj, G["q_norm"], G["k_norm"], tabs[0], tabs[1], dm, nm("qkprep"))
    comm = ws.comm(nm("attn"))
    if comm is None:
        o, ob, lse = attn_fwd(q_s, k_s, v_s, dm, nm("attn"))
    else:
        (o, ob, lse), couts = attn_fwd(q_s, k_s, v_s, dm, nm("attn"), comm=comm)
        ws.arrived(nm("attn"), couts)
    pooled, mixed, yc = pool_fwd(proj, ws("pool_w", l), G["pool_scale"], dm, nm("pool"))
    A = mm("br_a", ya, ws("w_branch_a", l), "nn", (F32,))
    B = mm("br_b", ob, ws("w_branch_b", l), "nn", (F32,))
    C = mm("br_c", yc, ws("w_branch_c", l), "nn", (F32,))
    merged = merge_fwd(proj, A, B, C, dm, nm("merge"))
    x1 = mm("wo", merged, ws("w_o", l), "nn", (F32,), extras=(xin,), epi=lambda acc, r: (acc + r,))
    h2 = rms_fwd(x1, 0, D, G["mlp_norm"], dm, nm("rms2"))
    up, act = mm("up", h2, ws("w_up", l), "nn", (F32, BF16), epi=lambda acc: (acc, jnp.square(jnp.maximum(acc, 0.0))))
    x2 = mm("down", act, ws("w_down", l), "nn", (F32,), extras=(x1,), epi=lambda acc, r: (acc + r,))
    saved = dict(xin=xin, h=h, proj=proj, ya=ya, ql=ql, kl=kl, q0=q0, kv0=kv0, q_s=q_s, k_s=k_s, v_s=v_s, o=o, ob=ob,
                 lse=lse, pooled=pooled, mixed=mixed, yc=yc, A=A, B=B, C=C, merged=merged, x1=x1, h2=h2, up=up, act=act)
    return x2, saved


def _layer_bwd(dx2, dx2b, S, l, ws, G, tabs, dm, pre_attn):
    nm = lambda s: f"l{l}_b_{s}"
    D, T = dm.D, dm.T
    g = {}
    d_up = matmul(dx2b, ws("w_down", l), "nt", (BF16,), nm("d_act"), extras=(S["up"],),
                  epi=lambda acc, up: (acc * (2.0 * jnp.maximum(up, 0.0)),))
    g["w_down"] = matmul(S["act"], dx2b, "tn", (BF16,), nm("g_down"))
    g["w_up"] = matmul(S["h2"], d_up, "tn", (BF16,), nm("g_up"))
    dh2 = matmul(d_up, ws("w_up", l), "nt", (F32,), nm("d_h2"))
    dx1, dx1b, g["mlp_norm"] = rms_bwd(S["x1"], 0, D, G["mlp_norm"], dh2, dx2, dm, nm("rms2"))
    dmerged = matmul(dx1b, ws("w_o", l), "nt", (F32,), nm("d_merged"))
    g["w_o"] = matmul(S["merged"], dx1b, "tn", (BF16,), nm("g_o"))
    dA, dB, dC, dl0, dl1, dl2 = merge_bwd(S["proj"], S["A"], S["B"], S["C"], dmerged, dm, nm("merge"))
    dya = matmul(dA, ws("w_branch_a", l), "nt", (F32,), nm("d_ya"))
    g["w_branch_a"] = matmul(S["ya"], dA, "tn", (BF16,), nm("g_a"))
    dyb = matmul(dB, ws("w_branch_b", l), "nt", (F32,), nm("d_yb"))
    g["w_branch_b"] = matmul(S["ob"], dB, "tn", (BF16,), nm("g_b"))
    dyc = matmul(dC, ws("w_branch_c", l), "nt", (F32,), nm("d_yc"))
    g["w_branch_c"] = matmul(S["yc"], dC, "tn", (BF16,), nm("g_c"))
    du, db, dc, g["conv_w"] = mixer_a_bwd(S["proj"], G["conv_w"], dya, dm, nm("mixa"))
    dpool, g["pool_w"], g["pool_scale"] = pool_bwd(dyc, S["mixed"], S["pooled"], ws("pool_w", l), G["pool_scale"], dm, nm("pool"))
    delta, dob = attn_delta(dyb, S["o"], dm, nm("delta"))
    nq = T // dm.tq
    comm_dq, done_dq, comm_dkv, done_dkv = pre_attn(g)
    dq = attn_bwd_dq(S["q_s"], S["k_s"], S["v_s"], dob, S["lse"], delta, dm, nm("attn_dq"), comm=comm_dq)
    if comm_dq is not None:
        dq, couts = dq
        done_dq(couts)
    dkv = attn_bwd_dkv(S["q_s"], S["k_s"], S["v_s"], dob, S["lse"].reshape(dm.H, nq, 1, dm.tq),
                       delta.reshape(dm.H, nq, 1, dm.tq), dm, nm("attn_dkv"), comm=comm_dkv)
    if comm_dkv is not None:
        dkv, couts = dkv
        done_dkv(couts)
    dk, dv = dkv
    dq0, dkv0, dkr, g["q_norm"], g["k_norm"] = qk_prep_bwd(S["q0"], S["kv0"], S["proj"], G["q_norm"], G["k_norm"],
                                                            tabs[0], tabs[1], dq, dk, dv, dm, nm("qkprep"))
    dql = matmul(dq0, ws("w_uq", l), "nt", (F32,), nm("d_ql"))
    g["w_uq"] = matmul(S["ql"], dq0, "tn", (BF16,), nm("g_uq"))
    dkl = matmul(dkv0, ws("w_ukv", l), "nt", (F32,), nm("d_kl"))
    g["w_ukv"] = matmul(S["kl"], dkv0, "tn", (BF16,), nm("g_ukv"))
    _, dqlat, g["q_lat_norm"] = rms_bwd(S["proj"], dm.o_ql // dm.QL, dm.QL, G["q_lat_norm"], dql, None, dm, nm("rms_q"))
    _, dkvlat, g["kv_lat_norm"] = rms_bwd(S["proj"], dm.o_kl // dm.KL, dm.KL, G["kv_lat_norm"], dkl, None, dm, nm("rms_kv"))
    dproj = jnp.concatenate([dl0, dl1, dl2, du, db, dc, dqlat, dkvlat, dpool, dkr.astype(BF16),
                             jnp.zeros((T, 128), BF16)], axis=1)
    dh = matmul(dproj, ws("w_in", l), "nt", (F32,), nm("d_h"))
    g["w_in"] = matmul(S["h"], dproj, "tn", (BF16,), nm("g_in"))
    dx, dxb, g["attn_norm"] = rms_bwd(S["xin"], 0, D, G["attn_norm"], dh, dx1, dm, nm("rms1"))
    return dx, dxb, g


WEIGHTS = ("meta_tokens", "attn_norm", "w_in", "conv_w", "q_lat_norm", "kv_lat_norm", "w_uq", "w_ukv", "q_norm", "k_norm",
           "pool_w", "pool_scale", "w_branch_a", "w_branch_b", "w_branch_c", "w_o", "mlp_norm", "w_up", "w_down")
SMALL = tuple(n for n in WEIGHTS if n not in BIG)


def kernel(x, meta_tokens, attn_norm, w_in, conv_w, q_lat_norm, kv_lat_norm, w_uq, w_ukv, q_norm, k_norm, pool_w, pool_scale, w_branch_a, w_branch_b, w_branch_c, w_o, mlp_norm, w_up, w_down, loss_target, m_meta_tokens, m_attn_norm, m_w_in, m_conv_w, m_q_lat_norm, m_kv_lat_norm, m_w_uq, m_w_ukv, m_q_norm, m_k_norm, m_pool_w, m_pool_scale, m_w_branch_a, m_w_branch_b, m_w_branch_c, m_w_o, m_mlp_norm, m_w_up, m_w_down, v_meta_tokens, v_attn_norm, v_w_in, v_conv_w, v_q_lat_norm, v_kv_lat_norm, v_w_uq, v_w_ukv, v_q_norm, v_k_norm, v_pool_w, v_pool_scale, v_w_branch_a, v_w_branch_b, v_w_branch_c, v_w_o, v_mlp_norm, v_w_up, v_w_down):
    w = dict(meta_tokens=meta_tokens, attn_norm=attn_norm, w_in=w_in, conv_w=conv_w, q_lat_norm=q_lat_norm,
             kv_lat_norm=kv_lat_norm, w_uq=w_uq, w_ukv=w_ukv, q_norm=q_norm, k_norm=k_norm, pool_w=pool_w,
             pool_scale=pool_scale, w_branch_a=w_branch_a, w_branch_b=w_branch_b, w_branch_c=w_branch_c, w_o=w_o,
             mlp_norm=mlp_norm, w_up=w_up, w_down=w_down)
    m = dict(meta_tokens=m_meta_tokens, attn_norm=m_attn_norm, w_in=m_w_in, conv_w=m_conv_w, q_lat_norm=m_q_lat_norm,
             kv_lat_norm=m_kv_lat_norm, w_uq=m_w_uq, w_ukv=m_w_ukv, q_norm=m_q_norm, k_norm=m_k_norm, pool_w=m_pool_w,
             pool_scale=m_pool_scale, w_branch_a=m_w_branch_a, w_branch_b=m_w_branch_b, w_branch_c=m_w_branch_c, w_o=m_w_o,
             mlp_norm=m_mlp_norm, w_up=m_w_up, w_down=m_w_down)
    v = dict(meta_tokens=v_meta_tokens, attn_norm=v_attn_norm, w_in=v_w_in, conv_w=v_conv_w, q_lat_norm=v_q_lat_norm,
             kv_lat_norm=v_kv_lat_norm, w_uq=v_w_uq, w_ukv=v_w_ukv, q_norm=v_q_norm, k_norm=v_k_norm, pool_w=v_pool_w,
             pool_scale=v_pool_scale, w_branch_a=v_w_branch_a, w_branch_b=v_w_branch_b, w_branch_c=v_w_branch_c, w_o=v_w_o,
             mlp_norm=v_mlp_norm, w_up=v_w_up, w_down=v_w_down)
    L = attn_norm.shape[0]
    assert L == 2, "the gather / reduce schedule below is written for two layers"
    seq, D = x.shape[1], x.shape[2]
    n_meta = meta_tokens.shape[0]
    dm = Dims(D, seq, n_meta)
    T = dm.T
    me = 4 * lax.axis_index("x") + 2 * lax.axis_index("y") + lax.axis_index("c")
    core = lax.axis_index("c").astype(jnp.int32).reshape(1)

    plan = {
        "l0_proj": [("w_branch_a", 0), ("w_branch_b", 0), ("w_branch_c", 0), ("w_o", 0), ("w_up", 0)],
        "l0_attn": [("w_down", 0), ("w_in", 1), ("w_uq", 1), ("w_ukv", 1), ("w_branch_a", 1), ("w_branch_b", 1),
                    ("w_branch_c", 1), ("w_o", 1)],
        "l0_up": [("w_up", 1)],
        "l0_down": [("w_down", 1)],
    }
    ws = _Weights(w, dm, plan)
    first = [("w_in", 0), ("w_uq", 0), ("w_ukv", 0), ("pool_w", 0), ("pool_w", 1)]
    ws.put(first, all_gather(ws.blocks(first), "gather_first"))
    st_small = all_gather([w["meta_tokens"], w["conv_w"]], "gather_small")
    meta_full = _full_from_stacked("meta_tokens", st_small[0])
    conv_full = _full_from_stacked("conv_w", st_small[1])
    pad_gain = lambda gn: jnp.pad(gn, (0, dm.HP - dm.QKH))

    def gains(l):
        G = {n: w[n][l][None, :] for n in ("attn_norm", "q_lat_norm", "kv_lat_norm", "pool_scale", "mlp_norm")}
        G["q_norm"], G["k_norm"] = pad_gain(w["q_norm"][l])[None, :], pad_gain(w["k_norm"][l])[None, :]
        G["conv_w"] = conv_full[l]
        return G

    Gs = [gains(l) for l in range(L)]

    pos = jnp.arange(dm.T_real, dtype=F32)
    inv = 10000.0 ** (-jnp.arange(0, dm.ROPE, 2, dtype=F32) / dm.ROPE)
    ang = pos[:, None] * inv[None, :]
    zpad = jnp.zeros((dm.T_real, LANES - dm.ROPE), F32)
    rows = ((0, T - dm.T_real), (0, 0))
    tabs = (jnp.pad(jnp.concatenate([jnp.cos(ang), jnp.cos(ang), zpad], 1), rows),
            jnp.pad(jnp.concatenate([jnp.sin(ang), jnp.sin(ang), zpad], 1), rows))

    xs = jnp.concatenate([meta_full, x[0], jnp.zeros((T - dm.T_real, D), F32)], axis=0)
    target = jnp.pad(loss_target[0], ((n_meta, T - dm.T_real), (0, 0)))
    saved = []
    for l in range(L):
        xs, S = _layer_fwd(xs, l, ws, Gs[l], tabs, dm)
        saved.append(S)
    dx, dxb, loss_acc = loss_head(xs, target, dm, "loss_head")

    red = _Reducer(dm, core)
    early = lambda l: [(n, l) for n in ("w_down", "w_up", "w_o", "w_branch_a", "w_branch_b", "w_branch_c")]
    late = lambda l: [(n, l) for n in ("w_uq", "w_ukv", "w_in")]
    grads = [None] * L

    def pre_attn_1(g):
        return (reduce_ici(red.prepare("e1", early(1), g)), lambda couts: red.put(early(1), couts), None, None)

    dx, dxb, grads[1] = _layer_bwd(dx, dxb, saved[1], 1, ws, Gs[1], tabs, dm, pre_attn_1)
    late1 = red.prepare("l1", late(1), grads[1])

    def pre_attn_0(g):
        return (reduce_ici(late1), lambda couts: red.put(late(1), couts),
                reduce_ici(red.prepare("e0", early(0), g)), lambda couts: red.put(early(0), couts))

    dx, dxb, grads[0] = _layer_bwd(dx, dxb, saved[0], 0, ws, Gs[0], tabs, dm, pre_attn_0)
    red.put(late(0), chip_exchange(red.prepare("l0", late(0), grads[0]), "reduce_chips_last"))
    grad_x = dx[n_meta:dm.T_real][None]
    summed = [jnp.concatenate([red.q[(n, l)] for l in range(L)], axis=1) for n in BIG]

    out_g, out_d, out_m, out_v = {}, {}, {}, {}

    def update(n, parts3):
        shp = w[n].shape
        C = shp[-1]
        res = adamw(w[n].reshape(-1, C), m[n].reshape(-1, C), v[n].reshape(-1, C), parts3, f"adamw_{n}")
        out_g[n], out_d[n], out_m[n], out_v[n] = [r.reshape(shp) for r in res]

    for n, q in zip(BIG, summed):
        update(n, q.reshape(4, -1, q.shape[-1]))

    small_full = {
        "meta_tokens": dx[:n_meta],
        "conv_w": jnp.stack([grads[l]["conv_w"] for l in range(L)]),
        "pool_w": jnp.stack([grads[l]["pool_w"] for l in range(L)]),
        "q_norm": jnp.stack([grads[l]["q_norm"][0, :dm.QKH] for l in range(L)]),
        "k_norm": jnp.stack([grads[l]["k_norm"][0, :dm.QKH] for l in range(L)]),
    }
    for n in ("attn_norm", "q_lat_norm", "kv_lat_norm", "pool_scale", "mlp_norm"):
        small_full[n] = jnp.stack([grads[l][n][0] for l in range(L)])
    flat = jnp.concatenate([small_full[n].reshape(-1) for n in SMALL] + [loss_acc[0, :1]])
    n_flat = flat.shape[0]
    rows_small = -(-n_flat // (8 * LANES)) * 8
    flat = jnp.pad(flat, (0, rows_small * LANES - n_flat)).reshape(rows_small, LANES)
    total = sum_parts(all_gather([flat], "gather_small_grads")[0], "sum_small").reshape(-1)
    off = 0
    for n in SMALL:
        size = math.prod(small_full[n].shape)
        gsum = total[off:off + size].reshape(small_full[n].shape)
        off += size
        if n in ("meta_tokens", "conv_w"):
            blk = w[n].shape[-1]
            gsum = lax.dynamic_slice_in_dim(gsum, me * blk, blk, axis=gsum.ndim - 1)
        elif n == "pool_w":
            blk = w[n].shape[2]
            gsum = lax.dynamic_slice_in_dim(gsum, me * blk, blk, axis=2)
        update(n, gsum.reshape(1, -1, gsum.shape[-1]))
    loss = total[off]

    return (loss, grad_x, *[out_g[n] for n in WEIGHTS], *[out_d[n] for n in WEIGHTS],
            *[out_m[n] for n in WEIGHTS], *[out_v[n] for n in WEIGHTS])
```

```python
import functools
import math

import jax
import jax.numpy as jnp
from jax import lax
from jax.experimental import pallas as pl
from jax.experimental.pallas import tpu as pltpu

F32 = jnp.float32
BF16 = jnp.bfloat16

VMEM_LIMIT_BYTES = 56 * 1024 * 1024
LANES = 128
EPS = 1e-6
HALO = 16

ADAM_LR = 0.001
ADAM_B1 = 0.9
ADAM_B2 = 0.999
ADAM_EPS = 1e-08
ADAM_WD = 0.01
ADAM_STEP = 10


def _params(sem):
    return pltpu.CompilerParams(dimension_semantics=sem, vmem_limit_bytes=VMEM_LIMIT_BYTES)


def _pick(n, prefs):
    for p in prefs:
        if p <= n and n % p == 0:
            return p
    return n


_MESH = pl.DeviceIdType.MESH
_HBM = pl.BlockSpec(memory_space=pltpu.HBM)


def _place():
    return lax.axis_index("x"), lax.axis_index("y"), lax.axis_index("c")


class Hosted:
    def __init__(self, ins, out_shapes, sems, start, finish):
        self.ins, self.out_shapes, self.sems, self.start, self.finish = ins, out_shapes, sems, start, finish


def _hosted_call(body, *, name, grid, in_specs, out_specs, out_shape, scratch_shapes, semantics, args, comm):
    n_in, n_out, n_scr = len(in_specs), len(out_specs), len(scratch_shapes)
    if comm is None:
        outs = pl.pallas_call(body, name=name, grid=grid, in_specs=in_specs, out_specs=out_specs, out_shape=out_shape,
                              scratch_shapes=scratch_shapes, compiler_params=_params(semantics))(*args)
        return list(outs), []
    ci, co = len(comm.ins), len(comm.out_shapes)

    def hosting(*refs):
        ins, cins = refs[:n_in], refs[n_in:n_in + ci]
        outs = refs[n_in + ci:n_in + ci + n_out]
        couts = refs[n_in + ci + n_out:n_in + ci + n_out + co]
        scr = refs[n_in + ci + n_out + co:n_in + ci + n_out + co + n_scr]
        csems = refs[n_in + ci + n_out + co + n_scr:]
        ids = [pl.program_id(d) for d in range(len(grid))]
        first = functools.reduce(jnp.logical_and, [i == 0 for i in ids])
        last = functools.reduce(jnp.logical_and, [i == g - 1 for i, g in zip(ids, grid)])

        @pl.when(first)
        def _():
            comm.start(cins, couts, csems)

        body(*ins, *outs, *scr)

        @pl.when(last)
        def _():
            comm.finish(cins, couts, csems)

    outs = pl.pallas_call(
        hosting, name=name, grid=grid,
        in_specs=list(in_specs) + [_HBM] * ci, out_specs=list(out_specs) + [_HBM] * co,
        out_shape=list(out_shape) + list(comm.out_shapes),
        scratch_shapes=list(scratch_shapes) + list(comm.sems),
        compiler_params=_params(("arbitrary",) * len(grid)),
    )(*args, *comm.ins)
    return list(outs[:n_out]), list(outs[n_out:])


MXU_FLOPS = 750e12
HBM_BYTES_PER_S = 3.0e12
ACC_RMW_BYTES_PER_S = 8e12
GRID_STEP_S = 0.4e-6
VMEM_COMPILER_RESERVE_BYTES = 8 * 1024 * 1024
MAX_TILE_ROWS, MAX_TILE_COLS = 2112, 2304


def _divisors(n, step):
    return [d for d in range(step, n + 1, step) if n % d == 0]


def _matmul_tiles(M, N, K, mode, out_sizes, n_extra):
    budget = VMEM_LIMIT_BYTES - VMEM_COMPILER_RESERVE_BYTES
    best = None
    for tk in _divisors(K, 16 if mode == "tn" else LANES):
        for tm in _divisors(M, LANES if mode == "tn" else 16):
            if tm > MAX_TILE_ROWS:
                continue
            for tn in _divisors(N, LANES):
                if tn > MAX_TILE_COLS:
                    continue
                need = 4 * tm * tk + 4 * tk * tn + 4 * tm * tn + 2 * tm * tn * sum(out_sizes) + 8 * n_extra * tm * tn
                if need > budget:
                    continue
                nk = K // tk
                steps = (M // tm) * (N // tn) * nk
                t_mxu = 2 * M * N * K / MXU_FLOPS + (nk > 1) * (8 * M * N * nk) / ACC_RMW_BYTES_PER_S
                t_hbm = (2 * M * K * (N // tn) + 2 * K * N * (M // tm) + M * N * (sum(out_sizes) + 4 * n_extra)) / HBM_BYTES_PER_S
                t = max(t_mxu, t_hbm) + steps * GRID_STEP_S
                if best is None or t < best[0]:
                    best = (t, tm, tn, tk)
    assert best is not None, (M, N, K, mode)
    return best[1:]


def matmul(a, b, mode, out_dtypes, name, extras=(), epi=None, tm=None, tn=None, tk=None, comm=None):
    if mode == "nn":
        (M, K), (K2, N) = a.shape, b.shape
    elif mode == "nt":
        (M, K), (N, K2) = a.shape, b.shape
    else:
        (K, M), (K2, N) = a.shape, b.shape
    assert K == K2, (a.shape, b.shape, mode)
    if not (tm and tn and tk):
        tm, tn, tk = _matmul_tiles(M, N, K, mode, [jnp.dtype(d).itemsize for d in out_dtypes], len(extras))
    nk = K // tk
    dims = {"nn": (((1,), (0,)), ((), ())), "nt": (((1,), (1,)), ((), ())), "tn": (((0,), (0,)), ((), ()))}[mode]
    n_extra, n_out = len(extras), len(out_dtypes)

    def body(*refs):
        a_ref, b_ref = refs[0], refs[1]
        extra_refs = refs[2:2 + n_extra]
        out_refs = refs[2 + n_extra:2 + n_extra + n_out]

        def finish(acc):
            outs = (acc,) if epi is None else epi(acc, *[r[...] for r in extra_refs])
            for o_ref, o in zip(out_refs, outs):
                o_ref[...] = o.astype(o_ref.dtype)

        part = lax.dot_general(a_ref[...], b_ref[...], dims, preferred_element_type=F32)
        if nk == 1:
            finish(part)
            return
        acc_ref = refs[-1]
        k = pl.program_id(2)

        @pl.when(k == 0)
        def _():
            acc_ref[...] = part

        @pl.when(k > 0)
        def _():
            acc_ref[...] += part

        @pl.when(k == nk - 1)
        def _():
            finish(acc_ref[...])

    a_spec = {"nn": pl.BlockSpec((tm, tk), lambda i, j, k: (i, k)),
              "nt": pl.BlockSpec((tm, tk), lambda i, j, k: (i, k)),
              "tn": pl.BlockSpec((tk, tm), lambda i, j, k: (k, i))}[mode]
    b_spec = {"nn": pl.BlockSpec((tk, tn), lambda i, j, k: (k, j)),
              "nt": pl.BlockSpec((tn, tk), lambda i, j, k: (j, k)),
              "tn": pl.BlockSpec((tk, tn), lambda i, j, k: (k, j))}[mode]
    o_spec = pl.BlockSpec((tm, tn), lambda i, j, k: (i, j))
    outs, couts = _hosted_call(
        body, name=name, grid=(M // tm, N // tn, nk),
        in_specs=[a_spec, b_spec] + [o_spec] * n_extra,
        out_specs=[o_spec] * n_out,
        out_shape=[jax.ShapeDtypeStruct((M, N), d) for d in out_dtypes],
        scratch_shapes=[pltpu.VMEM((tm, tn), F32)] if nk > 1 else [],
        semantics=("parallel", "parallel", "arbitrary"), args=(a, b, *extras), comm=comm)
    res = outs[0] if n_out == 1 else outs
    return res if comm is None else (res, couts)


class Dims:
    def __init__(self, d_model, seq, n_meta):
        self.D = d_model
        self.n_meta = n_meta
        self.T_real = seq + n_meta
        self.T = -(-self.T_real // LANES) * LANES
        self.H = d_model // 128
        self.DC = d_model // 2
        self.DP = d_model // 2
        self.PG = self.DP // 4
        self.QL = 512
        self.KL = 512
        self.ROPE = 64
        self.NOPE = 128
        self.QKH = 192
        self.HP = 256
        self.DFF = 4 * d_model
        self.o_gate = 0
        self.o_u = 3 * d_model
        self.o_b = self.o_u + self.DC
        self.o_c = self.o_b + self.DC
        self.o_ql = self.o_c + self.DC
        self.o_kl = self.o_ql + self.QL
        self.o_pool = self.o_kl + self.KL
        self.o_rope = self.o_pool + self.DP
        self.NIN = self.o_rope + 256
        self.tr = _pick(self.T, (384, 256, 128))
        self.tq = _pick(self.T, (384, 256, 128))


def _row_ids(i, tr):
    return i * tr + lax.broadcasted_iota(jnp.int32, (tr, 1), 0)


def rms_fwd(x, col_block, width, g, dm, name):
    tr = dm.tr

    def body(x_ref, g_ref, y_ref):
        xv = x_ref[...]
        r = lax.rsqrt(jnp.mean(xv * xv, axis=-1, keepdims=True) + EPS)
        y_ref[...] = (xv * r * g_ref[...]).astype(y_ref.dtype)

    return pl.pallas_call(
        body, name=name, grid=(dm.T // tr,),
        in_specs=[pl.BlockSpec((tr, width), lambda i: (i, col_block)), pl.BlockSpec((1, width), lambda i: (0, 0))],
        out_specs=pl.BlockSpec((tr, width), lambda i: (i, 0)),
        out_shape=jax.ShapeDtypeStruct((dm.T, width), BF16),
        compiler_params=_params(("parallel",)),
    )(x, g.reshape(1, width))


def rms_bwd(x, col_block, width, g, dy, dres, dm, name):
    tr = dm.tr
    has_res = dres is not None

    def body(*refs):
        if has_res:
            x_ref, g_ref, dy_ref, dres_ref, dx_ref, dxb_ref, dg_ref = refs
        else:
            x_ref, g_ref, dy_ref, dx_ref, dxb_ref, dg_ref = refs
        xv, dyv = x_ref[...], dy_ref[...]
        r = lax.rsqrt(jnp.mean(xv * xv, axis=-1, keepdims=True) + EPS)
        gdy = dyv * g_ref[...]
        dx = r * gdy - xv * (r * r * r) * jnp.mean(xv * gdy, axis=-1, keepdims=True)
        if has_res:
            dx = dx + dres_ref[...]
        dx_ref[...] = dx
        dxb_ref[...] = dx.astype(BF16)

        @pl.when(pl.program_id(0) == 0)
        def _():
            dg_ref[...] = jnp.zeros_like(dg_ref)

        dg_ref[...] += jnp.sum(dyv * xv * r, axis=0, keepdims=True)

    row = pl.BlockSpec((tr, width), lambda i: (i, 0))
    in_specs = [pl.BlockSpec((tr, width), lambda i: (i, col_block)), pl.BlockSpec((1, width), lambda i: (0, 0)), row]
    args = [x, g.reshape(1, width), dy]
    if has_res:
        in_specs.append(row)
        args.append(dres)
    return pl.pallas_call(
        body, name=name, grid=(dm.T // tr,),
        in_specs=in_specs,
        out_specs=[row, row, pl.BlockSpec((1, width), lambda i: (0, 0))],
        out_shape=[jax.ShapeDtypeStruct((dm.T, width), F32), jax.ShapeDtypeStruct((dm.T, width), BF16),
                   jax.ShapeDtypeStruct((1, width), F32)],
        compiler_params=_params(("arbitrary",)),
    )(*args)


def _fill_halo_buf(buf, src_fn, T, R, width):
    zeros = jnp.zeros((HALO, width), F32)
    buf[pl.ds(0, HALO), :] = zeros
    buf[pl.ds(HALO + T, HALO), :] = zeros

    def fill(r, c):
        r0 = pl.multiple_of(r * R, 8)
        buf[pl.ds(r0 + HALO, R), :] = src_fn(r0)
        return c

    lax.fori_loop(0, T // R, fill, 0)


def _back(win, sh):
    return pltpu.roll(win, sh, 0)


def _fwd(win, sh):
    return pltpu.roll(win, win.shape[0] - sh, 0)


def mixer_a_fwd(proj, conv_w, dm, name):
    T, cw = dm.T, 128
    R = dm.tr
    nb = dm.DC // cw

    def body(u_ref, b_ref, c_ref, w_ref, ya_ref, buf):
        _fill_halo_buf(buf, lambda r0: c_ref[pl.ds(r0, R), :] * u_ref[pl.ds(r0, R), :], T, R, cw)
        w0, w1, w2 = w_ref[0:1, :], w_ref[1:2, :], w_ref[2:3, :]

        def chunk(r, c):
            r0 = pl.multiple_of(r * R, 8)
            win = buf[pl.ds(r0, R + HALO), :]
            cv = w2 * win + w1 * _back(win, 1) + w0 * _back(win, 2)
            ya_ref[pl.ds(r0, R), :] = (b_ref[pl.ds(r0, R), :] * cv[HALO:, :]).astype(BF16)
            return c

        lax.fori_loop(0, T // R, chunk, 0)

    col = lambda off: pl.BlockSpec((T, cw), lambda j: (0, off // cw + j))
    return pl.pallas_call(
        body, name=name, grid=(nb,),
        in_specs=[col(dm.o_u), col(dm.o_b), col(dm.o_c), pl.BlockSpec((3, cw), lambda j: (0, j))],
        out_specs=pl.BlockSpec((T, cw), lambda j: (0, j)),
        out_shape=jax.ShapeDtypeStruct((T, dm.DC), BF16),
        scratch_shapes=[pltpu.VMEM((T + 2 * HALO, cw), F32)],
        compiler_params=_params(("parallel",)),
    )(proj, proj, proj, conv_w)


def mixer_a_bwd(proj, conv_w, dya, dm, name):
    T, cw = dm.T, 128
    R = dm.tr
    nb = dm.DC // cw

    def body(u_ref, b_ref, c_ref, w_ref, dya_ref, du_ref, db_ref, dc_ref, dw_ref, sbuf, gbuf):
        _fill_halo_buf(sbuf, lambda r0: c_ref[pl.ds(r0, R), :] * u_ref[pl.ds(r0, R), :], T, R, cw)
        _fill_halo_buf(gbuf, lambda r0: dya_ref[pl.ds(r0, R), :] * b_ref[pl.ds(r0, R), :], T, R, cw)
        w0, w1, w2 = w_ref[0:1, :], w_ref[1:2, :], w_ref[2:3, :]

        def chunk(r, acc):
            a0, a1, a2 = acc
            r0 = pl.multiple_of(r * R, 8)
            swin = sbuf[pl.ds(r0, R + HALO), :]
            s0, s1, s2 = swin[HALO:, :], _back(swin, 1)[HALO:, :], _back(swin, 2)[HALO:, :]
            gwin = gbuf[pl.ds(r0 + HALO, R + HALO), :]
            g0, g1, g2 = gwin[:R, :], _fwd(gwin, 1)[:R, :], _fwd(gwin, 2)[:R, :]
            cv = w2 * s0 + w1 * s1 + w0 * s2
            ds = w2 * g0 + w1 * g1 + w0 * g2
            db_ref[pl.ds(r0, R), :] = (dya_ref[pl.ds(r0, R), :] * cv).astype(BF16)
            du_ref[pl.ds(r0, R), :] = (ds * c_ref[pl.ds(r0, R), :]).astype(BF16)
            dc_ref[pl.ds(r0, R), :] = (ds * u_ref[pl.ds(r0, R), :]).astype(BF16)
            a2 = a2 + jnp.sum(g0 * s0, axis=0, keepdims=True)
            a1 = a1 + jnp.sum(g0 * s1, axis=0, keepdims=True)
            a0 = a0 + jnp.sum(g0 * s2, axis=0, keepdims=True)
            return a0, a1, a2

        z = jnp.zeros((1, cw), F32)
        a0, a1, a2 = lax.fori_loop(0, T // R, chunk, (z, z, z))
        dw_ref[0:1, :] = a0
        dw_ref[1:2, :] = a1
        dw_ref[2:3, :] = a2

    col = lambda off: pl.BlockSpec((T, cw), lambda j: (0, off // cw + j))
    own = pl.BlockSpec((T, cw), lambda j: (0, j))
    o = jax.ShapeDtypeStruct((T, dm.DC), BF16)
    return pl.pallas_call(
        body, name=name, grid=(nb,),
        in_specs=[col(dm.o_u), col(dm.o_b), col(dm.o_c), pl.BlockSpec((3, cw), lambda j: (0, j)), own],
        out_specs=[own, own, own, pl.BlockSpec((3, cw), lambda j: (0, j))],
        out_shape=[o, o, o, jax.ShapeDtypeStruct((3, dm.DC), F32)],
        scratch_shapes=[pltpu.VMEM((T + 2 * HALO, cw), F32), pltpu.VMEM((T + 2 * HALO, cw), F32)],
        compiler_params=_params(("parallel",)),
    )(proj, proj, proj, conv_w, dya)


def _rope(x, C, S):
    return x * C + (pltpu.roll(x, 32, 1) - pltpu.roll(x, 96, 1)) * S


def _rope_t(dy, C, S):
    return dy * C + (pltpu.roll(dy, 96, 1) - pltpu.roll(dy, 32, 1)) * S


def qk_prep_fwd(q0, kv0, proj, qn, kn, C, S, dm, name):
    T, H, tr = dm.T, dm.H, dm.tr
    inv = 1.0 / dm.QKH

    def body(q0_ref, kv_ref, kr_ref, qn_ref, kn_ref, c_ref, s_ref, q_ref, k_ref, v_ref):
        Cv, Sv = c_ref[...], s_ref[...]
        qa, qb = q0_ref[:, :128], q0_ref[:, 128:]
        r = lax.rsqrt((jnp.sum(qa * qa, -1, keepdims=True) + jnp.sum(qb * qb, -1, keepdims=True)) * inv + EPS)
        q_ref[:, :128] = (qa * r * qn_ref[:, :128]).astype(BF16)
        q_ref[:, 128:] = _rope(qb * r * qn_ref[:, 128:], Cv, Sv).astype(BF16)
        ka, kb = kv_ref[:, :128], kr_ref[...]
        r = lax.rsqrt((jnp.sum(ka * ka, -1, keepdims=True) + jnp.sum(kb * kb, -1, keepdims=True)) * inv + EPS)
        k_ref[:, :128] = (ka * r * kn_ref[:, :128]).astype(BF16)
        k_ref[:, 128:] = _rope(kb * r * kn_ref[:, 128:], Cv, Sv).astype(BF16)
        v_ref[...] = kv_ref[:, 128:].astype(BF16)

    head = pl.BlockSpec((tr, 256), lambda i, h: (i, h))
    gain = pl.BlockSpec((1, 256), lambda i, h: (0, 0))
    tab = pl.BlockSpec((tr, 128), lambda i, h: (i, 0))
    return pl.pallas_call(
        body, name=name, grid=(T // tr, H),
        in_specs=[head, head, pl.BlockSpec((tr, 128), lambda i, h: (i, dm.o_rope // 128)), gain, gain, tab, tab],
        out_specs=[head, head, pl.BlockSpec((tr, 128), lambda i, h: (i, h))],
        out_shape=[jax.ShapeDtypeStruct((T, H * 256), BF16), jax.ShapeDtypeStruct((T, H * 256), BF16),
                   jax.ShapeDtypeStruct((T, H * 128), BF16)],
        compiler_params=_params(("parallel", "parallel")),
    )(q0, kv0, proj, qn, kn, C, S)


def qk_prep_bwd(q0, kv0, proj, qn, kn, C, S, dq, dk, dv, dm, name):
    T, H, tr = dm.T, dm.H, dm.tr
    inv = 1.0 / dm.QKH

    def body(q0_ref, kv_ref, kr_ref, qn_ref, kn_ref, c_ref, s_ref, dq_ref, dk_ref, dv_ref,
             dq0_ref, dkv_ref, dkr_ref, dqn_ref, dkn_ref):
        i, h = pl.program_id(0), pl.program_id(1)
        Cv, Sv = c_ref[...], s_ref[...]

        def norm_bwd(xa, xb, ga, gb, dya, dyb):
            r = lax.rsqrt((jnp.sum(xa * xa, -1, keepdims=True) + jnp.sum(xb * xb, -1, keepdims=True)) * inv + EPS)
            dzb = _rope_t(dyb, Cv, Sv)
            gda, gdb = ga * dya, gb * dzb
            dot = (jnp.sum(xa * gda, -1, keepdims=True) + jnp.sum(xb * gdb, -1, keepdims=True)) * inv
            r3 = r * r * r
            dxa = r * gda - xa * r3 * dot
            dxb = r * gdb - xb * r3 * dot
            dga = jnp.sum(dya * xa * r, axis=0, keepdims=True)
            dgb = jnp.sum(dzb * xb * r, axis=0, keepdims=True)
            return dxa, dxb, dga, dgb

        @pl.when((i == 0) & (h == 0))
        def _():
            dqn_ref[...] = jnp.zeros_like(dqn_ref)
            dkn_ref[...] = jnp.zeros_like(dkn_ref)

        dxa, dxb, dga, dgb = norm_bwd(q0_ref[:, :128], q0_ref[:, 128:], qn_ref[:, :128], qn_ref[:, 128:],
                                      dq_ref[:, :128], dq_ref[:, 128:])
        dq0_ref[:, :128] = dxa.astype(BF16)
        dq0_ref[:, 128:] = dxb.astype(BF16)
        dqn_ref[:, :128] += dga
        dqn_ref[:, 128:] += dgb
        dxa, dxb, dga, dgb = norm_bwd(kv_ref[:, :128], kr_ref[...], kn_ref[:, :128], kn_ref[:, 128:],
                                      dk_ref[:, :128], dk_ref[:, 128:])
        dkv_ref[:, :128] = dxa.astype(BF16)
        dkv_ref[:, 128:] = dv_ref[...].astype(BF16)
        dkn_ref[:, :128] += dga
        dkn_ref[:, 128:] += dgb

        @pl.when(h == 0)
        def _():
            dkr_ref[...] = jnp.zeros_like(dkr_ref)

        dkr_ref[...] += dxb

    head = pl.BlockSpec((tr, 256), lambda i, h: (i, h))
    gain = pl.BlockSpec((1, 256), lambda i, h: (0, 0))
    tab = pl.BlockSpec((tr, 128), lambda i, h: (i, 0))
    return pl.pallas_call(
        body, name=name, grid=(T // tr, H),
        in_specs=[head, head, pl.BlockSpec((tr, 128), lambda i, h: (i, dm.o_rope // 128)), gain, gain, tab, tab,
                  head, head, pl.BlockSpec((tr, 128), lambda i, h: (i, h))],
        out_specs=[head, head, tab, gain, gain],
        out_shape=[jax.ShapeDtypeStruct((T, H * 256), BF16), jax.ShapeDtypeStruct((T, H * 256), BF16),
                   jax.ShapeDtypeStruct((T, 128), F32), jax.ShapeDtypeStruct((1, 256), F32),
                   jax.ShapeDtypeStruct((1, 256), F32)],
        compiler_params=_params(("arbitrary", "arbitrary")),
    )(q0, kv0, proj, qn, kn, C, S, dq, dk, dv)


_NT = (((1,), (1,)), ((), ()))


def _causal_mask(t):
    return lax.broadcasted_iota(jnp.int32, (t, t), 0) >= lax.broadcasted_iota(jnp.int32, (t, t), 1)


def _causal_mask_t(t):
    return lax.broadcasted_iota(jnp.int32, (t, t), 0) <= lax.broadcasted_iota(jnp.int32, (t, t), 1)


HB = 2


def attn_fwd(q, k, v, dm, name, comm=None):
    T, H, tq = dm.T, dm.H, dm.tq
    scale = dm.QKH ** -0.5

    def body(q_ref, k_ref, v_ref, o_ref, ob_ref, lse_ref):
        qi = pl.program_id(1)

        def step(j, carry, masked):
            j0 = pl.multiple_of(j * tq, tq)
            ss = [lax.dot_general(q_ref[:, 256 * a:256 * (a + 1)], k_ref[pl.ds(j0, tq), 256 * a:256 * (a + 1)], _NT,
                                  preferred_element_type=F32) for a in range(HB)]
            out = []
            for a in range(HB):
                m, l, acc = carry[a]
                s = ss[a] * scale
                if masked:
                    s = jnp.where(_causal_mask(tq), s, -jnp.inf)
                m_new = jnp.maximum(m, jnp.max(s, -1, keepdims=True))
                alpha = jnp.exp(m - m_new)
                p = jnp.exp(s - m_new)
                l = alpha * l + jnp.sum(p, -1, keepdims=True)
                acc = alpha * acc + jnp.dot(p.astype(BF16), v_ref[pl.ds(j0, tq), 128 * a:128 * (a + 1)],
                                            preferred_element_type=F32)
                out.append((m_new, l, acc))
            return tuple(out)

        one = (jnp.full((tq, 1), -jnp.inf, F32), jnp.zeros((tq, 1), F32), jnp.zeros((tq, 128), F32))
        carry = lax.fori_loop(0, qi, lambda j, c: step(j, c, False), (one,) * HB)
        carry = step(qi, carry, True)
        for a in range(HB):
            m, l, acc = carry[a]
            o = acc / l
            o_ref[:, 128 * a:128 * (a + 1)] = o
            ob_ref[:, 128 * a:128 * (a + 1)] = o.astype(BF16)
            lse_ref[a] = m + jnp.log(l)

    outs, couts = _hosted_call(
        body, name=name, grid=(H // HB, T // tq),
        in_specs=[pl.BlockSpec((tq, 256 * HB), lambda h, i: (i, h)), pl.BlockSpec((T, 256 * HB), lambda h, i: (0, h)),
                  pl.BlockSpec((T, 128 * HB), lambda h, i: (0, h))],
        out_specs=[pl.BlockSpec((tq, 128 * HB), lambda h, i: (i, h)), pl.BlockSpec((tq, 128 * HB), lambda h, i: (i, h)),
                   pl.BlockSpec((HB, tq, 1), lambda h, i: (h, i, 0))],
        out_shape=[jax.ShapeDtypeStruct((T, H * 128), F32), jax.ShapeDtypeStruct((T, H * 128), BF16),
                   jax.ShapeDtypeStruct((H, T, 1), F32)],
        scratch_shapes=[], semantics=("parallel", "parallel"), args=(q, k, v), comm=comm)
    return outs if comm is None else (outs, couts)


def attn_delta(do, o, dm, name):
    T, H, tr = dm.T, dm.H, dm.tr

    def body(do_ref, o_ref, delta_ref, dob_ref):
        d = do_ref[...]
        delta_ref[...] = jnp.sum(d * o_ref[...], -1, keepdims=True)
        dob_ref[...] = d.astype(BF16)

    blk = pl.BlockSpec((tr, 128), lambda i, h: (i, h))
    return pl.pallas_call(
        body, name=name, grid=(T // tr, H),
        in_specs=[blk, blk],
        out_specs=[pl.BlockSpec((None, tr, 1), lambda i, h: (h, i, 0)), blk],
        out_shape=[jax.ShapeDtypeStruct((H, T, 1), F32), jax.ShapeDtypeStruct((T, H * 128), BF16)],
        compiler_params=_params(("parallel", "parallel")),
    )(do, o)


def attn_bwd_dq(q, k, v, do, lse, delta, dm, name, comm=None):
    T, H, tq = dm.T, dm.H, dm.tq
    scale = dm.QKH ** -0.5

    def body(q_ref, k_ref, v_ref, do_ref, lse_ref, delta_ref, dq_ref):
        qi = pl.program_id(1)

        def step(j, dqs, masked):
            j0 = pl.multiple_of(j * tq, tq)
            hk = lambda a: slice(256 * a, 256 * (a + 1))
            hv = lambda a: slice(128 * a, 128 * (a + 1))
            ss = [lax.dot_general(q_ref[:, hk(a)], k_ref[pl.ds(j0, tq), hk(a)], _NT, preferred_element_type=F32)
                  for a in range(HB)]
            dps = [lax.dot_general(do_ref[:, hv(a)], v_ref[pl.ds(j0, tq), hv(a)], _NT, preferred_element_type=F32)
                   for a in range(HB)]
            out = []
            for a in range(HB):
                p = jnp.exp(ss[a] * scale - lse_ref[a])
                if masked:
                    p = jnp.where(_causal_mask(tq), p, 0.0)
                ds = p * (dps[a] - delta_ref[a]) * scale
                out.append(dqs[a] + jnp.dot(ds.astype(BF16), k_ref[pl.ds(j0, tq), hk(a)], preferred_element_type=F32))
            return tuple(out)

        dqs = lax.fori_loop(0, qi, lambda j, c: step(j, c, False), (jnp.zeros((tq, 256), F32),) * HB)
        dqs = step(qi, dqs, True)
        for a in range(HB):
            dq_ref[:, 256 * a:256 * (a + 1)] = dqs[a]

    stat = pl.BlockSpec((HB, tq, 1), lambda h, i: (h, i, 0))
    outs, couts = _hosted_call(
        body, name=name, grid=(H // HB, T // tq),
        in_specs=[pl.BlockSpec((tq, 256 * HB), lambda h, i: (i, h)), pl.BlockSpec((T, 256 * HB), lambda h, i: (0, h)),
                  pl.BlockSpec((T, 128 * HB), lambda h, i: (0, h)), pl.BlockSpec((tq, 128 * HB), lambda h, i: (i, h)),
                  stat, stat],
        out_specs=[pl.BlockSpec((tq, 256 * HB), lambda h, i: (i, h))],
        out_shape=[jax.ShapeDtypeStruct((T, H * 256), F32)],
        scratch_shapes=[], semantics=("parallel", "parallel"), args=(q, k, v, do, lse, delta), comm=comm)
    return outs[0] if comm is None else (outs[0], couts)


def attn_bwd_dkv(q, k, v, do, lse_rows, delta_rows, dm, name, comm=None):
    T, H, tq = dm.T, dm.H, dm.tq
    nq = T // tq
    scale = dm.QKH ** -0.5

    def body(q_ref, k_ref, v_ref, do_ref, lse_ref, delta_ref, dk_ref, dv_ref):
        kj = pl.program_id(1)

        def step(i, carry, masked):
            i0 = pl.multiple_of(i * tq, tq)
            hk = lambda a: slice(256 * a, 256 * (a + 1))
            hv = lambda a: slice(128 * a, 128 * (a + 1))
            sts = [lax.dot_general(k_ref[:, hk(a)], q_ref[pl.ds(i0, tq), hk(a)], _NT, preferred_element_type=F32)
                   for a in range(HB)]
            dpts = [lax.dot_general(v_ref[:, hv(a)], do_ref[pl.ds(i0, tq), hv(a)], _NT, preferred_element_type=F32)
                    for a in range(HB)]
            out = []
            for a in range(HB):
                dk, dv = carry[a]
                pt = jnp.exp(sts[a] * scale - lse_ref[a, i])
                if masked:
                    pt = jnp.where(_causal_mask_t(tq), pt, 0.0)
                dv = dv + jnp.dot(pt.astype(BF16), do_ref[pl.ds(i0, tq), hv(a)], preferred_element_type=F32)
                dst = pt * (dpts[a] - delta_ref[a, i]) * scale
                dk = dk + jnp.dot(dst.astype(BF16), q_ref[pl.ds(i0, tq), hk(a)], preferred_element_type=F32)
                out.append((dk, dv))
            return tuple(out)

        carry = step(kj, ((jnp.zeros((tq, 256), F32), jnp.zeros((tq, 128), F32)),) * HB, True)
        carry = lax.fori_loop(kj + 1, nq, lambda i, c: step(i, c, False), carry)
        for a in range(HB):
            dk_ref[:, 256 * a:256 * (a + 1)] = carry[a][0]
            dv_ref[:, 128 * a:128 * (a + 1)] = carry[a][1]

    rows = pl.BlockSpec((HB, nq, 1, tq), lambda h, j: (h, 0, 0, 0))
    outs, couts = _hosted_call(
        body, name=name, grid=(H // HB, nq),
        in_specs=[pl.BlockSpec((T, 256 * HB), lambda h, j: (0, h)), pl.BlockSpec((tq, 256 * HB), lambda h, j: (j, h)),
                  pl.BlockSpec((tq, 128 * HB), lambda h, j: (j, h)), pl.BlockSpec((T, 128 * HB), lambda h, j: (0, h)),
                  rows, rows],
        out_specs=[pl.BlockSpec((tq, 256 * HB), lambda h, j: (j, h)), pl.BlockSpec((tq, 128 * HB), lambda h, j: (j, h))],
        out_shape=[jax.ShapeDtypeStruct((T, H * 256), F32), jax.ShapeDtypeStruct((T, H * 128), F32)],
        scratch_shapes=[], semantics=("parallel", "parallel"), args=(q, k, v, do, lse_rows, delta_rows), comm=comm)
    return outs if comm is None else (outs, couts)


def _window_sum(win, g, shift):
    s1 = win + shift(win, 1)
    s2 = s1 + shift(s1, 2)
    s3 = s2 + shift(s2, 4)
    s4 = s3 + shift(s3, 8)
    return jnp.where(g == 0, s1, jnp.where(g == 1, s2, jnp.where(g == 2, s3, s4)))


def _count(r0, R, g, T_unused=None):
    t = r0 + lax.broadcasted_iota(jnp.int32, (R, 1), 0)
    return jnp.minimum(t + 1, jnp.left_shift(2, g)).astype(F32)


def pool_fwd(proj, pw, ps, dm, name):
    T, PG, R = dm.T, dm.PG, dm.tr

    def body(x_ref, pw_ref, ps_ref, pooled_ref, mixed_ref, yc_ref, buf):
        g = pl.program_id(0)
        _fill_halo_buf(buf, lambda r0: x_ref[pl.ds(r0, R), :], T, R, PG)

        def chunk(r, c):
            r0 = pl.multiple_of(r * R, 8)
            win = buf[pl.ds(r0, R + HALO), :]
            ws = _window_sum(win, g, _back)[HALO:, :]
            pooled = (ws / _count(r0, R, g) - win[HALO:, :]).astype(BF16)
            pooled_ref[pl.ds(r0, R), :] = pooled
            mixed = jnp.dot(pooled, pw_ref[...], preferred_element_type=F32)
            mixed_ref[pl.ds(r0, R), :] = mixed
            yc_ref[pl.ds(r0, R), :] = (mixed * ps_ref[...]).astype(BF16)
            return c

        lax.fori_loop(0, T // R, chunk, 0)

    own = pl.BlockSpec((T, PG), lambda g: (0, g))
    return pl.pallas_call(
        body, name=name, grid=(4,),
        in_specs=[pl.BlockSpec((T, PG), lambda g: (0, dm.o_pool // PG + g)), pl.BlockSpec((None, PG, PG), lambda g: (g, 0, 0)),
                  pl.BlockSpec((1, PG), lambda g: (0, g))],
        out_specs=[own, own, own],
        out_shape=[jax.ShapeDtypeStruct((T, dm.DP), BF16), jax.ShapeDtypeStruct((T, dm.DP), F32),
                   jax.ShapeDtypeStruct((T, dm.DP), BF16)],
        scratch_shapes=[pltpu.VMEM((T + 2 * HALO, PG), F32)],
        compiler_params=_params(("parallel",)),
    )(proj, pw, ps)


def pool_bwd(dyc, mixed, pooled, pw, ps, dm, name):
    T, PG, R = dm.T, dm.PG, dm.tr
    _TN = (((0,), (0,)), ((), ()))

    def body(dyc_ref, mixed_ref, pooled_ref, pw_ref, ps_ref, dx_ref, dpw_ref, dps_ref, qbuf, dpbuf):
        g = pl.program_id(0)
        zeros = jnp.zeros((HALO, PG), F32)
        qbuf[pl.ds(0, HALO), :] = zeros
        qbuf[pl.ds(HALO + T, HALO), :] = zeros
        dpw_ref[...] = jnp.zeros_like(dpw_ref)

        def first(r, dps):
            r0 = pl.multiple_of(r * R, 8)
            dyc = dyc_ref[pl.ds(r0, R), :]
            dps = dps + jnp.sum(dyc * mixed_ref[pl.ds(r0, R), :], axis=0, keepdims=True)
            dmb = (dyc * ps_ref[...]).astype(BF16)
            dpw_ref[...] += lax.dot_general(pooled_ref[pl.ds(r0, R), :], dmb, _TN, preferred_element_type=F32)
            dp = lax.dot_general(dmb, pw_ref[...], _NT, preferred_element_type=F32)
            dpbuf[pl.ds(r0, R), :] = dp
            qbuf[pl.ds(r0 + HALO, R), :] = dp / _count(r0, R, g)
            return dps

        dps_ref[...] = lax.fori_loop(0, T // R, first, jnp.zeros((1, PG), F32))

        def second(r, c):
            r0 = pl.multiple_of(r * R, 8)
            win = qbuf[pl.ds(r0 + HALO, R + HALO), :]
            ws = _window_sum(win, g, _fwd)[:R, :]
            dx_ref[pl.ds(r0, R), :] = (ws - dpbuf[pl.ds(r0, R), :]).astype(BF16)
            return c

        lax.fori_loop(0, T // R, second, 0)

    own = pl.BlockSpec((T, PG), lambda g: (0, g))
    return pl.pallas_call(
        body, name=name, grid=(4,),
        in_specs=[own, own, own, pl.BlockSpec((None, PG, PG), lambda g: (g, 0, 0)), pl.BlockSpec((1, PG), lambda g: (0, g))],
        out_specs=[own, pl.BlockSpec((None, PG, PG), lambda g: (g, 0, 0)), pl.BlockSpec((1, PG), lambda g: (0, g))],
        out_shape=[jax.ShapeDtypeStruct((T, dm.DP), BF16), jax.ShapeDtypeStruct((4, PG, PG), F32),
                   jax.ShapeDtypeStruct((1, dm.DP), F32)],
        scratch_shapes=[pltpu.VMEM((T + 2 * HALO, PG), F32), pltpu.VMEM((T, PG), F32)],
        compiler_params=_params(("parallel",)),
    )(dyc, mixed, pooled, pw, ps)


def _sigmoid(x):
    return 1.0 / (1.0 + jnp.exp(-x))


def merge_fwd(proj, A, B, C, dm, name):
    T, D, tr, tc = dm.T, dm.D, dm.tr, 512
    nc = D // tc

    def body(g0, g1, g2, a, b, c, out):
        out[...] = (_sigmoid(g0[...]) * a[...] + _sigmoid(g1[...]) * b[...] + _sigmoid(g2[...]) * c[...]).astype(BF16)

    gate = lambda k: pl.BlockSpec((tr, tc), lambda i, j: (i, k * nc + j))
    own = pl.BlockSpec((tr, tc), lambda i, j: (i, j))
    return pl.pallas_call(
        body, name=name, grid=(T // tr, nc),
        in_specs=[gate(0), gate(1), gate(2), own, own, own],
        out_specs=own,
        out_shape=jax.ShapeDtypeStruct((T, D), BF16),
        compiler_params=_params(("parallel", "parallel")),
    )(proj, proj, proj, A, B, C)


def merge_bwd(proj, A, B, C, dmerged, dm, name):
    T, D, tr, tc = dm.T, dm.D, dm.tr, 512
    nc = D // tc

    def body(g0, g1, g2, a, b, c, dmr, da, db, dc, dl0, dl1, dl2):
        d = dmr[...]
        for g_ref, y_ref, dy_ref, dl_ref in ((g0, a, da, dl0), (g1, b, db, dl1), (g2, c, dc, dl2)):
            s = _sigmoid(g_ref[...])
            dy_ref[...] = (d * s).astype(BF16)
            dl_ref[...] = (d * y_ref[...] * s * (1.0 - s)).astype(BF16)

    gate = lambda k: pl.BlockSpec((tr, tc), lambda i, j: (i, k * nc + j))
    own = pl.BlockSpec((tr, tc), lambda i, j: (i, j))
    o = jax.ShapeDtypeStruct((T, D), BF16)
    return pl.pallas_call(
        body, name=name, grid=(T // tr, nc),
        in_specs=[gate(0), gate(1), gate(2), own, own, own, own],
        out_specs=[own] * 6,
        out_shape=[o] * 6,
        compiler_params=_params(("parallel", "parallel")),
    )(proj, proj, proj, A, B, C, dmerged)


def loss_head(y, target, dm, name):
    T, D, tr = dm.T, dm.D, dm.tr

    def body(y_ref, t_ref, dy_ref, dyb_ref, loss_ref):
        i = pl.program_id(0)
        t = _row_ids(i, tr)
        real = (t >= dm.n_meta) & (t < dm.T_real)
        err = jnp.where(real, y_ref[...] - t_ref[...], 0.0)
        dy = err * (1.0 / D)
        dy_ref[...] = dy
        dyb_ref[...] = dy.astype(BF16)

        @pl.when(i == 0)
        def _():
            loss_ref[...] = jnp.zeros_like(loss_ref)

        loss_ref[...] += 0.5 * jnp.sum(jnp.sum(err * err, axis=-1, keepdims=True) * (1.0 / D))

    row = pl.BlockSpec((tr, D), lambda i: (i, 0))
    return pl.pallas_call(
        body, name=name, grid=(T // tr,),
        in_specs=[row, row],
        out_specs=[row, row, pl.BlockSpec((8, LANES), lambda i: (0, 0))],
        out_shape=[jax.ShapeDtypeStruct((T, D), F32), jax.ShapeDtypeStruct((T, D), BF16),
                   jax.ShapeDtypeStruct((8, LANES), F32)],
        compiler_params=_params(("arbitrary",)),
    )(y, target)


def adamw(w, m, v, parts, name):
    R, C = w.shape
    P = parts.shape[0]
    br = R
    for cand in (512, 256, 128, 64, 32, 16, 8):
        if R % cand == 0 and cand * C * 4 <= (1 << 20):
            br = cand
            break
    if R * C * 4 <= (1 << 20):
        br = R

    def body(w_ref, m_ref, v_ref, p_ref, g_ref, d_ref, nm_ref, nv_ref):
        g = p_ref[0].astype(F32)
        for k in range(1, P):
            g = g + p_ref[k].astype(F32)
        mm = ADAM_B1 * m_ref[...] + (1.0 - ADAM_B1) * g
        vv = ADAM_B2 * v_ref[...] + (1.0 - ADAM_B2) * (g * g)
        m_hat = mm / (1.0 - ADAM_B1 ** ADAM_STEP)
        v_hat = vv / (1.0 - ADAM_B2 ** ADAM_STEP)
        g_ref[...] = g
        d_ref[...] = -ADAM_LR * (m_hat / (jnp.sqrt(v_hat) + ADAM_EPS) + ADAM_WD * w_ref[...])
        nm_ref[...] = mm
        nv_ref[...] = vv

    blk = pl.BlockSpec((br, C), lambda i: (i, 0))
    o = jax.ShapeDtypeStruct((R, C), F32)
    return pl.pallas_call(
        body, name=name, grid=(R // br,),
        in_specs=[blk, blk, blk, pl.BlockSpec((P, br, C), lambda i: (0, i, 0))],
        out_specs=[blk] * 4,
        out_shape=[o] * 4,
        compiler_params=_params(("parallel",)),
    )(w, m, v, parts)


def sum_parts(parts, name):
    P, R, C = parts.shape

    def body(p_ref, o_ref):
        acc = p_ref[0]
        for k in range(1, P):
            acc = acc + p_ref[k]
        o_ref[...] = acc

    return pl.pallas_call(
        body, name=name, grid=(1,),
        in_specs=[pl.BlockSpec((P, R, C), lambda i: (0, 0, 0))],
        out_specs=pl.BlockSpec((R, C), lambda i: (0, 0)),
        out_shape=jax.ShapeDtypeStruct((R, C), F32),
        compiler_params=_params(("arbitrary",)),
    )(parts)


def add_sibling(parts, got, core, name):
    _, _, R, C = parts.shape
    br = _pick(R, (1024, 512, 256, 128, 64, 32, 16))

    def body(c_ref, a_ref, b_ref, o_ref):
        o_ref[...] = (a_ref[...].astype(F32) + b_ref[...].astype(F32)).astype(BF16)

    blk = pl.BlockSpec((None, br, C), lambda ch, i, c: (ch, i, 0))
    return pl.pallas_call(
        body, name=name,
        grid_spec=pltpu.PrefetchScalarGridSpec(
            num_scalar_prefetch=1, grid=(4, R // br),
            in_specs=[pl.BlockSpec((None, None, br, C), lambda ch, i, c: (ch, c[0], i, 0)), blk],
            out_specs=blk),
        out_shape=jax.ShapeDtypeStruct((4, R, C), BF16), compiler_params=_params(("parallel", "parallel")),
    )(core, parts, got)


def all_gather(arrs, name):
    n = len(arrs)

    def body(*refs):
        ins, outs = refs[:n], refs[n:2 * n]
        send_sems, recv_sems, local_sems = refs[2 * n:]
        x, y, c = _place()
        me, sibling = (x, y, c), (x, y, 1 - c)
        chips = [(1 - x, y), (x, 1 - y), (1 - x, 1 - y)]

        def copy(a, k, block, to, src=None):
            px, py, pc = block
            dst = outs[a].at[4 * px + 2 * py + pc]
            return pltpu.make_async_remote_copy(
                src_ref=dst if src is None else src, dst_ref=dst,
                send_sem=send_sems.at[7 * a + k], recv_sem=recv_sems.at[7 * a + k],
                device_id=to, device_id_type=_MESH)

        started = []
        for a in range(n):
            mine = pltpu.make_async_copy(ins[a], outs[a].at[4 * x + 2 * y + c], local_sems.at[a])
            mine.start()
            started.append(mine)
        sends = []
        for a in range(n):
            sends.append(copy(a, 0, me, sibling, src=ins[a]))
            for j, chip in enumerate(chips):
                sends.append(copy(a, 1 + j, me, (*chip, c), src=ins[a]))
        for cp in sends:
            cp.start()
        for j, chip in enumerate(chips):
            for a in range(n):
                copy(a, 1 + j, (*chip, c), me).wait_recv()
                fwd = copy(a, 4 + j, (*chip, c), sibling)
                fwd.start()
                sends.append(fwd)
        for a in range(n):
            copy(a, 0, sibling, me).wait_recv()
            for j, chip in enumerate(chips):
                copy(a, 4 + j, (*chip, 1 - c), me).wait_recv()
        for cp in sends:
            cp.wait_send()
        for cp in started:
            cp.wait()

    outs = pl.pallas_call(
        body, name=name,
        in_specs=[_HBM] * n, out_specs=[_HBM] * n,
        out_shape=[jax.ShapeDtypeStruct((8,) + a.shape, a.dtype) for a in arrs],
        scratch_shapes=[pltpu.SemaphoreType.DMA((7 * n,)), pltpu.SemaphoreType.DMA((7 * n,)), pltpu.SemaphoreType.DMA((n,))],
    )(*arrs)
    return list(outs)


def sibling_exchange(arrs, name):
    n = len(arrs)

    def body(*refs):
        ins, got = refs[:n], refs[n:2 * n]
        send_sems, recv_sems = refs[2 * n:]
        x, y, c = _place()
        work = []
        for a in range(n):
            for ch in range(4):
                cp = pltpu.make_async_remote_copy(
                    src_ref=ins[a].at[ch, 1 - c], dst_ref=got[a].at[ch],
                    send_sem=send_sems.at[4 * a + ch], recv_sem=recv_sems.at[4 * a + ch],
                    device_id=(x, y, 1 - c), device_id_type=_MESH)
                cp.start()
                work.append(cp)
        for cp in work:
            cp.wait()

    outs = pl.pallas_call(
        body, name=name,
        in_specs=[_HBM] * n, out_specs=[_HBM] * n,
        out_shape=[jax.ShapeDtypeStruct((4,) + a.shape[2:], a.dtype) for a in arrs],
        scratch_shapes=[pltpu.SemaphoreType.DMA((4 * n,)), pltpu.SemaphoreType.DMA((4 * n,))],
    )(*arrs)
    return list(outs)


def _remote(src, dst, send_sem, recv_sem, to):
    return pltpu.make_async_remote_copy(src_ref=src, dst_ref=dst, send_sem=send_sem, recv_sem=recv_sem,
                                        device_id=to, device_id_type=_MESH)


def gather_ici(blocks):
    n = len(blocks)

    def copies(cins, couts, sems):
        send_sems, recv_sems, local_sems = sems
        x, y, c = _place()
        mine = 4 * x + 2 * y + c
        local, sends, recvs = [], [], []
        for a in range(n):
            local.append(pltpu.make_async_copy(cins[a], couts[a].at[mine], local_sems.at[a]))
            for j, (px, py) in enumerate([(1 - x, y), (x, 1 - y), (1 - x, 1 - y)]):
                k = 3 * a + j
                sends.append(_remote(cins[a], couts[a].at[mine], send_sems.at[k], recv_sems.at[k], (px, py, c)))
                recvs.append(_remote(cins[a], couts[a].at[4 * px + 2 * py + c], send_sems.at[k], recv_sems.at[k], (px, py, c)))
        return local, sends, recvs

    def start(cins, couts, sems):
        local, sends, _ = copies(cins, couts, sems)
        for cp in local + sends:
            cp.start()

    def finish(cins, couts, sems):
        local, sends, recvs = copies(cins, couts, sems)
        for cp in sends:
            cp.wait_send()
        for cp in recvs:
            cp.wait_recv()
        for cp in local:
            cp.wait()

    return Hosted(list(blocks), [jax.ShapeDtypeStruct((8,) + b.shape, b.dtype) for b in blocks],
                  [pltpu.SemaphoreType.DMA((3 * n,)), pltpu.SemaphoreType.DMA((3 * n,)), pltpu.SemaphoreType.DMA((n,))],
                  start, finish)


def fill_sibling(stks, name):
    n = len(stks)

    def body(*refs):
        ins, outs = refs[:n], refs[n:2 * n]
        send_sems, recv_sems = refs[2 * n:]
        x, y, c = _place()
        sends, recvs = [], []
        for a in range(n):
            for ch in range(4):
                k = 4 * a + ch
                sends.append(_remote(ins[a].at[2 * ch + c], outs[a].at[2 * ch + c], send_sems.at[k], recv_sems.at[k], (x, y, 1 - c)))
                recvs.append(_remote(ins[a].at[2 * ch + c], outs[a].at[2 * ch + 1 - c], send_sems.at[k], recv_sems.at[k], (x, y, 1 - c)))
        for cp in sends:
            cp.start()
        for cp in sends:
            cp.wait_send()
        for cp in recvs:
            cp.wait_recv()

    outs = pl.pallas_call(
        body, name=name,
        in_specs=[_HBM] * n, out_specs=[_HBM] * n,
        out_shape=[jax.ShapeDtypeStruct(s.shape, s.dtype) for s in stks],
        scratch_shapes=[pltpu.SemaphoreType.DMA((4 * n,)), pltpu.SemaphoreType.DMA((4 * n,))],
        input_output_aliases={a: a for a in range(n)},
    )(*stks)
    return list(outs)


def reduce_ici(arrs):
    n = len(arrs)

    def copies(cins, couts, sems):
        send_sems, recv_sems, local_sems = sems
        x, y, c = _place()
        my_chip = 2 * x + y
        local, sends, recvs = [], [], []
        for a in range(n):
            local.append(pltpu.make_async_copy(cins[a].at[my_chip], couts[a].at[my_chip], local_sems.at[a]))
            for j, (px, py) in enumerate([(1 - x, y), (x, 1 - y), (1 - x, 1 - y)]):
                k = 3 * a + j
                sends.append(_remote(cins[a].at[2 * px + py], couts[a].at[my_chip], send_sems.at[k], recv_sems.at[k], (px, py, c)))
                recvs.append(_remote(cins[a].at[my_chip], couts[a].at[2 * px + py], send_sems.at[k], recv_sems.at[k], (px, py, c)))
        return local, sends, recvs

    def start(cins, couts, sems):
        local, sends, _ = copies(cins, couts, sems)
        for cp in local + sends:
            cp.start()

    def finish(cins, couts, sems):
        local, sends, recvs = copies(cins, couts, sems)
        for cp in sends:
            cp.wait_send()
        for cp in recvs:
            cp.wait_recv()
        for cp in local:
            cp.wait()

    return Hosted(list(arrs), [jax.ShapeDtypeStruct(a.shape, a.dtype) for a in arrs],
                  [pltpu.SemaphoreType.DMA((3 * n,)), pltpu.SemaphoreType.DMA((3 * n,)), pltpu.SemaphoreType.DMA((n,))],
                  start, finish)


COL_SHARDED = ("w_in", "w_uq", "w_ukv", "w_branch_a", "w_branch_c", "w_up")
ROW_SHARDED = ("w_branch_b", "w_o", "w_down")
BIG = COL_SHARDED + ROW_SHARDED


def _full_from_stacked(name, st):
    if name in COL_SHARDED:
        _, L, K, n = st.shape
        return st.transpose(1, 2, 0, 3).reshape(L, K, 8 * n)
    if name in ROW_SHARDED:
        _, L, k, N = st.shape
        return st.transpose(1, 0, 2, 3).reshape(L, 8 * k, N)
    if name == "pool_w":
        _, L, G, pk, PG = st.shape
        return st.transpose(1, 2, 0, 3, 4).reshape(L, G, 8 * pk, PG)
    if name == "meta_tokens":
        _, M, n = st.shape
        return st.transpose(1, 0, 2).reshape(M, 8 * n)
    if name == "conv_w":
        _, L, W, n = st.shape
        return st.transpose(1, 2, 0, 3).reshape(L, W, 8 * n)
    raise ValueError(name)


def _shards_from_full(name, g):
    if name in COL_SHARDED:
        L, K, N = g.shape
        s = g.reshape(L, K, 8, N // 8).transpose(2, 0, 1, 3)
    else:
        L, K, N = g.shape
        s = g.reshape(L, 8, K // 8, N).transpose(1, 0, 2, 3)
    return s.reshape((4, 2) + s.shape[1:])


def _w_in_to_padded(w, dm):
    o3 = 3 * dm.DC + dm.QL + dm.KL
    o4 = o3 + dm.ROPE
    o5 = o4 + dm.DP
    pad = jnp.zeros(w.shape[:-1] + (256 - dm.ROPE,), w.dtype)
    return jnp.concatenate([w[..., o5:], w[..., :o3], w[..., o4:o5], w[..., o3:o4], pad], axis=-1)


def _w_in_from_padded(g, dm):
    o3 = 3 * dm.DC + dm.QL + dm.KL
    a = 3 * dm.D
    return jnp.concatenate([g[..., a:a + o3], g[..., dm.o_rope:dm.o_rope + dm.ROPE], g[..., dm.o_pool:dm.o_pool + dm.DP],
                            g[..., :a]], axis=-1)


def _pad_heads(w, dm):
    w = w.reshape(w.shape[:-1] + (dm.H, dm.QKH))
    w = jnp.pad(w, [(0, 0)] * (w.ndim - 1) + [(0, dm.HP - dm.QKH)])
    return w.reshape(w.shape[:-2] + (dm.H * dm.HP,))


def _unpad_heads(g, dm):
    g = g.reshape(g.shape[:-1] + (dm.H, dm.HP))[..., :dm.QKH]
    return g.reshape(g.shape[:-2] + (dm.H * dm.QKH,))


class _Weights:
    def __init__(self, w, dm, plan):
        self.w, self.dm, self.plan, self.full = w, dm, plan, {}

    def blocks(self, items):
        return [self.w[n][l:l + 1].astype(BF16) for n, l in items]

    def put(self, items, stacked):
        for (n, l), st in zip(items, stacked):
            f = _full_from_stacked(n, st)[0]
            if n == "w_in":
                f = _w_in_to_padded(f, self.dm)
            if n == "w_uq":
                f = _pad_heads(f, self.dm)
            self.full[(n, l)] = f

    def comm(self, tag):
        items = self.plan.get(tag)
        return gather_ici(self.blocks(items)) if items else None

    def arrived(self, tag, couts):
        self.put(self.plan[tag], fill_sibling(couts, f"fill_{tag}"))

    def __call__(self, n, l):
        return self.full[(n, l)]


class _Reducer:
    def __init__(self, dm, core):
        self.dm, self.core, self.q = dm, core, {}

    def prepare(self, tag, items, g):
        parts = []
        for n, l in items:
            f = g[n]
            if n == "w_in":
                f = _w_in_from_padded(f, self.dm)
            if n == "w_uq":
                f = _unpad_heads(f, self.dm)
            parts.append(_shards_from_full(n, f[None]))
        got = sibling_exchange(parts, f"reduce_sibling_{tag}")
        out = []
        for (n, l), a, b in zip(items, parts, got):
            C = a.shape[-1]
            out.append(add_sibling(a.reshape(4, 2, -1, C), b.reshape(4, -1, C), self.core,
                                   f"reduce_add_{n}_{l}").reshape(b.shape))
        return out

    def put(self, items, summed):
        for key, q in zip(items, summed):
            self.q[key] = q


def _layer_fwd(xin, l, ws, G, tabs, dm):
    nm = lambda s: f"l{l}_{s}"
    D = dm.D

    def mm(tag, *args, **kw):
        comm = ws.comm(nm(tag))
        if comm is None:
            return matmul(*args, nm(tag), **kw)
        res, couts = matmul(*args, nm(tag), comm=comm, **kw)
        ws.arrived(nm(tag), couts)
        return res

    h = rms_fwd(xin, 0, D, G["attn_norm"], dm, nm("rms1"))
    proj = mm("proj", h, ws("w_in", l), "nn", (F32,))
    ya = mixer_a_fwd(proj, G["conv_w"], dm, nm("mixa"))
    ql = rms_fwd(proj, dm.o_ql // dm.QL, dm.QL, G["q_lat_norm"], dm, nm("rms_q"))
    kl = rms_fwd(proj, dm.o_kl // dm.KL, dm.KL, G["kv_lat_norm"], dm, nm("rms_kv"))
    q0 = mm("uq", ql, ws("w_uq", l), "nn", (F32,))
    kv0 = mm("ukv", kl, ws("w_ukv", l), "nn", (F32,))
    q_s, k_s, v_s = qk_prep_fwd(q0, kv0, proj, G["q_norm"], G["k_norm"], tabs[0], tabs[1], dm, nm("qkprep"))
    comm = ws.comm(nm("attn"))
    if comm is None:
        o, ob, lse = attn_fwd(q_s, k_s, v_s, dm, nm("attn"))
    else:
        (o, ob, lse), couts = attn_fwd(q_s, k_s, v_s, dm, nm("attn"), comm=comm)
        ws.arrived(nm("attn"), couts)
    pooled, mixed, yc = pool_fwd(proj, ws("pool_w", l), G["pool_scale"], dm, nm("pool"))
    A = mm("br_a", ya, ws("w_branch_a", l), "nn", (F32,))
    B = mm("br_b", ob, ws("w_branch_b", l), "nn", (F32,))
    C = mm("br_c", yc, ws("w_branch_c", l), "nn", (F32,))
    merged = merge_fwd(proj, A, B, C, dm, nm("merge"))
    x1 = mm("wo", merged, ws("w_o", l), "nn", (F32,), extras=(xin,), epi=lambda acc, r: (acc + r,))
    h2 = rms_fwd(x1, 0, D, G["mlp_norm"], dm, nm("rms2"))
    up, act = mm("up", h2, ws("w_up", l), "nn", (F32, BF16), epi=lambda acc: (acc, jnp.square(jnp.maximum(acc, 0.0))))
    x2 = mm("down", act, ws("w_down", l), "nn", (F32,), extras=(x1,), epi=lambda acc, r: (acc + r,))
    saved = dict(xin=xin, h=h, proj=proj, ya=ya, ql=ql, kl=kl, q0=q0, kv0=kv0, q_s=q_s, k_s=k_s, v_s=v_s, o=o, ob=ob,
                 lse=lse, pooled=pooled, mixed=mixed, yc=yc, A=A, B=B, C=C, merged=merged, x1=x1, h2=h2, up=up, act=act)
    return x2, saved


def _layer_bwd(dx2, dx2b, S, l, ws, G, tabs, dm, pre_attn, late=None):
    nm = lambda s: f"l{l}_b_{s}"
    D, T = dm.D, dm.T
    g = {}
    d_up = matmul(dx2b, ws("w_down", l), "nt", (BF16,), nm("d_act"), extras=(S["up"],),
                  epi=lambda acc, up: (acc * (2.0 * jnp.maximum(up, 0.0)),))
    g["w_down"] = matmul(S["act"], dx2b, "tn", (BF16,), nm("g_down"))
    g["w_up"] = matmul(S["h2"], d_up, "tn", (BF16,), nm("g_up"))
    dh2 = matmul(d_up, ws("w_up", l), "nt", (F32,), nm("d_h2"))
    dx1, dx1b, g["mlp_norm"] = rms_bwd(S["x1"], 0, D, G["mlp_norm"], dh2, dx2, dm, nm("rms2"))
    dmerged = matmul(dx1b, ws("w_o", l), "nt", (F32,), nm("d_merged"))
    g["w_o"] = matmul(S["merged"], dx1b, "tn", (BF16,), nm("g_o"))
    dA, dB, dC, dl0, dl1, dl2 = merge_bwd(S["proj"], S["A"], S["B"], S["C"], dmerged, dm, nm("merge"))
    dya = matmul(dA, ws("w_branch_a", l), "nt", (F32,), nm("d_ya"))
    g["w_branch_a"] = matmul(S["ya"], dA, "tn", (BF16,), nm("g_a"))
    dyb = matmul(dB, ws("w_branch_b", l), "nt", (F32,), nm("d_yb"))
    g["w_branch_b"] = matmul(S["ob"], dB, "tn", (BF16,), nm("g_b"))
    dyc = matmul(dC, ws("w_branch_c", l), "nt", (F32,), nm("d_yc"))
    g["w_branch_c"] = matmul(S["yc"], dC, "tn", (BF16,), nm("g_c"))
    du, db, dc, g["conv_w"] = mixer_a_bwd(S["proj"], G["conv_w"], dya, dm, nm("mixa"))
    dpool, g["pool_w"], g["pool_scale"] = pool_bwd(dyc, S["mixed"], S["pooled"], ws("pool_w", l), G["pool_scale"], dm, nm("pool"))
    delta, dob = attn_delta(dyb, S["o"], dm, nm("delta"))
    nq = T // dm.tq
    comm_dq, done_dq, comm_dkv, done_dkv = pre_attn(g)
    dq = attn_bwd_dq(S["q_s"], S["k_s"], S["v_s"], dob, S["lse"], delta, dm, nm("attn_dq"), comm=comm_dq)
    if comm_dq is not None:
        dq, couts = dq
        done_dq(couts)
    dkv = attn_bwd_dkv(S["q_s"], S["k_s"], S["v_s"], dob, S["lse"].reshape(dm.H, nq, 1, dm.tq),
                       delta.reshape(dm.H, nq, 1, dm.tq), dm, nm("attn_dkv"), comm=comm_dkv)
    if comm_dkv is not None:
        dkv, couts = dkv
        done_dkv(couts)
    dk, dv = dkv
    dq0, dkv0, dkr, g["q_norm"], g["k_norm"] = qk_prep_bwd(S["q0"], S["kv0"], S["proj"], G["q_norm"], G["k_norm"],
                                                            tabs[0], tabs[1], dq, dk, dv, dm, nm("qkprep"))
    dql = matmul(dq0, ws("w_uq", l), "nt", (F32,), nm("d_ql"))
    g["w_uq"] = matmul(S["ql"], dq0, "tn", (BF16,), nm("g_uq"))
    dkl = matmul(dkv0, ws("w_ukv", l), "nt", (F32,), nm("d_kl"))
    g["w_ukv"] = matmul(S["kl"], dkv0, "tn", (BF16,), nm("g_ukv"))
    _, dqlat, g["q_lat_norm"] = rms_bwd(S["proj"], dm.o_ql // dm.QL, dm.QL, G["q_lat_norm"], dql, None, dm, nm("rms_q"))
    _, dkvlat, g["kv_lat_norm"] = rms_bwd(S["proj"], dm.o_kl // dm.KL, dm.KL, G["kv_lat_norm"], dkl, None, dm, nm("rms_kv"))
    dproj = jnp.concatenate([dl0, dl1, dl2, du, db, dc, dqlat, dkvlat, dpool, dkr.astype(BF16),
                             jnp.zeros((T, 128), BF16)], axis=1)
    g["w_in"] = matmul(S["h"], dproj, "tn", (BF16,), nm("g_in"))
    comm_dh, done_dh = late(g) if late is not None else (None, None)
    dh = matmul(dproj, ws("w_in", l), "nt", (F32,), nm("d_h"), comm=comm_dh)
    if comm_dh is not None:
        dh, couts = dh
        done_dh(couts)
    dx, dxb, g["attn_norm"] = rms_bwd(S["xin"], 0, D, G["attn_norm"], dh, dx1, dm, nm("rms1"))
    return dx, dxb, g


WEIGHTS = ("meta_tokens", "attn_norm", "w_in", "conv_w", "q_lat_norm", "kv_lat_norm", "w_uq", "w_ukv", "q_norm", "k_norm",
           "pool_w", "pool_scale", "w_branch_a", "w_branch_b", "w_branch_c", "w_o", "mlp_norm", "w_up", "w_down")
SMALL = tuple(n for n in WEIGHTS if n not in BIG)


def kernel(x, meta_tokens, attn_norm, w_in, conv_w, q_lat_norm, kv_lat_norm, w_uq, w_ukv, q_norm, k_norm, pool_w, pool_scale, w_branch_a, w_branch_b, w_branch_c, w_o, mlp_norm, w_up, w_down, loss_target, m_meta_tokens, m_attn_norm, m_w_in, m_conv_w, m_q_lat_norm, m_kv_lat_norm, m_w_uq, m_w_ukv, m_q_norm, m_k_norm, m_pool_w, m_pool_scale, m_w_branch_a, m_w_branch_b, m_w_branch_c, m_w_o, m_mlp_norm, m_w_up, m_w_down, v_meta_tokens, v_attn_norm, v_w_in, v_conv_w, v_q_lat_norm, v_kv_lat_norm, v_w_uq, v_w_ukv, v_q_norm, v_k_norm, v_pool_w, v_pool_scale, v_w_branch_a, v_w_branch_b, v_w_branch_c, v_w_o, v_mlp_norm, v_w_up, v_w_down):
    w = dict(meta_tokens=meta_tokens, attn_norm=attn_norm, w_in=w_in, conv_w=conv_w, q_lat_norm=q_lat_norm,
             kv_lat_norm=kv_lat_norm, w_uq=w_uq, w_ukv=w_ukv, q_norm=q_norm, k_norm=k_norm, pool_w=pool_w,
             pool_scale=pool_scale, w_branch_a=w_branch_a, w_branch_b=w_branch_b, w_branch_c=w_branch_c, w_o=w_o,
             mlp_norm=mlp_norm, w_up=w_up, w_down=w_down)
    m = dict(meta_tokens=m_meta_tokens, attn_norm=m_attn_norm, w_in=m_w_in, conv_w=m_conv_w, q_lat_norm=m_q_lat_norm,
             kv_lat_norm=m_kv_lat_norm, w_uq=m_w_uq, w_ukv=m_w_ukv, q_norm=m_q_norm, k_norm=m_k_norm, pool_w=m_pool_w,
             pool_scale=m_pool_scale, w_branch_a=m_w_branch_a, w_branch_b=m_w_branch_b, w_branch_c=m_w_branch_c, w_o=m_w_o,
             mlp_norm=m_mlp_norm, w_up=m_w_up, w_down=m_w_down)
    v = dict(meta_tokens=v_meta_tokens, attn_norm=v_attn_norm, w_in=v_w_in, conv_w=v_conv_w, q_lat_norm=v_q_lat_norm,
             kv_lat_norm=v_kv_lat_norm, w_uq=v_w_uq, w_ukv=v_w_ukv, q_norm=v_q_norm, k_norm=v_k_norm, pool_w=v_pool_w,
             pool_scale=v_pool_scale, w_branch_a=v_w_branch_a, w_branch_b=v_w_branch_b, w_branch_c=v_w_branch_c, w_o=v_w_o,
             mlp_norm=v_mlp_norm, w_up=v_w_up, w_down=v_w_down)
    L = attn_norm.shape[0]
    assert L == 2, "the gather / reduce schedule below is written for two layers"
    seq, D = x.shape[1], x.shape[2]
    n_meta = meta_tokens.shape[0]
    dm = Dims(D, seq, n_meta)
    T = dm.T
    me = 4 * lax.axis_index("x") + 2 * lax.axis_index("y") + lax.axis_index("c")
    core = lax.axis_index("c").astype(jnp.int32).reshape(1)

    plan = {
        "l0_proj": [("w_branch_a", 0), ("w_branch_b", 0), ("w_branch_c", 0), ("w_o", 0), ("w_up", 0)],
        "l0_attn": [("w_down", 0), ("w_in", 1)],
        "l0_up": [("w_up", 1)],
        "l0_down": [("w_down", 1)],
        "l1_proj": [("w_uq", 1), ("w_ukv", 1), ("w_branch_a", 1), ("w_branch_b", 1), ("w_branch_c", 1), ("w_o", 1)],
    }
    ws = _Weights(w, dm, plan)
    first = [("w_in", 0), ("w_uq", 0), ("w_ukv", 0), ("pool_w", 0), ("pool_w", 1)]
    ws.put(first, all_gather(ws.blocks(first), "gather_first"))
    st_small = all_gather([w["meta_tokens"], w["conv_w"]], "gather_small")
    meta_full = _full_from_stacked("meta_tokens", st_small[0])
    conv_full = _full_from_stacked("conv_w", st_small[1])
    pad_gain = lambda gn: jnp.pad(gn, (0, dm.HP - dm.QKH))

    def gains(l):
        G = {n: w[n][l][None, :] for n in ("attn_norm", "q_lat_norm", "kv_lat_norm", "pool_scale", "mlp_norm")}
        G["q_norm"], G["k_norm"] = pad_gain(w["q_norm"][l])[None, :], pad_gain(w["k_norm"][l])[None, :]
        G["conv_w"] = conv_full[l]
        return G

    Gs = [gains(l) for l in range(L)]

    pos = jnp.arange(dm.T_real, dtype=F32)
    inv = 10000.0 ** (-jnp.arange(0, dm.ROPE, 2, dtype=F32) / dm.ROPE)
    ang = pos[:, None] * inv[None, :]
    zpad = jnp.zeros((dm.T_real, LANES - dm.ROPE), F32)
    rows = ((0, T - dm.T_real), (0, 0))
    tabs = (jnp.pad(jnp.concatenate([jnp.cos(ang), jnp.cos(ang), zpad], 1), rows),
            jnp.pad(jnp.concatenate([jnp.sin(ang), jnp.sin(ang), zpad], 1), rows))

    xs = jnp.concatenate([meta_full, x[0], jnp.zeros((T - dm.T_real, D), F32)], axis=0)
    target = jnp.pad(loss_target[0], ((n_meta, T - dm.T_real), (0, 0)))
    saved = []
    for l in range(L):
        xs, S = _layer_fwd(xs, l, ws, Gs[l], tabs, dm)
        saved.append(S)
    dx, dxb, loss_acc = loss_head(xs, target, dm, "loss_head")

    red = _Reducer(dm, core)
    early = lambda l: [(n, l) for n in ("w_down", "w_up", "w_o", "w_branch_a", "w_branch_b", "w_branch_c")]
    late = lambda l: [(n, l) for n in ("w_uq", "w_ukv", "w_in")]
    grads = [None] * L

    def pre_attn_1(g):
        return (reduce_ici(red.prepare("e1", early(1), g)), lambda couts: red.put(early(1), couts), None, None)

    dx, dxb, grads[1] = _layer_bwd(dx, dxb, saved[1], 1, ws, Gs[1], tabs, dm, pre_attn_1)
    late1 = red.prepare("l1", late(1), grads[1])

    def pre_attn_0(g):
        return (reduce_ici(late1), lambda couts: red.put(late(1), couts),
                reduce_ici(red.prepare("e0", early(0), g)), lambda couts: red.put(early(0), couts))

    def late_0(g):
        return reduce_ici(red.prepare("l0", late(0), g)), lambda couts: red.put(late(0), couts)

    dx, dxb, grads[0] = _layer_bwd(dx, dxb, saved[0], 0, ws, Gs[0], tabs, dm, pre_attn_0, late_0)
    grad_x = dx[n_meta:dm.T_real][None]
    summed = [jnp.concatenate([red.q[(n, l)] for l in range(L)], axis=1) for n in BIG]

    out_g, out_d, out_m, out_v = {}, {}, {}, {}

    def update(n, parts3):
        shp = w[n].shape
        C = shp[-1]
        res = adamw(w[n].reshape(-1, C), m[n].reshape(-1, C), v[n].reshape(-1, C), parts3, f"adamw_{n}")
        out_g[n], out_d[n], out_m[n], out_v[n] = [r.reshape(shp) for r in res]

    for n, q in zip(BIG, summed):
        update(n, q.reshape(4, -1, q.shape[-1]))

    small_full = {
        "meta_tokens": dx[:n_meta],
        "conv_w": jnp.stack([grads[l]["conv_w"] for l in range(L)]),
        "pool_w": jnp.stack([grads[l]["pool_w"] for l in range(L)]),
        "q_norm": jnp.stack([grads[l]["q_norm"][0, :dm.QKH] for l in range(L)]),
        "k_norm": jnp.stack([grads[l]["k_norm"][0, :dm.QKH] for l in range(L)]),
    }
    for n in ("attn_norm", "q_lat_norm", "kv_lat_norm", "pool_scale", "mlp_norm"):
        small_full[n] = jnp.stack([grads[l][n][0] for l in range(L)])
    flat = jnp.concatenate([small_full[n].reshape(-1) for n in SMALL] + [loss_acc[0, :1]])
    n_flat = flat.shape[0]
    rows_small = -(-n_flat // (8 * LANES)) * 8
    flat = jnp.pad(flat, (0, rows_small * LANES - n_flat)).reshape(rows_small, LANES)
    total = sum_parts(all_gather([flat], "gather_small_grads")[0], "sum_small").reshape(-1)
    off = 0
    for n in SMALL:
        size = math.prod(small_full[n].shape)
        gsum = total[off:off + size].reshape(small_full[n].shape)
        off += size
        if n in ("meta_tokens", "conv_w"):
            blk = w[n].shape[-1]
            gsum = lax.dynamic_slice_in_dim(gsum, me * blk, blk, axis=gsum.ndim - 1)
        elif n == "pool_w":
            blk = w[n].shape[2]
            gsum = lax.dynamic_slice_in_dim(gsum, me * blk, blk, axis=2)
        update(n, gsum.reshape(1, -1, gsum.shape[-1]))
    loss = total[off]

    return (loss, grad_x, *[out_g[n] for n in WEIGHTS], *[out_d[n] for n in WEIGHTS],
            *[out_m[n] for n in WEIGHTS], *[out_v[n] for n in WEIGHTS])
```

```python
import functools
import math

import jax
import jax.numpy as jnp
from jax import lax
from jax.experimental import pallas as pl
from jax.experimental.pallas import tpu as pltpu

F32 = jnp.float32
BF16 = jnp.bfloat16

VMEM_LIMIT_BYTES = 56 * 1024 * 1024
LANES = 128
EPS = 1e-6
HALO = 16

ADAM_LR = 0.001
ADAM_B1 = 0.9
ADAM_B2 = 0.999
ADAM_EPS = 1e-08
ADAM_WD = 0.01
ADAM_STEP = 10


def _params(sem):
    return pltpu.CompilerParams(dimension_semantics=sem, vmem_limit_bytes=VMEM_LIMIT_BYTES)


def _pick(n, prefs):
    for p in prefs:
        if p <= n and n % p == 0:
            return p
    return n


_MESH = pl.DeviceIdType.MESH
_HBM = pl.BlockSpec(memory_space=pltpu.HBM)


def _place():
    return lax.axis_index("x"), lax.axis_index("y"), lax.axis_index("c")


class Hosted:
    def __init__(self, ins, out_shapes, sems, start, finish):
        self.ins, self.out_shapes, self.sems, self.start, self.finish = ins, out_shapes, sems, start, finish


def _hosted_call(body, *, name, grid, in_specs, out_specs, out_shape, scratch_shapes, semantics, args, comm):
    n_in, n_out, n_scr = len(in_specs), len(out_specs), len(scratch_shapes)
    if comm is None:
        outs = pl.pallas_call(body, name=name, grid=grid, in_specs=in_specs, out_specs=out_specs, out_shape=out_shape,
                              scratch_shapes=scratch_shapes, compiler_params=_params(semantics))(*args)
        return list(outs), []
    ci, co = len(comm.ins), len(comm.out_shapes)

    def hosting(*refs):
        ins, cins = refs[:n_in], refs[n_in:n_in + ci]
        outs = refs[n_in + ci:n_in + ci + n_out]
        couts = refs[n_in + ci + n_out:n_in + ci + n_out + co]
        scr = refs[n_in + ci + n_out + co:n_in + ci + n_out + co + n_scr]
        csems = refs[n_in + ci + n_out + co + n_scr:]
        ids = [pl.program_id(d) for d in range(len(grid))]
        first = functools.reduce(jnp.logical_and, [i == 0 for i in ids])
        last = functools.reduce(jnp.logical_and, [i == g - 1 for i, g in zip(ids, grid)])

        @pl.when(first)
        def _():
            comm.start(cins, couts, csems)

        body(*ins, *outs, *scr)

        @pl.when(last)
        def _():
            comm.finish(cins, couts, csems)

    outs = pl.pallas_call(
        hosting, name=name, grid=grid,
        in_specs=list(in_specs) + [_HBM] * ci, out_specs=list(out_specs) + [_HBM] * co,
        out_shape=list(out_shape) + list(comm.out_shapes),
        scratch_shapes=list(scratch_shapes) + list(comm.sems),
        compiler_params=_params(("arbitrary",) * len(grid)),
    )(*args, *comm.ins)
    return list(outs[:n_out]), list(outs[n_out:])


MXU_FLOPS = 750e12
HBM_BYTES_PER_S = 3.0e12
ACC_RMW_BYTES_PER_S = 8e12
GRID_STEP_S = 0.4e-6
VMEM_COMPILER_RESERVE_BYTES = 8 * 1024 * 1024
MAX_TILE_ROWS, MAX_TILE_COLS = 2112, 2304


def _divisors(n, step):
    return [d for d in range(step, n + 1, step) if n % d == 0]


def _matmul_tiles(M, N, K, mode, out_sizes, n_extra):
    budget = VMEM_LIMIT_BYTES - VMEM_COMPILER_RESERVE_BYTES
    best = None
    for tk in _divisors(K, 16 if mode == "tn" else LANES):
        for tm in _divisors(M, LANES if mode == "tn" else 16):
            if tm > MAX_TILE_ROWS:
                continue
            for tn in _divisors(N, LANES):
                if tn > MAX_TILE_COLS:
                    continue
                need = 4 * tm * tk + 4 * tk * tn + 4 * tm * tn + 2 * tm * tn * sum(out_sizes) + 8 * n_extra * tm * tn
                if need > budget:
                    continue
                nk = K // tk
                steps = (M // tm) * (N // tn) * nk
                t_mxu = 2 * M * N * K / MXU_FLOPS + (nk > 1) * (8 * M * N * nk) / ACC_RMW_BYTES_PER_S
                t_hbm = (2 * M * K * (N // tn) + 2 * K * N * (M // tm) + M * N * (sum(out_sizes) + 4 * n_extra)) / HBM_BYTES_PER_S
                t = max(t_mxu, t_hbm) + steps * GRID_STEP_S
                if best is None or t < best[0]:
                    best = (t, tm, tn, tk)
    assert best is not None, (M, N, K, mode)
    return best[1:]


def matmul(a, b, mode, out_dtypes, name, extras=(), epi=None, tm=None, tn=None, tk=None, comm=None):
    if mode == "nn":
        (M, K), (K2, N) = a.shape, b.shape
    elif mode == "nt":
        (M, K), (N, K2) = a.shape, b.shape
    else:
        (K, M), (K2, N) = a.shape, b.shape
    assert K == K2, (a.shape, b.shape, mode)
    if not (tm and tn and tk):
        tm, tn, tk = _matmul_tiles(M, N, K, mode, [jnp.dtype(d).itemsize for d in out_dtypes], len(extras))
    nk = K // tk
    dims = {"nn": (((1,), (0,)), ((), ())), "nt": (((1,), (1,)), ((), ())), "tn": (((0,), (0,)), ((), ()))}[mode]
    n_extra, n_out = len(extras), len(out_dtypes)

    def body(*refs):
        a_ref, b_ref = refs[0], refs[1]
        extra_refs = refs[2:2 + n_extra]
        out_refs = refs[2 + n_extra:2 + n_extra + n_out]

        def finish(acc):
            outs = (acc,) if epi is None else epi(acc, *[r[...] for r in extra_refs])
            for o_ref, o in zip(out_refs, outs):
                o_ref[...] = o.astype(o_ref.dtype)

        part = lax.dot_general(a_ref[...], b_ref[...], dims, preferred_element_type=F32)
        if nk == 1:
            finish(part)
            return
        acc_ref = refs[-1]
        k = pl.program_id(2)

        @pl.when(k == 0)
        def _():
            acc_ref[...] = part

        @pl.when(k > 0)
        def _():
            acc_ref[...] += part

        @pl.when(k == nk - 1)
        def _():
            finish(acc_ref[...])

    a_spec = {"nn": pl.BlockSpec((tm, tk), lambda i, j, k: (i, k)),
              "nt": pl.BlockSpec((tm, tk), lambda i, j, k: (i, k)),
              "tn": pl.BlockSpec((tk, tm), lambda i, j, k: (k, i))}[mode]
    b_spec = {"nn": pl.BlockSpec((tk, tn), lambda i, j, k: (k, j)),
              "nt": pl.BlockSpec((tn, tk), lambda i, j, k: (j, k)),
              "tn": pl.BlockSpec((tk, tn), lambda i, j, k: (k, j))}[mode]
    o_spec = pl.BlockSpec((tm, tn), lambda i, j, k: (i, j))
    outs, couts = _hosted_call(
        body, name=name, grid=(M // tm, N // tn, nk),
        in_specs=[a_spec, b_spec] + [o_spec] * n_extra,
        out_specs=[o_spec] * n_out,
        out_shape=[jax.ShapeDtypeStruct((M, N), d) for d in out_dtypes],
        scratch_shapes=[pltpu.VMEM((tm, tn), F32)] if nk > 1 else [],
        semantics=("parallel", "parallel", "arbitrary"), args=(a, b, *extras), comm=comm)
    res = outs[0] if n_out == 1 else outs
    return res if comm is None else (res, couts)


class Dims:
    def __init__(self, d_model, seq, n_meta):
        self.D = d_model
        self.n_meta = n_meta
        self.T_real = seq + n_meta
        self.T = -(-self.T_real // LANES) * LANES
        self.H = d_model // 128
        self.DC = d_model // 2
        self.DP = d_model // 2
        self.PG = self.DP // 4
        self.QL = 512
        self.KL = 512
        self.ROPE = 64
        self.NOPE = 128
        self.QKH = 192
        self.HP = 256
        self.DFF = 4 * d_model
        self.o_gate = 0
        self.o_u = 3 * d_model
        self.o_b = self.o_u + self.DC
        self.o_c = self.o_b + self.DC
        self.o_ql = self.o_c + self.DC
        self.o_kl = self.o_ql + self.QL
        self.o_pool = self.o_kl + self.KL
        self.o_rope = self.o_pool + self.DP
        self.NIN = self.o_rope + 256
        self.tr = _pick(self.T, (384, 256, 128))
        self.tq = _pick(self.T, (384, 256, 128))


def _row_ids(i, tr):
    return i * tr + lax.broadcasted_iota(jnp.int32, (tr, 1), 0)


def rms_fwd(x, col_block, width, g, dm, name):
    tr = dm.tr

    def body(x_ref, g_ref, y_ref):
        xv = x_ref[...]
        r = lax.rsqrt(jnp.mean(xv * xv, axis=-1, keepdims=True) + EPS)
        y_ref[...] = (xv * r * g_ref[...]).astype(y_ref.dtype)

    return pl.pallas_call(
        body, name=name, grid=(dm.T // tr,),
        in_specs=[pl.BlockSpec((tr, width), lambda i: (i, col_block)), pl.BlockSpec((1, width), lambda i: (0, 0))],
        out_specs=pl.BlockSpec((tr, width), lambda i: (i, 0)),
        out_shape=jax.ShapeDtypeStruct((dm.T, width), BF16),
        compiler_params=_params(("parallel",)),
    )(x, g.reshape(1, width))


def rms_bwd(x, col_block, width, g, dy, dres, dm, name):
    tr = dm.tr
    has_res = dres is not None

    def body(*refs):
        if has_res:
            x_ref, g_ref, dy_ref, dres_ref, dx_ref, dxb_ref, dg_ref = refs
        else:
            x_ref, g_ref, dy_ref, dx_ref, dxb_ref, dg_ref = refs
        xv, dyv = x_ref[...], dy_ref[...]
        r = lax.rsqrt(jnp.mean(xv * xv, axis=-1, keepdims=True) + EPS)
        gdy = dyv * g_ref[...]
        dx = r * gdy - xv * (r * r * r) * jnp.mean(xv * gdy, axis=-1, keepdims=True)
        if has_res:
            dx = dx + dres_ref[...]
        dx_ref[...] = dx
        dxb_ref[...] = dx.astype(BF16)

        @pl.when(pl.program_id(0) == 0)
        def _():
            dg_ref[...] = jnp.zeros_like(dg_ref)

        dg_ref[...] += jnp.sum(dyv * xv * r, axis=0, keepdims=True)

    row = pl.BlockSpec((tr, width), lambda i: (i, 0))
    in_specs = [pl.BlockSpec((tr, width), lambda i: (i, col_block)), pl.BlockSpec((1, width), lambda i: (0, 0)), row]
    args = [x, g.reshape(1, width), dy]
    if has_res:
        in_specs.append(row)
        args.append(dres)
    return pl.pallas_call(
        body, name=name, grid=(dm.T // tr,),
        in_specs=in_specs,
        out_specs=[row, row, pl.BlockSpec((1, width), lambda i: (0, 0))],
        out_shape=[jax.ShapeDtypeStruct((dm.T, width), F32), jax.ShapeDtypeStruct((dm.T, width), BF16),
                   jax.ShapeDtypeStruct((1, width), F32)],
        compiler_params=_params(("arbitrary",)),
    )(*args)


def _fill_halo_buf(buf, src_fn, T, R, width):
    zeros = jnp.zeros((HALO, width), F32)
    buf[pl.ds(0, HALO), :] = zeros
    buf[pl.ds(HALO + T, HALO), :] = zeros

    def fill(r, c):
        r0 = pl.multiple_of(r * R, 8)
        buf[pl.ds(r0 + HALO, R), :] = src_fn(r0)
        return c

    lax.fori_loop(0, T // R, fill, 0)


def _back(win, sh):
    return pltpu.roll(win, sh, 0)


def _fwd(win, sh):
    return pltpu.roll(win, win.shape[0] - sh, 0)


def mixer_a_fwd(proj, conv_w, dm, name):
    T, cw = dm.T, 128
    R = dm.tr
    nb = dm.DC // cw

    def body(u_ref, b_ref, c_ref, w_ref, ya_ref, buf):
        _fill_halo_buf(buf, lambda r0: c_ref[pl.ds(r0, R), :] * u_ref[pl.ds(r0, R), :], T, R, cw)
        w0, w1, w2 = w_ref[0:1, :], w_ref[1:2, :], w_ref[2:3, :]

        def chunk(r, c):
            r0 = pl.multiple_of(r * R, 8)
            win = buf[pl.ds(r0, R + HALO), :]
            cv = w2 * win + w1 * _back(win, 1) + w0 * _back(win, 2)
            ya_ref[pl.ds(r0, R), :] = (b_ref[pl.ds(r0, R), :] * cv[HALO:, :]).astype(BF16)
            return c

        lax.fori_loop(0, T // R, chunk, 0)

    col = lambda off: pl.BlockSpec((T, cw), lambda j: (0, off // cw + j))
    return pl.pallas_call(
        body, name=name, grid=(nb,),
        in_specs=[col(dm.o_u), col(dm.o_b), col(dm.o_c), pl.BlockSpec((3, cw), lambda j: (0, j))],
        out_specs=pl.BlockSpec((T, cw), lambda j: (0, j)),
        out_shape=jax.ShapeDtypeStruct((T, dm.DC), BF16),
        scratch_shapes=[pltpu.VMEM((T + 2 * HALO, cw), F32)],
        compiler_params=_params(("parallel",)),
    )(proj, proj, proj, conv_w)


def mixer_a_bwd(proj, conv_w, dya, dm, name):
    T, cw = dm.T, 128
    R = dm.tr
    nb = dm.DC // cw

    def body(u_ref, b_ref, c_ref, w_ref, dya_ref, du_ref, db_ref, dc_ref, dw_ref, sbuf, gbuf):
        _fill_halo_buf(sbuf, lambda r0: c_ref[pl.ds(r0, R), :] * u_ref[pl.ds(r0, R), :], T, R, cw)
        _fill_halo_buf(gbuf, lambda r0: dya_ref[pl.ds(r0, R), :] * b_ref[pl.ds(r0, R), :], T, R, cw)
        w0, w1, w2 = w_ref[0:1, :], w_ref[1:2, :], w_ref[2:3, :]

        def chunk(r, acc):
            a0, a1, a2 = acc
            r0 = pl.multiple_of(r * R, 8)
            swin = sbuf[pl.ds(r0, R + HALO), :]
            s0, s1, s2 = swin[HALO:, :], _back(swin, 1)[HALO:, :], _back(swin, 2)[HALO:, :]
            gwin = gbuf[pl.ds(r0 + HALO, R + HALO), :]
            g0, g1, g2 = gwin[:R, :], _fwd(gwin, 1)[:R, :], _fwd(gwin, 2)[:R, :]
            cv = w2 * s0 + w1 * s1 + w0 * s2
            ds = w2 * g0 + w1 * g1 + w0 * g2
            db_ref[pl.ds(r0, R), :] = (dya_ref[pl.ds(r0, R), :] * cv).astype(BF16)
            du_ref[pl.ds(r0, R), :] = (ds * c_ref[pl.ds(r0, R), :]).astype(BF16)
            dc_ref[pl.ds(r0, R), :] = (ds * u_ref[pl.ds(r0, R), :]).astype(BF16)
            a2 = a2 + jnp.sum(g0 * s0, axis=0, keepdims=True)
            a1 = a1 + jnp.sum(g0 * s1, axis=0, keepdims=True)
            a0 = a0 + jnp.sum(g0 * s2, axis=0, keepdims=True)
            return a0, a1, a2

        z = jnp.zeros((1, cw), F32)
        a0, a1, a2 = lax.fori_loop(0, T // R, chunk, (z, z, z))
        dw_ref[0:1, :] = a0
        dw_ref[1:2, :] = a1
        dw_ref[2:3, :] = a2

    col = lambda off: pl.BlockSpec((T, cw), lambda j: (0, off // cw + j))
    own = pl.BlockSpec((T, cw), lambda j: (0, j))
    o = jax.ShapeDtypeStruct((T, dm.DC), BF16)
    return pl.pallas_call(
        body, name=name, grid=(nb,),
        in_specs=[col(dm.o_u), col(dm.o_b), col(dm.o_c), pl.BlockSpec((3, cw), lambda j: (0, j)), own],
        out_specs=[own, own, own, pl.BlockSpec((3, cw), lambda j: (0, j))],
        out_shape=[o, o, o, jax.ShapeDtypeStruct((3, dm.DC), F32)],
        scratch_shapes=[pltpu.VMEM((T + 2 * HALO, cw), F32), pltpu.VMEM((T + 2 * HALO, cw), F32)],
        compiler_params=_params(("parallel",)),
    )(proj, proj, proj, conv_w, dya)


def _rope(x, C, S):
    return x * C + (pltpu.roll(x, 32, 1) - pltpu.roll(x, 96, 1)) * S


def _rope_t(dy, C, S):
    return dy * C + (pltpu.roll(dy, 96, 1) - pltpu.roll(dy, 32, 1)) * S


HP = 4


def qk_prep_fwd(q0, kv0, proj, qn, kn, C, S, dm, name, comm=None):
    T, H, tr = dm.T, dm.H, dm.tr
    inv = 1.0 / dm.QKH

    def body(q0_ref, kv_ref, kr_ref, qn_ref, kn_ref, c_ref, s_ref, q_ref, k_ref, v_ref):
        Cv, Sv = c_ref[...], s_ref[...]
        kb = kr_ref[...]
        kb2 = jnp.sum(kb * kb, -1, keepdims=True)
        for a in range(HP):
            lo, mid, hi = 256 * a, 256 * a + 128, 256 * (a + 1)
            qa, qb = q0_ref[:, lo:mid], q0_ref[:, mid:hi]
            r = lax.rsqrt((jnp.sum(qa * qa, -1, keepdims=True) + jnp.sum(qb * qb, -1, keepdims=True)) * inv + EPS)
            q_ref[:, lo:mid] = (qa * r * qn_ref[:, :128]).astype(BF16)
            q_ref[:, mid:hi] = _rope(qb * r * qn_ref[:, 128:], Cv, Sv).astype(BF16)
            ka = kv_ref[:, lo:mid]
            r = lax.rsqrt((jnp.sum(ka * ka, -1, keepdims=True) + kb2) * inv + EPS)
            k_ref[:, lo:mid] = (ka * r * kn_ref[:, :128]).astype(BF16)
            k_ref[:, mid:hi] = _rope(kb * r * kn_ref[:, 128:], Cv, Sv).astype(BF16)
            v_ref[:, 128 * a:128 * (a + 1)] = kv_ref[:, mid:hi].astype(BF16)

    head = pl.BlockSpec((tr, 256 * HP), lambda i, h: (i, h))
    gain = pl.BlockSpec((1, 256), lambda i, h: (0, 0))
    tab = pl.BlockSpec((tr, 128), lambda i, h: (i, 0))
    outs, couts = _hosted_call(
        body, name=name, grid=(T // tr, H // HP),
        in_specs=[head, head, pl.BlockSpec((tr, 128), lambda i, h: (i, dm.o_rope // 128)), gain, gain, tab, tab],
        out_specs=[head, head, pl.BlockSpec((tr, 128 * HP), lambda i, h: (i, h))],
        out_shape=[jax.ShapeDtypeStruct((T, H * 256), BF16), jax.ShapeDtypeStruct((T, H * 256), BF16),
                   jax.ShapeDtypeStruct((T, H * 128), BF16)],
        scratch_shapes=[], semantics=("parallel", "parallel"), args=(q0, kv0, proj, qn, kn, C, S), comm=comm)
    return outs if comm is None else (outs, couts)


def qk_prep_bwd(q0, kv0, proj, qn, kn, C, S, dq, dk, dv, dm, name):
    T, H, tr = dm.T, dm.H, dm.tr
    inv = 1.0 / dm.QKH

    def body(q0_ref, kv_ref, kr_ref, qn_ref, kn_ref, c_ref, s_ref, dq_ref, dk_ref, dv_ref,
             dq0_ref, dkv_ref, dkr_ref, dqn_ref, dkn_ref):
        i, h = pl.program_id(0), pl.program_id(1)
        Cv, Sv = c_ref[...], s_ref[...]

        def norm_bwd(xa, xb, ga, gb, dya, dyb):
            r = lax.rsqrt((jnp.sum(xa * xa, -1, keepdims=True) + jnp.sum(xb * xb, -1, keepdims=True)) * inv + EPS)
            dzb = _rope_t(dyb, Cv, Sv)
            gda, gdb = ga * dya, gb * dzb
            dot = (jnp.sum(xa * gda, -1, keepdims=True) + jnp.sum(xb * gdb, -1, keepdims=True)) * inv
            r3 = r * r * r
            dxa = r * gda - xa * r3 * dot
            dxb = r * gdb - xb * r3 * dot
            dga = jnp.sum(dya * xa * r, axis=0, keepdims=True)
            dgb = jnp.sum(dzb * xb * r, axis=0, keepdims=True)
            return dxa, dxb, dga, dgb

        @pl.when((i == 0) & (h == 0))
        def _():
            dqn_ref[...] = jnp.zeros_like(dqn_ref)
            dkn_ref[...] = jnp.zeros_like(dkn_ref)

        @pl.when(h == 0)
        def _():
            dkr_ref[...] = jnp.zeros_like(dkr_ref)

        for a in range(HP):
            lo, mid, hi = 256 * a, 256 * a + 128, 256 * (a + 1)
            dxa, dxb, dga, dgb = norm_bwd(q0_ref[:, lo:mid], q0_ref[:, mid:hi], qn_ref[:, :128], qn_ref[:, 128:],
                                          dq_ref[:, lo:mid], dq_ref[:, mid:hi])
            dq0_ref[:, lo:mid] = dxa.astype(BF16)
            dq0_ref[:, mid:hi] = dxb.astype(BF16)
            dqn_ref[:, :128] += dga
            dqn_ref[:, 128:] += dgb
            dxa, dxb, dga, dgb = norm_bwd(kv_ref[:, lo:mid], kr_ref[...], kn_ref[:, :128], kn_ref[:, 128:],
                                          dk_ref[:, lo:mid], dk_ref[:, mid:hi])
            dkv_ref[:, lo:mid] = dxa.astype(BF16)
            dkv_ref[:, mid:hi] = dv_ref[:, 128 * a:128 * (a + 1)].astype(BF16)
            dkn_ref[:, :128] += dga
            dkn_ref[:, 128:] += dgb
            dkr_ref[...] += dxb

    head = pl.BlockSpec((tr, 256 * HP), lambda i, h: (i, h))
    gain = pl.BlockSpec((1, 256), lambda i, h: (0, 0))
    tab = pl.BlockSpec((tr, 128), lambda i, h: (i, 0))
    return pl.pallas_call(
        body, name=name, grid=(T // tr, H // HP),
        in_specs=[head, head, pl.BlockSpec((tr, 128), lambda i, h: (i, dm.o_rope // 128)), gain, gain, tab, tab,
                  head, head, pl.BlockSpec((tr, 128 * HP), lambda i, h: (i, h))],
        out_specs=[head, head, tab, gain, gain],
        out_shape=[jax.ShapeDtypeStruct((T, H * 256), BF16), jax.ShapeDtypeStruct((T, H * 256), BF16),
                   jax.ShapeDtypeStruct((T, 128), F32), jax.ShapeDtypeStruct((1, 256), F32),
                   jax.ShapeDtypeStruct((1, 256), F32)],
        compiler_params=_params(("arbitrary", "arbitrary")),
    )(q0, kv0, proj, qn, kn, C, S, dq, dk, dv)


_NT = (((1,), (1,)), ((), ()))


def _causal_mask(t):
    return lax.broadcasted_iota(jnp.int32, (t, t), 0) >= lax.broadcasted_iota(jnp.int32, (t, t), 1)


def _causal_mask_t(t):
    return lax.broadcasted_iota(jnp.int32, (t, t), 0) <= lax.broadcasted_iota(jnp.int32, (t, t), 1)


HB = 2


def attn_fwd(q, k, v, dm, name, comm=None):
    T, H, tq = dm.T, dm.H, dm.tq
    scale = dm.QKH ** -0.5

    def body(q_ref, k_ref, v_ref, o_ref, ob_ref, lse_ref):
        qi = pl.program_id(1)

        def step(j, carry, masked):
            j0 = pl.multiple_of(j * tq, tq)
            ss = [lax.dot_general(q_ref[:, 256 * a:256 * (a + 1)], k_ref[pl.ds(j0, tq), 256 * a:256 * (a + 1)], _NT,
                                  preferred_element_type=F32) for a in range(HB)]
            out = []
            for a in range(HB):
                m, l, acc = carry[a]
                s = ss[a] * scale
                if masked:
                    s = jnp.where(_causal_mask(tq), s, -jnp.inf)
                m_new = jnp.maximum(m, jnp.max(s, -1, keepdims=True))
                alpha = jnp.exp(m - m_new)
                p = jnp.exp(s - m_new)
                l = alpha * l + jnp.sum(p, -1, keepdims=True)
                acc = alpha * acc + jnp.dot(p.astype(BF16), v_ref[pl.ds(j0, tq), 128 * a:128 * (a + 1)],
                                            preferred_element_type=F32)
                out.append((m_new, l, acc))
            return tuple(out)

        one = (jnp.full((tq, 1), -jnp.inf, F32), jnp.zeros((tq, 1), F32), jnp.zeros((tq, 128), F32))
        carry = lax.fori_loop(0, qi, lambda j, c: step(j, c, False), (one,) * HB)
        carry = step(qi, carry, True)
        for a in range(HB):
            m, l, acc = carry[a]
            o = acc / l
            o_ref[:, 128 * a:128 * (a + 1)] = o
            ob_ref[:, 128 * a:128 * (a + 1)] = o.astype(BF16)
            lse_ref[a] = m + jnp.log(l)

    outs, couts = _hosted_call(
        body, name=name, grid=(H // HB, T // tq),
        in_specs=[pl.BlockSpec((tq, 256 * HB), lambda h, i: (i, h)), pl.BlockSpec((T, 256 * HB), lambda h, i: (0, h)),
                  pl.BlockSpec((T, 128 * HB), lambda h, i: (0, h))],
        out_specs=[pl.BlockSpec((tq, 128 * HB), lambda h, i: (i, h)), pl.BlockSpec((tq, 128 * HB), lambda h, i: (i, h)),
                   pl.BlockSpec((HB, tq, 1), lambda h, i: (h, i, 0))],
        out_shape=[jax.ShapeDtypeStruct((T, H * 128), F32), jax.ShapeDtypeStruct((T, H * 128), BF16),
                   jax.ShapeDtypeStruct((H, T, 1), F32)],
        scratch_shapes=[], semantics=("parallel", "parallel"), args=(q, k, v), comm=comm)
    return outs if comm is None else (outs, couts)


def attn_delta(do, o, dm, name):
    T, H, tr = dm.T, dm.H, dm.tr

    def body(do_ref, o_ref, delta_ref, dob_ref):
        for a in range(H):
            d = do_ref[:, 128 * a:128 * (a + 1)]
            delta_ref[a] = jnp.sum(d * o_ref[:, 128 * a:128 * (a + 1)], -1, keepdims=True)
            dob_ref[:, 128 * a:128 * (a + 1)] = d.astype(BF16)

    blk = pl.BlockSpec((tr, 128 * H), lambda i: (i, 0))
    return pl.pallas_call(
        body, name=name, grid=(T // tr,),
        in_specs=[blk, blk],
        out_specs=[pl.BlockSpec((H, tr, 1), lambda i: (0, i, 0)), blk],
        out_shape=[jax.ShapeDtypeStruct((H, T, 1), F32), jax.ShapeDtypeStruct((T, H * 128), BF16)],
        compiler_params=_params(("parallel",)),
    )(do, o)


def attn_bwd_dq(q, k, v, do, lse, delta, dm, name, comm=None):
    T, H, tq = dm.T, dm.H, dm.tq
    scale = dm.QKH ** -0.5

    def body(q_ref, k_ref, v_ref, do_ref, lse_ref, delta_ref, dq_ref):
        qi = pl.program_id(1)

        def step(j, dqs, masked):
            j0 = pl.multiple_of(j * tq, tq)
            hk = lambda a: slice(256 * a, 256 * (a + 1))
            hv = lambda a: slice(128 * a, 128 * (a + 1))
            ss = [lax.dot_general(q_ref[:, hk(a)], k_ref[pl.ds(j0, tq), hk(a)], _NT, preferred_element_type=F32)
                  for a in range(HB)]
            dps = [lax.dot_general(do_ref[:, hv(a)], v_ref[pl.ds(j0, tq), hv(a)], _NT, preferred_element_type=F32)
                   for a in range(HB)]
            out = []
            for a in range(HB):
                p = jnp.exp(ss[a] * scale - lse_ref[a])
                if masked:
                    p = jnp.where(_causal_mask(tq), p, 0.0)
                ds = p * (dps[a] - delta_ref[a]) * scale
                out.append(dqs[a] + jnp.dot(ds.astype(BF16), k_ref[pl.ds(j0, tq), hk(a)], preferred_element_type=F32))
            return tuple(out)

        dqs = lax.fori_loop(0, qi, lambda j, c: step(j, c, False), (jnp.zeros((tq, 256), F32),) * HB)
        dqs = step(qi, dqs, True)
        for a in range(HB):
            dq_ref[:, 256 * a:256 * (a + 1)] = dqs[a]

    stat = pl.BlockSpec((HB, tq, 1), lambda h, i: (h, i, 0))
    outs, couts = _hosted_call(
        body, name=name, grid=(H // HB, T // tq),
        in_specs=[pl.BlockSpec((tq, 256 * HB), lambda h, i: (i, h)), pl.BlockSpec((T, 256 * HB), lambda h, i: (0, h)),
                  pl.BlockSpec((T, 128 * HB), lambda h, i: (0, h)), pl.BlockSpec((tq, 128 * HB), lambda h, i: (i, h)),
                  stat, stat],
        out_specs=[pl.BlockSpec((tq, 256 * HB), lambda h, i: (i, h))],
        out_shape=[jax.ShapeDtypeStruct((T, H * 256), F32)],
        scratch_shapes=[], semantics=("parallel", "parallel"), args=(q, k, v, do, lse, delta), comm=comm)
    return outs[0] if comm is None else (outs[0], couts)


def attn_bwd_dkv(q, k, v, do, lse_rows, delta_rows, dm, name, comm=None):
    T, H, tq = dm.T, dm.H, dm.tq
    nq = T // tq
    scale = dm.QKH ** -0.5

    def body(q_ref, k_ref, v_ref, do_ref, lse_ref, delta_ref, dk_ref, dv_ref):
        kj = pl.program_id(1)

        def step(i, carry, masked):
            i0 = pl.multiple_of(i * tq, tq)
            hk = lambda a: slice(256 * a, 256 * (a + 1))
            hv = lambda a: slice(128 * a, 128 * (a + 1))
            sts = [lax.dot_general(k_ref[:, hk(a)], q_ref[pl.ds(i0, tq), hk(a)], _NT, preferred_element_type=F32)
                   for a in range(HB)]
            dpts = [lax.dot_general(v_ref[:, hv(a)], do_ref[pl.ds(i0, tq), hv(a)], _NT, preferred_element_type=F32)
                    for a in range(HB)]
            out = []
            for a in range(HB):
                dk, dv = carry[a]
                pt = jnp.exp(sts[a] * scale - lse_ref[a, i])
                if masked:
                    pt = jnp.where(_causal_mask_t(tq), pt, 0.0)
                dv = dv + jnp.dot(pt.astype(BF16), do_ref[pl.ds(i0, tq), hv(a)], preferred_element_type=F32)
                dst = pt * (dpts[a] - delta_ref[a, i]) * scale
                dk = dk + jnp.dot(dst.astype(BF16), q_ref[pl.ds(i0, tq), hk(a)], preferred_element_type=F32)
                out.append((dk, dv))
            return tuple(out)

        carry = step(kj, ((jnp.zeros((tq, 256), F32), jnp.zeros((tq, 128), F32)),) * HB, True)
        carry = lax.fori_loop(kj + 1, nq, lambda i, c: step(i, c, False), carry)
        for a in range(HB):
            dk_ref[:, 256 * a:256 * (a + 1)] = carry[a][0]
            dv_ref[:, 128 * a:128 * (a + 1)] = carry[a][1]

    rows = pl.BlockSpec((HB, nq, 1, tq), lambda h, j: (h, 0, 0, 0))
    outs, couts = _hosted_call(
        body, name=name, grid=(H // HB, nq),
        in_specs=[pl.BlockSpec((T, 256 * HB), lambda h, j: (0, h)), pl.BlockSpec((tq, 256 * HB), lambda h, j: (j, h)),
                  pl.BlockSpec((tq, 128 * HB), lambda h, j: (j, h)), pl.BlockSpec((T, 128 * HB), lambda h, j: (0, h)),
                  rows, rows],
        out_specs=[pl.BlockSpec((tq, 256 * HB), lambda h, j: (j, h)), pl.BlockSpec((tq, 128 * HB), lambda h, j: (j, h))],
        out_shape=[jax.ShapeDtypeStruct((T, H * 256), F32), jax.ShapeDtypeStruct((T, H * 128), F32)],
        scratch_shapes=[], semantics=("parallel", "parallel"), args=(q, k, v, do, lse_rows, delta_rows), comm=comm)
    return outs if comm is None else (outs, couts)


def _window_sum(win, g, shift):
    s1 = win + shift(win, 1)
    s2 = s1 + shift(s1, 2)
    s3 = s2 + shift(s2, 4)
    s4 = s3 + shift(s3, 8)
    return jnp.where(g == 0, s1, jnp.where(g == 1, s2, jnp.where(g == 2, s3, s4)))


def _count(r0, R, g, T_unused=None):
    t = r0 + lax.broadcasted_iota(jnp.int32, (R, 1), 0)
    return jnp.minimum(t + 1, jnp.left_shift(2, g)).astype(F32)


def pool_fwd(proj, pw, ps, dm, name):
    T, PG, R = dm.T, dm.PG, dm.tr

    def body(x_ref, pw_ref, ps_ref, pooled_ref, mixed_ref, yc_ref, buf):
        g = pl.program_id(0)
        _fill_halo_buf(buf, lambda r0: x_ref[pl.ds(r0, R), :], T, R, PG)

        def chunk(r, c):
            r0 = pl.multiple_of(r * R, 8)
            win = buf[pl.ds(r0, R + HALO), :]
            ws = _window_sum(win, g, _back)[HALO:, :]
            pooled = (ws / _count(r0, R, g) - win[HALO:, :]).astype(BF16)
            pooled_ref[pl.ds(r0, R), :] = pooled
            mixed = jnp.dot(pooled, pw_ref[...], preferred_element_type=F32)
            mixed_ref[pl.ds(r0, R), :] = mixed
            yc_ref[pl.ds(r0, R), :] = (mixed * ps_ref[...]).astype(BF16)
            return c

        lax.fori_loop(0, T // R, chunk, 0)

    own = pl.BlockSpec((T, PG), lambda g: (0, g))
    return pl.pallas_call(
        body, name=name, grid=(4,),
        in_specs=[pl.BlockSpec((T, PG), lambda g: (0, dm.o_pool // PG + g)), pl.BlockSpec((None, PG, PG), lambda g: (g, 0, 0)),
                  pl.BlockSpec((1, PG), lambda g: (0, g))],
        out_specs=[own, own, own],
        out_shape=[jax.ShapeDtypeStruct((T, dm.DP), BF16), jax.ShapeDtypeStruct((T, dm.DP), F32),
                   jax.ShapeDtypeStruct((T, dm.DP), BF16)],
        scratch_shapes=[pltpu.VMEM((T + 2 * HALO, PG), F32)],
        compiler_params=_params(("parallel",)),
    )(proj, pw, ps)


def pool_bwd(dyc, mixed, pooled, pw, ps, dm, name):
    T, PG, R = dm.T, dm.PG, dm.tr
    _TN = (((0,), (0,)), ((), ()))

    def body(dyc_ref, mixed_ref, pooled_ref, pw_ref, ps_ref, dx_ref, dpw_ref, dps_ref, qbuf, dpbuf):
        g = pl.program_id(0)
        zeros = jnp.zeros((HALO, PG), F32)
        qbuf[pl.ds(0, HALO), :] = zeros
        qbuf[pl.ds(HALO + T, HALO), :] = zeros
        dpw_ref[...] = jnp.zeros_like(dpw_ref)

        def first(r, dps):
            r0 = pl.multiple_of(r * R, 8)
            dyc = dyc_ref[pl.ds(r0, R), :]
            dps = dps + jnp.sum(dyc * mixed_ref[pl.ds(r0, R), :], axis=0, keepdims=True)
            dmb = (dyc * ps_ref[...]).astype(BF16)
            dpw_ref[...] += lax.dot_general(pooled_ref[pl.ds(r0, R), :], dmb, _TN, preferred_element_type=F32)
            dp = lax.dot_general(dmb, pw_ref[...], _NT, preferred_element_type=F32)
            dpbuf[pl.ds(r0, R), :] = dp
            qbuf[pl.ds(r0 + HALO, R), :] = dp / _count(r0, R, g)
            return dps

        dps_ref[...] = lax.fori_loop(0, T // R, first, jnp.zeros((1, PG), F32))

        def second(r, c):
            r0 = pl.multiple_of(r * R, 8)
            win = qbuf[pl.ds(r0 + HALO, R + HALO), :]
            ws = _window_sum(win, g, _fwd)[:R, :]
            dx_ref[pl.ds(r0, R), :] = (ws - dpbuf[pl.ds(r0, R), :]).astype(BF16)
            return c

        lax.fori_loop(0, T // R, second, 0)

    own = pl.BlockSpec((T, PG), lambda g: (0, g))
    return pl.pallas_call(
        body, name=name, grid=(4,),
        in_specs=[own, own, own, pl.BlockSpec((None, PG, PG), lambda g: (g, 0, 0)), pl.BlockSpec((1, PG), lambda g: (0, g))],
        out_specs=[own, pl.BlockSpec((None, PG, PG), lambda g: (g, 0, 0)), pl.BlockSpec((1, PG), lambda g: (0, g))],
        out_shape=[jax.ShapeDtypeStruct((T, dm.DP), BF16), jax.ShapeDtypeStruct((4, PG, PG), F32),
                   jax.ShapeDtypeStruct((1, dm.DP), F32)],
        scratch_shapes=[pltpu.VMEM((T + 2 * HALO, PG), F32), pltpu.VMEM((T, PG), F32)],
        compiler_params=_params(("parallel",)),
    )(dyc, mixed, pooled, pw, ps)


def _sigmoid(x):
    return 1.0 / (1.0 + jnp.exp(-x))


def merge_fwd(proj, A, B, C, dm, name):
    T, D, tr, tc = dm.T, dm.D, dm.tr, 512
    nc = D // tc

    def body(g0, g1, g2, a, b, c, out):
        out[...] = (_sigmoid(g0[...]) * a[...] + _sigmoid(g1[...]) * b[...] + _sigmoid(g2[...]) * c[...]).astype(BF16)

    gate = lambda k: pl.BlockSpec((tr, tc), lambda i, j: (i, k * nc + j))
    own = pl.BlockSpec((tr, tc), lambda i, j: (i, j))
    return pl.pallas_call(
        body, name=name, grid=(T // tr, nc),
        in_specs=[gate(0), gate(1), gate(2), own, own, own],
        out_specs=own,
        out_shape=jax.ShapeDtypeStruct((T, D), BF16),
        compiler_params=_params(("parallel", "parallel")),
    )(proj, proj, proj, A, B, C)


def merge_bwd(proj, A, B, C, dmerged, dm, name):
    T, D, tr, tc = dm.T, dm.D, dm.tr, 512
    nc = D // tc

    def body(g0, g1, g2, a, b, c, dmr, da, db, dc, dl0, dl1, dl2):
        d = dmr[...]
        for g_ref, y_ref, dy_ref, dl_ref in ((g0, a, da, dl0), (g1, b, db, dl1), (g2, c, dc, dl2)):
            s = _sigmoid(g_ref[...])
            dy_ref[...] = (d * s).astype(BF16)
            dl_ref[...] = (d * y_ref[...] * s * (1.0 - s)).astype(BF16)

    gate = lambda k: pl.BlockSpec((tr, tc), lambda i, j: (i, k * nc + j))
    own = pl.BlockSpec((tr, tc), lambda i, j: (i, j))
    o = jax.ShapeDtypeStruct((T, D), BF16)
    return pl.pallas_call(
        body, name=name, grid=(T // tr, nc),
        in_specs=[gate(0), gate(1), gate(2), own, own, own, own],
        out_specs=[own] * 6,
        out_shape=[o] * 6,
        compiler_params=_params(("parallel", "parallel")),
    )(proj, proj, proj, A, B, C, dmerged)


def loss_head(y, target, dm, name):
    T, D, tr = dm.T, dm.D, dm.tr

    def body(y_ref, t_ref, dy_ref, dyb_ref, loss_ref):
        i = pl.program_id(0)
        t = _row_ids(i, tr)
        real = (t >= dm.n_meta) & (t < dm.T_real)
        err = jnp.where(real, y_ref[...] - t_ref[...], 0.0)
        dy = err * (1.0 / D)
        dy_ref[...] = dy
        dyb_ref[...] = dy.astype(BF16)

        @pl.when(i == 0)
        def _():
            loss_ref[...] = jnp.zeros_like(loss_ref)

        loss_ref[...] += 0.5 * jnp.sum(jnp.sum(err * err, axis=-1, keepdims=True) * (1.0 / D))

    row = pl.BlockSpec((tr, D), lambda i: (i, 0))
    return pl.pallas_call(
        body, name=name, grid=(T // tr,),
        in_specs=[row, row],
        out_specs=[row, row, pl.BlockSpec((8, LANES), lambda i: (0, 0))],
        out_shape=[jax.ShapeDtypeStruct((T, D), F32), jax.ShapeDtypeStruct((T, D), BF16),
                   jax.ShapeDtypeStruct((8, LANES), F32)],
        compiler_params=_params(("arbitrary",)),
    )(y, target)


def adamw(w, m, v, parts, name):
    R, C = w.shape
    P = parts.shape[0]
    br = R
    for cand in (512, 256, 128, 64, 32, 16, 8):
        if R % cand == 0 and cand * C * 4 <= (1 << 20):
            br = cand
            break
    if R * C * 4 <= (1 << 20):
        br = R

    def body(w_ref, m_ref, v_ref, p_ref, g_ref, d_ref, nm_ref, nv_ref):
        g = p_ref[0].astype(F32)
        for k in range(1, P):
            g = g + p_ref[k].astype(F32)
        mm = ADAM_B1 * m_ref[...] + (1.0 - ADAM_B1) * g
        vv = ADAM_B2 * v_ref[...] + (1.0 - ADAM_B2) * (g * g)
        m_hat = mm / (1.0 - ADAM_B1 ** ADAM_STEP)
        v_hat = vv / (1.0 - ADAM_B2 ** ADAM_STEP)
        g_ref[...] = g
        d_ref[...] = -ADAM_LR * (m_hat / (jnp.sqrt(v_hat) + ADAM_EPS) + ADAM_WD * w_ref[...])
        nm_ref[...] = mm
        nv_ref[...] = vv

    blk = pl.BlockSpec((br, C), lambda i: (i, 0))
    o = jax.ShapeDtypeStruct((R, C), F32)
    return pl.pallas_call(
        body, name=name, grid=(R // br,),
        in_specs=[blk, blk, blk, pl.BlockSpec((P, br, C), lambda i: (0, i, 0))],
        out_specs=[blk] * 4,
        out_shape=[o] * 4,
        compiler_params=_params(("parallel",)),
    )(w, m, v, parts)


def sum_parts(parts, name):
    P, R, C = parts.shape

    def body(p_ref, o_ref):
        acc = p_ref[0]
        for k in range(1, P):
            acc = acc + p_ref[k]
        o_ref[...] = acc

    return pl.pallas_call(
        body, name=name, grid=(1,),
        in_specs=[pl.BlockSpec((P, R, C), lambda i: (0, 0, 0))],
        out_specs=pl.BlockSpec((R, C), lambda i: (0, 0)),
        out_shape=jax.ShapeDtypeStruct((R, C), F32),
        compiler_params=_params(("arbitrary",)),
    )(parts)


def add_sibling(parts, got, core, name):
    _, _, R, C = parts.shape
    br = _pick(R, (1024, 512, 256, 128, 64, 32, 16))

    def body(c_ref, a_ref, b_ref, o_ref):
        o_ref[...] = (a_ref[...].astype(F32) + b_ref[...].astype(F32)).astype(BF16)

    blk = pl.BlockSpec((None, br, C), lambda ch, i, c: (ch, i, 0))
    return pl.pallas_call(
        body, name=name,
        grid_spec=pltpu.PrefetchScalarGridSpec(
            num_scalar_prefetch=1, grid=(4, R // br),
            in_specs=[pl.BlockSpec((None, None, br, C), lambda ch, i, c: (ch, c[0], i, 0)), blk],
            out_specs=blk),
        out_shape=jax.ShapeDtypeStruct((4, R, C), BF16), compiler_params=_params(("parallel", "parallel")),
    )(core, parts, got)


def all_gather(arrs, name):
    n = len(arrs)

    def body(*refs):
        ins, outs = refs[:n], refs[n:2 * n]
        send_sems, recv_sems, local_sems = refs[2 * n:]
        x, y, c = _place()
        me, sibling = (x, y, c), (x, y, 1 - c)
        chips = [(1 - x, y), (x, 1 - y), (1 - x, 1 - y)]

        def copy(a, k, block, to, src=None):
            px, py, pc = block
            dst = outs[a].at[4 * px + 2 * py + pc]
            return pltpu.make_async_remote_copy(
                src_ref=dst if src is None else src, dst_ref=dst,
                send_sem=send_sems.at[7 * a + k], recv_sem=recv_sems.at[7 * a + k],
                device_id=to, device_id_type=_MESH)

        started = []
        for a in range(n):
            mine = pltpu.make_async_copy(ins[a], outs[a].at[4 * x + 2 * y + c], local_sems.at[a])
            mine.start()
            started.append(mine)
        sends = []
        for a in range(n):
            sends.append(copy(a, 0, me, sibling, src=ins[a]))
            for j, chip in enumerate(chips):
                sends.append(copy(a, 1 + j, me, (*chip, c), src=ins[a]))
        for cp in sends:
            cp.start()
        for j, chip in enumerate(chips):
            for a in range(n):
                copy(a, 1 + j, (*chip, c), me).wait_recv()
                fwd = copy(a, 4 + j, (*chip, c), sibling)
                fwd.start()
                sends.append(fwd)
        for a in range(n):
            copy(a, 0, sibling, me).wait_recv()
            for j, chip in enumerate(chips):
                copy(a, 4 + j, (*chip, 1 - c), me).wait_recv()
        for cp in sends:
            cp.wait_send()
        for cp in started:
            cp.wait()

    outs = pl.pallas_call(
        body, name=name,
        in_specs=[_HBM] * n, out_specs=[_HBM] * n,
        out_shape=[jax.ShapeDtypeStruct((8,) + a.shape, a.dtype) for a in arrs],
        scratch_shapes=[pltpu.SemaphoreType.DMA((7 * n,)), pltpu.SemaphoreType.DMA((7 * n,)), pltpu.SemaphoreType.DMA((n,))],
    )(*arrs)
    return list(outs)


def sibling_exchange(arrs, name):
    n = len(arrs)

    def body(*refs):
        ins, got = refs[:n], refs[n:2 * n]
        send_sems, recv_sems = refs[2 * n:]
        x, y, c = _place()
        work = []
        for a in range(n):
            for ch in range(4):
                cp = pltpu.make_async_remote_copy(
                    src_ref=ins[a].at[ch, 1 - c], dst_ref=got[a].at[ch],
                    send_sem=send_sems.at[4 * a + ch], recv_sem=recv_sems.at[4 * a + ch],
                    device_id=(x, y, 1 - c), device_id_type=_MESH)
                cp.start()
                work.append(cp)
        for cp in work:
            cp.wait()

    outs = pl.pallas_call(
        body, name=name,
        in_specs=[_HBM] * n, out_specs=[_HBM] * n,
        out_shape=[jax.ShapeDtypeStruct((4,) + a.shape[2:], a.dtype) for a in arrs],
        scratch_shapes=[pltpu.SemaphoreType.DMA((4 * n,)), pltpu.SemaphoreType.DMA((4 * n,))],
    )(*arrs)
    return list(outs)


def _remote(src, dst, send_sem, recv_sem, to):
    return pltpu.make_async_remote_copy(src_ref=src, dst_ref=dst, send_sem=send_sem, recv_sem=recv_sem,
                                        device_id=to, device_id_type=_MESH)


def gather_ici(blocks):
    n = len(blocks)

    def copies(cins, couts, sems):
        send_sems, recv_sems, local_sems = sems
        x, y, c = _place()
        mine = 4 * x + 2 * y + c
        local, sends, recvs = [], [], []
        for a in range(n):
            local.append(pltpu.make_async_copy(cins[a], couts[a].at[mine], local_sems.at[a]))
            for j, (px, py) in enumerate([(1 - x, y), (x, 1 - y), (1 - x, 1 - y)]):
                k = 3 * a + j
                sends.append(_remote(cins[a], couts[a].at[mine], send_sems.at[k], recv_sems.at[k], (px, py, c)))
                recvs.append(_remote(cins[a], couts[a].at[4 * px + 2 * py + c], send_sems.at[k], recv_sems.at[k], (px, py, c)))
        return local, sends, recvs

    def start(cins, couts, sems):
        local, sends, _ = copies(cins, couts, sems)
        for cp in local + sends:
            cp.start()

    def finish(cins, couts, sems):
        local, sends, recvs = copies(cins, couts, sems)
        for cp in sends:
            cp.wait_send()
        for cp in recvs:
            cp.wait_recv()
        for cp in local:
            cp.wait()

    return Hosted(list(blocks), [jax.ShapeDtypeStruct((8,) + b.shape, b.dtype) for b in blocks],
                  [pltpu.SemaphoreType.DMA((3 * n,)), pltpu.SemaphoreType.DMA((3 * n,)), pltpu.SemaphoreType.DMA((n,))],
                  start, finish)


def fill_sibling(stks, name):
    n = len(stks)

    def body(*refs):
        ins, outs = refs[:n], refs[n:2 * n]
        send_sems, recv_sems = refs[2 * n:]
        x, y, c = _place()
        sends, recvs = [], []
        for a in range(n):
            for ch in range(4):
                k = 4 * a + ch
                sends.append(_remote(ins[a].at[2 * ch + c], outs[a].at[2 * ch + c], send_sems.at[k], recv_sems.at[k], (x, y, 1 - c)))
                recvs.append(_remote(ins[a].at[2 * ch + c], outs[a].at[2 * ch + 1 - c], send_sems.at[k], recv_sems.at[k], (x, y, 1 - c)))
        for cp in sends:
            cp.start()
        for cp in sends:
            cp.wait_send()
        for cp in recvs:
            cp.wait_recv()

    outs = pl.pallas_call(
        body, name=name,
        in_specs=[_HBM] * n, out_specs=[_HBM] * n,
        out_shape=[jax.ShapeDtypeStruct(s.shape, s.dtype) for s in stks],
        scratch_shapes=[pltpu.SemaphoreType.DMA((4 * n,)), pltpu.SemaphoreType.DMA((4 * n,))],
        input_output_aliases={a: a for a in range(n)},
    )(*stks)
    return list(outs)


def reduce_ici(arrs):
    n = len(arrs)

    def copies(cins, couts, sems):
        send_sems, recv_sems, local_sems = sems
        x, y, c = _place()
        my_chip = 2 * x + y
        local, sends, recvs = [], [], []
        for a in range(n):
            local.append(pltpu.make_async_copy(cins[a].at[my_chip], couts[a].at[my_chip], local_sems.at[a]))
            for j, (px, py) in enumerate([(1 - x, y), (x, 1 - y), (1 - x, 1 - y)]):
                k = 3 * a + j
                sends.append(_remote(cins[a].at[2 * px + py], couts[a].at[my_chip], send_sems.at[k], recv_sems.at[k], (px, py, c)))
                recvs.append(_remote(cins[a].at[my_chip], couts[a].at[2 * px + py], send_sems.at[k], recv_sems.at[k], (px, py, c)))
        return local, sends, recvs

    def start(cins, couts, sems):
        local, sends, _ = copies(cins, couts, sems)
        for cp in local + sends:
            cp.start()

    def finish(cins, couts, sems):
        local, sends, recvs = copies(cins, couts, sems)
        for cp in sends:
            cp.wait_send()
        for cp in recvs:
            cp.wait_recv()
        for cp in local:
            cp.wait()

    return Hosted(list(arrs), [jax.ShapeDtypeStruct(a.shape, a.dtype) for a in arrs],
                  [pltpu.SemaphoreType.DMA((3 * n,)), pltpu.SemaphoreType.DMA((3 * n,)), pltpu.SemaphoreType.DMA((n,))],
                  start, finish)


COL_SHARDED = ("w_in", "w_uq", "w_ukv", "w_branch_a", "w_branch_c", "w_up")
ROW_SHARDED = ("w_branch_b", "w_o", "w_down")
BIG = COL_SHARDED + ROW_SHARDED


def _full_from_stacked(name, st):
    if name in COL_SHARDED:
        _, L, K, n = st.shape
        return st.transpose(1, 2, 0, 3).reshape(L, K, 8 * n)
    if name in ROW_SHARDED:
        _, L, k, N = st.shape
        return st.transpose(1, 0, 2, 3).reshape(L, 8 * k, N)
    if name == "pool_w":
        _, L, G, pk, PG = st.shape
        return st.transpose(1, 2, 0, 3, 4).reshape(L, G, 8 * pk, PG)
    if name == "meta_tokens":
        _, M, n = st.shape
        return st.transpose(1, 0, 2).reshape(M, 8 * n)
    if name == "conv_w":
        _, L, W, n = st.shape
        return st.transpose(1, 2, 0, 3).reshape(L, W, 8 * n)
    raise ValueError(name)


def _shards_from_full(name, g):
    if name in COL_SHARDED:
        L, K, N = g.shape
        s = g.reshape(L, K, 8, N // 8).transpose(2, 0, 1, 3)
    else:
        L, K, N = g.shape
        s = g.reshape(L, 8, K // 8, N).transpose(1, 0, 2, 3)
    return s.reshape((4, 2) + s.shape[1:])


def _w_in_to_padded(w, dm):
    o3 = 3 * dm.DC + dm.QL + dm.KL
    o4 = o3 + dm.ROPE
    o5 = o4 + dm.DP
    pad = jnp.zeros(w.shape[:-1] + (256 - dm.ROPE,), w.dtype)
    return jnp.concatenate([w[..., o5:], w[..., :o3], w[..., o4:o5], w[..., o3:o4], pad], axis=-1)


def _w_in_from_padded(g, dm):
    o3 = 3 * dm.DC + dm.QL + dm.KL
    a = 3 * dm.D
    return jnp.concatenate([g[..., a:a + o3], g[..., dm.o_rope:dm.o_rope + dm.ROPE], g[..., dm.o_pool:dm.o_pool + dm.DP],
                            g[..., :a]], axis=-1)


def _pad_heads(w, dm):
    w = w.reshape(w.shape[:-1] + (dm.H, dm.QKH))
    w = jnp.pad(w, [(0, 0)] * (w.ndim - 1) + [(0, dm.HP - dm.QKH)])
    return w.reshape(w.shape[:-2] + (dm.H * dm.HP,))


def _unpad_heads(g, dm):
    g = g.reshape(g.shape[:-1] + (dm.H, dm.HP))[..., :dm.QKH]
    return g.reshape(g.shape[:-2] + (dm.H * dm.QKH,))


class _Weights:
    def __init__(self, w, dm, plan):
        self.w, self.dm, self.plan, self.full = w, dm, plan, {}

    def blocks(self, items):
        return [self.w[n][l:l + 1].astype(BF16) for n, l in items]

    def put(self, items, stacked):
        for (n, l), st in zip(items, stacked):
            f = _full_from_stacked(n, st)[0]
            if n == "w_in":
                f = _w_in_to_padded(f, self.dm)
            if n == "w_uq":
                f = _pad_heads(f, self.dm)
            self.full[(n, l)] = f

    def comm(self, tag):
        items = self.plan.get(tag)
        return gather_ici(self.blocks(items)) if items else None

    def arrived(self, tag, couts):
        self.put(self.plan[tag], fill_sibling(couts, f"fill_{tag}"))

    def __call__(self, n, l):
        return self.full[(n, l)]


class _Reducer:
    def __init__(self, dm, core):
        self.dm, self.core, self.q = dm, core, {}

    def prepare(self, tag, items, g):
        parts = []
        for n, l in items:
            f = g[n]
            if n == "w_in":
                f = _w_in_from_padded(f, self.dm)
            if n == "w_uq":
                f = _unpad_heads(f, self.dm)
            parts.append(_shards_from_full(n, f[None]))
        got = sibling_exchange(parts, f"reduce_sibling_{tag}")
        out = []
        for (n, l), a, b in zip(items, parts, got):
            C = a.shape[-1]
            out.append(add_sibling(a.reshape(4, 2, -1, C), b.reshape(4, -1, C), self.core,
                                   f"reduce_add_{n}_{l}").reshape(b.shape))
        return out

    def put(self, items, summed):
        for key, q in zip(items, summed):
            self.q[key] = q


def _layer_fwd(xin, l, ws, G, tabs, dm):
    nm = lambda s: f"l{l}_{s}"
    D = dm.D

    def mm(tag, *args, **kw):
        comm = ws.comm(nm(tag))
        if comm is None:
            return matmul(*args, nm(tag), **kw)
        res, couts = matmul(*args, nm(tag), comm=comm, **kw)
        ws.arrived(nm(tag), couts)
        return res

    h = rms_fwd(xin, 0, D, G["attn_norm"], dm, nm("rms1"))
    proj = mm("proj", h, ws("w_in", l), "nn", (F32,))
    ya = mixer_a_fwd(proj, G["conv_w"], dm, nm("mixa"))
    ql = rms_fwd(proj, dm.o_ql // dm.QL, dm.QL, G["q_lat_norm"], dm, nm("rms_q"))
    kl = rms_fwd(proj, dm.o_kl // dm.KL, dm.KL, G["kv_lat_norm"], dm, nm("rms_kv"))
    q0 = mm("uq", ql, ws("w_uq", l), "nn", (F32,))
    kv0 = mm("ukv", kl, ws("w_ukv", l), "nn", (F32,))
    comm = ws.comm(nm("qkprep"))
    if comm is None:
        q_s, k_s, v_s = qk_prep_fwd(q0, kv0, proj, G["q_norm"], G["k_norm"], tabs[0], tabs[1], dm, nm("qkprep"))
    else:
        (q_s, k_s, v_s), couts = qk_prep_fwd(q0, kv0, proj, G["q_norm"], G["k_norm"], tabs[0], tabs[1], dm, nm("qkprep"),
                                            comm=comm)
        ws.arrived(nm("qkprep"), couts)
    comm = ws.comm(nm("attn"))
    if comm is None:
        o, ob, lse = attn_fwd(q_s, k_s, v_s, dm, nm("attn"))
    else:
        (o, ob, lse), couts = attn_fwd(q_s, k_s, v_s, dm, nm("attn"), comm=comm)
        ws.arrived(nm("attn"), couts)
    pooled, mixed, yc = pool_fwd(proj, ws("pool_w", l), G["pool_scale"], dm, nm("pool"))
    A = mm("br_a", ya, ws("w_branch_a", l), "nn", (F32,))
    B = mm("br_b", ob, ws("w_branch_b", l), "nn", (F32,))
    C = mm("br_c", yc, ws("w_branch_c", l), "nn", (F32,))
    merged = merge_fwd(proj, A, B, C, dm, nm("merge"))
    x1 = mm("wo", merged, ws("w_o", l), "nn", (F32,), extras=(xin,), epi=lambda acc, r: (acc + r,))
    h2 = rms_fwd(x1, 0, D, G["mlp_norm"], dm, nm("rms2"))
    up, act = mm("up", h2, ws("w_up", l), "nn", (F32, BF16), epi=lambda acc: (acc, jnp.square(jnp.maximum(acc, 0.0))))
    x2 = mm("down", act, ws("w_down", l), "nn", (F32,), extras=(x1,), epi=lambda acc, r: (acc + r,))
    saved = dict(xin=xin, h=h, proj=proj, ya=ya, ql=ql, kl=kl, q0=q0, kv0=kv0, q_s=q_s, k_s=k_s, v_s=v_s, o=o, ob=ob,
                 lse=lse, pooled=pooled, mixed=mixed, yc=yc, A=A, B=B, C=C, merged=merged, x1=x1, h2=h2, up=up, act=act)
    return x2, saved


def _layer_bwd(dx2, dx2b, S, l, ws, G, tabs, dm, pre_attn, late=None):
    nm = lambda s: f"l{l}_b_{s}"
    D, T = dm.D, dm.T
    g = {}
    d_up = matmul(dx2b, ws("w_down", l), "nt", (BF16,), nm("d_act"), extras=(S["up"],),
                  epi=lambda acc, up: (acc * (2.0 * jnp.maximum(up, 0.0)),))
    g["w_down"] = matmul(S["act"], dx2b, "tn", (BF16,), nm("g_down"))
    g["w_up"] = matmul(S["h2"], d_up, "tn", (BF16,), nm("g_up"))
    dh2 = matmul(d_up, ws("w_up", l), "nt", (F32,), nm("d_h2"))
    dx1, dx1b, g["mlp_norm"] = rms_bwd(S["x1"], 0, D, G["mlp_norm"], dh2, dx2, dm, nm("rms2"))
    dmerged = matmul(dx1b, ws("w_o", l), "nt", (F32,), nm("d_merged"))
    g["w_o"] = matmul(S["merged"], dx1b, "tn", (BF16,), nm("g_o"))
    dA, dB, dC, dl0, dl1, dl2 = merge_bwd(S["proj"], S["A"], S["B"], S["C"], dmerged, dm, nm("merge"))
    dya = matmul(dA, ws("w_branch_a", l), "nt", (F32,), nm("d_ya"))
    g["w_branch_a"] = matmul(S["ya"], dA, "tn", (BF16,), nm("g_a"))
    dyb = matmul(dB, ws("w_branch_b", l), "nt", (F32,), nm("d_yb"))
    g["w_branch_b"] = matmul(S["ob"], dB, "tn", (BF16,), nm("g_b"))
    dyc = matmul(dC, ws("w_branch_c", l), "nt", (F32,), nm("d_yc"))
    g["w_branch_c"] = matmul(S["yc"], dC, "tn", (BF16,), nm("g_c"))
    du, db, dc, g["conv_w"] = mixer_a_bwd(S["proj"], G["conv_w"], dya, dm, nm("mixa"))
    dpool, g["pool_w"], g["pool_scale"] = pool_bwd(dyc, S["mixed"], S["pooled"], ws("pool_w", l), G["pool_scale"], dm, nm("pool"))
    delta, dob = attn_delta(dyb, S["o"], dm, nm("delta"))
    nq = T // dm.tq
    comm_dq, done_dq, comm_dkv, done_dkv = pre_attn(g)
    dq = attn_bwd_dq(S["q_s"], S["k_s"], S["v_s"], dob, S["lse"], delta, dm, nm("attn_dq"), comm=comm_dq)
    if comm_dq is not None:
        dq, couts = dq
        done_dq(couts)
    dkv = attn_bwd_dkv(S["q_s"], S["k_s"], S["v_s"], dob, S["lse"].reshape(dm.H, nq, 1, dm.tq),
                       delta.reshape(dm.H, nq, 1, dm.tq), dm, nm("attn_dkv"), comm=comm_dkv)
    if comm_dkv is not None:
        dkv, couts = dkv
        done_dkv(couts)
    dk, dv = dkv
    dq0, dkv0, dkr, g["q_norm"], g["k_norm"] = qk_prep_bwd(S["q0"], S["kv0"], S["proj"], G["q_norm"], G["k_norm"],
                                                            tabs[0], tabs[1], dq, dk, dv, dm, nm("qkprep"))
    dql = matmul(dq0, ws("w_uq", l), "nt", (F32,), nm("d_ql"))
    g["w_uq"] = matmul(S["ql"], dq0, "tn", (BF16,), nm("g_uq"))
    dkl = matmul(dkv0, ws("w_ukv", l), "nt", (F32,), nm("d_kl"))
    g["w_ukv"] = matmul(S["kl"], dkv0, "tn", (BF16,), nm("g_ukv"))
    _, dqlat, g["q_lat_norm"] = rms_bwd(S["proj"], dm.o_ql // dm.QL, dm.QL, G["q_lat_norm"], dql, None, dm, nm("rms_q"))
    _, dkvlat, g["kv_lat_norm"] = rms_bwd(S["proj"], dm.o_kl // dm.KL, dm.KL, G["kv_lat_norm"], dkl, None, dm, nm("rms_kv"))
    dproj = jnp.concatenate([dl0, dl1, dl2, du, db, dc, dqlat, dkvlat, dpool, dkr.astype(BF16),
                             jnp.zeros((T, 128), BF16)], axis=1)
    g["w_in"] = matmul(S["h"], dproj, "tn", (BF16,), nm("g_in"))
    comm_dh, done_dh = late(g) if late is not None else (None, None)
    dh = matmul(dproj, ws("w_in", l), "nt", (F32,), nm("d_h"), comm=comm_dh)
    if comm_dh is not None:
        dh, couts = dh
        done_dh(couts)
    dx, dxb, g["attn_norm"] = rms_bwd(S["xin"], 0, D, G["attn_norm"], dh, dx1, dm, nm("rms1"))
    return dx, dxb, g


WEIGHTS = ("meta_tokens", "attn_norm", "w_in", "conv_w", "q_lat_norm", "kv_lat_norm", "w_uq", "w_ukv", "q_norm", "k_norm",
           "pool_w", "pool_scale", "w_branch_a", "w_branch_b", "w_branch_c", "w_o", "mlp_norm", "w_up", "w_down")
SMALL = tuple(n for n in WEIGHTS if n not in BIG)


def kernel(x, meta_tokens, attn_norm, w_in, conv_w, q_lat_norm, kv_lat_norm, w_uq, w_ukv, q_norm, k_norm, pool_w, pool_scale, w_branch_a, w_branch_b, w_branch_c, w_o, mlp_norm, w_up, w_down, loss_target, m_meta_tokens, m_attn_norm, m_w_in, m_conv_w, m_q_lat_norm, m_kv_lat_norm, m_w_uq, m_w_ukv, m_q_norm, m_k_norm, m_pool_w, m_pool_scale, m_w_branch_a, m_w_branch_b, m_w_branch_c, m_w_o, m_mlp_norm, m_w_up, m_w_down, v_meta_tokens, v_attn_norm, v_w_in, v_conv_w, v_q_lat_norm, v_kv_lat_norm, v_w_uq, v_w_ukv, v_q_norm, v_k_norm, v_pool_w, v_pool_scale, v_w_branch_a, v_w_branch_b, v_w_branch_c, v_w_o, v_mlp_norm, v_w_up, v_w_down):
    w = dict(meta_tokens=meta_tokens, attn_norm=attn_norm, w_in=w_in, conv_w=conv_w, q_lat_norm=q_lat_norm,
             kv_lat_norm=kv_lat_norm, w_uq=w_uq, w_ukv=w_ukv, q_norm=q_norm, k_norm=k_norm, pool_w=pool_w,
             pool_scale=pool_scale, w_branch_a=w_branch_a, w_branch_b=w_branch_b, w_branch_c=w_branch_c, w_o=w_o,
             mlp_norm=mlp_norm, w_up=w_up, w_down=w_down)
    m = dict(meta_tokens=m_meta_tokens, attn_norm=m_attn_norm, w_in=m_w_in, conv_w=m_conv_w, q_lat_norm=m_q_lat_norm,
             kv_lat_norm=m_kv_lat_norm, w_uq=m_w_uq, w_ukv=m_w_ukv, q_norm=m_q_norm, k_norm=m_k_norm, pool_w=m_pool_w,
             pool_scale=m_pool_scale, w_branch_a=m_w_branch_a, w_branch_b=m_w_branch_b, w_branch_c=m_w_branch_c, w_o=m_w_o,
             mlp_norm=m_mlp_norm, w_up=m_w_up, w_down=m_w_down)
    v = dict(meta_tokens=v_meta_tokens, attn_norm=v_attn_norm, w_in=v_w_in, conv_w=v_conv_w, q_lat_norm=v_q_lat_norm,
             kv_lat_norm=v_kv_lat_norm, w_uq=v_w_uq, w_ukv=v_w_ukv, q_norm=v_q_norm, k_norm=v_k_norm, pool_w=v_pool_w,
             pool_scale=v_pool_scale, w_branch_a=v_w_branch_a, w_branch_b=v_w_branch_b, w_branch_c=v_w_branch_c, w_o=v_w_o,
             mlp_norm=v_mlp_norm, w_up=v_w_up, w_down=v_w_down)
    L = attn_norm.shape[0]
    assert L == 2, "the gather / reduce schedule below is written for two layers"
    seq, D = x.shape[1], x.shape[2]
    n_meta = meta_tokens.shape[0]
    dm = Dims(D, seq, n_meta)
    T = dm.T
    me = 4 * lax.axis_index("x") + 2 * lax.axis_index("y") + lax.axis_index("c")
    core = lax.axis_index("c").astype(jnp.int32).reshape(1)

    plan = {
        "l0_proj": [("w_uq", 0), ("w_ukv", 0), ("pool_w", 0), ("pool_w", 1), ("w_up", 0)],
        "l0_qkprep": [("w_branch_a", 0), ("w_branch_b", 0), ("w_branch_c", 0), ("w_o", 0)],
        "l0_attn": [("w_down", 0), ("w_in", 1)],
        "l0_up": [("w_up", 1)],
        "l0_down": [("w_down", 1)],
        "l1_proj": [("w_uq", 1), ("w_ukv", 1), ("w_branch_a", 1), ("w_branch_b", 1), ("w_branch_c", 1), ("w_o", 1)],
    }
    ws = _Weights(w, dm, plan)
    first = [("w_in", 0)]
    ws.put(first, all_gather(ws.blocks(first), "gather_first"))
    st_small = all_gather([w["meta_tokens"], w["conv_w"]], "gather_small")
    meta_full = _full_from_stacked("meta_tokens", st_small[0])
    conv_full = _full_from_stacked("conv_w", st_small[1])
    pad_gain = lambda gn: jnp.pad(gn, (0, dm.HP - dm.QKH))

    def gains(l):
        G = {n: w[n][l][None, :] for n in ("attn_norm", "q_lat_norm", "kv_lat_norm", "pool_scale", "mlp_norm")}
        G["q_norm"], G["k_norm"] = pad_gain(w["q_norm"][l])[None, :], pad_gain(w["k_norm"][l])[None, :]
        G["conv_w"] = conv_full[l]
        return G

    Gs = [gains(l) for l in range(L)]

    pos = jnp.arange(dm.T_real, dtype=F32)
    inv = 10000.0 ** (-jnp.arange(0, dm.ROPE, 2, dtype=F32) / dm.ROPE)
    ang = pos[:, None] * inv[None, :]
    zpad = jnp.zeros((dm.T_real, LANES - dm.ROPE), F32)
    rows = ((0, T - dm.T_real), (0, 0))
    tabs = (jnp.pad(jnp.concatenate([jnp.cos(ang), jnp.cos(ang), zpad], 1), rows),
            jnp.pad(jnp.concatenate([jnp.sin(ang), jnp.sin(ang), zpad], 1), rows))

    xs = jnp.concatenate([meta_full, x[0], jnp.zeros((T - dm.T_real, D), F32)], axis=0)
    target = jnp.pad(loss_target[0], ((n_meta, T - dm.T_real), (0, 0)))
    saved = []
    for l in range(L):
        xs, S = _layer_fwd(xs, l, ws, Gs[l], tabs, dm)
        saved.append(S)
    dx, dxb, loss_acc = loss_head(xs, target, dm, "loss_head")

    red = _Reducer(dm, core)
    early = lambda l: [(n, l) for n in ("w_down", "w_up", "w_o", "w_branch_a", "w_branch_b", "w_branch_c")]
    late = lambda l: [(n, l) for n in ("w_uq", "w_ukv", "w_in")]
    grads = [None] * L

    def pre_attn_1(g):
        return (reduce_ici(red.prepare("e1", early(1), g)), lambda couts: red.put(early(1), couts), None, None)

    dx, dxb, grads[1] = _layer_bwd(dx, dxb, saved[1], 1, ws, Gs[1], tabs, dm, pre_attn_1)
    late1 = red.prepare("l1", late(1), grads[1])

    def pre_attn_0(g):
        return (reduce_ici(late1), lambda couts: red.put(late(1), couts),
                reduce_ici(red.prepare("e0", early(0), g)), lambda couts: red.put(early(0), couts))

    def late_0(g):
        return reduce_ici(red.prepare("l0", late(0), g)), lambda couts: red.put(late(0), couts)

    dx, dxb, grads[0] = _layer_bwd(dx, dxb, saved[0], 0, ws, Gs[0], tabs, dm, pre_attn_0, late_0)
    grad_x = dx[n_meta:dm.T_real][None]
    summed = [jnp.concatenate([red.q[(n, l)] for l in range(L)], axis=1) for n in BIG]

    out_g, out_d, out_m, out_v = {}, {}, {}, {}

    def update(n, parts3):
        shp = w[n].shape
        C = shp[-1]
        res = adamw(w[n].reshape(-1, C), m[n].reshape(-1, C), v[n].reshape(-1, C), parts3, f"adamw_{n}")
        out_g[n], out_d[n], out_m[n], out_v[n] = [r.reshape(shp) for r in res]

    for n, q in zip(BIG, summed):
        update(n, q.reshape(4, -1, q.shape[-1]))

    small_full = {
        "meta_tokens": dx[:n_meta],
        "conv_w": jnp.stack([grads[l]["conv_w"] for l in range(L)]),
        "pool_w": jnp.stack([grads[l]["pool_w"] for l in range(L)]),
        "q_norm": jnp.stack([grads[l]["q_norm"][0, :dm.QKH] for l in range(L)]),
        "k_norm": jnp.stack([grads[l]["k_norm"][0, :dm.QKH] for l in range(L)]),
    }
    for n in ("attn_norm", "q_lat_norm", "kv_lat_norm", "pool_scale", "mlp_norm"):
        small_full[n] = jnp.stack([grads[l][n][0] for l in range(L)])
    flat = jnp.concatenate([small_full[n].reshape(-1) for n in SMALL] + [loss_acc[0, :1]])
    n_flat = flat.shape[0]
    rows_small = -(-n_flat // (8 * LANES)) * 8
    flat = jnp.pad(flat, (0, rows_small * LANES - n_flat)).reshape(rows_small, LANES)
    total = sum_parts(all_gather([flat], "gather_small_grads")[0], "sum_small").reshape(-1)
    off = 0
    for n in SMALL:
        size = math.prod(small_full[n].shape)
        gsum = total[off:off + size].reshape(small_full[n].shape)
        off += size
        if n in ("meta_tokens", "conv_w"):
            blk = w[n].shape[-1]
            gsum = lax.dynamic_slice_in_dim(gsum, me * blk, blk, axis=gsum.ndim - 1)
        elif n == "pool_w":
            blk = w[n].shape[2]
            gsum = lax.dynamic_slice_in_dim(gsum, me * blk, blk, axis=2)
        update(n, gsum.reshape(1, -1, gsum.shape[-1]))
    loss = total[off]

    return (loss, grad_x, *[out_g[n] for n in WEIGHTS], *[out_d[n] for n in WEIGHTS],
            *[out_m[n] for n in WEIGHTS], *[out_v[n] for n in WEIGHTS])
```

```python
import functools
import math

import jax
import jax.numpy as jnp
from jax import lax
from jax.experimental import pallas as pl
from jax.experimental.pallas import tpu as pltpu

F32 = jnp.float32
BF16 = jnp.bfloat16

VMEM_LIMIT_BYTES = 56 * 1024 * 1024
LANES = 128
EPS = 1e-6
HALO = 16

ADAM_LR = 0.001
ADAM_B1 = 0.9
ADAM_B2 = 0.999
ADAM_EPS = 1e-08
ADAM_WD = 0.01
ADAM_STEP = 10


def _params(sem):
    return pltpu.CompilerParams(dimension_semantics=sem, vmem_limit_bytes=VMEM_LIMIT_BYTES)


def _pick(n, prefs):
    for p in prefs:
        if p <= n and n % p == 0:
            return p
    return n


_MESH = pl.DeviceIdType.MESH
_HBM = pl.BlockSpec(memory_space=pltpu.HBM)


def _place():
    return lax.axis_index("x"), lax.axis_index("y"), lax.axis_index("c")


class Hosted:
    def __init__(self, ins, out_shapes, sems, start, finish):
        self.ins, self.out_shapes, self.sems, self.start, self.finish = ins, out_shapes, sems, start, finish


def _hosted_call(body, *, name, grid, in_specs, out_specs, out_shape, scratch_shapes, semantics, args, comm):
    n_in, n_out, n_scr = len(in_specs), len(out_specs), len(scratch_shapes)
    if comm is None:
        outs = pl.pallas_call(body, name=name, grid=grid, in_specs=in_specs, out_specs=out_specs, out_shape=out_shape,
                              scratch_shapes=scratch_shapes, compiler_params=_params(semantics))(*args)
        return list(outs), []
    ci, co = len(comm.ins), len(comm.out_shapes)

    def hosting(*refs):
        ins, cins = refs[:n_in], refs[n_in:n_in + ci]
        outs = refs[n_in + ci:n_in + ci + n_out]
        couts = refs[n_in + ci + n_out:n_in + ci + n_out + co]
        scr = refs[n_in + ci + n_out + co:n_in + ci + n_out + co + n_scr]
        csems = refs[n_in + ci + n_out + co + n_scr:]
        ids = [pl.program_id(d) for d in range(len(grid))]
        first = functools.reduce(jnp.logical_and, [i == 0 for i in ids])
        last = functools.reduce(jnp.logical_and, [i == g - 1 for i, g in zip(ids, grid)])

        @pl.when(first)
        def _():
            comm.start(cins, couts, csems)

        body(*ins, *outs, *scr)

        @pl.when(last)
        def _():
            comm.finish(cins, couts, csems)

    outs = pl.pallas_call(
        hosting, name=name, grid=grid,
        in_specs=list(in_specs) + [_HBM] * ci, out_specs=list(out_specs) + [_HBM] * co,
        out_shape=list(out_shape) + list(comm.out_shapes),
        scratch_shapes=list(scratch_shapes) + list(comm.sems),
        compiler_params=_params(("arbitrary",) * len(grid)),
    )(*args, *comm.ins)
    return list(outs[:n_out]), list(outs[n_out:])


MXU_FLOPS = 750e12
HBM_BYTES_PER_S = 3.0e12
ACC_RMW_BYTES_PER_S = 8e12
GRID_STEP_S = 0.4e-6
VMEM_COMPILER_RESERVE_BYTES = 8 * 1024 * 1024
MAX_TILE_ROWS, MAX_TILE_COLS = 2112, 2304


def _divisors(n, step):
    return [d for d in range(step, n + 1, step) if n % d == 0]


def _matmul_tiles(M, N, K, mode, out_sizes, n_extra):
    budget = VMEM_LIMIT_BYTES - VMEM_COMPILER_RESERVE_BYTES
    best = None
    for tk in _divisors(K, 16 if mode == "tn" else LANES):
        for tm in _divisors(M, LANES if mode == "tn" else 16):
            if tm > MAX_TILE_ROWS:
                continue
            for tn in _divisors(N, LANES):
                if tn > MAX_TILE_COLS:
                    continue
                need = 4 * tm * tk + 4 * tk * tn + 4 * tm * tn + 2 * tm * tn * sum(out_sizes) + 8 * n_extra * tm * tn
                if need > budget:
                    continue
                nk = K // tk
                steps = (M // tm) * (N // tn) * nk
                t_mxu = 2 * M * N * K / MXU_FLOPS + (nk > 1) * (8 * M * N * nk) / ACC_RMW_BYTES_PER_S
                t_hbm = (2 * M * K * (N // tn) + 2 * K * N * (M // tm) + M * N * (sum(out_sizes) + 4 * n_extra)) / HBM_BYTES_PER_S
                t = max(t_mxu, t_hbm) + steps * GRID_STEP_S
                if best is None or t < best[0]:
                    best = (t, tm, tn, tk)
    assert best is not None, (M, N, K, mode)
    return best[1:]


def matmul(a, b, mode, out_dtypes, name, extras=(), epi=None, tm=None, tn=None, tk=None, comm=None):
    if mode == "nn":
        (M, K), (K2, N) = a.shape, b.shape
    elif mode == "nt":
        (M, K), (N, K2) = a.shape, b.shape
    else:
        (K, M), (K2, N) = a.shape, b.shape
    assert K == K2, (a.shape, b.shape, mode)
    if not (tm and tn and tk):
        tm, tn, tk = _matmul_tiles(M, N, K, mode, [jnp.dtype(d).itemsize for d in out_dtypes], len(extras))
    nk = K // tk
    dims = {"nn": (((1,), (0,)), ((), ())), "nt": (((1,), (1,)), ((), ())), "tn": (((0,), (0,)), ((), ()))}[mode]
    n_extra, n_out = len(extras), len(out_dtypes)

    def body(*refs):
        a_ref, b_ref = refs[0], refs[1]
        extra_refs = refs[2:2 + n_extra]
        out_refs = refs[2 + n_extra:2 + n_extra + n_out]

        def finish(acc):
            outs = (acc,) if epi is None else epi(acc, *[r[...] for r in extra_refs])
            for o_ref, o in zip(out_refs, outs):
                o_ref[...] = o.astype(o_ref.dtype)

        part = lax.dot_general(a_ref[...], b_ref[...], dims, preferred_element_type=F32)
        if nk == 1:
            finish(part)
            return
        acc_ref = refs[-1]
        k = pl.program_id(2)

        @pl.when(k == 0)
        def _():
            acc_ref[...] = part

        @pl.when(k > 0)
        def _():
            acc_ref[...] += part

        @pl.when(k == nk - 1)
        def _():
            finish(acc_ref[...])

    a_spec = {"nn": pl.BlockSpec((tm, tk), lambda i, j, k: (i, k)),
              "nt": pl.BlockSpec((tm, tk), lambda i, j, k: (i, k)),
              "tn": pl.BlockSpec((tk, tm), lambda i, j, k: (k, i))}[mode]
    b_spec = {"nn": pl.BlockSpec((tk, tn), lambda i, j, k: (k, j)),
              "nt": pl.BlockSpec((tn, tk), lambda i, j, k: (j, k)),
              "tn": pl.BlockSpec((tk, tn), lambda i, j, k: (k, j))}[mode]
    o_spec = pl.BlockSpec((tm, tn), lambda i, j, k: (i, j))
    outs, couts = _hosted_call(
        body, name=name, grid=(M // tm, N // tn, nk),
        in_specs=[a_spec, b_spec] + [o_spec] * n_extra,
        out_specs=[o_spec] * n_out,
        out_shape=[jax.ShapeDtypeStruct((M, N), d) for d in out_dtypes],
        scratch_shapes=[pltpu.VMEM((tm, tn), F32)] if nk > 1 else [],
        semantics=("parallel", "parallel", "arbitrary"), args=(a, b, *extras), comm=comm)
    res = outs[0] if n_out == 1 else outs
    return res if comm is None else (res, couts)


class Dims:
    def __init__(self, d_model, seq, n_meta):
        self.D = d_model
        self.n_meta = n_meta
        self.T_real = seq + n_meta
        self.T = -(-self.T_real // LANES) * LANES
        self.H = d_model // 128
        self.DC = d_model // 2
        self.DP = d_model // 2
        self.PG = self.DP // 4
        self.QL = 512
        self.KL = 512
        self.ROPE = 64
        self.NOPE = 128
        self.QKH = 192
        self.HP = 256
        self.DFF = 4 * d_model
        self.o_gate = 0
        self.o_u = 3 * d_model
        self.o_b = self.o_u + self.DC
        self.o_c = self.o_b + self.DC
        self.o_ql = self.o_c + self.DC
        self.o_kl = self.o_ql + self.QL
        self.o_pool = self.o_kl + self.KL
        self.o_rope = self.o_pool + self.DP
        self.NIN = self.o_rope + 256
        self.tr = _pick(self.T, (384, 256, 128))
        self.tq = _pick(self.T, (384, 256, 128))


def _row_ids(i, tr):
    return i * tr + lax.broadcasted_iota(jnp.int32, (tr, 1), 0)


def rms_fwd(x, col_block, width, g, dm, name):
    tr = dm.tr

    def body(x_ref, g_ref, y_ref):
        xv = x_ref[...]
        r = lax.rsqrt(jnp.mean(xv * xv, axis=-1, keepdims=True) + EPS)
        y_ref[...] = (xv * r * g_ref[...]).astype(y_ref.dtype)

    return pl.pallas_call(
        body, name=name, grid=(dm.T // tr,),
        in_specs=[pl.BlockSpec((tr, width), lambda i: (i, col_block)), pl.BlockSpec((1, width), lambda i: (0, 0))],
        out_specs=pl.BlockSpec((tr, width), lambda i: (i, 0)),
        out_shape=jax.ShapeDtypeStruct((dm.T, width), BF16),
        compiler_params=_params(("parallel",)),
    )(x, g.reshape(1, width))


def rms_bwd(x, col_block, width, g, dy, dres, dm, name):
    tr = dm.tr
    has_res = dres is not None

    def body(*refs):
        if has_res:
            x_ref, g_ref, dy_ref, dres_ref, dx_ref, dxb_ref, dg_ref = refs
        else:
            x_ref, g_ref, dy_ref, dx_ref, dxb_ref, dg_ref = refs
        xv, dyv = x_ref[...], dy_ref[...]
        r = lax.rsqrt(jnp.mean(xv * xv, axis=-1, keepdims=True) + EPS)
        gdy = dyv * g_ref[...]
        dx = r * gdy - xv * (r * r * r) * jnp.mean(xv * gdy, axis=-1, keepdims=True)
        if has_res:
            dx = dx + dres_ref[...]
        dx_ref[...] = dx
        dxb_ref[...] = dx.astype(BF16)

        @pl.when(pl.program_id(0) == 0)
        def _():
            dg_ref[...] = jnp.zeros_like(dg_ref)

        dg_ref[...] += jnp.sum(dyv * xv * r, axis=0, keepdims=True)

    row = pl.BlockSpec((tr, width), lambda i: (i, 0))
    in_specs = [pl.BlockSpec((tr, width), lambda i: (i, col_block)), pl.BlockSpec((1, width), lambda i: (0, 0)), row]
    args = [x, g.reshape(1, width), dy]
    if has_res:
        in_specs.append(row)
        args.append(dres)
    return pl.pallas_call(
        body, name=name, grid=(dm.T // tr,),
        in_specs=in_specs,
        out_specs=[row, row, pl.BlockSpec((1, width), lambda i: (0, 0))],
        out_shape=[jax.ShapeDtypeStruct((dm.T, width), F32), jax.ShapeDtypeStruct((dm.T, width), BF16),
                   jax.ShapeDtypeStruct((1, width), F32)],
        compiler_params=_params(("arbitrary",)),
    )(*args)


def _fill_halo_buf(buf, src_fn, T, R, width):
    zeros = jnp.zeros((HALO, width), F32)
    buf[pl.ds(0, HALO), :] = zeros
    buf[pl.ds(HALO + T, HALO), :] = zeros

    def fill(r, c):
        r0 = pl.multiple_of(r * R, 8)
        buf[pl.ds(r0 + HALO, R), :] = src_fn(r0)
        return c

    lax.fori_loop(0, T // R, fill, 0)


def _back(win, sh):
    return pltpu.roll(win, sh, 0)


def _fwd(win, sh):
    return pltpu.roll(win, win.shape[0] - sh, 0)


def mixer_a_fwd(proj, conv_w, dm, name):
    T, cw = dm.T, 128
    R = dm.tr
    nb = dm.DC // cw

    def body(u_ref, b_ref, c_ref, w_ref, ya_ref, buf):
        _fill_halo_buf(buf, lambda r0: c_ref[pl.ds(r0, R), :] * u_ref[pl.ds(r0, R), :], T, R, cw)
        w0, w1, w2 = w_ref[0:1, :], w_ref[1:2, :], w_ref[2:3, :]

        def chunk(r, c):
            r0 = pl.multiple_of(r * R, 8)
            win = buf[pl.ds(r0, R + HALO), :]
            cv = w2 * win + w1 * _back(win, 1) + w0 * _back(win, 2)
            ya_ref[pl.ds(r0, R), :] = (b_ref[pl.ds(r0, R), :] * cv[HALO:, :]).astype(BF16)
            return c

        lax.fori_loop(0, T // R, chunk, 0)

    col = lambda off: pl.BlockSpec((T, cw), lambda j: (0, off // cw + j))
    return pl.pallas_call(
        body, name=name, grid=(nb,),
        in_specs=[col(dm.o_u), col(dm.o_b), col(dm.o_c), pl.BlockSpec((3, cw), lambda j: (0, j))],
        out_specs=pl.BlockSpec((T, cw), lambda j: (0, j)),
        out_shape=jax.ShapeDtypeStruct((T, dm.DC), BF16),
        scratch_shapes=[pltpu.VMEM((T + 2 * HALO, cw), F32)],
        compiler_params=_params(("parallel",)),
    )(proj, proj, proj, conv_w)


def mixer_a_bwd(proj, conv_w, dya, dm, name):
    T, cw = dm.T, 128
    R = dm.tr
    nb = dm.DC // cw

    def body(u_ref, b_ref, c_ref, w_ref, dya_ref, du_ref, db_ref, dc_ref, dw_ref, sbuf, gbuf):
        _fill_halo_buf(sbuf, lambda r0: c_ref[pl.ds(r0, R), :] * u_ref[pl.ds(r0, R), :], T, R, cw)
        _fill_halo_buf(gbuf, lambda r0: dya_ref[pl.ds(r0, R), :] * b_ref[pl.ds(r0, R), :], T, R, cw)
        w0, w1, w2 = w_ref[0:1, :], w_ref[1:2, :], w_ref[2:3, :]

        def chunk(r, acc):
            a0, a1, a2 = acc
            r0 = pl.multiple_of(r * R, 8)
            swin = sbuf[pl.ds(r0, R + HALO), :]
            s0, s1, s2 = swin[HALO:, :], _back(swin, 1)[HALO:, :], _back(swin, 2)[HALO:, :]
            gwin = gbuf[pl.ds(r0 + HALO, R + HALO), :]
            g0, g1, g2 = gwin[:R, :], _fwd(gwin, 1)[:R, :], _fwd(gwin, 2)[:R, :]
            cv = w2 * s0 + w1 * s1 + w0 * s2
            ds = w2 * g0 + w1 * g1 + w0 * g2
            db_ref[pl.ds(r0, R), :] = (dya_ref[pl.ds(r0, R), :] * cv).astype(BF16)
            du_ref[pl.ds(r0, R), :] = (ds * c_ref[pl.ds(r0, R), :]).astype(BF16)
            dc_ref[pl.ds(r0, R), :] = (ds * u_ref[pl.ds(r0, R), :]).astype(BF16)
            a2 = a2 + jnp.sum(g0 * s0, axis=0, keepdims=True)
            a1 = a1 + jnp.sum(g0 * s1, axis=0, keepdims=True)
            a0 = a0 + jnp.sum(g0 * s2, axis=0, keepdims=True)
            return a0, a1, a2

        z = jnp.zeros((1, cw), F32)
        a0, a1, a2 = lax.fori_loop(0, T // R, chunk, (z, z, z))
        dw_ref[0:1, :] = a0
        dw_ref[1:2, :] = a1
        dw_ref[2:3, :] = a2

    col = lambda off: pl.BlockSpec((T, cw), lambda j: (0, off // cw + j))
    own = pl.BlockSpec((T, cw), lambda j: (0, j))
    o = jax.ShapeDtypeStruct((T, dm.DC), BF16)
    return pl.pallas_call(
        body, name=name, grid=(nb,),
        in_specs=[col(dm.o_u), col(dm.o_b), col(dm.o_c), pl.BlockSpec((3, cw), lambda j: (0, j)), own],
        out_specs=[own, own, own, pl.BlockSpec((3, cw), lambda j: (0, j))],
        out_shape=[o, o, o, jax.ShapeDtypeStruct((3, dm.DC), F32)],
        scratch_shapes=[pltpu.VMEM((T + 2 * HALO, cw), F32), pltpu.VMEM((T + 2 * HALO, cw), F32)],
        compiler_params=_params(("parallel",)),
    )(proj, proj, proj, conv_w, dya)


def _rope(x, C, S):
    return x * C + (pltpu.roll(x, 32, 1) - pltpu.roll(x, 96, 1)) * S


def _rope_t(dy, C, S):
    return dy * C + (pltpu.roll(dy, 96, 1) - pltpu.roll(dy, 32, 1)) * S


HP = 4


def qk_prep_fwd(q0, kv0, proj, qn, kn, C, S, dm, name, comm=None):
    T, H, tr = dm.T, dm.H, dm.tr
    inv = 1.0 / dm.QKH

    def body(q0_ref, kv_ref, kr_ref, qn_ref, kn_ref, c_ref, s_ref, q_ref, k_ref, v_ref):
        Cv, Sv = c_ref[...], s_ref[...]
        kb = kr_ref[...]
        kb2 = jnp.sum(kb * kb, -1, keepdims=True)
        for a in range(HP):
            lo, mid, hi = 256 * a, 256 * a + 128, 256 * (a + 1)
            qa, qb = q0_ref[:, lo:mid], q0_ref[:, mid:hi]
            r = lax.rsqrt((jnp.sum(qa * qa, -1, keepdims=True) + jnp.sum(qb * qb, -1, keepdims=True)) * inv + EPS)
            q_ref[:, lo:mid] = (qa * r * qn_ref[:, :128]).astype(BF16)
            q_ref[:, mid:hi] = _rope(qb * r * qn_ref[:, 128:], Cv, Sv).astype(BF16)
            ka = kv_ref[:, lo:mid]
            r = lax.rsqrt((jnp.sum(ka * ka, -1, keepdims=True) + kb2) * inv + EPS)
            k_ref[:, lo:mid] = (ka * r * kn_ref[:, :128]).astype(BF16)
            k_ref[:, mid:hi] = _rope(kb * r * kn_ref[:, 128:], Cv, Sv).astype(BF16)
            v_ref[:, 128 * a:128 * (a + 1)] = kv_ref[:, mid:hi].astype(BF16)

    head = pl.BlockSpec((tr, 256 * HP), lambda i, h: (i, h))
    gain = pl.BlockSpec((1, 256), lambda i, h: (0, 0))
    tab = pl.BlockSpec((tr, 128), lambda i, h: (i, 0))
    outs, couts = _hosted_call(
        body, name=name, grid=(T // tr, H // HP),
        in_specs=[head, head, pl.BlockSpec((tr, 128), lambda i, h: (i, dm.o_rope // 128)), gain, gain, tab, tab],
        out_specs=[head, head, pl.BlockSpec((tr, 128 * HP), lambda i, h: (i, h))],
        out_shape=[jax.ShapeDtypeStruct((T, H * 256), BF16), jax.ShapeDtypeStruct((T, H * 256), BF16),
                   jax.ShapeDtypeStruct((T, H * 128), BF16)],
        scratch_shapes=[], semantics=("parallel", "parallel"), args=(q0, kv0, proj, qn, kn, C, S), comm=comm)
    return outs if comm is None else (outs, couts)


def qk_prep_bwd(q0, kv0, proj, qn, kn, C, S, dq, dk, dv, dm, name):
    T, H, tr = dm.T, dm.H, dm.tr
    inv = 1.0 / dm.QKH

    def body(q0_ref, kv_ref, kr_ref, qn_ref, kn_ref, c_ref, s_ref, dq_ref, dk_ref, dv_ref,
             dq0_ref, dkv_ref, dkr_ref, dqn_ref, dkn_ref):
        i, h = pl.program_id(0), pl.program_id(1)
        Cv, Sv = c_ref[...], s_ref[...]

        def norm_bwd(xa, xb, ga, gb, dya, dyb):
            r = lax.rsqrt((jnp.sum(xa * xa, -1, keepdims=True) + jnp.sum(xb * xb, -1, keepdims=True)) * inv + EPS)
            dzb = _rope_t(dyb, Cv, Sv)
            gda, gdb = ga * dya, gb * dzb
            dot = (jnp.sum(xa * gda, -1, keepdims=True) + jnp.sum(xb * gdb, -1, keepdims=True)) * inv
            r3 = r * r * r
            dxa = r * gda - xa * r3 * dot
            dxb = r * gdb - xb * r3 * dot
            dga = jnp.sum(dya * xa * r, axis=0, keepdims=True)
            dgb = jnp.sum(dzb * xb * r, axis=0, keepdims=True)
            return dxa, dxb, dga, dgb

        @pl.when((i == 0) & (h == 0))
        def _():
            dqn_ref[...] = jnp.zeros_like(dqn_ref)
            dkn_ref[...] = jnp.zeros_like(dkn_ref)

        @pl.when(h == 0)
        def _():
            dkr_ref[...] = jnp.zeros_like(dkr_ref)

        for a in range(HP):
            lo, mid, hi = 256 * a, 256 * a + 128, 256 * (a + 1)
            dxa, dxb, dga, dgb = norm_bwd(q0_ref[:, lo:mid], q0_ref[:, mid:hi], qn_ref[:, :128], qn_ref[:, 128:],
                                          dq_ref[:, lo:mid], dq_ref[:, mid:hi])
            dq0_ref[:, lo:mid] = dxa.astype(BF16)
            dq0_ref[:, mid:hi] = dxb.astype(BF16)
            dqn_ref[:, :128] += dga
            dqn_ref[:, 128:] += dgb
            dxa, dxb, dga, dgb = norm_bwd(kv_ref[:, lo:mid], kr_ref[...], kn_ref[:, :128], kn_ref[:, 128:],
                                          dk_ref[:, lo:mid], dk_ref[:, mid:hi])
            dkv_ref[:, lo:mid] = dxa.astype(BF16)
            dkv_ref[:, mid:hi] = dv_ref[:, 128 * a:128 * (a + 1)].astype(BF16)
            dkn_ref[:, :128] += dga
            dkn_ref[:, 128:] += dgb
            dkr_ref[...] += dxb

    head = pl.BlockSpec((tr, 256 * HP), lambda i, h: (i, h))
    gain = pl.BlockSpec((1, 256), lambda i, h: (0, 0))
    tab = pl.BlockSpec((tr, 128), lambda i, h: (i, 0))
    return pl.pallas_call(
        body, name=name, grid=(T // tr, H // HP),
        in_specs=[head, head, pl.BlockSpec((tr, 128), lambda i, h: (i, dm.o_rope // 128)), gain, gain, tab, tab,
                  head, head, pl.BlockSpec((tr, 128 * HP), lambda i, h: (i, h))],
        out_specs=[head, head, tab, gain, gain],
        out_shape=[jax.ShapeDtypeStruct((T, H * 256), BF16), jax.ShapeDtypeStruct((T, H * 256), BF16),
                   jax.ShapeDtypeStruct((T, 128), F32), jax.ShapeDtypeStruct((1, 256), F32),
                   jax.ShapeDtypeStruct((1, 256), F32)],
        compiler_params=_params(("arbitrary", "arbitrary")),
    )(q0, kv0, proj, qn, kn, C, S, dq, dk, dv)


_NT = (((1,), (1,)), ((), ()))


def _causal_mask(t):
    return lax.broadcasted_iota(jnp.int32, (t, t), 0) >= lax.broadcasted_iota(jnp.int32, (t, t), 1)


def _causal_mask_t(t):
    return lax.broadcasted_iota(jnp.int32, (t, t), 0) <= lax.broadcasted_iota(jnp.int32, (t, t), 1)


HB = 2


def attn_fwd(q, k, v, dm, name, comm=None):
    T, H, tq = dm.T, dm.H, dm.tq
    scale = dm.QKH ** -0.5

    def body(q_ref, k_ref, v_ref, o_ref, ob_ref, lse_ref):
        qi = pl.program_id(1)

        def step(j0, w, carry, masked):
            ss = [lax.dot_general(q_ref[:, 256 * a:256 * (a + 1)], k_ref[pl.ds(j0, w), 256 * a:256 * (a + 1)], _NT,
                                  preferred_element_type=F32) for a in range(HB)]
            out = []
            for a in range(HB):
                m, l, acc = carry[a]
                s = ss[a] * scale
                if masked:
                    s = jnp.where(_causal_mask(tq), s, -jnp.inf)
                m_new = jnp.maximum(m, jnp.max(s, -1, keepdims=True))
                alpha = jnp.exp(m - m_new)
                p = jnp.exp(s - m_new)
                l = alpha * l + jnp.sum(p, -1, keepdims=True)
                acc = alpha * acc + jnp.dot(p.astype(BF16), v_ref[pl.ds(j0, w), 128 * a:128 * (a + 1)],
                                            preferred_element_type=F32)
                out.append((m_new, l, acc))
            return tuple(out)

        one = (jnp.full((tq, 1), -jnp.inf, F32), jnp.zeros((tq, 1), F32), jnp.zeros((tq, 128), F32))
        carry = lax.fori_loop(0, jnp.right_shift(qi, 1), lambda t, c: step(pl.multiple_of(t * 2 * tq, tq), 2 * tq, c, False), (one,) * HB)
        carry = lax.cond(jnp.bitwise_and(qi, 1) == 1, lambda c: step(pl.multiple_of((qi - 1) * tq, tq), tq, c, False), lambda c: c, carry)
        carry = step(pl.multiple_of(qi * tq, tq), tq, carry, True)
        for a in range(HB):
            m, l, acc = carry[a]
            o = acc / l
            o_ref[:, 128 * a:128 * (a + 1)] = o
            ob_ref[:, 128 * a:128 * (a + 1)] = o.astype(BF16)
            lse_ref[a] = m + jnp.log(l)

    outs, couts = _hosted_call(
        body, name=name, grid=(H // HB, T // tq),
        in_specs=[pl.BlockSpec((tq, 256 * HB), lambda h, i: (i, h)), pl.BlockSpec((T, 256 * HB), lambda h, i: (0, h)),
                  pl.BlockSpec((T, 128 * HB), lambda h, i: (0, h))],
        out_specs=[pl.BlockSpec((tq, 128 * HB), lambda h, i: (i, h)), pl.BlockSpec((tq, 128 * HB), lambda h, i: (i, h)),
                   pl.BlockSpec((HB, tq, 1), lambda h, i: (h, i, 0))],
        out_shape=[jax.ShapeDtypeStruct((T, H * 128), F32), jax.ShapeDtypeStruct((T, H * 128), BF16),
                   jax.ShapeDtypeStruct((H, T, 1), F32)],
        scratch_shapes=[], semantics=("parallel", "parallel"), args=(q, k, v), comm=comm)
    return outs if comm is None else (outs, couts)


def attn_delta(do, o, dm, name):
    T, H, tr = dm.T, dm.H, dm.tr

    def body(do_ref, o_ref, delta_ref, dob_ref):
        for a in range(H):
            d = do_ref[:, 128 * a:128 * (a + 1)]
            delta_ref[a] = jnp.sum(d * o_ref[:, 128 * a:128 * (a + 1)], -1, keepdims=True)
            dob_ref[:, 128 * a:128 * (a + 1)] = d.astype(BF16)

    blk = pl.BlockSpec((tr, 128 * H), lambda i: (i, 0))
    return pl.pallas_call(
        body, name=name, grid=(T // tr,),
        in_specs=[blk, blk],
        out_specs=[pl.BlockSpec((H, tr, 1), lambda i: (0, i, 0)), blk],
        out_shape=[jax.ShapeDtypeStruct((H, T, 1), F32), jax.ShapeDtypeStruct((T, H * 128), BF16)],
        compiler_params=_params(("parallel",)),
    )(do, o)


def attn_bwd_dq(q, k, v, do, lse, delta, dm, name, comm=None):
    T, H, tq = dm.T, dm.H, dm.tq
    scale = dm.QKH ** -0.5

    def body(q_ref, k_ref, v_ref, do_ref, lse_ref, delta_ref, dq_ref):
        qi = pl.program_id(1)

        def step(j0, w, dqs, masked):
            hk = lambda a: slice(256 * a, 256 * (a + 1))
            hv = lambda a: slice(128 * a, 128 * (a + 1))
            ss = [lax.dot_general(q_ref[:, hk(a)], k_ref[pl.ds(j0, w), hk(a)], _NT, preferred_element_type=F32)
                  for a in range(HB)]
            dps = [lax.dot_general(do_ref[:, hv(a)], v_ref[pl.ds(j0, w), hv(a)], _NT, preferred_element_type=F32)
                   for a in range(HB)]
            out = []
            for a in range(HB):
                p = jnp.exp(ss[a] * scale - lse_ref[a])
                if masked:
                    p = jnp.where(_causal_mask(tq), p, 0.0)
                ds = p * (dps[a] - delta_ref[a]) * scale
                out.append(dqs[a] + jnp.dot(ds.astype(BF16), k_ref[pl.ds(j0, w), hk(a)], preferred_element_type=F32))
            return tuple(out)

        dqs = lax.fori_loop(0, jnp.right_shift(qi, 1), lambda t, c: step(pl.multiple_of(t * 2 * tq, tq), 2 * tq, c, False),
                            (jnp.zeros((tq, 256), F32),) * HB)
        dqs = lax.cond(jnp.bitwise_and(qi, 1) == 1, lambda c: step(pl.multiple_of((qi - 1) * tq, tq), tq, c, False),
                       lambda c: c, dqs)
        dqs = step(pl.multiple_of(qi * tq, tq), tq, dqs, True)
        for a in range(HB):
            dq_ref[:, 256 * a:256 * (a + 1)] = dqs[a]

    stat = pl.BlockSpec((HB, tq, 1), lambda h, i: (h, i, 0))
    outs, couts = _hosted_call(
        body, name=name, grid=(H // HB, T // tq),
        in_specs=[pl.BlockSpec((tq, 256 * HB), lambda h, i: (i, h)), pl.BlockSpec((T, 256 * HB), lambda h, i: (0, h)),
                  pl.BlockSpec((T, 128 * HB), lambda h, i: (0, h)), pl.BlockSpec((tq, 128 * HB), lambda h, i: (i, h)),
                  stat, stat],
        out_specs=[pl.BlockSpec((tq, 256 * HB), lambda h, i: (i, h))],
        out_shape=[jax.ShapeDtypeStruct((T, H * 256), F32)],
        scratch_shapes=[], semantics=("parallel", "parallel"), args=(q, k, v, do, lse, delta), comm=comm)
    return outs[0] if comm is None else (outs[0], couts)


def attn_bwd_dkv(q, k, v, do, lse_rows, delta_rows, dm, name, comm=None):
    T, H, tq = dm.T, dm.H, dm.tq
    nq = T // tq
    scale = dm.QKH ** -0.5

    def body(q_ref, k_ref, v_ref, do_ref, lse_ref, delta_ref, dk_ref, dv_ref):
        kj = pl.program_id(1)

        def step(i, n, carry, masked):
            i0 = pl.multiple_of(i * tq, tq)
            w = n * tq
            hk = lambda a: slice(256 * a, 256 * (a + 1))
            hv = lambda a: slice(128 * a, 128 * (a + 1))
            row = lambda ref, a: ref[a, i] if n == 1 else jnp.concatenate([ref[a, i], ref[a, i + 1]], axis=1)
            sts = [lax.dot_general(k_ref[:, hk(a)], q_ref[pl.ds(i0, w), hk(a)], _NT, preferred_element_type=F32)
                   for a in range(HB)]
            dpts = [lax.dot_general(v_ref[:, hv(a)], do_ref[pl.ds(i0, w), hv(a)], _NT, preferred_element_type=F32)
                    for a in range(HB)]
            out = []
            for a in range(HB):
                dk, dv = carry[a]
                pt = jnp.exp(sts[a] * scale - row(lse_ref, a))
                if masked:
                    pt = jnp.where(_causal_mask_t(tq), pt, 0.0)
                dv = dv + jnp.dot(pt.astype(BF16), do_ref[pl.ds(i0, w), hv(a)], preferred_element_type=F32)
                dst = pt * (dpts[a] - row(delta_ref, a)) * scale
                dk = dk + jnp.dot(dst.astype(BF16), q_ref[pl.ds(i0, w), hk(a)], preferred_element_type=F32)
                out.append((dk, dv))
            return tuple(out)

        carry = step(kj, 1, ((jnp.zeros((tq, 256), F32), jnp.zeros((tq, 128), F32)),) * HB, True)
        rest = nq - 1 - kj
        carry = lax.fori_loop(0, jnp.right_shift(rest, 1), lambda t, c: step(kj + 1 + 2 * t, 2, c, False), carry)
        carry = lax.cond(jnp.bitwise_and(rest, 1) == 1, lambda c: step(nq - 1, 1, c, False), lambda c: c, carry)
        for a in range(HB):
            dk_ref[:, 256 * a:256 * (a + 1)] = carry[a][0]
            dv_ref[:, 128 * a:128 * (a + 1)] = carry[a][1]

    rows = pl.BlockSpec((HB, nq, 1, tq), lambda h, j: (h, 0, 0, 0))
    outs, couts = _hosted_call(
        body, name=name, grid=(H // HB, nq),
        in_specs=[pl.BlockSpec((T, 256 * HB), lambda h, j: (0, h)), pl.BlockSpec((tq, 256 * HB), lambda h, j: (j, h)),
                  pl.BlockSpec((tq, 128 * HB), lambda h, j: (j, h)), pl.BlockSpec((T, 128 * HB), lambda h, j: (0, h)),
                  rows, rows],
        out_specs=[pl.BlockSpec((tq, 256 * HB), lambda h, j: (j, h)), pl.BlockSpec((tq, 128 * HB), lambda h, j: (j, h))],
        out_shape=[jax.ShapeDtypeStruct((T, H * 256), F32), jax.ShapeDtypeStruct((T, H * 128), F32)],
        scratch_shapes=[], semantics=("parallel", "parallel"), args=(q, k, v, do, lse_rows, delta_rows), comm=comm)
    return outs if comm is None else (outs, couts)


def _window_sum(win, g, shift):
    s1 = win + shift(win, 1)
    s2 = s1 + shift(s1, 2)
    s3 = s2 + shift(s2, 4)
    s4 = s3 + shift(s3, 8)
    return jnp.where(g == 0, s1, jnp.where(g == 1, s2, jnp.where(g == 2, s3, s4)))


def _count(r0, R, g, T_unused=None):
    t = r0 + lax.broadcasted_iota(jnp.int32, (R, 1), 0)
    return jnp.minimum(t + 1, jnp.left_shift(2, g)).astype(F32)


def pool_fwd(proj, pw, ps, dm, name):
    T, PG, R = dm.T, dm.PG, dm.tr

    def body(x_ref, pw_ref, ps_ref, pooled_ref, mixed_ref, yc_ref, buf):
        g = pl.program_id(0)
        _fill_halo_buf(buf, lambda r0: x_ref[pl.ds(r0, R), :], T, R, PG)

        def chunk(r, c):
            r0 = pl.multiple_of(r * R, 8)
            win = buf[pl.ds(r0, R + HALO), :]
            ws = _window_sum(win, g, _back)[HALO:, :]
            pooled = (ws / _count(r0, R, g) - win[HALO:, :]).astype(BF16)
            pooled_ref[pl.ds(r0, R), :] = pooled
            mixed = jnp.dot(pooled, pw_ref[...], preferred_element_type=F32)
            mixed_ref[pl.ds(r0, R), :] = mixed
            yc_ref[pl.ds(r0, R), :] = (mixed * ps_ref[...]).astype(BF16)
            return c

        lax.fori_loop(0, T // R, chunk, 0)

    own = pl.BlockSpec((T, PG), lambda g: (0, g))
    return pl.pallas_call(
        body, name=name, grid=(4,),
        in_specs=[pl.BlockSpec((T, PG), lambda g: (0, dm.o_pool // PG + g)), pl.BlockSpec((None, PG, PG), lambda g: (g, 0, 0)),
                  pl.BlockSpec((1, PG), lambda g: (0, g))],
        out_specs=[own, own, own],
        out_shape=[jax.ShapeDtypeStruct((T, dm.DP), BF16), jax.ShapeDtypeStruct((T, dm.DP), F32),
                   jax.ShapeDtypeStruct((T, dm.DP), BF16)],
        scratch_shapes=[pltpu.VMEM((T + 2 * HALO, PG), F32)],
        compiler_params=_params(("parallel",)),
    )(proj, pw, ps)


def pool_bwd(dyc, mixed, pooled, pw, ps, dm, name):
    T, PG, R = dm.T, dm.PG, dm.tr
    _TN = (((0,), (0,)), ((), ()))

    def body(dyc_ref, mixed_ref, pooled_ref, pw_ref, ps_ref, dx_ref, dpw_ref, dps_ref, qbuf, dpbuf):
        g = pl.program_id(0)
        zeros = jnp.zeros((HALO, PG), F32)
        qbuf[pl.ds(0, HALO), :] = zeros
        qbuf[pl.ds(HALO + T, HALO), :] = zeros
        dpw_ref[...] = jnp.zeros_like(dpw_ref)

        def first(r, dps):
            r0 = pl.multiple_of(r * R, 8)
            dyc = dyc_ref[pl.ds(r0, R), :]
            dps = dps + jnp.sum(dyc * mixed_ref[pl.ds(r0, R), :], axis=0, keepdims=True)
            dmb = (dyc * ps_ref[...]).astype(BF16)
            dpw_ref[...] += lax.dot_general(pooled_ref[pl.ds(r0, R), :], dmb, _TN, preferred_element_type=F32)
            dp = lax.dot_general(dmb, pw_ref[...], _NT, preferred_element_type=F32)
            dpbuf[pl.ds(r0, R), :] = dp
            qbuf[pl.ds(r0 + HALO, R), :] = dp / _count(r0, R, g)
            return dps

        dps_ref[...] = lax.fori_loop(0, T // R, first, jnp.zeros((1, PG), F32))

        def second(r, c):
            r0 = pl.multiple_of(r * R, 8)
            win = qbuf[pl.ds(r0 + HALO, R + HALO), :]
            ws = _window_sum(win, g, _fwd)[:R, :]
            dx_ref[pl.ds(r0, R), :] = (ws - dpbuf[pl.ds(r0, R), :]).astype(BF16)
            return c

        lax.fori_loop(0, T // R, second, 0)

    own = pl.BlockSpec((T, PG), lambda g: (0, g))
    return pl.pallas_call(
        body, name=name, grid=(4,),
        in_specs=[own, own, own, pl.BlockSpec((None, PG, PG), lambda g: (g, 0, 0)), pl.BlockSpec((1, PG), lambda g: (0, g))],
        out_specs=[own, pl.BlockSpec((None, PG, PG), lambda g: (g, 0, 0)), pl.BlockSpec((1, PG), lambda g: (0, g))],
        out_shape=[jax.ShapeDtypeStruct((T, dm.DP), BF16), jax.ShapeDtypeStruct((4, PG, PG), F32),
                   jax.ShapeDtypeStruct((1, dm.DP), F32)],
        scratch_shapes=[pltpu.VMEM((T + 2 * HALO, PG), F32), pltpu.VMEM((T, PG), F32)],
        compiler_params=_params(("parallel",)),
    )(dyc, mixed, pooled, pw, ps)


def _sigmoid(x):
    return 1.0 / (1.0 + jnp.exp(-x))


def merge_fwd(proj, A, B, C, dm, name):
    T, D, tr, tc = dm.T, dm.D, dm.tr, 512
    nc = D // tc

    def body(g0, g1, g2, a, b, c, out):
        out[...] = (_sigmoid(g0[...]) * a[...] + _sigmoid(g1[...]) * b[...] + _sigmoid(g2[...]) * c[...]).astype(BF16)

    gate = lambda k: pl.BlockSpec((tr, tc), lambda i, j: (i, k * nc + j))
    own = pl.BlockSpec((tr, tc), lambda i, j: (i, j))
    return pl.pallas_call(
        body, name=name, grid=(T // tr, nc),
        in_specs=[gate(0), gate(1), gate(2), own, own, own],
        out_specs=own,
        out_shape=jax.ShapeDtypeStruct((T, D), BF16),
        compiler_params=_params(("parallel", "parallel")),
    )(proj, proj, proj, A, B, C)


def merge_bwd(proj, A, B, C, dmerged, dm, name):
    T, D, tr, tc = dm.T, dm.D, dm.tr, 512
    nc = D // tc

    def body(g0, g1, g2, a, b, c, dmr, da, db, dc, dl0, dl1, dl2):
        d = dmr[...]
        for g_ref, y_ref, dy_ref, dl_ref in ((g0, a, da, dl0), (g1, b, db, dl1), (g2, c, dc, dl2)):
            s = _sigmoid(g_ref[...])
            dy_ref[...] = (d * s).astype(BF16)
            dl_ref[...] = (d * y_ref[...] * s * (1.0 - s)).astype(BF16)

    gate = lambda k: pl.BlockSpec((tr, tc), lambda i, j: (i, k * nc + j))
    own = pl.BlockSpec((tr, tc), lambda i, j: (i, j))
    o = jax.ShapeDtypeStruct((T, D), BF16)
    return pl.pallas_call(
        body, name=name, grid=(T // tr, nc),
        in_specs=[gate(0), gate(1), gate(2), own, own, own, own],
        out_specs=[own] * 6,
        out_shape=[o] * 6,
        compiler_params=_params(("parallel", "parallel")),
    )(proj, proj, proj, A, B, C, dmerged)


def loss_head(y, target, dm, name):
    T, D, tr = dm.T, dm.D, dm.tr

    def body(y_ref, t_ref, dy_ref, dyb_ref, loss_ref):
        i = pl.program_id(0)
        t = _row_ids(i, tr)
        real = (t >= dm.n_meta) & (t < dm.T_real)
        err = jnp.where(real, y_ref[...] - t_ref[...], 0.0)
        dy = err * (1.0 / D)
        dy_ref[...] = dy
        dyb_ref[...] = dy.astype(BF16)

        @pl.when(i == 0)
        def _():
            loss_ref[...] = jnp.zeros_like(loss_ref)

        loss_ref[...] += 0.5 * jnp.sum(jnp.sum(err * err, axis=-1, keepdims=True) * (1.0 / D))

    row = pl.BlockSpec((tr, D), lambda i: (i, 0))
    return pl.pallas_call(
        body, name=name, grid=(T // tr,),
        in_specs=[row, row],
        out_specs=[row, row, pl.BlockSpec((8, LANES), lambda i: (0, 0))],
        out_shape=[jax.ShapeDtypeStruct((T, D), F32), jax.ShapeDtypeStruct((T, D), BF16),
                   jax.ShapeDtypeStruct((8, LANES), F32)],
        compiler_params=_params(("arbitrary",)),
    )(y, target)


def adamw(w, m, v, parts, name):
    R, C = w.shape
    P = parts.shape[0]
    br = R
    for cand in (512, 256, 128, 64, 32, 16, 8):
        if R % cand == 0 and cand * C * 4 <= (1 << 20):
            br = cand
            break
    if R * C * 4 <= (1 << 20):
        br = R

    def body(w_ref, m_ref, v_ref, p_ref, g_ref, d_ref, nm_ref, nv_ref):
        g = p_ref[0].astype(F32)
        for k in range(1, P):
            g = g + p_ref[k].astype(F32)
        mm = ADAM_B1 * m_ref[...] + (1.0 - ADAM_B1) * g
        vv = ADAM_B2 * v_ref[...] + (1.0 - ADAM_B2) * (g * g)
        m_hat = mm / (1.0 - ADAM_B1 ** ADAM_STEP)
        v_hat = vv / (1.0 - ADAM_B2 ** ADAM_STEP)
        g_ref[...] = g
        d_ref[...] = -ADAM_LR * (m_hat / (jnp.sqrt(v_hat) + ADAM_EPS) + ADAM_WD * w_ref[...])
        nm_ref[...] = mm
        nv_ref[...] = vv

    blk = pl.BlockSpec((br, C), lambda i: (i, 0))
    o = jax.ShapeDtypeStruct((R, C), F32)
    return pl.pallas_call(
        body, name=name, grid=(R // br,),
        in_specs=[blk, blk, blk, pl.BlockSpec((P, br, C), lambda i: (0, i, 0))],
        out_specs=[blk] * 4,
        out_shape=[o] * 4,
        compiler_params=_params(("parallel",)),
    )(w, m, v, parts)


def sum_parts(parts, name):
    P, R, C = parts.shape

    def body(p_ref, o_ref):
        acc = p_ref[0]
        for k in range(1, P):
            acc = acc + p_ref[k]
        o_ref[...] = acc

    return pl.pallas_call(
        body, name=name, grid=(1,),
        in_specs=[pl.BlockSpec((P, R, C), lambda i: (0, 0, 0))],
        out_specs=pl.BlockSpec((R, C), lambda i: (0, 0)),
        out_shape=jax.ShapeDtypeStruct((R, C), F32),
        compiler_params=_params(("arbitrary",)),
    )(parts)


def add_sibling(parts, got, core, name):
    _, _, R, C = parts.shape
    br = _pick(R, (1024, 512, 256, 128, 64, 32, 16))

    def body(c_ref, a_ref, b_ref, o_ref):
        o_ref[...] = (a_ref[...].astype(F32) + b_ref[...].astype(F32)).astype(BF16)

    blk = pl.BlockSpec((None, br, C), lambda ch, i, c: (ch, i, 0))
    return pl.pallas_call(
        body, name=name,
        grid_spec=pltpu.PrefetchScalarGridSpec(
            num_scalar_prefetch=1, grid=(4, R // br),
            in_specs=[pl.BlockSpec((None, None, br, C), lambda ch, i, c: (ch, c[0], i, 0)), blk],
            out_specs=blk),
        out_shape=jax.ShapeDtypeStruct((4, R, C), BF16), compiler_params=_params(("parallel", "parallel")),
    )(core, parts, got)


def all_gather(arrs, name):
    n = len(arrs)

    def body(*refs):
        ins, outs = refs[:n], refs[n:2 * n]
        send_sems, recv_sems, local_sems = refs[2 * n:]
        x, y, c = _place()
        me, sibling = (x, y, c), (x, y, 1 - c)
        chips = [(1 - x, y), (x, 1 - y), (1 - x, 1 - y)]

        def copy(a, k, block, to, src=None):
            px, py, pc = block
            dst = outs[a].at[4 * px + 2 * py + pc]
            return pltpu.make_async_remote_copy(
                src_ref=dst if src is None else src, dst_ref=dst,
                send_sem=send_sems.at[7 * a + k], recv_sem=recv_sems.at[7 * a + k],
                device_id=to, device_id_type=_MESH)

        started = []
        for a in range(n):
            mine = pltpu.make_async_copy(ins[a], outs[a].at[4 * x + 2 * y + c], local_sems.at[a])
            mine.start()
            started.append(mine)
        sends = []
        for a in range(n):
            sends.append(copy(a, 0, me, sibling, src=ins[a]))
            for j, chip in enumerate(chips):
                sends.append(copy(a, 1 + j, me, (*chip, c), src=ins[a]))
        for cp in sends:
            cp.start()
        for j, chip in enumerate(chips):
            for a in range(n):
                copy(a, 1 + j, (*chip, c), me).wait_recv()
                fwd = copy(a, 4 + j, (*chip, c), sibling)
                fwd.start()
                sends.append(fwd)
        for a in range(n):
            copy(a, 0, sibling, me).wait_recv()
            for j, chip in enumerate(chips):
                copy(a, 4 + j, (*chip, 1 - c), me).wait_recv()
        for cp in sends:
            cp.wait_send()
        for cp in started:
            cp.wait()

    outs = pl.pallas_call(
        body, name=name,
        in_specs=[_HBM] * n, out_specs=[_HBM] * n,
        out_shape=[jax.ShapeDtypeStruct((8,) + a.shape, a.dtype) for a in arrs],
        scratch_shapes=[pltpu.SemaphoreType.DMA((7 * n,)), pltpu.SemaphoreType.DMA((7 * n,)), pltpu.SemaphoreType.DMA((n,))],
    )(*arrs)
    return list(outs)


def sibling_exchange(arrs, name):
    n = len(arrs)

    def body(*refs):
        ins, got = refs[:n], refs[n:2 * n]
        send_sems, recv_sems = refs[2 * n:]
        x, y, c = _place()
        work = []
        for a in range(n):
            for ch in range(4):
                cp = pltpu.make_async_remote_copy(
                    src_ref=ins[a].at[ch, 1 - c], dst_ref=got[a].at[ch],
                    send_sem=send_sems.at[4 * a + ch], recv_sem=recv_sems.at[4 * a + ch],
                    device_id=(x, y, 1 - c), device_id_type=_MESH)
                cp.start()
                work.append(cp)
        for cp in work:
            cp.wait()

    outs = pl.pallas_call(
        body, name=name,
        in_specs=[_HBM] * n, out_specs=[_HBM] * n,
        out_shape=[jax.ShapeDtypeStruct((4,) + a.shape[2:], a.dtype) for a in arrs],
        scratch_shapes=[pltpu.SemaphoreType.DMA((4 * n,)), pltpu.SemaphoreType.DMA((4 * n,))],
    )(*arrs)
    return list(outs)


def _remote(src, dst, send_sem, recv_sem, to):
    return pltpu.make_async_remote_copy(src_ref=src, dst_ref=dst, send_sem=send_sem, recv_sem=recv_sem,
                                        device_id=to, device_id_type=_MESH)


def gather_ici(blocks):
    n = len(blocks)

    def copies(cins, couts, sems):
        send_sems, recv_sems, local_sems = sems
        x, y, c = _place()
        mine = 4 * x + 2 * y + c
        local, sends, recvs = [], [], []
        for a in range(n):
            local.append(pltpu.make_async_copy(cins[a], couts[a].at[mine], local_sems.at[a]))
            for j, (px, py) in enumerate([(1 - x, y), (x, 1 - y), (1 - x, 1 - y)]):
                k = 3 * a + j
                sends.append(_remote(cins[a], couts[a].at[mine], send_sems.at[k], recv_sems.at[k], (px, py, c)))
                recvs.append(_remote(cins[a], couts[a].at[4 * px + 2 * py + c], send_sems.at[k], recv_sems.at[k], (px, py, c)))
        return local, sends, recvs

    def start(cins, couts, sems):
        local, sends, _ = copies(cins, couts, sems)
        for cp in local + sends:
            cp.start()

    def finish(cins, couts, sems):
        local, sends, recvs = copies(cins, couts, sems)
        for cp in sends:
            cp.wait_send()
        for cp in recvs:
            cp.wait_recv()
        for cp in local:
            cp.wait()

    return Hosted(list(blocks), [jax.ShapeDtypeStruct((8,) + b.shape, b.dtype) for b in blocks],
                  [pltpu.SemaphoreType.DMA((3 * n,)), pltpu.SemaphoreType.DMA((3 * n,)), pltpu.SemaphoreType.DMA((n,))],
                  start, finish)


def fill_sibling(stks, name):
    n = len(stks)

    def body(*refs):
        ins, outs = refs[:n], refs[n:2 * n]
        send_sems, recv_sems = refs[2 * n:]
        x, y, c = _place()
        sends, recvs = [], []
        for a in range(n):
            for ch in range(4):
                k = 4 * a + ch
                sends.append(_remote(ins[a].at[2 * ch + c], outs[a].at[2 * ch + c], send_sems.at[k], recv_sems.at[k], (x, y, 1 - c)))
                recvs.append(_remote(ins[a].at[2 * ch + c], outs[a].at[2 * ch + 1 - c], send_sems.at[k], recv_sems.at[k], (x, y, 1 - c)))
        for cp in sends:
            cp.start()
        for cp in sends:
            cp.wait_send()
        for cp in recvs:
            cp.wait_recv()

    outs = pl.pallas_call(
        body, name=name,
        in_specs=[_HBM] * n, out_specs=[_HBM] * n,
        out_shape=[jax.ShapeDtypeStruct(s.shape, s.dtype) for s in stks],
        scratch_shapes=[pltpu.SemaphoreType.DMA((4 * n,)), pltpu.SemaphoreType.DMA((4 * n,))],
        input_output_aliases={a: a for a in range(n)},
    )(*stks)
    return list(outs)


def reduce_ici(arrs):
    n = len(arrs)

    def copies(cins, couts, sems):
        send_sems, recv_sems, local_sems = sems
        x, y, c = _place()
        my_chip = 2 * x + y
        local, sends, recvs = [], [], []
        for a in range(n):
            local.append(pltpu.make_async_copy(cins[a].at[my_chip], couts[a].at[my_chip], local_sems.at[a]))
            for j, (px, py) in enumerate([(1 - x, y), (x, 1 - y), (1 - x, 1 - y)]):
                k = 3 * a + j
                sends.append(_remote(cins[a].at[2 * px + py], couts[a].at[my_chip], send_sems.at[k], recv_sems.at[k], (px, py, c)))
                recvs.append(_remote(cins[a].at[my_chip], couts[a].at[2 * px + py], send_sems.at[k], recv_sems.at[k], (px, py, c)))
        return local, sends, recvs

    def start(cins, couts, sems):
        local, sends, _ = copies(cins, couts, sems)
        for cp in local + sends:
            cp.start()

    def finish(cins, couts, sems):
        local, sends, recvs = copies(cins, couts, sems)
        for cp in sends:
            cp.wait_send()
        for cp in recvs:
            cp.wait_recv()
        for cp in local:
            cp.wait()

    return Hosted(list(arrs), [jax.ShapeDtypeStruct(a.shape, a.dtype) for a in arrs],
                  [pltpu.SemaphoreType.DMA((3 * n,)), pltpu.SemaphoreType.DMA((3 * n,)), pltpu.SemaphoreType.DMA((n,))],
                  start, finish)


COL_SHARDED = ("w_in", "w_uq", "w_ukv", "w_branch_a", "w_branch_c", "w_up")
ROW_SHARDED = ("w_branch_b", "w_o", "w_down")
BIG = COL_SHARDED + ROW_SHARDED


def _full_from_stacked(name, st):
    if name in COL_SHARDED:
        _, L, K, n = st.shape
        return st.transpose(1, 2, 0, 3).reshape(L, K, 8 * n)
    if name in ROW_SHARDED:
        _, L, k, N = st.shape
        return st.transpose(1, 0, 2, 3).reshape(L, 8 * k, N)
    if name == "pool_w":
        _, L, G, pk, PG = st.shape
        return st.transpose(1, 2, 0, 3, 4).reshape(L, G, 8 * pk, PG)
    if name == "meta_tokens":
        _, M, n = st.shape
        return st.transpose(1, 0, 2).reshape(M, 8 * n)
    if name == "conv_w":
        _, L, W, n = st.shape
        return st.transpose(1, 2, 0, 3).reshape(L, W, 8 * n)
    raise ValueError(name)


def _shards_from_full(name, g):
    if name in COL_SHARDED:
        L, K, N = g.shape
        s = g.reshape(L, K, 8, N // 8).transpose(2, 0, 1, 3)
    else:
        L, K, N = g.shape
        s = g.reshape(L, 8, K // 8, N).transpose(1, 0, 2, 3)
    return s.reshape((4, 2) + s.shape[1:])


def _w_in_to_padded(w, dm):
    o3 = 3 * dm.DC + dm.QL + dm.KL
    o4 = o3 + dm.ROPE
    o5 = o4 + dm.DP
    pad = jnp.zeros(w.shape[:-1] + (256 - dm.ROPE,), w.dtype)
    return jnp.concatenate([w[..., o5:], w[..., :o3], w[..., o4:o5], w[..., o3:o4], pad], axis=-1)


def _w_in_from_padded(g, dm):
    o3 = 3 * dm.DC + dm.QL + dm.KL
    a = 3 * dm.D
    return jnp.concatenate([g[..., a:a + o3], g[..., dm.o_rope:dm.o_rope + dm.ROPE], g[..., dm.o_pool:dm.o_pool + dm.DP],
                            g[..., :a]], axis=-1)


def _pad_heads(w, dm):
    w = w.reshape(w.shape[:-1] + (dm.H, dm.QKH))
    w = jnp.pad(w, [(0, 0)] * (w.ndim - 1) + [(0, dm.HP - dm.QKH)])
    return w.reshape(w.shape[:-2] + (dm.H * dm.HP,))


def _unpad_heads(g, dm):
    g = g.reshape(g.shape[:-1] + (dm.H, dm.HP))[..., :dm.QKH]
    return g.reshape(g.shape[:-2] + (dm.H * dm.QKH,))


class _Weights:
    def __init__(self, w, dm, plan):
        self.w, self.dm, self.plan, self.full = w, dm, plan, {}

    def blocks(self, items):
        return [self.w[n][l:l + 1].astype(BF16) for n, l in items]

    def put(self, items, stacked):
        for (n, l), st in zip(items, stacked):
            f = _full_from_stacked(n, st)[0]
            if n == "w_in":
                f = _w_in_to_padded(f, self.dm)
            if n == "w_uq":
                f = _pad_heads(f, self.dm)
            self.full[(n, l)] = f

    def comm(self, tag):
        items = self.plan.get(tag)
        return gather_ici(self.blocks(items)) if items else None

    def arrived(self, tag, couts):
        self.put(self.plan[tag], fill_sibling(couts, f"fill_{tag}"))

    def __call__(self, n, l):
        return self.full[(n, l)]


class _Reducer:
    def __init__(self, dm, core):
        self.dm, self.core, self.q = dm, core, {}

    def prepare(self, tag, items, g):
        parts = []
        for n, l in items:
            f = g[n]
            if n == "w_in":
                f = _w_in_from_padded(f, self.dm)
            if n == "w_uq":
                f = _unpad_heads(f, self.dm)
            parts.append(_shards_from_full(n, f[None]))
        got = sibling_exchange(parts, f"reduce_sibling_{tag}")
        out = []
        for (n, l), a, b in zip(items, parts, got):
            C = a.shape[-1]
            out.append(add_sibling(a.reshape(4, 2, -1, C), b.reshape(4, -1, C), self.core,
                                   f"reduce_add_{n}_{l}").reshape(b.shape))
        return out

    def put(self, items, summed):
        for key, q in zip(items, summed):
            self.q[key] = q


def _layer_fwd(xin, l, ws, G, tabs, dm):
    nm = lambda s: f"l{l}_{s}"
    D = dm.D

    def mm(tag, *args, **kw):
        comm = ws.comm(nm(tag))
        if comm is None:
            return matmul(*args, nm(tag), **kw)
        res, couts = matmul(*args, nm(tag), comm=comm, **kw)
        ws.arrived(nm(tag), couts)
        return res

    h = rms_fwd(xin, 0, D, G["attn_norm"], dm, nm("rms1"))
    proj = mm("proj", h, ws("w_in", l), "nn", (F32,))
    ya = mixer_a_fwd(proj, G["conv_w"], dm, nm("mixa"))
    ql = rms_fwd(proj, dm.o_ql // dm.QL, dm.QL, G["q_lat_norm"], dm, nm("rms_q"))
    kl = rms_fwd(proj, dm.o_kl // dm.KL, dm.KL, G["kv_lat_norm"], dm, nm("rms_kv"))
    q0 = mm("uq", ql, ws("w_uq", l), "nn", (F32,))
    kv0 = mm("ukv", kl, ws("w_ukv", l), "nn", (F32,))
    comm = ws.comm(nm("qkprep"))
    if comm is None:
        q_s, k_s, v_s = qk_prep_fwd(q0, kv0, proj, G["q_norm"], G["k_norm"], tabs[0], tabs[1], dm, nm("qkprep"))
    else:
        (q_s, k_s, v_s), couts = qk_prep_fwd(q0, kv0, proj, G["q_norm"], G["k_norm"], tabs[0], tabs[1], dm, nm("qkprep"),
                                            comm=comm)
        ws.arrived(nm("qkprep"), couts)
    comm = ws.comm(nm("attn"))
    if comm is None:
        o, ob, lse = attn_fwd(q_s, k_s, v_s, dm, nm("attn"))
    else:
        (o, ob, lse), couts = attn_fwd(q_s, k_s, v_s, dm, nm("attn"), comm=comm)
        ws.arrived(nm("attn"), couts)
    pooled, mixed, yc = pool_fwd(proj, ws("pool_w", l), G["pool_scale"], dm, nm("pool"))
    A = mm("br_a", ya, ws("w_branch_a", l), "nn", (F32,))
    B = mm("br_b", ob, ws("w_branch_b", l), "nn", (F32,))
    C = mm("br_c", yc, ws("w_branch_c", l), "nn", (F32,))
    merged = merge_fwd(proj, A, B, C, dm, nm("merge"))
    x1 = mm("wo", merged, ws("w_o", l), "nn", (F32,), extras=(xin,), epi=lambda acc, r: (acc + r,))
    h2 = rms_fwd(x1, 0, D, G["mlp_norm"], dm, nm("rms2"))
    up, act = mm("up", h2, ws("w_up", l), "nn", (F32, BF16), epi=lambda acc: (acc, jnp.square(jnp.maximum(acc, 0.0))))
    x2 = mm("down", act, ws("w_down", l), "nn", (F32,), extras=(x1,), epi=lambda acc, r: (acc + r,))
    saved = dict(xin=xin, h=h, proj=proj, ya=ya, ql=ql, kl=kl, q0=q0, kv0=kv0, q_s=q_s, k_s=k_s, v_s=v_s, o=o, ob=ob,
                 lse=lse, pooled=pooled, mixed=mixed, yc=yc, A=A, B=B, C=C, merged=merged, x1=x1, h2=h2, up=up, act=act)
    return x2, saved


def _layer_bwd(dx2, dx2b, S, l, ws, G, tabs, dm, pre_attn, late=None):
    nm = lambda s: f"l{l}_b_{s}"
    D, T = dm.D, dm.T
    g = {}
    d_up = matmul(dx2b, ws("w_down", l), "nt", (BF16,), nm("d_act"), extras=(S["up"],),
                  epi=lambda acc, up: (acc * (2.0 * jnp.maximum(up, 0.0)),))
    g["w_down"] = matmul(S["act"], dx2b, "tn", (BF16,), nm("g_down"))
    g["w_up"] = matmul(S["h2"], d_up, "tn", (BF16,), nm("g_up"))
    dh2 = matmul(d_up, ws("w_up", l), "nt", (F32,), nm("d_h2"))
    dx1, dx1b, g["mlp_norm"] = rms_bwd(S["x1"], 0, D, G["mlp_norm"], dh2, dx2, dm, nm("rms2"))
    dmerged = matmul(dx1b, ws("w_o", l), "nt", (F32,), nm("d_merged"))
    g["w_o"] = matmul(S["merged"], dx1b, "tn", (BF16,), nm("g_o"))
    dA, dB, dC, dl0, dl1, dl2 = merge_bwd(S["proj"], S["A"], S["B"], S["C"], dmerged, dm, nm("merge"))
    dya = matmul(dA, ws("w_branch_a", l), "nt", (F32,), nm("d_ya"))
    g["w_branch_a"] = matmul(S["ya"], dA, "tn", (BF16,), nm("g_a"))
    dyb = matmul(dB, ws("w_branch_b", l), "nt", (F32,), nm("d_yb"))
    g["w_branch_b"] = matmul(S["ob"], dB, "tn", (BF16,), nm("g_b"))
    dyc = matmul(dC, ws("w_branch_c", l), "nt", (F32,), nm("d_yc"))
    g["w_branch_c"] = matmul(S["yc"], dC, "tn", (BF16,), nm("g_c"))
    du, db, dc, g["conv_w"] = mixer_a_bwd(S["proj"], G["conv_w"], dya, dm, nm("mixa"))
    dpool, g["pool_w"], g["pool_scale"] = pool_bwd(dyc, S["mixed"], S["pooled"], ws("pool_w", l), G["pool_scale"], dm, nm("pool"))
    delta, dob = attn_delta(dyb, S["o"], dm, nm("delta"))
    nq = T // dm.tq
    comm_dq, done_dq, comm_dkv, done_dkv = pre_attn(g)
    dq = attn_bwd_dq(S["q_s"], S["k_s"], S["v_s"], dob, S["lse"], delta, dm, nm("attn_dq"), comm=comm_dq)
    if comm_dq is not None:
        dq, couts = dq
        done_dq(couts)
    dkv = attn_bwd_dkv(S["q_s"], S["k_s"], S["v_s"], dob, S["lse"].reshape(dm.H, nq, 1, dm.tq),
                       delta.reshape(dm.H, nq, 1, dm.tq), dm, nm("attn_dkv"), comm=comm_dkv)
    if comm_dkv is not None:
        dkv, couts = dkv
        done_dkv(couts)
    dk, dv = dkv
    dq0, dkv0, dkr, g["q_norm"], g["k_norm"] = qk_prep_bwd(S["q0"], S["kv0"], S["proj"], G["q_norm"], G["k_norm"],
                                                            tabs[0], tabs[1], dq, dk, dv, dm, nm("qkprep"))
    dql = matmul(dq0, ws("w_uq", l), "nt", (F32,), nm("d_ql"))
    g["w_uq"] = matmul(S["ql"], dq0, "tn", (BF16,), nm("g_uq"))
    dkl = matmul(dkv0, ws("w_ukv", l), "nt", (F32,), nm("d_kl"))
    g["w_ukv"] = matmul(S["kl"], dkv0, "tn", (BF16,), nm("g_ukv"))
    _, dqlat, g["q_lat_norm"] = rms_bwd(S["proj"], dm.o_ql // dm.QL, dm.QL, G["q_lat_norm"], dql, None, dm, nm("rms_q"))
    _, dkvlat, g["kv_lat_norm"] = rms_bwd(S["proj"], dm.o_kl // dm.KL, dm.KL, G["kv_lat_norm"], dkl, None, dm, nm("rms_kv"))
    dproj = jnp.concatenate([dl0, dl1, dl2, du, db, dc, dqlat, dkvlat, dpool, dkr.astype(BF16),
                             jnp.zeros((T, 128), BF16)], axis=1)
    g["w_in"] = matmul(S["h"], dproj, "tn", (BF16,), nm("g_in"))
    comm_dh, done_dh = late(g) if late is not None else (None, None)
    dh = matmul(dproj, ws("w_in", l), "nt", (F32,), nm("d_h"), comm=comm_dh)
    if comm_dh is not None:
        dh, couts = dh
        done_dh(couts)
    dx, dxb, g["attn_norm"] = rms_bwd(S["xin"], 0, D, G["attn_norm"], dh, dx1, dm, nm("rms1"))
    return dx, dxb, g


WEIGHTS = ("meta_tokens", "attn_norm", "w_in", "conv_w", "q_lat_norm", "kv_lat_norm", "w_uq", "w_ukv", "q_norm", "k_norm",
           "pool_w", "pool_scale", "w_branch_a", "w_branch_b", "w_branch_c", "w_o", "mlp_norm", "w_up", "w_down")
SMALL = tuple(n for n in WEIGHTS if n not in BIG)


def kernel(x, meta_tokens, attn_norm, w_in, conv_w, q_lat_norm, kv_lat_norm, w_uq, w_ukv, q_norm, k_norm, pool_w, pool_scale, w_branch_a, w_branch_b, w_branch_c, w_o, mlp_norm, w_up, w_down, loss_target, m_meta_tokens, m_attn_norm, m_w_in, m_conv_w, m_q_lat_norm, m_kv_lat_norm, m_w_uq, m_w_ukv, m_q_norm, m_k_norm, m_pool_w, m_pool_scale, m_w_branch_a, m_w_branch_b, m_w_branch_c, m_w_o, m_mlp_norm, m_w_up, m_w_down, v_meta_tokens, v_attn_norm, v_w_in, v_conv_w, v_q_lat_norm, v_kv_lat_norm, v_w_uq, v_w_ukv, v_q_norm, v_k_norm, v_pool_w, v_pool_scale, v_w_branch_a, v_w_branch_b, v_w_branch_c, v_w_o, v_mlp_norm, v_w_up, v_w_down):
    w = dict(meta_tokens=meta_tokens, attn_norm=attn_norm, w_in=w_in, conv_w=conv_w, q_lat_norm=q_lat_norm,
             kv_lat_norm=kv_lat_norm, w_uq=w_uq, w_ukv=w_ukv, q_norm=q_norm, k_norm=k_norm, pool_w=pool_w,
             pool_scale=pool_scale, w_branch_a=w_branch_a, w_branch_b=w_branch_b, w_branch_c=w_branch_c, w_o=w_o,
             mlp_norm=mlp_norm, w_up=w_up, w_down=w_down)
    m = dict(meta_tokens=m_meta_tokens, attn_norm=m_attn_norm, w_in=m_w_in, conv_w=m_conv_w, q_lat_norm=m_q_lat_norm,
             kv_lat_norm=m_kv_lat_norm, w_uq=m_w_uq, w_ukv=m_w_ukv, q_norm=m_q_norm, k_norm=m_k_norm, pool_w=m_pool_w,
             pool_scale=m_pool_scale, w_branch_a=m_w_branch_a, w_branch_b=m_w_branch_b, w_branch_c=m_w_branch_c, w_o=m_w_o,
             mlp_norm=m_mlp_norm, w_up=m_w_up, w_down=m_w_down)
    v = dict(meta_tokens=v_meta_tokens, attn_norm=v_attn_norm, w_in=v_w_in, conv_w=v_conv_w, q_lat_norm=v_q_lat_norm,
             kv_lat_norm=v_kv_lat_norm, w_uq=v_w_uq, w_ukv=v_w_ukv, q_norm=v_q_norm, k_norm=v_k_norm, pool_w=v_pool_w,
             pool_scale=v_pool_scale, w_branch_a=v_w_branch_a, w_branch_b=v_w_branch_b, w_branch_c=v_w_branch_c, w_o=v_w_o,
             mlp_norm=v_mlp_norm, w_up=v_w_up, w_down=v_w_down)
    L = attn_norm.shape[0]
    assert L == 2, "the gather / reduce schedule below is written for two layers"
    seq, D = x.shape[1], x.shape[2]
    n_meta = meta_tokens.shape[0]
    dm = Dims(D, seq, n_meta)
    T = dm.T
    me = 4 * lax.axis_index("x") + 2 * lax.axis_index("y") + lax.axis_index("c")
    core = lax.axis_index("c").astype(jnp.int32).reshape(1)

    plan = {
        "l0_proj": [("w_uq", 0), ("w_ukv", 0), ("pool_w", 0), ("pool_w", 1), ("w_up", 0)],
        "l0_qkprep": [("w_branch_a", 0), ("w_branch_b", 0), ("w_branch_c", 0), ("w_o", 0)],
        "l0_attn": [("w_down", 0), ("w_in", 1)],
        "l0_up": [("w_up", 1)],
        "l0_down": [("w_down", 1)],
        "l1_proj": [("w_uq", 1), ("w_ukv", 1), ("w_branch_a", 1), ("w_branch_b", 1), ("w_branch_c", 1), ("w_o", 1)],
    }
    ws = _Weights(w, dm, plan)
    first = [("w_in", 0)]
    ws.put(first, all_gather(ws.blocks(first), "gather_first"))
    st_small = all_gather([w["meta_tokens"], w["conv_w"]], "gather_small")
    meta_full = _full_from_stacked("meta_tokens", st_small[0])
    conv_full = _full_from_stacked("conv_w", st_small[1])
    pad_gain = lambda gn: jnp.pad(gn, (0, dm.HP - dm.QKH))

    def gains(l):
        G = {n: w[n][l][None, :] for n in ("attn_norm", "q_lat_norm", "kv_lat_norm", "pool_scale", "mlp_norm")}
        G["q_norm"], G["k_norm"] = pad_gain(w["q_norm"][l])[None, :], pad_gain(w["k_norm"][l])[None, :]
        G["conv_w"] = conv_full[l]
        return G

    Gs = [gains(l) for l in range(L)]

    pos = jnp.arange(dm.T_real, dtype=F32)
    inv = 10000.0 ** (-jnp.arange(0, dm.ROPE, 2, dtype=F32) / dm.ROPE)
    ang = pos[:, None] * inv[None, :]
    zpad = jnp.zeros((dm.T_real, LANES - dm.ROPE), F32)
    rows = ((0, T - dm.T_real), (0, 0))
    tabs = (jnp.pad(jnp.concatenate([jnp.cos(ang), jnp.cos(ang), zpad], 1), rows),
            jnp.pad(jnp.concatenate([jnp.sin(ang), jnp.sin(ang), zpad], 1), rows))

    xs = jnp.concatenate([meta_full, x[0], jnp.zeros((T - dm.T_real, D), F32)], axis=0)
    target = jnp.pad(loss_target[0], ((n_meta, T - dm.T_real), (0, 0)))
    saved = []
    for l in range(L):
        xs, S = _layer_fwd(xs, l, ws, Gs[l], tabs, dm)
        saved.append(S)
    dx, dxb, loss_acc = loss_head(xs, target, dm, "loss_head")

    red = _Reducer(dm, core)
    early = lambda l: [(n, l) for n in ("w_down", "w_up", "w_o", "w_branch_a", "w_branch_b", "w_branch_c")]
    late = lambda l: [(n, l) for n in ("w_uq", "w_ukv", "w_in")]
    grads = [None] * L

    def pre_attn_1(g):
        return (reduce_ici(red.prepare("e1", early(1), g)), lambda couts: red.put(early(1), couts), None, None)

    dx, dxb, grads[1] = _layer_bwd(dx, dxb, saved[1], 1, ws, Gs[1], tabs, dm, pre_attn_1)
    late1 = red.prepare("l1", late(1), grads[1])

    def pre_attn_0(g):
        return (reduce_ici(late1), lambda couts: red.put(late(1), couts),
                reduce_ici(red.prepare("e0", early(0), g)), lambda couts: red.put(early(0), couts))

    def late_0(g):
        return reduce_ici(red.prepare("l0", late(0), g)), lambda couts: red.put(late(0), couts)

    dx, dxb, grads[0] = _layer_bwd(dx, dxb, saved[0], 0, ws, Gs[0], tabs, dm, pre_attn_0, late_0)
    grad_x = dx[n_meta:dm.T_real][None]
    summed = [jnp.concatenate([red.q[(n, l)] for l in range(L)], axis=1) for n in BIG]

    out_g, out_d, out_m, out_v = {}, {}, {}, {}

    def update(n, parts3):
        shp = w[n].shape
        C = shp[-1]
        res = adamw(w[n].reshape(-1, C), m[n].reshape(-1, C), v[n].reshape(-1, C), parts3, f"adamw_{n}")
        out_g[n], out_d[n], out_m[n], out_v[n] = [r.reshape(shp) for r in res]

    for n, q in zip(BIG, summed):
        update(n, q.reshape(4, -1, q.shape[-1]))

    small_full = {
        "meta_tokens": dx[:n_meta],
        "conv_w": jnp.stack([grads[l]["conv_w"] for l in range(L)]),
        "pool_w": jnp.stack([grads[l]["pool_w"] for l in range(L)]),
        "q_norm": jnp.stack([grads[l]["q_norm"][0, :dm.QKH] for l in range(L)]),
        "k_norm": jnp.stack([grads[l]["k_norm"][0, :dm.QKH] for l in range(L)]),
    }
    for n in ("attn_norm", "q_lat_norm", "kv_lat_norm", "pool_scale", "mlp_norm"):
        small_full[n] = jnp.stack([grads[l][n][0] for l in range(L)])
    flat = jnp.concatenate([small_full[n].reshape(-1) for n in SMALL] + [loss_acc[0, :1]])
    n_flat = flat.shape[0]
    rows_small = -(-n_flat // (8 * LANES)) * 8
    flat = jnp.pad(flat, (0, rows_small * LANES - n_flat)).reshape(rows_small, LANES)
    total = sum_parts(all_gather([flat], "gather_small_grads")[0], "sum_small").reshape(-1)
    off = 0
    for n in SMALL:
        size = math.prod(small_full[n].shape)
        gsum = total[off:off + size].reshape(small_full[n].shape)
        off += size
        if n in ("meta_tokens", "conv_w"):
            blk = w[n].shape[-1]
            gsum = lax.dynamic_slice_in_dim(gsum, me * blk, blk, axis=gsum.ndim - 1)
        elif n == "pool_w":
            blk = w[n].shape[2]
            gsum = lax.dynamic_slice_in_dim(gsum, me * blk, blk, axis=2)
        update(n, gsum.reshape(1, -1, gsum.shape[-1]))
    loss = total[off]

    return (loss, grad_x, *[out_g[n] for n in WEIGHTS], *[out_d[n] for n in WEIGHTS],
            *[out_m[n] for n in WEIGHTS], *[out_v[n] for n in WEIGHTS])
```

```python
import functools
import math

import jax
import jax.numpy as jnp
from jax import lax
from jax.experimental import pallas as pl
from jax.experimental.pallas import tpu as pltpu

F32 = jnp.float32
BF16 = jnp.bfloat16

VMEM_LIMIT_BYTES = 56 * 1024 * 1024
LANES = 128
EPS = 1e-6
HALO = 16

ADAM_LR = 0.001
ADAM_B1 = 0.9
ADAM_B2 = 0.999
ADAM_EPS = 1e-08
ADAM_WD = 0.01
ADAM_STEP = 10


def _params(sem):
    return pltpu.CompilerParams(dimension_semantics=sem, vmem_limit_bytes=VMEM_LIMIT_BYTES)


def _pick(n, prefs):
    for p in prefs:
        if p <= n and n % p == 0:
            return p
    return n


_MESH = pl.DeviceIdType.MESH
_HBM = pl.BlockSpec(memory_space=pltpu.HBM)


def _place():
    return lax.axis_index("x"), lax.axis_index("y"), lax.axis_index("c")


class Hosted:
    def __init__(self, ins, out_shapes, sems, start, finish):
        self.ins, self.out_shapes, self.sems, self.start, self.finish = ins, out_shapes, sems, start, finish


def _hosted_call(body, *, name, grid, in_specs, out_specs, out_shape, scratch_shapes, semantics, args, comm):
    n_in, n_out, n_scr = len(in_specs), len(out_specs), len(scratch_shapes)
    if comm is None:
        outs = pl.pallas_call(body, name=name, grid=grid, in_specs=in_specs, out_specs=out_specs, out_shape=out_shape,
                              scratch_shapes=scratch_shapes, compiler_params=_params(semantics))(*args)
        return list(outs), []
    ci, co = len(comm.ins), len(comm.out_shapes)

    def hosting(*refs):
        ins, cins = refs[:n_in], refs[n_in:n_in + ci]
        outs = refs[n_in + ci:n_in + ci + n_out]
        couts = refs[n_in + ci + n_out:n_in + ci + n_out + co]
        scr = refs[n_in + ci + n_out + co:n_in + ci + n_out + co + n_scr]
        csems = refs[n_in + ci + n_out + co + n_scr:]
        ids = [pl.program_id(d) for d in range(len(grid))]
        first = functools.reduce(jnp.logical_and, [i == 0 for i in ids])
        last = functools.reduce(jnp.logical_and, [i == g - 1 for i, g in zip(ids, grid)])

        @pl.when(first)
        def _():
            comm.start(cins, couts, csems)

        body(*ins, *outs, *scr)

        @pl.when(last)
        def _():
            comm.finish(cins, couts, csems)

    outs = pl.pallas_call(
        hosting, name=name, grid=grid,
        in_specs=list(in_specs) + [_HBM] * ci, out_specs=list(out_specs) + [_HBM] * co,
        out_shape=list(out_shape) + list(comm.out_shapes),
        scratch_shapes=list(scratch_shapes) + list(comm.sems),
        compiler_params=_params(("arbitrary",) * len(grid)),
    )(*args, *comm.ins)
    return list(outs[:n_out]), list(outs[n_out:])


MXU_FLOPS = 750e12
HBM_BYTES_PER_S = 3.0e12
ACC_RMW_BYTES_PER_S = 8e12
GRID_STEP_S = 0.4e-6
VMEM_COMPILER_RESERVE_BYTES = 8 * 1024 * 1024
MAX_TILE_ROWS, MAX_TILE_COLS = 2112, 2304


def _divisors(n, step):
    return [d for d in range(step, n + 1, step) if n % d == 0]


def _matmul_tiles(M, N, K, mode, out_sizes, n_extra):
    budget = VMEM_LIMIT_BYTES - VMEM_COMPILER_RESERVE_BYTES
    best = None
    for tk in _divisors(K, 16 if mode == "tn" else LANES):
        for tm in _divisors(M, LANES if mode == "tn" else 16):
            if tm > MAX_TILE_ROWS:
                continue
            for tn in _divisors(N, LANES):
                if tn > MAX_TILE_COLS:
                    continue
                need = 4 * tm * tk + 4 * tk * tn + 4 * tm * tn + 2 * tm * tn * sum(out_sizes) + 8 * n_extra * tm * tn
                if need > budget:
                    continue
                nk = K // tk
                steps = (M // tm) * (N // tn) * nk
                t_mxu = 2 * M * N * K / MXU_FLOPS + (nk > 1) * (8 * M * N * nk) / ACC_RMW_BYTES_PER_S
                t_hbm = (2 * M * K * (N // tn) + 2 * K * N * (M // tm) + M * N * (sum(out_sizes) + 4 * n_extra)) / HBM_BYTES_PER_S
                t = max(t_mxu, t_hbm) + steps * GRID_STEP_S
                if best is None or t < best[0]:
                    best = (t, tm, tn, tk)
    assert best is not None, (M, N, K, mode)
    return best[1:]


def matmul(a, b, mode, out_dtypes, name, extras=(), epi=None, tm=None, tn=None, tk=None, comm=None):
    if mode == "nn":
        (M, K), (K2, N) = a.shape, b.shape
    elif mode == "nt":
        (M, K), (N, K2) = a.shape, b.shape
    else:
        (K, M), (K2, N) = a.shape, b.shape
    assert K == K2, (a.shape, b.shape, mode)
    if not (tm and tn and tk):
        tm, tn, tk = _matmul_tiles(M, N, K, mode, [jnp.dtype(d).itemsize for d in out_dtypes], len(extras))
    nk = K // tk
    dims = {"nn": (((1,), (0,)), ((), ())), "nt": (((1,), (1,)), ((), ())), "tn": (((0,), (0,)), ((), ()))}[mode]
    n_extra, n_out = len(extras), len(out_dtypes)

    def body(*refs):
        a_ref, b_ref = refs[0], refs[1]
        extra_refs = refs[2:2 + n_extra]
        out_refs = refs[2 + n_extra:2 + n_extra + n_out]

        def finish(acc):
            outs = (acc,) if epi is None else epi(acc, *[r[...] for r in extra_refs])
            for o_ref, o in zip(out_refs, outs):
                o_ref[...] = o.astype(o_ref.dtype)

        part = lax.dot_general(a_ref[...], b_ref[...], dims, preferred_element_type=F32)
        if nk == 1:
            finish(part)
            return
        acc_ref = refs[-1]
        k = pl.program_id(2)

        @pl.when(k == 0)
        def _():
            acc_ref[...] = part

        @pl.when(k > 0)
        def _():
            acc_ref[...] += part

        @pl.when(k == nk - 1)
        def _():
            finish(acc_ref[...])

    a_spec = {"nn": pl.BlockSpec((tm, tk), lambda i, j, k: (i, k)),
              "nt": pl.BlockSpec((tm, tk), lambda i, j, k: (i, k)),
              "tn": pl.BlockSpec((tk, tm), lambda i, j, k: (k, i))}[mode]
    b_spec = {"nn": pl.BlockSpec((tk, tn), lambda i, j, k: (k, j)),
              "nt": pl.BlockSpec((tn, tk), lambda i, j, k: (j, k)),
              "tn": pl.BlockSpec((tk, tn), lambda i, j, k: (k, j))}[mode]
    o_spec = pl.BlockSpec((tm, tn), lambda i, j, k: (i, j))
    outs, couts = _hosted_call(
        body, name=name, grid=(M // tm, N // tn, nk),
        in_specs=[a_spec, b_spec] + [o_spec] * n_extra,
        out_specs=[o_spec] * n_out,
        out_shape=[jax.ShapeDtypeStruct((M, N), d) for d in out_dtypes],
        scratch_shapes=[pltpu.VMEM((tm, tn), F32)] if nk > 1 else [],
        semantics=("parallel", "parallel", "arbitrary"), args=(a, b, *extras), comm=comm)
    res = outs[0] if n_out == 1 else outs
    return res if comm is None else (res, couts)


class Dims:
    def __init__(self, d_model, seq, n_meta):
        self.D = d_model
        self.n_meta = n_meta
        self.T_real = seq + n_meta
        self.T = -(-self.T_real // LANES) * LANES
        self.H = d_model // 128
        self.DC = d_model // 2
        self.DP = d_model // 2
        self.PG = self.DP // 4
        self.QL = 512
        self.KL = 512
        self.ROPE = 64
        self.NOPE = 128
        self.QKH = 192
        self.HP = 256
        self.DFF = 4 * d_model
        self.o_gate = 0
        self.o_u = 3 * d_model
        self.o_b = self.o_u + self.DC
        self.o_c = self.o_b + self.DC
        self.o_ql = self.o_c + self.DC
        self.o_kl = self.o_ql + self.QL
        self.o_pool = self.o_kl + self.KL
        self.o_rope = self.o_pool + self.DP
        self.NIN = self.o_rope + 256
        self.tr = _pick(self.T, (384, 256, 128))
        self.tq = _pick(self.T, (384, 256, 128))


def _row_ids(i, tr):
    return i * tr + lax.broadcasted_iota(jnp.int32, (tr, 1), 0)


def rms_fwd(x, col_block, width, g, dm, name):
    tr = dm.tr

    def body(x_ref, g_ref, y_ref):
        xv = x_ref[...]
        r = lax.rsqrt(jnp.mean(xv * xv, axis=-1, keepdims=True) + EPS)
        y_ref[...] = (xv * r * g_ref[...]).astype(y_ref.dtype)

    return pl.pallas_call(
        body, name=name, grid=(dm.T // tr,),
        in_specs=[pl.BlockSpec((tr, width), lambda i: (i, col_block)), pl.BlockSpec((1, width), lambda i: (0, 0))],
        out_specs=pl.BlockSpec((tr, width), lambda i: (i, 0)),
        out_shape=jax.ShapeDtypeStruct((dm.T, width), BF16),
        compiler_params=_params(("parallel",)),
    )(x, g.reshape(1, width))


def rms_bwd(x, col_block, width, g, dy, dres, dm, name):
    tr = dm.tr
    has_res = dres is not None

    def body(*refs):
        if has_res:
            x_ref, g_ref, dy_ref, dres_ref, dx_ref, dxb_ref, dg_ref = refs
        else:
            x_ref, g_ref, dy_ref, dx_ref, dxb_ref, dg_ref = refs
        xv, dyv = x_ref[...], dy_ref[...]
        r = lax.rsqrt(jnp.mean(xv * xv, axis=-1, keepdims=True) + EPS)
        gdy = dyv * g_ref[...]
        dx = r * gdy - xv * (r * r * r) * jnp.mean(xv * gdy, axis=-1, keepdims=True)
        if has_res:
            dx = dx + dres_ref[...]
        dx_ref[...] = dx
        dxb_ref[...] = dx.astype(BF16)

        @pl.when(pl.program_id(0) == 0)
        def _():
            dg_ref[...] = jnp.zeros_like(dg_ref)

        dg_ref[...] += jnp.sum(dyv * xv * r, axis=0, keepdims=True)

    row = pl.BlockSpec((tr, width), lambda i: (i, 0))
    in_specs = [pl.BlockSpec((tr, width), lambda i: (i, col_block)), pl.BlockSpec((1, width), lambda i: (0, 0)), row]
    args = [x, g.reshape(1, width), dy]
    if has_res:
        in_specs.append(row)
        args.append(dres)
    return pl.pallas_call(
        body, name=name, grid=(dm.T // tr,),
        in_specs=in_specs,
        out_specs=[row, row, pl.BlockSpec((1, width), lambda i: (0, 0))],
        out_shape=[jax.ShapeDtypeStruct((dm.T, width), F32), jax.ShapeDtypeStruct((dm.T, width), BF16),
                   jax.ShapeDtypeStruct((1, width), F32)],
        compiler_params=_params(("arbitrary",)),
    )(*args)


def _fill_halo_buf(buf, src_fn, T, R, width):
    zeros = jnp.zeros((HALO, width), F32)
    buf[pl.ds(0, HALO), :] = zeros
    buf[pl.ds(HALO + T, HALO), :] = zeros

    def fill(r, c):
        r0 = pl.multiple_of(r * R, 8)
        buf[pl.ds(r0 + HALO, R), :] = src_fn(r0)
        return c

    lax.fori_loop(0, T // R, fill, 0)


def _back(win, sh):
    return pltpu.roll(win, sh, 0)


def _fwd(win, sh):
    return pltpu.roll(win, win.shape[0] - sh, 0)


def mixer_a_fwd(proj, conv_w, dm, name):
    T, cw = dm.T, 128
    R = dm.tr
    nb = dm.DC // cw

    def body(u_ref, b_ref, c_ref, w_ref, ya_ref, buf):
        _fill_halo_buf(buf, lambda r0: c_ref[pl.ds(r0, R), :] * u_ref[pl.ds(r0, R), :], T, R, cw)
        w0, w1, w2 = w_ref[0:1, :], w_ref[1:2, :], w_ref[2:3, :]

        def chunk(r, c):
            r0 = pl.multiple_of(r * R, 8)
            win = buf[pl.ds(r0, R + HALO), :]
            cv = w2 * win + w1 * _back(win, 1) + w0 * _back(win, 2)
            ya_ref[pl.ds(r0, R), :] = (b_ref[pl.ds(r0, R), :] * cv[HALO:, :]).astype(BF16)
            return c

        lax.fori_loop(0, T // R, chunk, 0)

    col = lambda off: pl.BlockSpec((T, cw), lambda j: (0, off // cw + j))
    return pl.pallas_call(
        body, name=name, grid=(nb,),
        in_specs=[col(dm.o_u), col(dm.o_b), col(dm.o_c), pl.BlockSpec((3, cw), lambda j: (0, j))],
        out_specs=pl.BlockSpec((T, cw), lambda j: (0, j)),
        out_shape=jax.ShapeDtypeStruct((T, dm.DC), BF16),
        scratch_shapes=[pltpu.VMEM((T + 2 * HALO, cw), F32)],
        compiler_params=_params(("parallel",)),
    )(proj, proj, proj, conv_w)


def mixer_a_bwd(proj, conv_w, dya, dm, name):
    T, cw = dm.T, 128
    R = dm.tr
    nb = dm.DC // cw

    def body(u_ref, b_ref, c_ref, w_ref, dya_ref, du_ref, db_ref, dc_ref, dw_ref, sbuf, gbuf):
        _fill_halo_buf(sbuf, lambda r0: c_ref[pl.ds(r0, R), :] * u_ref[pl.ds(r0, R), :], T, R, cw)
        _fill_halo_buf(gbuf, lambda r0: dya_ref[pl.ds(r0, R), :] * b_ref[pl.ds(r0, R), :], T, R, cw)
        w0, w1, w2 = w_ref[0:1, :], w_ref[1:2, :], w_ref[2:3, :]

        def chunk(r, acc):
            a0, a1, a2 = acc
            r0 = pl.multiple_of(r * R, 8)
            swin = sbuf[pl.ds(r0, R + HALO), :]
            s0, s1, s2 = swin[HALO:, :], _back(swin, 1)[HALO:, :], _back(swin, 2)[HALO:, :]
            gwin = gbuf[pl.ds(r0 + HALO, R + HALO), :]
            g0, g1, g2 = gwin[:R, :], _fwd(gwin, 1)[:R, :], _fwd(gwin, 2)[:R, :]
            cv = w2 * s0 + w1 * s1 + w0 * s2
            ds = w2 * g0 + w1 * g1 + w0 * g2
            db_ref[pl.ds(r0, R), :] = (dya_ref[pl.ds(r0, R), :] * cv).astype(BF16)
            du_ref[pl.ds(r0, R), :] = (ds * c_ref[pl.ds(r0, R), :]).astype(BF16)
            dc_ref[pl.ds(r0, R), :] = (ds * u_ref[pl.ds(r0, R), :]).astype(BF16)
            a2 = a2 + jnp.sum(g0 * s0, axis=0, keepdims=True)
            a1 = a1 + jnp.sum(g0 * s1, axis=0, keepdims=True)
            a0 = a0 + jnp.sum(g0 * s2, axis=0, keepdims=True)
            return a0, a1, a2

        z = jnp.zeros((1, cw), F32)
        a0, a1, a2 = lax.fori_loop(0, T // R, chunk, (z, z, z))
        dw_ref[0:1, :] = a0
        dw_ref[1:2, :] = a1
        dw_ref[2:3, :] = a2

    col = lambda off: pl.BlockSpec((T, cw), lambda j: (0, off // cw + j))
    own = pl.BlockSpec((T, cw), lambda j: (0, j))
    o = jax.ShapeDtypeStruct((T, dm.DC), BF16)
    return pl.pallas_call(
        body, name=name, grid=(nb,),
        in_specs=[col(dm.o_u), col(dm.o_b), col(dm.o_c), pl.BlockSpec((3, cw), lambda j: (0, j)), own],
        out_specs=[own, own, own, pl.BlockSpec((3, cw), lambda j: (0, j))],
        out_shape=[o, o, o, jax.ShapeDtypeStruct((3, dm.DC), F32)],
        scratch_shapes=[pltpu.VMEM((T + 2 * HALO, cw), F32), pltpu.VMEM((T + 2 * HALO, cw), F32)],
        compiler_params=_params(("parallel",)),
    )(proj, proj, proj, conv_w, dya)


def _rope(x, C, S):
    return x * C + (pltpu.roll(x, 32, 1) - pltpu.roll(x, 96, 1)) * S


def _rope_t(dy, C, S):
    return dy * C + (pltpu.roll(dy, 96, 1) - pltpu.roll(dy, 32, 1)) * S


HP = 4


def qk_prep_fwd(q0, kv0, proj, qn, kn, C, S, dm, name, comm=None):
    T, H, tr = dm.T, dm.H, dm.tr
    inv = 1.0 / dm.QKH

    def body(q0_ref, kv_ref, kr_ref, qn_ref, kn_ref, c_ref, s_ref, q_ref, k_ref, v_ref):
        Cv, Sv = c_ref[...], s_ref[...]
        kb = kr_ref[...]
        kb2 = jnp.sum(kb * kb, -1, keepdims=True)
        for a in range(HP):
            lo, mid, hi = 256 * a, 256 * a + 128, 256 * (a + 1)
            qa, qb = q0_ref[:, lo:mid], q0_ref[:, mid:hi]
            r = lax.rsqrt((jnp.sum(qa * qa, -1, keepdims=True) + jnp.sum(qb * qb, -1, keepdims=True)) * inv + EPS)
            q_ref[:, lo:mid] = (qa * r * qn_ref[:, :128]).astype(BF16)
            q_ref[:, mid:hi] = _rope(qb * r * qn_ref[:, 128:], Cv, Sv).astype(BF16)
            ka = kv_ref[:, lo:mid]
            r = lax.rsqrt((jnp.sum(ka * ka, -1, keepdims=True) + kb2) * inv + EPS)
            k_ref[:, lo:mid] = (ka * r * kn_ref[:, :128]).astype(BF16)
            k_ref[:, mid:hi] = _rope(kb * r * kn_ref[:, 128:], Cv, Sv).astype(BF16)
            v_ref[:, 128 * a:128 * (a + 1)] = kv_ref[:, mid:hi].astype(BF16)

    head = pl.BlockSpec((tr, 256 * HP), lambda i, h: (i, h))
    gain = pl.BlockSpec((1, 256), lambda i, h: (0, 0))
    tab = pl.BlockSpec((tr, 128), lambda i, h: (i, 0))
    outs, couts = _hosted_call(
        body, name=name, grid=(T // tr, H // HP),
        in_specs=[head, head, pl.BlockSpec((tr, 128), lambda i, h: (i, dm.o_rope // 128)), gain, gain, tab, tab],
        out_specs=[head, head, pl.BlockSpec((tr, 128 * HP), lambda i, h: (i, h))],
        out_shape=[jax.ShapeDtypeStruct((T, H * 256), BF16), jax.ShapeDtypeStruct((T, H * 256), BF16),
                   jax.ShapeDtypeStruct((T, H * 128), BF16)],
        scratch_shapes=[], semantics=("parallel", "parallel"), args=(q0, kv0, proj, qn, kn, C, S), comm=comm)
    return outs if comm is None else (outs, couts)


def qk_prep_bwd(q0, kv0, proj, qn, kn, C, S, dq, dk, dv, dm, name):
    T, H, tr = dm.T, dm.H, dm.tr
    inv = 1.0 / dm.QKH

    def body(q0_ref, kv_ref, kr_ref, qn_ref, kn_ref, c_ref, s_ref, dq_ref, dk_ref, dv_ref,
             dq0_ref, dkv_ref, dkr_ref, dqn_ref, dkn_ref):
        i, h = pl.program_id(0), pl.program_id(1)
        Cv, Sv = c_ref[...], s_ref[...]

        def norm_bwd(xa, xb, ga, gb, dya, dyb):
            r = lax.rsqrt((jnp.sum(xa * xa, -1, keepdims=True) + jnp.sum(xb * xb, -1, keepdims=True)) * inv + EPS)
            dzb = _rope_t(dyb, Cv, Sv)
            gda, gdb = ga * dya, gb * dzb
            dot = (jnp.sum(xa * gda, -1, keepdims=True) + jnp.sum(xb * gdb, -1, keepdims=True)) * inv
            r3 = r * r * r
            dxa = r * gda - xa * r3 * dot
            dxb = r * gdb - xb * r3 * dot
            dga = jnp.sum(dya * xa * r, axis=0, keepdims=True)
            dgb = jnp.sum(dzb * xb * r, axis=0, keepdims=True)
            return dxa, dxb, dga, dgb

        @pl.when((i == 0) & (h == 0))
        def _():
            dqn_ref[...] = jnp.zeros_like(dqn_ref)
            dkn_ref[...] = jnp.zeros_like(dkn_ref)

        @pl.when(h == 0)
        def _():
            dkr_ref[...] = jnp.zeros_like(dkr_ref)

        for a in range(HP):
            lo, mid, hi = 256 * a, 256 * a + 128, 256 * (a + 1)
            dxa, dxb, dga, dgb = norm_bwd(q0_ref[:, lo:mid], q0_ref[:, mid:hi], qn_ref[:, :128], qn_ref[:, 128:],
                                          dq_ref[:, lo:mid], dq_ref[:, mid:hi])
            dq0_ref[:, lo:mid] = dxa.astype(BF16)
            dq0_ref[:, mid:hi] = dxb.astype(BF16)
            dqn_ref[:, :128] += dga
            dqn_ref[:, 128:] += dgb
            dxa, dxb, dga, dgb = norm_bwd(kv_ref[:, lo:mid], kr_ref[...], kn_ref[:, :128], kn_ref[:, 128:],
                                          dk_ref[:, lo:mid], dk_ref[:, mid:hi])
            dkv_ref[:, lo:mid] = dxa.astype(BF16)
            dkv_ref[:, mid:hi] = dv_ref[:, 128 * a:128 * (a + 1)].astype(BF16)
            dkn_ref[:, :128] += dga
            dkn_ref[:, 128:] += dgb
            dkr_ref[...] += dxb

    head = pl.BlockSpec((tr, 256 * HP), lambda i, h: (i, h))
    gain = pl.BlockSpec((1, 256), lambda i, h: (0, 0))
    tab = pl.BlockSpec((tr, 128), lambda i, h: (i, 0))
    return pl.pallas_call(
        body, name=name, grid=(T // tr, H // HP),
        in_specs=[head, head, pl.BlockSpec((tr, 128), lambda i, h: (i, dm.o_rope // 128)), gain, gain, tab, tab,
                  head, head, pl.BlockSpec((tr, 128 * HP), lambda i, h: (i, h))],
        out_specs=[head, head, tab, gain, gain],
        out_shape=[jax.ShapeDtypeStruct((T, H * 256), BF16), jax.ShapeDtypeStruct((T, H * 256), BF16),
                   jax.ShapeDtypeStruct((T, 128), F32), jax.ShapeDtypeStruct((1, 256), F32),
                   jax.ShapeDtypeStruct((1, 256), F32)],
        compiler_params=_params(("arbitrary", "arbitrary")),
    )(q0, kv0, proj, qn, kn, C, S, dq, dk, dv)


_NT = (((1,), (1,)), ((), ()))


def _causal_mask(t):
    return lax.broadcasted_iota(jnp.int32, (t, t), 0) >= lax.broadcasted_iota(jnp.int32, (t, t), 1)


def _causal_mask_t(t):
    return lax.broadcasted_iota(jnp.int32, (t, t), 0) <= lax.broadcasted_iota(jnp.int32, (t, t), 1)


HB = 2


def attn_fwd(q, k, v, dm, name, comm=None):
    T, H, tq = dm.T, dm.H, dm.tq
    scale = dm.QKH ** -0.5

    def body(q_ref, k_ref, v_ref, o_ref, ob_ref, lse_ref):
        qi = pl.program_id(1)

        def step(j0, w, carry, masked):
            ss = [lax.dot_general(q_ref[:, 256 * a:256 * (a + 1)], k_ref[pl.ds(j0, w), 256 * a:256 * (a + 1)], _NT,
                                  preferred_element_type=F32) for a in range(HB)]
            out = []
            for a in range(HB):
                m, l, acc = carry[a]
                s = ss[a] * scale
                if masked:
                    s = jnp.where(_causal_mask(tq), s, -jnp.inf)
                m_new = jnp.maximum(m, jnp.max(s, -1, keepdims=True))
                alpha = jnp.exp(m - m_new)
                p = jnp.exp(s - m_new)
                l = alpha * l + jnp.sum(p, -1, keepdims=True)
                acc = alpha * acc + jnp.dot(p.astype(BF16), v_ref[pl.ds(j0, w), 128 * a:128 * (a + 1)],
                                            preferred_element_type=F32)
                out.append((m_new, l, acc))
            return tuple(out)

        one = (jnp.full((tq, 1), -jnp.inf, F32), jnp.zeros((tq, 1), F32), jnp.zeros((tq, 128), F32))
        carry = lax.fori_loop(0, jnp.right_shift(qi, 1), lambda t, c: step(pl.multiple_of(t * 2 * tq, tq), 2 * tq, c, False), (one,) * HB)
        carry = lax.cond(jnp.bitwise_and(qi, 1) == 1, lambda c: step(pl.multiple_of((qi - 1) * tq, tq), tq, c, False), lambda c: c, carry)
        carry = step(pl.multiple_of(qi * tq, tq), tq, carry, True)
        for a in range(HB):
            m, l, acc = carry[a]
            o = acc / l
            o_ref[:, 128 * a:128 * (a + 1)] = o
            ob_ref[:, 128 * a:128 * (a + 1)] = o.astype(BF16)
            lse_ref[a] = m + jnp.log(l)

    outs, couts = _hosted_call(
        body, name=name, grid=(H // HB, T // tq),
        in_specs=[pl.BlockSpec((tq, 256 * HB), lambda h, i: (i, h)), pl.BlockSpec((T, 256 * HB), lambda h, i: (0, h)),
                  pl.BlockSpec((T, 128 * HB), lambda h, i: (0, h))],
        out_specs=[pl.BlockSpec((tq, 128 * HB), lambda h, i: (i, h)), pl.BlockSpec((tq, 128 * HB), lambda h, i: (i, h)),
                   pl.BlockSpec((HB, tq, 1), lambda h, i: (h, i, 0))],
        out_shape=[jax.ShapeDtypeStruct((T, H * 128), F32), jax.ShapeDtypeStruct((T, H * 128), BF16),
                   jax.ShapeDtypeStruct((H, T, 1), F32)],
        scratch_shapes=[], semantics=("parallel", "parallel"), args=(q, k, v), comm=comm)
    return outs if comm is None else (outs, couts)


def attn_delta(do, o, dm, name):
    T, H, tr = dm.T, dm.H, dm.tr

    def body(do_ref, o_ref, delta_ref, dob_ref):
        for a in range(H):
            d = do_ref[:, 128 * a:128 * (a + 1)]
            delta_ref[a] = jnp.sum(d * o_ref[:, 128 * a:128 * (a + 1)], -1, keepdims=True)
            dob_ref[:, 128 * a:128 * (a + 1)] = d.astype(BF16)

    blk = pl.BlockSpec((tr, 128 * H), lambda i: (i, 0))
    return pl.pallas_call(
        body, name=name, grid=(T // tr,),
        in_specs=[blk, blk],
        out_specs=[pl.BlockSpec((H, tr, 1), lambda i: (0, i, 0)), blk],
        out_shape=[jax.ShapeDtypeStruct((H, T, 1), F32), jax.ShapeDtypeStruct((T, H * 128), BF16)],
        compiler_params=_params(("parallel",)),
    )(do, o)


def attn_bwd_dq(q, k, v, do, lse, delta, dm, name, comm=None):
    T, H, tq = dm.T, dm.H, dm.tq
    scale = dm.QKH ** -0.5

    def body(q_ref, k_ref, v_ref, do_ref, lse_ref, delta_ref, dq_ref):
        qi = pl.program_id(1)

        def step(j0, w, dqs, masked):
            hk = lambda a: slice(256 * a, 256 * (a + 1))
            hv = lambda a: slice(128 * a, 128 * (a + 1))
            ss = [lax.dot_general(q_ref[:, hk(a)], k_ref[pl.ds(j0, w), hk(a)], _NT, preferred_element_type=F32)
                  for a in range(HB)]
            dps = [lax.dot_general(do_ref[:, hv(a)], v_ref[pl.ds(j0, w), hv(a)], _NT, preferred_element_type=F32)
                   for a in range(HB)]
            out = []
            for a in range(HB):
                p = jnp.exp(ss[a] * scale - lse_ref[a])
                if masked:
                    p = jnp.where(_causal_mask(tq), p, 0.0)
                ds = p * (dps[a] - delta_ref[a]) * scale
                out.append(dqs[a] + jnp.dot(ds.astype(BF16), k_ref[pl.ds(j0, w), hk(a)], preferred_element_type=F32))
            return tuple(out)

        dqs = lax.fori_loop(0, jnp.right_shift(qi, 1), lambda t, c: step(pl.multiple_of(t * 2 * tq, tq), 2 * tq, c, False),
                            (jnp.zeros((tq, 256), F32),) * HB)
        dqs = lax.cond(jnp.bitwise_and(qi, 1) == 1, lambda c: step(pl.multiple_of((qi - 1) * tq, tq), tq, c, False),
                       lambda c: c, dqs)
        dqs = step(pl.multiple_of(qi * tq, tq), tq, dqs, True)
        for a in range(HB):
            dq_ref[:, 256 * a:256 * (a + 1)] = dqs[a]

    stat = pl.BlockSpec((HB, tq, 1), lambda h, i: (h, i, 0))
    outs, couts = _hosted_call(
        body, name=name, grid=(H // HB, T // tq),
        in_specs=[pl.BlockSpec((tq, 256 * HB), lambda h, i: (i, h)), pl.BlockSpec((T, 256 * HB), lambda h, i: (0, h)),
                  pl.BlockSpec((T, 128 * HB), lambda h, i: (0, h)), pl.BlockSpec((tq, 128 * HB), lambda h, i: (i, h)),
                  stat, stat],
        out_specs=[pl.BlockSpec((tq, 256 * HB), lambda h, i: (i, h))],
        out_shape=[jax.ShapeDtypeStruct((T, H * 256), F32)],
        scratch_shapes=[], semantics=("parallel", "parallel"), args=(q, k, v, do, lse, delta), comm=comm)
    return outs[0] if comm is None else (outs[0], couts)


def attn_bwd_dkv(q, k, v, do, lse_rows, delta_rows, dm, name, comm=None):
    T, H, tq = dm.T, dm.H, dm.tq
    nq = T // tq
    scale = dm.QKH ** -0.5

    def body(q_ref, k_ref, v_ref, do_ref, lse_ref, delta_ref, dk_ref, dv_ref):
        kj = pl.program_id(1)

        def step(i, n, carry, masked):
            i0 = pl.multiple_of(i * tq, tq)
            w = n * tq
            hk = lambda a: slice(256 * a, 256 * (a + 1))
            hv = lambda a: slice(128 * a, 128 * (a + 1))
            row = lambda ref, a: ref[a, i] if n == 1 else jnp.concatenate([ref[a, i], ref[a, i + 1]], axis=1)
            sts = [lax.dot_general(k_ref[:, hk(a)], q_ref[pl.ds(i0, w), hk(a)], _NT, preferred_element_type=F32)
                   for a in range(HB)]
            dpts = [lax.dot_general(v_ref[:, hv(a)], do_ref[pl.ds(i0, w), hv(a)], _NT, preferred_element_type=F32)
                    for a in range(HB)]
            out = []
            for a in range(HB):
                dk, dv = carry[a]
                pt = jnp.exp(sts[a] * scale - row(lse_ref, a))
                if masked:
                    pt = jnp.where(_causal_mask_t(tq), pt, 0.0)
                dv = dv + jnp.dot(pt.astype(BF16), do_ref[pl.ds(i0, w), hv(a)], preferred_element_type=F32)
                dst = pt * (dpts[a] - row(delta_ref, a)) * scale
                dk = dk + jnp.dot(dst.astype(BF16), q_ref[pl.ds(i0, w), hk(a)], preferred_element_type=F32)
                out.append((dk, dv))
            return tuple(out)

        carry = step(kj, 1, ((jnp.zeros((tq, 256), F32), jnp.zeros((tq, 128), F32)),) * HB, True)
        rest = nq - 1 - kj
        carry = lax.fori_loop(0, jnp.right_shift(rest, 1), lambda t, c: step(kj + 1 + 2 * t, 2, c, False), carry)
        carry = lax.cond(jnp.bitwise_and(rest, 1) == 1, lambda c: step(nq - 1, 1, c, False), lambda c: c, carry)
        for a in range(HB):
            dk_ref[:, 256 * a:256 * (a + 1)] = carry[a][0]
            dv_ref[:, 128 * a:128 * (a + 1)] = carry[a][1]

    rows = pl.BlockSpec((HB, nq, 1, tq), lambda h, j: (h, 0, 0, 0))
    outs, couts = _hosted_call(
        body, name=name, grid=(H // HB, nq),
        in_specs=[pl.BlockSpec((T, 256 * HB), lambda h, j: (0, h)), pl.BlockSpec((tq, 256 * HB), lambda h, j: (j, h)),
                  pl.BlockSpec((tq, 128 * HB), lambda h, j: (j, h)), pl.BlockSpec((T, 128 * HB), lambda h, j: (0, h)),
                  rows, rows],
        out_specs=[pl.BlockSpec((tq, 256 * HB), lambda h, j: (j, h)), pl.BlockSpec((tq, 128 * HB), lambda h, j: (j, h))],
        out_shape=[jax.ShapeDtypeStruct((T, H * 256), F32), jax.ShapeDtypeStruct((T, H * 128), F32)],
        scratch_shapes=[], semantics=("parallel", "parallel"), args=(q, k, v, do, lse_rows, delta_rows), comm=comm)
    return outs if comm is None else (outs, couts)


def _window_sum(win, g, shift):
    s1 = win + shift(win, 1)
    s2 = s1 + shift(s1, 2)
    s3 = s2 + shift(s2, 4)
    s4 = s3 + shift(s3, 8)
    return jnp.where(g == 0, s1, jnp.where(g == 1, s2, jnp.where(g == 2, s3, s4)))


def _count(r0, R, g, T_unused=None):
    t = r0 + lax.broadcasted_iota(jnp.int32, (R, 1), 0)
    return jnp.minimum(t + 1, jnp.left_shift(2, g)).astype(F32)


def pool_fwd(proj, pw, ps, dm, name):
    T, PG, R = dm.T, dm.PG, dm.tr

    def body(x_ref, pw_ref, ps_ref, pooled_ref, mixed_ref, yc_ref, buf):
        g = pl.program_id(0)
        _fill_halo_buf(buf, lambda r0: x_ref[pl.ds(r0, R), :], T, R, PG)

        def chunk(r, c):
            r0 = pl.multiple_of(r * R, 8)
            win = buf[pl.ds(r0, R + HALO), :]
            ws = _window_sum(win, g, _back)[HALO:, :]
            pooled = (ws / _count(r0, R, g) - win[HALO:, :]).astype(BF16)
            pooled_ref[pl.ds(r0, R), :] = pooled
            mixed = jnp.dot(pooled, pw_ref[...], preferred_element_type=F32)
            mixed_ref[pl.ds(r0, R), :] = mixed
            yc_ref[pl.ds(r0, R), :] = (mixed * ps_ref[...]).astype(BF16)
            return c

        lax.fori_loop(0, T // R, chunk, 0)

    own = pl.BlockSpec((T, PG), lambda g: (0, g))
    return pl.pallas_call(
        body, name=name, grid=(4,),
        in_specs=[pl.BlockSpec((T, PG), lambda g: (0, dm.o_pool // PG + g)), pl.BlockSpec((None, PG, PG), lambda g: (g, 0, 0)),
                  pl.BlockSpec((1, PG), lambda g: (0, g))],
        out_specs=[own, own, own],
        out_shape=[jax.ShapeDtypeStruct((T, dm.DP), BF16), jax.ShapeDtypeStruct((T, dm.DP), F32),
                   jax.ShapeDtypeStruct((T, dm.DP), BF16)],
        scratch_shapes=[pltpu.VMEM((T + 2 * HALO, PG), F32)],
        compiler_params=_params(("parallel",)),
    )(proj, pw, ps)


def pool_bwd(dyc, mixed, pooled, pw, ps, dm, name):
    T, PG, R = dm.T, dm.PG, dm.tr
    _TN = (((0,), (0,)), ((), ()))

    def body(dyc_ref, mixed_ref, pooled_ref, pw_ref, ps_ref, dx_ref, dpw_ref, dps_ref, qbuf, dpbuf):
        g = pl.program_id(0)
        zeros = jnp.zeros((HALO, PG), F32)
        qbuf[pl.ds(0, HALO), :] = zeros
        qbuf[pl.ds(HALO + T, HALO), :] = zeros
        dpw_ref[...] = jnp.zeros_like(dpw_ref)

        def first(r, dps):
            r0 = pl.multiple_of(r * R, 8)
            dyc = dyc_ref[pl.ds(r0, R), :]
            dps = dps + jnp.sum(dyc * mixed_ref[pl.ds(r0, R), :], axis=0, keepdims=True)
            dmb = (dyc * ps_ref[...]).astype(BF16)
            dpw_ref[...] += lax.dot_general(pooled_ref[pl.ds(r0, R), :], dmb, _TN, preferred_element_type=F32)
            dp = lax.dot_general(dmb, pw_ref[...], _NT, preferred_element_type=F32)
            dpbuf[pl.ds(r0, R), :] = dp
            qbuf[pl.ds(r0 + HALO, R), :] = dp / _count(r0, R, g)
            return dps

        dps_ref[...] = lax.fori_loop(0, T // R, first, jnp.zeros((1, PG), F32))

        def second(r, c):
            r0 = pl.multiple_of(r * R, 8)
            win = qbuf[pl.ds(r0 + HALO, R + HALO), :]
            ws = _window_sum(win, g, _fwd)[:R, :]
            dx_ref[pl.ds(r0, R), :] = (ws - dpbuf[pl.ds(r0, R), :]).astype(BF16)
            return c

        lax.fori_loop(0, T // R, second, 0)

    own = pl.BlockSpec((T, PG), lambda g: (0, g))
    return pl.pallas_call(
        body, name=name, grid=(4,),
        in_specs=[own, own, own, pl.BlockSpec((None, PG, PG), lambda g: (g, 0, 0)), pl.BlockSpec((1, PG), lambda g: (0, g))],
        out_specs=[own, pl.BlockSpec((None, PG, PG), lambda g: (g, 0, 0)), pl.BlockSpec((1, PG), lambda g: (0, g))],
        out_shape=[jax.ShapeDtypeStruct((T, dm.DP), BF16), jax.ShapeDtypeStruct((4, PG, PG), F32),
                   jax.ShapeDtypeStruct((1, dm.DP), F32)],
        scratch_shapes=[pltpu.VMEM((T + 2 * HALO, PG), F32), pltpu.VMEM((T, PG), F32)],
        compiler_params=_params(("parallel",)),
    )(dyc, mixed, pooled, pw, ps)


def _sigmoid(x):
    return 1.0 / (1.0 + jnp.exp(-x))


def merge_fwd(proj, A, B, C, dm, name):
    T, D, tr, tc = dm.T, dm.D, dm.tr, 512
    nc = D // tc

    def body(g0, g1, g2, a, b, c, out):
        out[...] = (_sigmoid(g0[...]) * a[...] + _sigmoid(g1[...]) * b[...] + _sigmoid(g2[...]) * c[...]).astype(BF16)

    gate = lambda k: pl.BlockSpec((tr, tc), lambda i, j: (i, k * nc + j))
    own = pl.BlockSpec((tr, tc), lambda i, j: (i, j))
    return pl.pallas_call(
        body, name=name, grid=(T // tr, nc),
        in_specs=[gate(0), gate(1), gate(2), own, own, own],
        out_specs=own,
        out_shape=jax.ShapeDtypeStruct((T, D), BF16),
        compiler_params=_params(("parallel", "parallel")),
    )(proj, proj, proj, A, B, C)


def merge_bwd(proj, A, B, C, dmerged, dm, name):
    T, D, tr, tc = dm.T, dm.D, dm.tr, 512
    nc = D // tc

    def body(g0, g1, g2, a, b, c, dmr, da, db, dc, dl0, dl1, dl2):
        d = dmr[...]
        for g_ref, y_ref, dy_ref, dl_ref in ((g0, a, da, dl0), (g1, b, db, dl1), (g2, c, dc, dl2)):
            s = _sigmoid(g_ref[...])
            dy_ref[...] = (d * s).astype(BF16)
            dl_ref[...] = (d * y_ref[...] * s * (1.0 - s)).astype(BF16)

    gate = lambda k: pl.BlockSpec((tr, tc), lambda i, j: (i, k * nc + j))
    own = pl.BlockSpec((tr, tc), lambda i, j: (i, j))
    o = jax.ShapeDtypeStruct((T, D), BF16)
    return pl.pallas_call(
        body, name=name, grid=(T // tr, nc),
        in_specs=[gate(0), gate(1), gate(2), own, own, own, own],
        out_specs=[own] * 6,
        out_shape=[o] * 6,
        compiler_params=_params(("parallel", "parallel")),
    )(proj, proj, proj, A, B, C, dmerged)


def loss_head(y, target, dm, name):
    T, D, tr = dm.T, dm.D, dm.tr

    def body(y_ref, t_ref, dy_ref, dyb_ref, loss_ref):
        i = pl.program_id(0)
        t = _row_ids(i, tr)
        real = (t >= dm.n_meta) & (t < dm.T_real)
        err = jnp.where(real, y_ref[...] - t_ref[...], 0.0)
        dy = err * (1.0 / D)
        dy_ref[...] = dy
        dyb_ref[...] = dy.astype(BF16)

        @pl.when(i == 0)
        def _():
            loss_ref[...] = jnp.zeros_like(loss_ref)

        loss_ref[...] += 0.5 * jnp.sum(jnp.sum(err * err, axis=-1, keepdims=True) * (1.0 / D))

    row = pl.BlockSpec((tr, D), lambda i: (i, 0))
    return pl.pallas_call(
        body, name=name, grid=(T // tr,),
        in_specs=[row, row],
        out_specs=[row, row, pl.BlockSpec((8, LANES), lambda i: (0, 0))],
        out_shape=[jax.ShapeDtypeStruct((T, D), F32), jax.ShapeDtypeStruct((T, D), BF16),
                   jax.ShapeDtypeStruct((8, LANES), F32)],
        compiler_params=_params(("arbitrary",)),
    )(y, target)


def adamw(w, m, v, parts, name):
    R, C = w.shape
    P = parts.shape[0]
    br = R
    for cand in (512, 256, 128, 64, 32, 16, 8):
        if R % cand == 0 and cand * C * 4 <= (1 << 20):
            br = cand
            break
    if R * C * 4 <= (1 << 20):
        br = R

    def body(w_ref, m_ref, v_ref, p_ref, g_ref, d_ref, nm_ref, nv_ref):
        g = p_ref[0].astype(F32)
        for k in range(1, P):
            g = g + p_ref[k].astype(F32)
        mm = ADAM_B1 * m_ref[...] + (1.0 - ADAM_B1) * g
        vv = ADAM_B2 * v_ref[...] + (1.0 - ADAM_B2) * (g * g)
        m_hat = mm / (1.0 - ADAM_B1 ** ADAM_STEP)
        v_hat = vv / (1.0 - ADAM_B2 ** ADAM_STEP)
        g_ref[...] = g
        d_ref[...] = -ADAM_LR * (m_hat / (jnp.sqrt(v_hat) + ADAM_EPS) + ADAM_WD * w_ref[...])
        nm_ref[...] = mm
        nv_ref[...] = vv

    blk = pl.BlockSpec((br, C), lambda i: (i, 0))
    o = jax.ShapeDtypeStruct((R, C), F32)
    return pl.pallas_call(
        body, name=name, grid=(R // br,),
        in_specs=[blk, blk, blk, pl.BlockSpec((P, br, C), lambda i: (0, i, 0))],
        out_specs=[blk] * 4,
        out_shape=[o] * 4,
        compiler_params=_params(("parallel",)),
    )(w, m, v, parts)


def sum_parts(parts, name):
    P, R, C = parts.shape

    def body(p_ref, o_ref):
        acc = p_ref[0]
        for k in range(1, P):
            acc = acc + p_ref[k]
        o_ref[...] = acc

    return pl.pallas_call(
        body, name=name, grid=(1,),
        in_specs=[pl.BlockSpec((P, R, C), lambda i: (0, 0, 0))],
        out_specs=pl.BlockSpec((R, C), lambda i: (0, 0)),
        out_shape=jax.ShapeDtypeStruct((R, C), F32),
        compiler_params=_params(("arbitrary",)),
    )(parts)


def add_sibling(parts, got, core, name):
    _, _, R, C = parts.shape
    br = _pick(R, (1024, 512, 256, 128, 64, 32, 16))

    def body(c_ref, a_ref, b_ref, o_ref):
        o_ref[...] = (a_ref[...].astype(F32) + b_ref[...].astype(F32)).astype(BF16)

    blk = pl.BlockSpec((None, br, C), lambda ch, i, c: (ch, i, 0))
    return pl.pallas_call(
        body, name=name,
        grid_spec=pltpu.PrefetchScalarGridSpec(
            num_scalar_prefetch=1, grid=(4, R // br),
            in_specs=[pl.BlockSpec((None, None, br, C), lambda ch, i, c: (ch, c[0], i, 0)), blk],
            out_specs=blk),
        out_shape=jax.ShapeDtypeStruct((4, R, C), BF16), compiler_params=_params(("parallel", "parallel")),
    )(core, parts, got)


def all_gather(arrs, name):
    n = len(arrs)

    def body(*refs):
        ins, outs = refs[:n], refs[n:2 * n]
        send_sems, recv_sems, local_sems = refs[2 * n:]
        x, y, c = _place()
        me, sibling = (x, y, c), (x, y, 1 - c)
        chips = [(1 - x, y), (x, 1 - y), (1 - x, 1 - y)]

        def copy(a, k, block, to, src=None):
            px, py, pc = block
            dst = outs[a].at[4 * px + 2 * py + pc]
            return pltpu.make_async_remote_copy(
                src_ref=dst if src is None else src, dst_ref=dst,
                send_sem=send_sems.at[7 * a + k], recv_sem=recv_sems.at[7 * a + k],
                device_id=to, device_id_type=_MESH)

        started = []
        for a in range(n):
            mine = pltpu.make_async_copy(ins[a], outs[a].at[4 * x + 2 * y + c], local_sems.at[a])
            mine.start()
            started.append(mine)
        sends = []
        for a in range(n):
            sends.append(copy(a, 0, me, sibling, src=ins[a]))
            for j, chip in enumerate(chips):
                sends.append(copy(a, 1 + j, me, (*chip, c), src=ins[a]))
        for cp in sends:
            cp.start()
        for j, chip in enumerate(chips):
            for a in range(n):
                copy(a, 1 + j, (*chip, c), me).wait_recv()
                fwd = copy(a, 4 + j, (*chip, c), sibling)
                fwd.start()
                sends.append(fwd)
        for a in range(n):
            copy(a, 0, sibling, me).wait_recv()
            for j, chip in enumerate(chips):
                copy(a, 4 + j, (*chip, 1 - c), me).wait_recv()
        for cp in sends:
            cp.wait_send()
        for cp in started:
            cp.wait()

    outs = pl.pallas_call(
        body, name=name,
        in_specs=[_HBM] * n, out_specs=[_HBM] * n,
        out_shape=[jax.ShapeDtypeStruct((8,) + a.shape, a.dtype) for a in arrs],
        scratch_shapes=[pltpu.SemaphoreType.DMA((7 * n,)), pltpu.SemaphoreType.DMA((7 * n,)), pltpu.SemaphoreType.DMA((n,))],
    )(*arrs)
    return list(outs)


def sibling_exchange(arrs, name):
    n = len(arrs)

    def body(*refs):
        ins, got = refs[:n], refs[n:2 * n]
        send_sems, recv_sems = refs[2 * n:]
        x, y, c = _place()
        work = []
        for a in range(n):
            for ch in range(4):
                cp = pltpu.make_async_remote_copy(
                    src_ref=ins[a].at[ch, 1 - c], dst_ref=got[a].at[ch],
                    send_sem=send_sems.at[4 * a + ch], recv_sem=recv_sems.at[4 * a + ch],
                    device_id=(x, y, 1 - c), device_id_type=_MESH)
                cp.start()
                work.append(cp)
        for cp in work:
            cp.wait()

    outs = pl.pallas_call(
        body, name=name,
        in_specs=[_HBM] * n, out_specs=[_HBM] * n,
        out_shape=[jax.ShapeDtypeStruct((4,) + a.shape[2:], a.dtype) for a in arrs],
        scratch_shapes=[pltpu.SemaphoreType.DMA((4 * n,)), pltpu.SemaphoreType.DMA((4 * n,))],
    )(*arrs)
    return list(outs)


def _remote(src, dst, send_sem, recv_sem, to):
    return pltpu.make_async_remote_copy(src_ref=src, dst_ref=dst, send_sem=send_sem, recv_sem=recv_sem,
                                        device_id=to, device_id_type=_MESH)


def gather_ici(blocks):
    n = len(blocks)

    def copies(cins, couts, sems):
        send_sems, recv_sems, local_sems = sems
        x, y, c = _place()
        mine = 4 * x + 2 * y + c
        local, sends, recvs = [], [], []
        for a in range(n):
            local.append(pltpu.make_async_copy(cins[a], couts[a].at[mine], local_sems.at[a]))
            for j, (px, py) in enumerate([(1 - x, y), (x, 1 - y), (1 - x, 1 - y)]):
                k = 3 * a + j
                sends.append(_remote(cins[a], couts[a].at[mine], send_sems.at[k], recv_sems.at[k], (px, py, c)))
                recvs.append(_remote(cins[a], couts[a].at[4 * px + 2 * py + c], send_sems.at[k], recv_sems.at[k], (px, py, c)))
        return local, sends, recvs

    def start(cins, couts, sems):
        local, sends, _ = copies(cins, couts, sems)
        for cp in local + sends:
            cp.start()

    def finish(cins, couts, sems):
        local, sends, recvs = copies(cins, couts, sems)
        for cp in sends:
            cp.wait_send()
        for cp in recvs:
            cp.wait_recv()
        for cp in local:
            cp.wait()

    return Hosted(list(blocks), [jax.ShapeDtypeStruct((8,) + b.shape, b.dtype) for b in blocks],
                  [pltpu.SemaphoreType.DMA((3 * n,)), pltpu.SemaphoreType.DMA((3 * n,)), pltpu.SemaphoreType.DMA((n,))],
                  start, finish)


def fill_sibling(stks, name):
    n = len(stks)

    def body(*refs):
        ins, outs = refs[:n], refs[n:2 * n]
        send_sems, recv_sems = refs[2 * n:]
        x, y, c = _place()
        sends, recvs = [], []
        for a in range(n):
            for ch in range(4):
                k = 4 * a + ch
                sends.append(_remote(ins[a].at[2 * ch + c], outs[a].at[2 * ch + c], send_sems.at[k], recv_sems.at[k], (x, y, 1 - c)))
                recvs.append(_remote(ins[a].at[2 * ch + c], outs[a].at[2 * ch + 1 - c], send_sems.at[k], recv_sems.at[k], (x, y, 1 - c)))
        for cp in sends:
            cp.start()
        for cp in sends:
            cp.wait_send()
        for cp in recvs:
            cp.wait_recv()

    outs = pl.pallas_call(
        body, name=name,
        in_specs=[_HBM] * n, out_specs=[_HBM] * n,
        out_shape=[jax.ShapeDtypeStruct(s.shape, s.dtype) for s in stks],
        scratch_shapes=[pltpu.SemaphoreType.DMA((4 * n,)), pltpu.SemaphoreType.DMA((4 * n,))],
        input_output_aliases={a: a for a in range(n)},
    )(*stks)
    return list(outs)


def reduce_ici(arrs):
    n = len(arrs)

    def copies(cins, couts, sems):
        send_sems, recv_sems, local_sems = sems
        x, y, c = _place()
        my_chip = 2 * x + y
        local, sends, recvs = [], [], []
        for a in range(n):
            local.append(pltpu.make_async_copy(cins[a].at[my_chip], couts[a].at[my_chip], local_sems.at[a]))
            for j, (px, py) in enumerate([(1 - x, y), (x, 1 - y), (1 - x, 1 - y)]):
                k = 3 * a + j
                sends.append(_remote(cins[a].at[2 * px + py], couts[a].at[my_chip], send_sems.at[k], recv_sems.at[k], (px, py, c)))
                recvs.append(_remote(cins[a].at[my_chip], couts[a].at[2 * px + py], send_sems.at[k], recv_sems.at[k], (px, py, c)))
        return local, sends, recvs

    def start(cins, couts, sems):
        local, sends, _ = copies(cins, couts, sems)
        for cp in local + sends:
            cp.start()

    def finish(cins, couts, sems):
        local, sends, recvs = copies(cins, couts, sems)
        for cp in sends:
            cp.wait_send()
        for cp in recvs:
            cp.wait_recv()
        for cp in local:
            cp.wait()

    return Hosted(list(arrs), [jax.ShapeDtypeStruct(a.shape, a.dtype) for a in arrs],
                  [pltpu.SemaphoreType.DMA((3 * n,)), pltpu.SemaphoreType.DMA((3 * n,)), pltpu.SemaphoreType.DMA((n,))],
                  start, finish)


def sibling_hosted(arrs):
    n = len(arrs)

    def copies(cins, couts, sems):
        send_sems, recv_sems = sems
        x, y, c = _place()
        return [_remote(cins[a].at[ch, 1 - c], couts[a].at[ch], send_sems.at[4 * a + ch], recv_sems.at[4 * a + ch],
                        (x, y, 1 - c)) for a in range(n) for ch in range(4)]

    def start(cins, couts, sems):
        for cp in copies(cins, couts, sems):
            cp.start()

    def finish(cins, couts, sems):
        for cp in copies(cins, couts, sems):
            cp.wait()

    return Hosted(list(arrs), [jax.ShapeDtypeStruct((4,) + a.shape[2:], a.dtype) for a in arrs],
                  [pltpu.SemaphoreType.DMA((4 * n,)), pltpu.SemaphoreType.DMA((4 * n,))], start, finish)


def merge_hosted(first, second):
    ni, no, ns = len(first.ins), len(first.out_shapes), len(first.sems)

    def start(cins, couts, sems):
        first.start(cins[:ni], couts[:no], sems[:ns])
        second.start(cins[ni:], couts[no:], sems[ns:])

    def finish(cins, couts, sems):
        first.finish(cins[:ni], couts[:no], sems[:ns])
        second.finish(cins[ni:], couts[no:], sems[ns:])

    both = Hosted(first.ins + second.ins, first.out_shapes + second.out_shapes, first.sems + second.sems, start, finish)
    return both, lambda couts: (couts[:no], couts[no:])


COL_SHARDED = ("w_in", "w_uq", "w_ukv", "w_branch_a", "w_branch_c", "w_up")
ROW_SHARDED = ("w_branch_b", "w_o", "w_down")
BIG = COL_SHARDED + ROW_SHARDED


def _full_from_stacked(name, st):
    if name in COL_SHARDED:
        _, L, K, n = st.shape
        return st.transpose(1, 2, 0, 3).reshape(L, K, 8 * n)
    if name in ROW_SHARDED:
        _, L, k, N = st.shape
        return st.transpose(1, 0, 2, 3).reshape(L, 8 * k, N)
    if name == "pool_w":
        _, L, G, pk, PG = st.shape
        return st.transpose(1, 2, 0, 3, 4).reshape(L, G, 8 * pk, PG)
    if name == "meta_tokens":
        _, M, n = st.shape
        return st.transpose(1, 0, 2).reshape(M, 8 * n)
    if name == "conv_w":
        _, L, W, n = st.shape
        return st.transpose(1, 2, 0, 3).reshape(L, W, 8 * n)
    raise ValueError(name)


def _shards_from_full(name, g):
    if name in COL_SHARDED:
        L, K, N = g.shape
        s = g.reshape(L, K, 8, N // 8).transpose(2, 0, 1, 3)
    else:
        L, K, N = g.shape
        s = g.reshape(L, 8, K // 8, N).transpose(1, 0, 2, 3)
    return s.reshape((4, 2) + s.shape[1:])


def _w_in_to_padded(w, dm):
    o3 = 3 * dm.DC + dm.QL + dm.KL
    o4 = o3 + dm.ROPE
    o5 = o4 + dm.DP
    pad = jnp.zeros(w.shape[:-1] + (256 - dm.ROPE,), w.dtype)
    return jnp.concatenate([w[..., o5:], w[..., :o3], w[..., o4:o5], w[..., o3:o4], pad], axis=-1)


def _w_in_from_padded(g, dm):
    o3 = 3 * dm.DC + dm.QL + dm.KL
    a = 3 * dm.D
    return jnp.concatenate([g[..., a:a + o3], g[..., dm.o_rope:dm.o_rope + dm.ROPE], g[..., dm.o_pool:dm.o_pool + dm.DP],
                            g[..., :a]], axis=-1)


def _pad_heads(w, dm):
    w = w.reshape(w.shape[:-1] + (dm.H, dm.QKH))
    w = jnp.pad(w, [(0, 0)] * (w.ndim - 1) + [(0, dm.HP - dm.QKH)])
    return w.reshape(w.shape[:-2] + (dm.H * dm.HP,))


def _unpad_heads(g, dm):
    g = g.reshape(g.shape[:-1] + (dm.H, dm.HP))[..., :dm.QKH]
    return g.reshape(g.shape[:-2] + (dm.H * dm.QKH,))


class _Weights:
    def __init__(self, w, dm, plan):
        self.w, self.dm, self.plan, self.full = w, dm, plan, {}

    def blocks(self, items):
        return [self.w[n][l:l + 1].astype(BF16) for n, l in items]

    def put(self, items, stacked):
        for (n, l), st in zip(items, stacked):
            f = _full_from_stacked(n, st)[0]
            if n == "w_in":
                f = _w_in_to_padded(f, self.dm)
            if n == "w_uq":
                f = _pad_heads(f, self.dm)
            self.full[(n, l)] = f

    def comm(self, tag):
        items = self.plan.get(tag)
        return gather_ici(self.blocks(items)) if items else None

    def arrived(self, tag, couts):
        self.put(self.plan[tag], fill_sibling(couts, f"fill_{tag}"))

    def __call__(self, n, l):
        return self.full[(n, l)]


class _Reducer:
    def __init__(self, dm, core):
        self.dm, self.core, self.q = dm, core, {}

    def parts(self, items, g):
        parts = []
        for n, l in items:
            f = g[n]
            if n == "w_in":
                f = _w_in_from_padded(f, self.dm)
            if n == "w_uq":
                f = _unpad_heads(f, self.dm)
            parts.append(_shards_from_full(n, f[None]))
        return parts

    def add(self, items, parts, got):
        out = []
        for (n, l), a, b in zip(items, parts, got):
            C = a.shape[-1]
            out.append(add_sibling(a.reshape(4, 2, -1, C), b.reshape(4, -1, C), self.core,
                                   f"reduce_add_{n}_{l}").reshape(b.shape))
        return out

    def prepare(self, tag, items, g):
        parts = self.parts(items, g)
        return self.add(items, parts, sibling_exchange(parts, f"reduce_sibling_{tag}"))

    def put(self, items, summed):
        for key, q in zip(items, summed):
            self.q[key] = q


def _layer_fwd(xin, l, ws, G, tabs, dm):
    nm = lambda s: f"l{l}_{s}"
    D = dm.D

    def mm(tag, *args, **kw):
        comm = ws.comm(nm(tag))
        if comm is None:
            return matmul(*args, nm(tag), **kw)
        res, couts = matmul(*args, nm(tag), comm=comm, **kw)
        ws.arrived(nm(tag), couts)
        return res

    h = rms_fwd(xin, 0, D, G["attn_norm"], dm, nm("rms1"))
    proj = mm("proj", h, ws("w_in", l), "nn", (F32,))
    ya = mixer_a_fwd(proj, G["conv_w"], dm, nm("mixa"))
    ql = rms_fwd(proj, dm.o_ql // dm.QL, dm.QL, G["q_lat_norm"], dm, nm("rms_q"))
    kl = rms_fwd(proj, dm.o_kl // dm.KL, dm.KL, G["kv_lat_norm"], dm, nm("rms_kv"))
    q0 = mm("uq", ql, ws("w_uq", l), "nn", (F32,))
    kv0 = mm("ukv", kl, ws("w_ukv", l), "nn", (F32,))
    comm = ws.comm(nm("qkprep"))
    if comm is None:
        q_s, k_s, v_s = qk_prep_fwd(q0, kv0, proj, G["q_norm"], G["k_norm"], tabs[0], tabs[1], dm, nm("qkprep"))
    else:
        (q_s, k_s, v_s), couts = qk_prep_fwd(q0, kv0, proj, G["q_norm"], G["k_norm"], tabs[0], tabs[1], dm, nm("qkprep"),
                                            comm=comm)
        ws.arrived(nm("qkprep"), couts)
    comm = ws.comm(nm("attn"))
    if comm is None:
        o, ob, lse = attn_fwd(q_s, k_s, v_s, dm, nm("attn"))
    else:
        (o, ob, lse), couts = attn_fwd(q_s, k_s, v_s, dm, nm("attn"), comm=comm)
        ws.arrived(nm("attn"), couts)
    pooled, mixed, yc = pool_fwd(proj, ws("pool_w", l), G["pool_scale"], dm, nm("pool"))
    A = mm("br_a", ya, ws("w_branch_a", l), "nn", (F32,))
    B = mm("br_b", ob, ws("w_branch_b", l), "nn", (F32,))
    C = mm("br_c", yc, ws("w_branch_c", l), "nn", (F32,))
    merged = merge_fwd(proj, A, B, C, dm, nm("merge"))
    x1 = mm("wo", merged, ws("w_o", l), "nn", (F32,), extras=(xin,), epi=lambda acc, r: (acc + r,))
    h2 = rms_fwd(x1, 0, D, G["mlp_norm"], dm, nm("rms2"))
    up, act = mm("up", h2, ws("w_up", l), "nn", (F32, BF16), epi=lambda acc: (acc, jnp.square(jnp.maximum(acc, 0.0))))
    x2 = mm("down", act, ws("w_down", l), "nn", (F32,), extras=(x1,), epi=lambda acc, r: (acc + r,))
    saved = dict(xin=xin, h=h, proj=proj, ya=ya, ql=ql, kl=kl, q0=q0, kv0=kv0, q_s=q_s, k_s=k_s, v_s=v_s, o=o, ob=ob,
                 lse=lse, pooled=pooled, mixed=mixed, yc=yc, A=A, B=B, C=C, merged=merged, x1=x1, h2=h2, up=up, act=act)
    return x2, saved


def _layer_bwd(dx2, dx2b, S, l, ws, G, tabs, dm, pre_attn, late=None, first=None):
    nm = lambda s: f"l{l}_b_{s}"
    D, T = dm.D, dm.T
    g = {}
    d_up = matmul(dx2b, ws("w_down", l), "nt", (BF16,), nm("d_act"), extras=(S["up"],),
                  epi=lambda acc, up: (acc * (2.0 * jnp.maximum(up, 0.0)),), comm=first[0] if first else None)
    if first:
        d_up, couts = d_up
        first[1](couts)
    g["w_down"] = matmul(S["act"], dx2b, "tn", (BF16,), nm("g_down"))
    g["w_up"] = matmul(S["h2"], d_up, "tn", (BF16,), nm("g_up"))
    dh2 = matmul(d_up, ws("w_up", l), "nt", (F32,), nm("d_h2"))
    dx1, dx1b, g["mlp_norm"] = rms_bwd(S["x1"], 0, D, G["mlp_norm"], dh2, dx2, dm, nm("rms2"))
    dmerged = matmul(dx1b, ws("w_o", l), "nt", (F32,), nm("d_merged"))
    g["w_o"] = matmul(S["merged"], dx1b, "tn", (BF16,), nm("g_o"))
    dA, dB, dC, dl0, dl1, dl2 = merge_bwd(S["proj"], S["A"], S["B"], S["C"], dmerged, dm, nm("merge"))
    dya = matmul(dA, ws("w_branch_a", l), "nt", (F32,), nm("d_ya"))
    g["w_branch_a"] = matmul(S["ya"], dA, "tn", (BF16,), nm("g_a"))
    dyb = matmul(dB, ws("w_branch_b", l), "nt", (F32,), nm("d_yb"))
    g["w_branch_b"] = matmul(S["ob"], dB, "tn", (BF16,), nm("g_b"))
    dyc = matmul(dC, ws("w_branch_c", l), "nt", (F32,), nm("d_yc"))
    g["w_branch_c"] = matmul(S["yc"], dC, "tn", (BF16,), nm("g_c"))
    du, db, dc, g["conv_w"] = mixer_a_bwd(S["proj"], G["conv_w"], dya, dm, nm("mixa"))
    dpool, g["pool_w"], g["pool_scale"] = pool_bwd(dyc, S["mixed"], S["pooled"], ws("pool_w", l), G["pool_scale"], dm, nm("pool"))
    delta, dob = attn_delta(dyb, S["o"], dm, nm("delta"))
    nq = T // dm.tq
    comm_dq, after_dq = pre_attn(g)
    dq, couts = attn_bwd_dq(S["q_s"], S["k_s"], S["v_s"], dob, S["lse"], delta, dm, nm("attn_dq"), comm=comm_dq)
    comm_dkv, done_dkv = after_dq(couts)
    dkv = attn_bwd_dkv(S["q_s"], S["k_s"], S["v_s"], dob, S["lse"].reshape(dm.H, nq, 1, dm.tq),
                       delta.reshape(dm.H, nq, 1, dm.tq), dm, nm("attn_dkv"), comm=comm_dkv)
    if comm_dkv is not None:
        dkv, couts = dkv
        done_dkv(couts)
    dk, dv = dkv
    dq0, dkv0, dkr, g["q_norm"], g["k_norm"] = qk_prep_bwd(S["q0"], S["kv0"], S["proj"], G["q_norm"], G["k_norm"],
                                                            tabs[0], tabs[1], dq, dk, dv, dm, nm("qkprep"))
    dql = matmul(dq0, ws("w_uq", l), "nt", (F32,), nm("d_ql"))
    g["w_uq"] = matmul(S["ql"], dq0, "tn", (BF16,), nm("g_uq"))
    dkl = matmul(dkv0, ws("w_ukv", l), "nt", (F32,), nm("d_kl"))
    g["w_ukv"] = matmul(S["kl"], dkv0, "tn", (BF16,), nm("g_ukv"))
    _, dqlat, g["q_lat_norm"] = rms_bwd(S["proj"], dm.o_ql // dm.QL, dm.QL, G["q_lat_norm"], dql, None, dm, nm("rms_q"))
    _, dkvlat, g["kv_lat_norm"] = rms_bwd(S["proj"], dm.o_kl // dm.KL, dm.KL, G["kv_lat_norm"], dkl, None, dm, nm("rms_kv"))
    dproj = jnp.concatenate([dl0, dl1, dl2, du, db, dc, dqlat, dkvlat, dpool, dkr.astype(BF16),
                             jnp.zeros((T, 128), BF16)], axis=1)
    g["w_in"] = matmul(S["h"], dproj, "tn", (BF16,), nm("g_in"))
    comm_dh, done_dh = late(g) if late is not None else (None, None)
    dh = matmul(dproj, ws("w_in", l), "nt", (F32,), nm("d_h"), comm=comm_dh)
    if comm_dh is not None:
        dh, couts = dh
        done_dh(couts)
    dx, dxb, g["attn_norm"] = rms_bwd(S["xin"], 0, D, G["attn_norm"], dh, dx1, dm, nm("rms1"))
    return dx, dxb, g


WEIGHTS = ("meta_tokens", "attn_norm", "w_in", "conv_w", "q_lat_norm", "kv_lat_norm", "w_uq", "w_ukv", "q_norm", "k_norm",
           "pool_w", "pool_scale", "w_branch_a", "w_branch_b", "w_branch_c", "w_o", "mlp_norm", "w_up", "w_down")
SMALL = tuple(n for n in WEIGHTS if n not in BIG)


def kernel(x, meta_tokens, attn_norm, w_in, conv_w, q_lat_norm, kv_lat_norm, w_uq, w_ukv, q_norm, k_norm, pool_w, pool_scale, w_branch_a, w_branch_b, w_branch_c, w_o, mlp_norm, w_up, w_down, loss_target, m_meta_tokens, m_attn_norm, m_w_in, m_conv_w, m_q_lat_norm, m_kv_lat_norm, m_w_uq, m_w_ukv, m_q_norm, m_k_norm, m_pool_w, m_pool_scale, m_w_branch_a, m_w_branch_b, m_w_branch_c, m_w_o, m_mlp_norm, m_w_up, m_w_down, v_meta_tokens, v_attn_norm, v_w_in, v_conv_w, v_q_lat_norm, v_kv_lat_norm, v_w_uq, v_w_ukv, v_q_norm, v_k_norm, v_pool_w, v_pool_scale, v_w_branch_a, v_w_branch_b, v_w_branch_c, v_w_o, v_mlp_norm, v_w_up, v_w_down):
    w = dict(meta_tokens=meta_tokens, attn_norm=attn_norm, w_in=w_in, conv_w=conv_w, q_lat_norm=q_lat_norm,
             kv_lat_norm=kv_lat_norm, w_uq=w_uq, w_ukv=w_ukv, q_norm=q_norm, k_norm=k_norm, pool_w=pool_w,
             pool_scale=pool_scale, w_branch_a=w_branch_a, w_branch_b=w_branch_b, w_branch_c=w_branch_c, w_o=w_o,
             mlp_norm=mlp_norm, w_up=w_up, w_down=w_down)
    m = dict(meta_tokens=m_meta_tokens, attn_norm=m_attn_norm, w_in=m_w_in, conv_w=m_conv_w, q_lat_norm=m_q_lat_norm,
             kv_lat_norm=m_kv_lat_norm, w_uq=m_w_uq, w_ukv=m_w_ukv, q_norm=m_q_norm, k_norm=m_k_norm, pool_w=m_pool_w,
             pool_scale=m_pool_scale, w_branch_a=m_w_branch_a, w_branch_b=m_w_branch_b, w_branch_c=m_w_branch_c, w_o=m_w_o,
             mlp_norm=m_mlp_norm, w_up=m_w_up, w_down=m_w_down)
    v = dict(meta_tokens=v_meta_tokens, attn_norm=v_attn_norm, w_in=v_w_in, conv_w=v_conv_w, q_lat_norm=v_q_lat_norm,
             kv_lat_norm=v_kv_lat_norm, w_uq=v_w_uq, w_ukv=v_w_ukv, q_norm=v_q_norm, k_norm=v_k_norm, pool_w=v_pool_w,
             pool_scale=v_pool_scale, w_branch_a=v_w_branch_a, w_branch_b=v_w_branch_b, w_branch_c=v_w_branch_c, w_o=v_w_o,
             mlp_norm=v_mlp_norm, w_up=v_w_up, w_down=v_w_down)
    L = attn_norm.shape[0]
    assert L == 2, "the gather / reduce schedule below is written for two layers"
    seq, D = x.shape[1], x.shape[2]
    n_meta = meta_tokens.shape[0]
    dm = Dims(D, seq, n_meta)
    T = dm.T
    me = 4 * lax.axis_index("x") + 2 * lax.axis_index("y") + lax.axis_index("c")
    core = lax.axis_index("c").astype(jnp.int32).reshape(1)

    plan = {
        "l0_proj": [("w_uq", 0), ("w_ukv", 0), ("pool_w", 0), ("pool_w", 1), ("w_up", 0)],
        "l0_qkprep": [("w_branch_a", 0), ("w_branch_b", 0), ("w_branch_c", 0), ("w_o", 0)],
        "l0_attn": [("w_down", 0), ("w_in", 1)],
        "l0_up": [("w_up", 1)],
        "l0_down": [("w_down", 1)],
        "l1_proj": [("w_uq", 1), ("w_ukv", 1), ("w_branch_a", 1), ("w_branch_b", 1), ("w_branch_c", 1), ("w_o", 1)],
    }
    ws = _Weights(w, dm, plan)
    first = [("w_in", 0)]
    ws.put(first, all_gather(ws.blocks(first), "gather_first"))
    st_small = all_gather([w["meta_tokens"], w["conv_w"]], "gather_small")
    meta_full = _full_from_stacked("meta_tokens", st_small[0])
    conv_full = _full_from_stacked("conv_w", st_small[1])
    pad_gain = lambda gn: jnp.pad(gn, (0, dm.HP - dm.QKH))

    def gains(l):
        G = {n: w[n][l][None, :] for n in ("attn_norm", "q_lat_norm", "kv_lat_norm", "pool_scale", "mlp_norm")}
        G["q_norm"], G["k_norm"] = pad_gain(w["q_norm"][l])[None, :], pad_gain(w["k_norm"][l])[None, :]
        G["conv_w"] = conv_full[l]
        return G

    Gs = [gains(l) for l in range(L)]

    pos = jnp.arange(dm.T_real, dtype=F32)
    inv = 10000.0 ** (-jnp.arange(0, dm.ROPE, 2, dtype=F32) / dm.ROPE)
    ang = pos[:, None] * inv[None, :]
    zpad = jnp.zeros((dm.T_real, LANES - dm.ROPE), F32)
    rows = ((0, T - dm.T_real), (0, 0))
    tabs = (jnp.pad(jnp.concatenate([jnp.cos(ang), jnp.cos(ang), zpad], 1), rows),
            jnp.pad(jnp.concatenate([jnp.sin(ang), jnp.sin(ang), zpad], 1), rows))

    xs = jnp.concatenate([meta_full, x[0], jnp.zeros((T - dm.T_real, D), F32)], axis=0)
    target = jnp.pad(loss_target[0], ((n_meta, T - dm.T_real), (0, 0)))
    saved = []
    for l in range(L):
        xs, S = _layer_fwd(xs, l, ws, Gs[l], tabs, dm)
        saved.append(S)
    dx, dxb, loss_acc = loss_head(xs, target, dm, "loss_head")

    red = _Reducer(dm, core)
    early = lambda l: [(n, l) for n in ("w_down", "w_up", "w_o", "w_branch_a", "w_branch_b", "w_branch_c")]
    late = lambda l: [(n, l) for n in ("w_uq", "w_ukv", "w_in")]
    grads = [None] * L

    def pre_attn_1(g):
        parts = red.parts(early(1), g)

        def after(got):
            return reduce_ici(red.add(early(1), parts, got)), lambda couts: red.put(early(1), couts)

        return sibling_hosted(parts), after

    dx, dxb, grads[1] = _layer_bwd(dx, dxb, saved[1], 1, ws, Gs[1], tabs, dm, pre_attn_1)
    parts_late1 = red.parts(late(1), grads[1])
    chip_late1 = []

    def pre_attn_0(g):
        parts = red.parts(early(0), g)
        both, split = merge_hosted(reduce_ici(chip_late1[0]), sibling_hosted(parts))

        def after(couts):
            summed_late1, got = split(couts)
            red.put(late(1), summed_late1)
            return reduce_ici(red.add(early(0), parts, got)), lambda couts: red.put(early(0), couts)

        return both, after

    def late_0(g):
        return reduce_ici(red.prepare("l0", late(0), g)), lambda couts: red.put(late(0), couts)

    dx, dxb, grads[0] = _layer_bwd(
        dx, dxb, saved[0], 0, ws, Gs[0], tabs, dm, pre_attn_0, late_0,
        first=(sibling_hosted(parts_late1), lambda got: chip_late1.append(red.add(late(1), parts_late1, got))))
    grad_x = dx[n_meta:dm.T_real][None]
    summed = [jnp.concatenate([red.q[(n, l)] for l in range(L)], axis=1) for n in BIG]

    out_g, out_d, out_m, out_v = {}, {}, {}, {}

    def update(n, parts3):
        shp = w[n].shape
        C = shp[-1]
        res = adamw(w[n].reshape(-1, C), m[n].reshape(-1, C), v[n].reshape(-1, C), parts3, f"adamw_{n}")
        out_g[n], out_d[n], out_m[n], out_v[n] = [r.reshape(shp) for r in res]

    for n, q in zip(BIG, summed):
        update(n, q.reshape(4, -1, q.shape[-1]))

    small_full = {
        "meta_tokens": dx[:n_meta],
        "conv_w": jnp.stack([grads[l]["conv_w"] for l in range(L)]),
        "pool_w": jnp.stack([grads[l]["pool_w"] for l in range(L)]),
        "q_norm": jnp.stack([grads[l]["q_norm"][0, :dm.QKH] for l in range(L)]),
        "k_norm": jnp.stack([grads[l]["k_norm"][0, :dm.QKH] for l in range(L)]),
    }
    for n in ("attn_norm", "q_lat_norm", "kv_lat_norm", "pool_scale", "mlp_norm"):
        small_full[n] = jnp.stack([grads[l][n][0] for l in range(L)])
    flat = jnp.concatenate([small_full[n].reshape(-1) for n in SMALL] + [loss_acc[0, :1]])
    n_flat = flat.shape[0]
    rows_small = -(-n_flat // (8 * LANES)) * 8
    flat = jnp.pad(flat, (0, rows_small * LANES - n_flat)).reshape(rows_small, LANES)
    total = sum_parts(all_gather([flat], "gather_small_grads")[0], "sum_small").reshape(-1)
    off = 0
    for n in SMALL:
        size = math.prod(small_full[n].shape)
        gsum = total[off:off + size].reshape(small_full[n].shape)
        off += size
        if n in ("meta_tokens", "conv_w"):
            blk = w[n].shape[-1]
            gsum = lax.dynamic_slice_in_dim(gsum, me * blk, blk, axis=gsum.ndim - 1)
        elif n == "pool_w":
            blk = w[n].shape[2]
            gsum = lax.dynamic_slice_in_dim(gsum, me * blk, blk, axis=2)
        update(n, gsum.reshape(1, -1, gsum.shape[-1]))
    loss = total[off]

    return (loss, grad_x, *[out_g[n] for n in WEIGHTS], *[out_d[n] for n in WEIGHTS],
            *[out_m[n] for n in WEIGHTS], *[out_v[n] for n in WEIGHTS])
```

```python
import functools
import math

import jax
import jax.numpy as jnp
from jax import lax
from jax.experimental import pallas as pl
from jax.experimental.pallas import tpu as pltpu

F32 = jnp.float32
BF16 = jnp.bfloat16

VMEM_LIMIT_BYTES = 56 * 1024 * 1024
LANES = 128
EPS = 1e-6
HALO = 16

ADAM_LR = 0.001
ADAM_B1 = 0.9
ADAM_B2 = 0.999
ADAM_EPS = 1e-08
ADAM_WD = 0.01
ADAM_STEP = 10


def _params(sem):
    return pltpu.CompilerParams(dimension_semantics=sem, vmem_limit_bytes=VMEM_LIMIT_BYTES)


def _pick(n, prefs):
    for p in prefs:
        if p <= n and n % p == 0:
            return p
    return n


_MESH = pl.DeviceIdType.MESH
_HBM = pl.BlockSpec(memory_space=pltpu.HBM)


def _place():
    return lax.axis_index("x"), lax.axis_index("y"), lax.axis_index("c")


class Hosted:
    def __init__(self, ins, out_shapes, sems, start, finish, aliases=None):
        self.ins, self.out_shapes, self.sems, self.start, self.finish = ins, out_shapes, sems, start, finish
        self.aliases = aliases or {}


def _hosted_call(body, *, name, grid, in_specs, out_specs, out_shape, scratch_shapes, semantics, args, comm):
    n_in, n_out, n_scr = len(in_specs), len(out_specs), len(scratch_shapes)
    if comm is None:
        outs = pl.pallas_call(body, name=name, grid=grid, in_specs=in_specs, out_specs=out_specs, out_shape=out_shape,
                              scratch_shapes=scratch_shapes, compiler_params=_params(semantics))(*args)
        return list(outs), []
    ci, co = len(comm.ins), len(comm.out_shapes)

    def hosting(*refs):
        ins, cins = refs[:n_in], refs[n_in:n_in + ci]
        outs = refs[n_in + ci:n_in + ci + n_out]
        couts = refs[n_in + ci + n_out:n_in + ci + n_out + co]
        scr = refs[n_in + ci + n_out + co:n_in + ci + n_out + co + n_scr]
        csems = refs[n_in + ci + n_out + co + n_scr:]
        ids = [pl.program_id(d) for d in range(len(grid))]
        first = functools.reduce(jnp.logical_and, [i == 0 for i in ids])
        last = functools.reduce(jnp.logical_and, [i == g - 1 for i, g in zip(ids, grid)])

        @pl.when(first)
        def _():
            comm.start(cins, couts, csems)

        body(*ins, *outs, *scr)

        @pl.when(last)
        def _():
            comm.finish(cins, couts, csems)

    outs = pl.pallas_call(
        hosting, name=name, grid=grid,
        in_specs=list(in_specs) + [_HBM] * ci, out_specs=list(out_specs) + [_HBM] * co,
        out_shape=list(out_shape) + list(comm.out_shapes),
        scratch_shapes=list(scratch_shapes) + list(comm.sems),
        input_output_aliases={n_in + i: n_out + o for i, o in comm.aliases.items()},
        compiler_params=_params(("arbitrary",) * len(grid)),
    )(*args, *comm.ins)
    return list(outs[:n_out]), list(outs[n_out:])


MXU_FLOPS = 750e12
HBM_BYTES_PER_S = 3.0e12
ACC_RMW_BYTES_PER_S = 8e12
GRID_STEP_S = 0.4e-6
VMEM_COMPILER_RESERVE_BYTES = 8 * 1024 * 1024
MAX_TILE_ROWS, MAX_TILE_COLS = 2112, 2304


def _divisors(n, step):
    return [d for d in range(step, n + 1, step) if n % d == 0]


def _matmul_tiles(M, N, K, mode, out_sizes, n_extra):
    budget = VMEM_LIMIT_BYTES - VMEM_COMPILER_RESERVE_BYTES
    best = None
    for tk in _divisors(K, 16 if mode == "tn" else LANES):
        for tm in _divisors(M, LANES if mode == "tn" else 16):
            if tm > MAX_TILE_ROWS:
                continue
            for tn in _divisors(N, LANES):
                if tn > MAX_TILE_COLS:
                    continue
                need = 4 * tm * tk + 4 * tk * tn + 4 * tm * tn + 2 * tm * tn * sum(out_sizes) + 8 * n_extra * tm * tn
                if need > budget:
                    continue
                nk = K // tk
                steps = (M // tm) * (N // tn) * nk
                t_mxu = 2 * M * N * K / MXU_FLOPS + (nk > 1) * (8 * M * N * nk) / ACC_RMW_BYTES_PER_S
                t_hbm = (2 * M * K * (N // tn) + 2 * K * N * (M // tm) + M * N * (sum(out_sizes) + 4 * n_extra)) / HBM_BYTES_PER_S
                t = max(t_mxu, t_hbm) + steps * GRID_STEP_S
                if best is None or t < best[0]:
                    best = (t, tm, tn, tk)
    assert best is not None, (M, N, K, mode)
    return best[1:]


def matmul(a, b, mode, out_dtypes, name, extras=(), epi=None, tm=None, tn=None, tk=None, comm=None):
    if mode == "nn":
        (M, K), (K2, N) = a.shape, b.shape
    elif mode == "nt":
        (M, K), (N, K2) = a.shape, b.shape
    else:
        (K, M), (K2, N) = a.shape, b.shape
    assert K == K2, (a.shape, b.shape, mode)
    if not (tm and tn and tk):
        tm, tn, tk = _matmul_tiles(M, N, K, mode, [jnp.dtype(d).itemsize for d in out_dtypes], len(extras))
    nk = K // tk
    dims = {"nn": (((1,), (0,)), ((), ())), "nt": (((1,), (1,)), ((), ())), "tn": (((0,), (0,)), ((), ()))}[mode]
    n_extra, n_out = len(extras), len(out_dtypes)

    def body(*refs):
        a_ref, b_ref = refs[0], refs[1]
        extra_refs = refs[2:2 + n_extra]
        out_refs = refs[2 + n_extra:2 + n_extra + n_out]

        def finish(acc):
            outs = (acc,) if epi is None else epi(acc, *[r[...] for r in extra_refs])
            for o_ref, o in zip(out_refs, outs):
                o_ref[...] = o.astype(o_ref.dtype)

        part = lax.dot_general(a_ref[...], b_ref[...], dims, preferred_element_type=F32)
        if nk == 1:
            finish(part)
            return
        acc_ref = refs[-1]
        k = pl.program_id(2)

        @pl.when(k == 0)
        def _():
            acc_ref[...] = part

        @pl.when(k > 0)
        def _():
            acc_ref[...] += part

        @pl.when(k == nk - 1)
        def _():
            finish(acc_ref[...])

    a_spec = {"nn": pl.BlockSpec((tm, tk), lambda i, j, k: (i, k)),
              "nt": pl.BlockSpec((tm, tk), lambda i, j, k: (i, k)),
              "tn": pl.BlockSpec((tk, tm), lambda i, j, k: (k, i))}[mode]
    b_spec = {"nn": pl.BlockSpec((tk, tn), lambda i, j, k: (k, j)),
              "nt": pl.BlockSpec((tn, tk), lambda i, j, k: (j, k)),
              "tn": pl.BlockSpec((tk, tn), lambda i, j, k: (k, j))}[mode]
    o_spec = pl.BlockSpec((tm, tn), lambda i, j, k: (i, j))
    outs, couts = _hosted_call(
        body, name=name, grid=(M // tm, N // tn, nk),
        in_specs=[a_spec, b_spec] + [o_spec] * n_extra,
        out_specs=[o_spec] * n_out,
        out_shape=[jax.ShapeDtypeStruct((M, N), d) for d in out_dtypes],
        scratch_shapes=[pltpu.VMEM((tm, tn), F32)] if nk > 1 else [],
        semantics=("parallel", "parallel", "arbitrary"), args=(a, b, *extras), comm=comm)
    res = outs[0] if n_out == 1 else outs
    return res if comm is None else (res, couts)


class Dims:
    def __init__(self, d_model, seq, n_meta):
        self.D = d_model
        self.n_meta = n_meta
        self.T_real = seq + n_meta
        self.T = -(-self.T_real // LANES) * LANES
        self.H = d_model // 128
        self.DC = d_model // 2
        self.DP = d_model // 2
        self.PG = self.DP // 4
        self.QL = 512
        self.KL = 512
        self.ROPE = 64
        self.NOPE = 128
        self.QKH = 192
        self.HP = 256
        self.DFF = 4 * d_model
        self.o_gate = 0
        self.o_u = 3 * d_model
        self.o_b = self.o_u + self.DC
        self.o_c = self.o_b + self.DC
        self.o_ql = self.o_c + self.DC
        self.o_kl = self.o_ql + self.QL
        self.o_pool = self.o_kl + self.KL
        self.o_rope = self.o_pool + self.DP
        self.NIN = self.o_rope + 256
        self.tr = _pick(self.T, (384, 256, 128))
        self.tq = _pick(self.T, (384, 256, 128))


def _row_ids(i, tr):
    return i * tr + lax.broadcasted_iota(jnp.int32, (tr, 1), 0)


def rms_fwd(x, col_block, width, g, dm, name):
    tr = dm.tr

    def body(x_ref, g_ref, y_ref):
        xv = x_ref[...]
        r = lax.rsqrt(jnp.mean(xv * xv, axis=-1, keepdims=True) + EPS)
        y_ref[...] = (xv * r * g_ref[...]).astype(y_ref.dtype)

    return pl.pallas_call(
        body, name=name, grid=(dm.T // tr,),
        in_specs=[pl.BlockSpec((tr, width), lambda i: (i, col_block)), pl.BlockSpec((1, width), lambda i: (0, 0))],
        out_specs=pl.BlockSpec((tr, width), lambda i: (i, 0)),
        out_shape=jax.ShapeDtypeStruct((dm.T, width), BF16),
        compiler_params=_params(("parallel",)),
    )(x, g.reshape(1, width))


def rms_bwd(x, col_block, width, g, dy, dres, dm, name):
    tr = dm.tr
    has_res = dres is not None

    def body(*refs):
        if has_res:
            x_ref, g_ref, dy_ref, dres_ref, dx_ref, dxb_ref, dg_ref = refs
        else:
            x_ref, g_ref, dy_ref, dx_ref, dxb_ref, dg_ref = refs
        xv, dyv = x_ref[...], dy_ref[...]
        r = lax.rsqrt(jnp.mean(xv * xv, axis=-1, keepdims=True) + EPS)
        gdy = dyv * g_ref[...]
        dx = r * gdy - xv * (r * r * r) * jnp.mean(xv * gdy, axis=-1, keepdims=True)
        if has_res:
            dx = dx + dres_ref[...]
        dx_ref[...] = dx
        dxb_ref[...] = dx.astype(BF16)

        @pl.when(pl.program_id(0) == 0)
        def _():
            dg_ref[...] = jnp.zeros_like(dg_ref)

        dg_ref[...] += jnp.sum(dyv * xv * r, axis=0, keepdims=True)

    row = pl.BlockSpec((tr, width), lambda i: (i, 0))
    in_specs = [pl.BlockSpec((tr, width), lambda i: (i, col_block)), pl.BlockSpec((1, width), lambda i: (0, 0)), row]
    args = [x, g.reshape(1, width), dy]
    if has_res:
        in_specs.append(row)
        args.append(dres)
    return pl.pallas_call(
        body, name=name, grid=(dm.T // tr,),
        in_specs=in_specs,
        out_specs=[row, row, pl.BlockSpec((1, width), lambda i: (0, 0))],
        out_shape=[jax.ShapeDtypeStruct((dm.T, width), F32), jax.ShapeDtypeStruct((dm.T, width), BF16),
                   jax.ShapeDtypeStruct((1, width), F32)],
        compiler_params=_params(("arbitrary",)),
    )(*args)


def _fill_halo_buf(buf, src_fn, T, R, width):
    zeros = jnp.zeros((HALO, width), F32)
    buf[pl.ds(0, HALO), :] = zeros
    buf[pl.ds(HALO + T, HALO), :] = zeros

    def fill(r, c):
        r0 = pl.multiple_of(r * R, 8)
        buf[pl.ds(r0 + HALO, R), :] = src_fn(r0)
        return c

    lax.fori_loop(0, T // R, fill, 0)


def _back(win, sh):
    return pltpu.roll(win, sh, 0)


def _fwd(win, sh):
    return pltpu.roll(win, win.shape[0] - sh, 0)


def mixer_a_fwd(proj, conv_w, dm, name):
    T, cw = dm.T, 128
    R = dm.tr
    nb = dm.DC // cw

    def body(u_ref, b_ref, c_ref, w_ref, ya_ref, buf):
        _fill_halo_buf(buf, lambda r0: c_ref[pl.ds(r0, R), :] * u_ref[pl.ds(r0, R), :], T, R, cw)
        w0, w1, w2 = w_ref[0:1, :], w_ref[1:2, :], w_ref[2:3, :]

        def chunk(r, c):
            r0 = pl.multiple_of(r * R, 8)
            win = buf[pl.ds(r0, R + HALO), :]
            cv = w2 * win + w1 * _back(win, 1) + w0 * _back(win, 2)
            ya_ref[pl.ds(r0, R), :] = (b_ref[pl.ds(r0, R), :] * cv[HALO:, :]).astype(BF16)
            return c

        lax.fori_loop(0, T // R, chunk, 0)

    col = lambda off: pl.BlockSpec((T, cw), lambda j: (0, off // cw + j))
    return pl.pallas_call(
        body, name=name, grid=(nb,),
        in_specs=[col(dm.o_u), col(dm.o_b), col(dm.o_c), pl.BlockSpec((3, cw), lambda j: (0, j))],
        out_specs=pl.BlockSpec((T, cw), lambda j: (0, j)),
        out_shape=jax.ShapeDtypeStruct((T, dm.DC), BF16),
        scratch_shapes=[pltpu.VMEM((T + 2 * HALO, cw), F32)],
        compiler_params=_params(("parallel",)),
    )(proj, proj, proj, conv_w)


def mixer_a_bwd(proj, conv_w, dya, dm, name):
    T, cw = dm.T, 128
    R = dm.tr
    nb = dm.DC // cw

    def body(u_ref, b_ref, c_ref, w_ref, dya_ref, du_ref, db_ref, dc_ref, dw_ref, sbuf, gbuf):
        _fill_halo_buf(sbuf, lambda r0: c_ref[pl.ds(r0, R), :] * u_ref[pl.ds(r0, R), :], T, R, cw)
        _fill_halo_buf(gbuf, lambda r0: dya_ref[pl.ds(r0, R), :] * b_ref[pl.ds(r0, R), :], T, R, cw)
        w0, w1, w2 = w_ref[0:1, :], w_ref[1:2, :], w_ref[2:3, :]

        def chunk(r, acc):
            a0, a1, a2 = acc
            r0 = pl.multiple_of(r * R, 8)
            swin = sbuf[pl.ds(r0, R + HALO), :]
            s0, s1, s2 = swin[HALO:, :], _back(swin, 1)[HALO:, :], _back(swin, 2)[HALO:, :]
            gwin = gbuf[pl.ds(r0 + HALO, R + HALO), :]
            g0, g1, g2 = gwin[:R, :], _fwd(gwin, 1)[:R, :], _fwd(gwin, 2)[:R, :]
            cv = w2 * s0 + w1 * s1 + w0 * s2
            ds = w2 * g0 + w1 * g1 + w0 * g2
            db_ref[pl.ds(r0, R), :] = (dya_ref[pl.ds(r0, R), :] * cv).astype(BF16)
            du_ref[pl.ds(r0, R), :] = (ds * c_ref[pl.ds(r0, R), :]).astype(BF16)
            dc_ref[pl.ds(r0, R), :] = (ds * u_ref[pl.ds(r0, R), :]).astype(BF16)
            a2 = a2 + jnp.sum(g0 * s0, axis=0, keepdims=True)
            a1 = a1 + jnp.sum(g0 * s1, axis=0, keepdims=True)
            a0 = a0 + jnp.sum(g0 * s2, axis=0, keepdims=True)
            return a0, a1, a2

        z = jnp.zeros((1, cw), F32)
        a0, a1, a2 = lax.fori_loop(0, T // R, chunk, (z, z, z))
        dw_ref[0:1, :] = a0
        dw_ref[1:2, :] = a1
        dw_ref[2:3, :] = a2

    col = lambda off: pl.BlockSpec((T, cw), lambda j: (0, off // cw + j))
    own = pl.BlockSpec((T, cw), lambda j: (0, j))
    o = jax.ShapeDtypeStruct((T, dm.DC), BF16)
    return pl.pallas_call(
        body, name=name, grid=(nb,),
        in_specs=[col(dm.o_u), col(dm.o_b), col(dm.o_c), pl.BlockSpec((3, cw), lambda j: (0, j)), own],
        out_specs=[own, own, own, pl.BlockSpec((3, cw), lambda j: (0, j))],
        out_shape=[o, o, o, jax.ShapeDtypeStruct((3, dm.DC), F32)],
        scratch_shapes=[pltpu.VMEM((T + 2 * HALO, cw), F32), pltpu.VMEM((T + 2 * HALO, cw), F32)],
        compiler_params=_params(("parallel",)),
    )(proj, proj, proj, conv_w, dya)


def _rope(x, C, S):
    return x * C + (pltpu.roll(x, 32, 1) - pltpu.roll(x, 96, 1)) * S


def _rope_t(dy, C, S):
    return dy * C + (pltpu.roll(dy, 96, 1) - pltpu.roll(dy, 32, 1)) * S


HP = 4


def qk_prep_fwd(q0, kv0, proj, qn, kn, C, S, dm, name, comm=None):
    T, H, tr = dm.T, dm.H, dm.tr
    inv = 1.0 / dm.QKH

    def body(q0_ref, kv_ref, kr_ref, qn_ref, kn_ref, c_ref, s_ref, q_ref, k_ref, v_ref):
        Cv, Sv = c_ref[...], s_ref[...]
        kb = kr_ref[...]
        kb2 = jnp.sum(kb * kb, -1, keepdims=True)
        for a in range(HP):
            lo, mid, hi = 256 * a, 256 * a + 128, 256 * (a + 1)
            qa, qb = q0_ref[:, lo:mid], q0_ref[:, mid:hi]
            r = lax.rsqrt((jnp.sum(qa * qa, -1, keepdims=True) + jnp.sum(qb * qb, -1, keepdims=True)) * inv + EPS)
            q_ref[:, lo:mid] = (qa * r * qn_ref[:, :128]).astype(BF16)
            q_ref[:, mid:hi] = _rope(qb * r * qn_ref[:, 128:], Cv, Sv).astype(BF16)
            ka = kv_ref[:, lo:mid]
            r = lax.rsqrt((jnp.sum(ka * ka, -1, keepdims=True) + kb2) * inv + EPS)
            k_ref[:, lo:mid] = (ka * r * kn_ref[:, :128]).astype(BF16)
            k_ref[:, mid:hi] = _rope(kb * r * kn_ref[:, 128:], Cv, Sv).astype(BF16)
            v_ref[:, 128 * a:128 * (a + 1)] = kv_ref[:, mid:hi].astype(BF16)

    head = pl.BlockSpec((tr, 256 * HP), lambda i, h: (i, h))
    gain = pl.BlockSpec((1, 256), lambda i, h: (0, 0))
    tab = pl.BlockSpec((tr, 128), lambda i, h: (i, 0))
    outs, couts = _hosted_call(
        body, name=name, grid=(T // tr, H // HP),
        in_specs=[head, head, pl.BlockSpec((tr, 128), lambda i, h: (i, dm.o_rope // 128)), gain, gain, tab, tab],
        out_specs=[head, head, pl.BlockSpec((tr, 128 * HP), lambda i, h: (i, h))],
        out_shape=[jax.ShapeDtypeStruct((T, H * 256), BF16), jax.ShapeDtypeStruct((T, H * 256), BF16),
                   jax.ShapeDtypeStruct((T, H * 128), BF16)],
        scratch_shapes=[], semantics=("parallel", "parallel"), args=(q0, kv0, proj, qn, kn, C, S), comm=comm)
    return outs if comm is None else (outs, couts)


def qk_prep_bwd(q0, kv0, proj, qn, kn, C, S, dq, dk, dv, dm, name):
    T, H, tr = dm.T, dm.H, dm.tr
    inv = 1.0 / dm.QKH

    def body(q0_ref, kv_ref, kr_ref, qn_ref, kn_ref, c_ref, s_ref, dq_ref, dk_ref, dv_ref,
             dq0_ref, dkv_ref, dkr_ref, dqn_ref, dkn_ref):
        i, h = pl.program_id(0), pl.program_id(1)
        Cv, Sv = c_ref[...], s_ref[...]

        def norm_bwd(xa, xb, ga, gb, dya, dyb):
            r = lax.rsqrt((jnp.sum(xa * xa, -1, keepdims=True) + jnp.sum(xb * xb, -1, keepdims=True)) * inv + EPS)
            dzb = _rope_t(dyb, Cv, Sv)
            gda, gdb = ga * dya, gb * dzb
            dot = (jnp.sum(xa * gda, -1, keepdims=True) + jnp.sum(xb * gdb, -1, keepdims=True)) * inv
            r3 = r * r * r
            dxa = r * gda - xa * r3 * dot
            dxb = r * gdb - xb * r3 * dot
            dga = jnp.sum(dya * xa * r, axis=0, keepdims=True)
            dgb = jnp.sum(dzb * xb * r, axis=0, keepdims=True)
            return dxa, dxb, dga, dgb

        @pl.when((i == 0) & (h == 0))
        def _():
            dqn_ref[...] = jnp.zeros_like(dqn_ref)
            dkn_ref[...] = jnp.zeros_like(dkn_ref)

        @pl.when(h == 0)
        def _():
            dkr_ref[...] = jnp.zeros_like(dkr_ref)

        for a in range(HP):
            lo, mid, hi = 256 * a, 256 * a + 128, 256 * (a + 1)
            dxa, dxb, dga, dgb = norm_bwd(q0_ref[:, lo:mid], q0_ref[:, mid:hi], qn_ref[:, :128], qn_ref[:, 128:],
                                          dq_ref[:, lo:mid], dq_ref[:, mid:hi])
            dq0_ref[:, lo:mid] = dxa.astype(BF16)
            dq0_ref[:, mid:hi] = dxb.astype(BF16)
            dqn_ref[:, :128] += dga
            dqn_ref[:, 128:] += dgb
            dxa, dxb, dga, dgb = norm_bwd(kv_ref[:, lo:mid], kr_ref[...], kn_ref[:, :128], kn_ref[:, 128:],
                                          dk_ref[:, lo:mid], dk_ref[:, mid:hi])
            dkv_ref[:, lo:mid] = dxa.astype(BF16)
            dkv_ref[:, mid:hi] = dv_ref[:, 128 * a:128 * (a + 1)].astype(BF16)
            dkn_ref[:, :128] += dga
            dkn_ref[:, 128:] += dgb
            dkr_ref[...] += dxb

    head = pl.BlockSpec((tr, 256 * HP), lambda i, h: (i, h))
    gain = pl.BlockSpec((1, 256), lambda i, h: (0, 0))
    tab = pl.BlockSpec((tr, 128), lambda i, h: (i, 0))
    return pl.pallas_call(
        body, name=name, grid=(T // tr, H // HP),
        in_specs=[head, head, pl.BlockSpec((tr, 128), lambda i, h: (i, dm.o_rope // 128)), gain, gain, tab, tab,
                  head, head, pl.BlockSpec((tr, 128 * HP), lambda i, h: (i, h))],
        out_specs=[head, head, tab, gain, gain],
        out_shape=[jax.ShapeDtypeStruct((T, H * 256), BF16), jax.ShapeDtypeStruct((T, H * 256), BF16),
                   jax.ShapeDtypeStruct((T, 128), F32), jax.ShapeDtypeStruct((1, 256), F32),
                   jax.ShapeDtypeStruct((1, 256), F32)],
        compiler_params=_params(("arbitrary", "arbitrary")),
    )(q0, kv0, proj, qn, kn, C, S, dq, dk, dv)


_NT = (((1,), (1,)), ((), ()))


def _causal_mask(t):
    return lax.broadcasted_iota(jnp.int32, (t, t), 0) >= lax.broadcasted_iota(jnp.int32, (t, t), 1)


def _causal_mask_t(t):
    return lax.broadcasted_iota(jnp.int32, (t, t), 0) <= lax.broadcasted_iota(jnp.int32, (t, t), 1)


HB = 2


def attn_fwd(q, k, v, dm, name, comm=None):
    T, H, tq = dm.T, dm.H, dm.tq
    scale = dm.QKH ** -0.5

    def body(q_ref, k_ref, v_ref, o_ref, ob_ref, lse_ref):
        qi = pl.program_id(1)

        def step(j0, w, carry, masked):
            ss = [lax.dot_general(q_ref[:, 256 * a:256 * (a + 1)], k_ref[pl.ds(j0, w), 256 * a:256 * (a + 1)], _NT,
                                  preferred_element_type=F32) for a in range(HB)]
            out = []
            for a in range(HB):
                m, l, acc = carry[a]
                s = ss[a] * scale
                if masked:
                    s = jnp.where(_causal_mask(tq), s, -jnp.inf)
                m_new = jnp.maximum(m, jnp.max(s, -1, keepdims=True))
                alpha = jnp.exp(m - m_new)
                p = jnp.exp(s - m_new)
                l = alpha * l + jnp.sum(p, -1, keepdims=True)
                acc = alpha * acc + jnp.dot(p.astype(BF16), v_ref[pl.ds(j0, w), 128 * a:128 * (a + 1)],
                                            preferred_element_type=F32)
                out.append((m_new, l, acc))
            return tuple(out)

        one = (jnp.full((tq, 1), -jnp.inf, F32), jnp.zeros((tq, 1), F32), jnp.zeros((tq, 128), F32))
        carry = lax.fori_loop(0, jnp.right_shift(qi, 1), lambda t, c: step(pl.multiple_of(t * 2 * tq, tq), 2 * tq, c, False), (one,) * HB)
        carry = lax.cond(jnp.bitwise_and(qi, 1) == 1, lambda c: step(pl.multiple_of((qi - 1) * tq, tq), tq, c, False), lambda c: c, carry)
        carry = step(pl.multiple_of(qi * tq, tq), tq, carry, True)
        for a in range(HB):
            m, l, acc = carry[a]
            o = acc / l
            o_ref[:, 128 * a:128 * (a + 1)] = o
            ob_ref[:, 128 * a:128 * (a + 1)] = o.astype(BF16)
            lse_ref[a] = m + jnp.log(l)

    outs, couts = _hosted_call(
        body, name=name, grid=(H // HB, T // tq),
        in_specs=[pl.BlockSpec((tq, 256 * HB), lambda h, i: (i, h)), pl.BlockSpec((T, 256 * HB), lambda h, i: (0, h)),
                  pl.BlockSpec((T, 128 * HB), lambda h, i: (0, h))],
        out_specs=[pl.BlockSpec((tq, 128 * HB), lambda h, i: (i, h)), pl.BlockSpec((tq, 128 * HB), lambda h, i: (i, h)),
                   pl.BlockSpec((HB, tq, 1), lambda h, i: (h, i, 0))],
        out_shape=[jax.ShapeDtypeStruct((T, H * 128), F32), jax.ShapeDtypeStruct((T, H * 128), BF16),
                   jax.ShapeDtypeStruct((H, T, 1), F32)],
        scratch_shapes=[], semantics=("parallel", "parallel"), args=(q, k, v), comm=comm)
    return outs if comm is None else (outs, couts)


def attn_delta(do, o, dm, name):
    T, H, tr = dm.T, dm.H, dm.tr

    def body(do_ref, o_ref, delta_ref, dob_ref):
        for a in range(H):
            d = do_ref[:, 128 * a:128 * (a + 1)]
            delta_ref[a] = jnp.sum(d * o_ref[:, 128 * a:128 * (a + 1)], -1, keepdims=True)
            dob_ref[:, 128 * a:128 * (a + 1)] = d.astype(BF16)

    blk = pl.BlockSpec((tr, 128 * H), lambda i: (i, 0))
    return pl.pallas_call(
        body, name=name, grid=(T // tr,),
        in_specs=[blk, blk],
        out_specs=[pl.BlockSpec((H, tr, 1), lambda i: (0, i, 0)), blk],
        out_shape=[jax.ShapeDtypeStruct((H, T, 1), F32), jax.ShapeDtypeStruct((T, H * 128), BF16)],
        compiler_params=_params(("parallel",)),
    )(do, o)


def attn_bwd_dq(q, k, v, do, lse, delta, dm, name, comm=None):
    T, H, tq = dm.T, dm.H, dm.tq
    scale = dm.QKH ** -0.5

    def body(q_ref, k_ref, v_ref, do_ref, lse_ref, delta_ref, dq_ref):
        qi = pl.program_id(1)

        def step(j0, w, dqs, masked):
            hk = lambda a: slice(256 * a, 256 * (a + 1))
            hv = lambda a: slice(128 * a, 128 * (a + 1))
            ss = [lax.dot_general(q_ref[:, hk(a)], k_ref[pl.ds(j0, w), hk(a)], _NT, preferred_element_type=F32)
                  for a in range(HB)]
            dps = [lax.dot_general(do_ref[:, hv(a)], v_ref[pl.ds(j0, w), hv(a)], _NT, preferred_element_type=F32)
                   for a in range(HB)]
            out = []
            for a in range(HB):
                p = jnp.exp(ss[a] * scale - lse_ref[a])
                if masked:
                    p = jnp.where(_causal_mask(tq), p, 0.0)
                ds = p * (dps[a] - delta_ref[a]) * scale
                out.append(dqs[a] + jnp.dot(ds.astype(BF16), k_ref[pl.ds(j0, w), hk(a)], preferred_element_type=F32))
            return tuple(out)

        dqs = lax.fori_loop(0, jnp.right_shift(qi, 1), lambda t, c: step(pl.multiple_of(t * 2 * tq, tq), 2 * tq, c, False),
                            (jnp.zeros((tq, 256), F32),) * HB)
        dqs = lax.cond(jnp.bitwise_and(qi, 1) == 1, lambda c: step(pl.multiple_of((qi - 1) * tq, tq), tq, c, False),
                       lambda c: c, dqs)
        dqs = step(pl.multiple_of(qi * tq, tq), tq, dqs, True)
        for a in range(HB):
            dq_ref[:, 256 * a:256 * (a + 1)] = dqs[a]

    stat = pl.BlockSpec((HB, tq, 1), lambda h, i: (h, i, 0))
    outs, couts = _hosted_call(
        body, name=name, grid=(H // HB, T // tq),
        in_specs=[pl.BlockSpec((tq, 256 * HB), lambda h, i: (i, h)), pl.BlockSpec((T, 256 * HB), lambda h, i: (0, h)),
                  pl.BlockSpec((T, 128 * HB), lambda h, i: (0, h)), pl.BlockSpec((tq, 128 * HB), lambda h, i: (i, h)),
                  stat, stat],
        out_specs=[pl.BlockSpec((tq, 256 * HB), lambda h, i: (i, h))],
        out_shape=[jax.ShapeDtypeStruct((T, H * 256), F32)],
        scratch_shapes=[], semantics=("parallel", "parallel"), args=(q, k, v, do, lse, delta), comm=comm)
    return outs[0] if comm is None else (outs[0], couts)


def attn_bwd_dkv(q, k, v, do, lse_rows, delta_rows, dm, name, comm=None):
    T, H, tq = dm.T, dm.H, dm.tq
    nq = T // tq
    scale = dm.QKH ** -0.5

    def body(q_ref, k_ref, v_ref, do_ref, lse_ref, delta_ref, dk_ref, dv_ref):
        kj = pl.program_id(1)

        def step(i, n, carry, masked):
            i0 = pl.multiple_of(i * tq, tq)
            w = n * tq
            hk = lambda a: slice(256 * a, 256 * (a + 1))
            hv = lambda a: slice(128 * a, 128 * (a + 1))
            row = lambda ref, a: ref[a, i] if n == 1 else jnp.concatenate([ref[a, i], ref[a, i + 1]], axis=1)
            sts = [lax.dot_general(k_ref[:, hk(a)], q_ref[pl.ds(i0, w), hk(a)], _NT, preferred_element_type=F32)
                   for a in range(HB)]
            dpts = [lax.dot_general(v_ref[:, hv(a)], do_ref[pl.ds(i0, w), hv(a)], _NT, preferred_element_type=F32)
                    for a in range(HB)]
            out = []
            for a in range(HB):
                dk, dv = carry[a]
                pt = jnp.exp(sts[a] * scale - row(lse_ref, a))
                if masked:
                    pt = jnp.where(_causal_mask_t(tq), pt, 0.0)
                dv = dv + jnp.dot(pt.astype(BF16), do_ref[pl.ds(i0, w), hv(a)], preferred_element_type=F32)
                dst = pt * (dpts[a] - row(delta_ref, a)) * scale
                dk = dk + jnp.dot(dst.astype(BF16), q_ref[pl.ds(i0, w), hk(a)], preferred_element_type=F32)
                out.append((dk, dv))
            return tuple(out)

        carry = step(kj, 1, ((jnp.zeros((tq, 256), F32), jnp.zeros((tq, 128), F32)),) * HB, True)
        rest = nq - 1 - kj
        carry = lax.fori_loop(0, jnp.right_shift(rest, 1), lambda t, c: step(kj + 1 + 2 * t, 2, c, False), carry)
        carry = lax.cond(jnp.bitwise_and(rest, 1) == 1, lambda c: step(nq - 1, 1, c, False), lambda c: c, carry)
        for a in range(HB):
            dk_ref[:, 256 * a:256 * (a + 1)] = carry[a][0]
            dv_ref[:, 128 * a:128 * (a + 1)] = carry[a][1]

    rows = pl.BlockSpec((HB, nq, 1, tq), lambda h, j: (h, 0, 0, 0))
    outs, couts = _hosted_call(
        body, name=name, grid=(H // HB, nq),
        in_specs=[pl.BlockSpec((T, 256 * HB), lambda h, j: (0, h)), pl.BlockSpec((tq, 256 * HB), lambda h, j: (j, h)),
                  pl.BlockSpec((tq, 128 * HB), lambda h, j: (j, h)), pl.BlockSpec((T, 128 * HB), lambda h, j: (0, h)),
                  rows, rows],
        out_specs=[pl.BlockSpec((tq, 256 * HB), lambda h, j: (j, h)), pl.BlockSpec((tq, 128 * HB), lambda h, j: (j, h))],
        out_shape=[jax.ShapeDtypeStruct((T, H * 256), F32), jax.ShapeDtypeStruct((T, H * 128), F32)],
        scratch_shapes=[], semantics=("parallel", "parallel"), args=(q, k, v, do, lse_rows, delta_rows), comm=comm)
    return outs if comm is None else (outs, couts)


def _window_sum(win, g, shift):
    s1 = win + shift(win, 1)
    s2 = s1 + shift(s1, 2)
    s3 = s2 + shift(s2, 4)
    s4 = s3 + shift(s3, 8)
    return jnp.where(g == 0, s1, jnp.where(g == 1, s2, jnp.where(g == 2, s3, s4)))


def _count(r0, R, g, T_unused=None):
    t = r0 + lax.broadcasted_iota(jnp.int32, (R, 1), 0)
    return jnp.minimum(t + 1, jnp.left_shift(2, g)).astype(F32)


def pool_fwd(proj, pw, ps, dm, name):
    T, PG, R = dm.T, dm.PG, dm.tr

    def body(x_ref, pw_ref, ps_ref, pooled_ref, mixed_ref, yc_ref, buf):
        g = pl.program_id(0)
        _fill_halo_buf(buf, lambda r0: x_ref[pl.ds(r0, R), :], T, R, PG)

        def chunk(r, c):
            r0 = pl.multiple_of(r * R, 8)
            win = buf[pl.ds(r0, R + HALO), :]
            ws = _window_sum(win, g, _back)[HALO:, :]
            pooled = (ws / _count(r0, R, g) - win[HALO:, :]).astype(BF16)
            pooled_ref[pl.ds(r0, R), :] = pooled
            mixed = jnp.dot(pooled, pw_ref[...], preferred_element_type=F32)
            mixed_ref[pl.ds(r0, R), :] = mixed
            yc_ref[pl.ds(r0, R), :] = (mixed * ps_ref[...]).astype(BF16)
            return c

        lax.fori_loop(0, T // R, chunk, 0)

    own = pl.BlockSpec((T, PG), lambda g: (0, g))
    return pl.pallas_call(
        body, name=name, grid=(4,),
        in_specs=[pl.BlockSpec((T, PG), lambda g: (0, dm.o_pool // PG + g)), pl.BlockSpec((None, PG, PG), lambda g: (g, 0, 0)),
                  pl.BlockSpec((1, PG), lambda g: (0, g))],
        out_specs=[own, own, own],
        out_shape=[jax.ShapeDtypeStruct((T, dm.DP), BF16), jax.ShapeDtypeStruct((T, dm.DP), F32),
                   jax.ShapeDtypeStruct((T, dm.DP), BF16)],
        scratch_shapes=[pltpu.VMEM((T + 2 * HALO, PG), F32)],
        compiler_params=_params(("parallel",)),
    )(proj, pw, ps)


def pool_bwd(dyc, mixed, pooled, pw, ps, dm, name):
    T, PG, R = dm.T, dm.PG, dm.tr
    _TN = (((0,), (0,)), ((), ()))

    def body(dyc_ref, mixed_ref, pooled_ref, pw_ref, ps_ref, dx_ref, dpw_ref, dps_ref, qbuf, dpbuf):
        g = pl.program_id(0)
        zeros = jnp.zeros((HALO, PG), F32)
        qbuf[pl.ds(0, HALO), :] = zeros
        qbuf[pl.ds(HALO + T, HALO), :] = zeros
        dpw_ref[...] = jnp.zeros_like(dpw_ref)

        def first(r, dps):
            r0 = pl.multiple_of(r * R, 8)
            dyc = dyc_ref[pl.ds(r0, R), :]
            dps = dps + jnp.sum(dyc * mixed_ref[pl.ds(r0, R), :], axis=0, keepdims=True)
            dmb = (dyc * ps_ref[...]).astype(BF16)
            dpw_ref[...] += lax.dot_general(pooled_ref[pl.ds(r0, R), :], dmb, _TN, preferred_element_type=F32)
            dp = lax.dot_general(dmb, pw_ref[...], _NT, preferred_element_type=F32)
            dpbuf[pl.ds(r0, R), :] = dp
            qbuf[pl.ds(r0 + HALO, R), :] = dp / _count(r0, R, g)
            return dps

        dps_ref[...] = lax.fori_loop(0, T // R, first, jnp.zeros((1, PG), F32))

        def second(r, c):
            r0 = pl.multiple_of(r * R, 8)
            win = qbuf[pl.ds(r0 + HALO, R + HALO), :]
            ws = _window_sum(win, g, _fwd)[:R, :]
            dx_ref[pl.ds(r0, R), :] = (ws - dpbuf[pl.ds(r0, R), :]).astype(BF16)
            return c

        lax.fori_loop(0, T // R, second, 0)

    own = pl.BlockSpec((T, PG), lambda g: (0, g))
    return pl.pallas_call(
        body, name=name, grid=(4,),
        in_specs=[own, own, own, pl.BlockSpec((None, PG, PG), lambda g: (g, 0, 0)), pl.BlockSpec((1, PG), lambda g: (0, g))],
        out_specs=[own, pl.BlockSpec((None, PG, PG), lambda g: (g, 0, 0)), pl.BlockSpec((1, PG), lambda g: (0, g))],
        out_shape=[jax.ShapeDtypeStruct((T, dm.DP), BF16), jax.ShapeDtypeStruct((4, PG, PG), F32),
                   jax.ShapeDtypeStruct((1, dm.DP), F32)],
        scratch_shapes=[pltpu.VMEM((T + 2 * HALO, PG), F32), pltpu.VMEM((T, PG), F32)],
        compiler_params=_params(("parallel",)),
    )(dyc, mixed, pooled, pw, ps)


def _sigmoid(x):
    return 1.0 / (1.0 + jnp.exp(-x))


def merge_fwd(proj, A, B, C, dm, name):
    T, D, tr, tc = dm.T, dm.D, dm.tr, 512
    nc = D // tc

    def body(g0, g1, g2, a, b, c, out):
        out[...] = (_sigmoid(g0[...]) * a[...] + _sigmoid(g1[...]) * b[...] + _sigmoid(g2[...]) * c[...]).astype(BF16)

    gate = lambda k: pl.BlockSpec((tr, tc), lambda i, j: (i, k * nc + j))
    own = pl.BlockSpec((tr, tc), lambda i, j: (i, j))
    return pl.pallas_call(
        body, name=name, grid=(T // tr, nc),
        in_specs=[gate(0), gate(1), gate(2), own, own, own],
        out_specs=own,
        out_shape=jax.ShapeDtypeStruct((T, D), BF16),
        compiler_params=_params(("parallel", "parallel")),
    )(proj, proj, proj, A, B, C)


def merge_bwd(proj, A, B, C, dmerged, dm, name):
    T, D, tr, tc = dm.T, dm.D, dm.tr, 512
    nc = D // tc

    def body(g0, g1, g2, a, b, c, dmr, da, db, dc, dl0, dl1, dl2):
        d = dmr[...]
        for g_ref, y_ref, dy_ref, dl_ref in ((g0, a, da, dl0), (g1, b, db, dl1), (g2, c, dc, dl2)):
            s = _sigmoid(g_ref[...])
            dy_ref[...] = (d * s).astype(BF16)
            dl_ref[...] = (d * y_ref[...] * s * (1.0 - s)).astype(BF16)

    gate = lambda k: pl.BlockSpec((tr, tc), lambda i, j: (i, k * nc + j))
    own = pl.BlockSpec((tr, tc), lambda i, j: (i, j))
    o = jax.ShapeDtypeStruct((T, D), BF16)
    return pl.pallas_call(
        body, name=name, grid=(T // tr, nc),
        in_specs=[gate(0), gate(1), gate(2), own, own, own, own],
        out_specs=[own] * 6,
        out_shape=[o] * 6,
        compiler_params=_params(("parallel", "parallel")),
    )(proj, proj, proj, A, B, C, dmerged)


def loss_head(y, target, dm, name):
    T, D, tr = dm.T, dm.D, dm.tr

    def body(y_ref, t_ref, dy_ref, dyb_ref, loss_ref):
        i = pl.program_id(0)
        t = _row_ids(i, tr)
        real = (t >= dm.n_meta) & (t < dm.T_real)
        err = jnp.where(real, y_ref[...] - t_ref[...], 0.0)
        dy = err * (1.0 / D)
        dy_ref[...] = dy
        dyb_ref[...] = dy.astype(BF16)

        @pl.when(i == 0)
        def _():
            loss_ref[...] = jnp.zeros_like(loss_ref)

        loss_ref[...] += 0.5 * jnp.sum(jnp.sum(err * err, axis=-1, keepdims=True) * (1.0 / D))

    row = pl.BlockSpec((tr, D), lambda i: (i, 0))
    return pl.pallas_call(
        body, name=name, grid=(T // tr,),
        in_specs=[row, row],
        out_specs=[row, row, pl.BlockSpec((8, LANES), lambda i: (0, 0))],
        out_shape=[jax.ShapeDtypeStruct((T, D), F32), jax.ShapeDtypeStruct((T, D), BF16),
                   jax.ShapeDtypeStruct((8, LANES), F32)],
        compiler_params=_params(("arbitrary",)),
    )(y, target)


def adamw(w, m, v, parts, name):
    R, C = w.shape
    P = parts.shape[0]
    br = R
    for cand in (512, 256, 128, 64, 32, 16, 8):
        if R % cand == 0 and cand * C * 4 <= (1 << 20):
            br = cand
            break
    if R * C * 4 <= (1 << 20):
        br = R

    def body(w_ref, m_ref, v_ref, p_ref, g_ref, d_ref, nm_ref, nv_ref):
        g = p_ref[0].astype(F32)
        for k in range(1, P):
            g = g + p_ref[k].astype(F32)
        mm = ADAM_B1 * m_ref[...] + (1.0 - ADAM_B1) * g
        vv = ADAM_B2 * v_ref[...] + (1.0 - ADAM_B2) * (g * g)
        m_hat = mm / (1.0 - ADAM_B1 ** ADAM_STEP)
        v_hat = vv / (1.0 - ADAM_B2 ** ADAM_STEP)
        g_ref[...] = g
        d_ref[...] = -ADAM_LR * (m_hat / (jnp.sqrt(v_hat) + ADAM_EPS) + ADAM_WD * w_ref[...])
        nm_ref[...] = mm
        nv_ref[...] = vv

    blk = pl.BlockSpec((br, C), lambda i: (i, 0))
    o = jax.ShapeDtypeStruct((R, C), F32)
    return pl.pallas_call(
        body, name=name, grid=(R // br,),
        in_specs=[blk, blk, blk, pl.BlockSpec((P, br, C), lambda i: (0, i, 0))],
        out_specs=[blk] * 4,
        out_shape=[o] * 4,
        compiler_params=_params(("parallel",)),
    )(w, m, v, parts)


def sum_parts(parts, name):
    P, R, C = parts.shape

    def body(p_ref, o_ref):
        acc = p_ref[0]
        for k in range(1, P):
            acc = acc + p_ref[k]
        o_ref[...] = acc

    return pl.pallas_call(
        body, name=name, grid=(1,),
        in_specs=[pl.BlockSpec((P, R, C), lambda i: (0, 0, 0))],
        out_specs=pl.BlockSpec((R, C), lambda i: (0, 0)),
        out_shape=jax.ShapeDtypeStruct((R, C), F32),
        compiler_params=_params(("arbitrary",)),
    )(parts)


def add_sibling(parts, got, core, name):
    _, _, R, C = parts.shape
    br = _pick(R, (1024, 512, 256, 128, 64, 32, 16))

    def body(c_ref, a_ref, b_ref, o_ref):
        o_ref[...] = (a_ref[...].astype(F32) + b_ref[...].astype(F32)).astype(BF16)

    blk = pl.BlockSpec((None, br, C), lambda ch, i, c: (ch, i, 0))
    return pl.pallas_call(
        body, name=name,
        grid_spec=pltpu.PrefetchScalarGridSpec(
            num_scalar_prefetch=1, grid=(4, R // br),
            in_specs=[pl.BlockSpec((None, None, br, C), lambda ch, i, c: (ch, c[0], i, 0)), blk],
            out_specs=blk),
        out_shape=jax.ShapeDtypeStruct((4, R, C), BF16), compiler_params=_params(("parallel", "parallel")),
    )(core, parts, got)


def all_gather(arrs, name):
    n = len(arrs)

    def body(*refs):
        ins, outs = refs[:n], refs[n:2 * n]
        send_sems, recv_sems, local_sems = refs[2 * n:]
        x, y, c = _place()
        me, sibling = (x, y, c), (x, y, 1 - c)
        chips = [(1 - x, y), (x, 1 - y), (1 - x, 1 - y)]

        def copy(a, k, block, to, src=None):
            px, py, pc = block
            dst = outs[a].at[4 * px + 2 * py + pc]
            return pltpu.make_async_remote_copy(
                src_ref=dst if src is None else src, dst_ref=dst,
                send_sem=send_sems.at[7 * a + k], recv_sem=recv_sems.at[7 * a + k],
                device_id=to, device_id_type=_MESH)

        started = []
        for a in range(n):
            mine = pltpu.make_async_copy(ins[a], outs[a].at[4 * x + 2 * y + c], local_sems.at[a])
            mine.start()
            started.append(mine)
        sends = []
        for a in range(n):
            sends.append(copy(a, 0, me, sibling, src=ins[a]))
            for j, chip in enumerate(chips):
                sends.append(copy(a, 1 + j, me, (*chip, c), src=ins[a]))
        for cp in sends:
            cp.start()
        for j, chip in enumerate(chips):
            for a in range(n):
                copy(a, 1 + j, (*chip, c), me).wait_recv()
                fwd = copy(a, 4 + j, (*chip, c), sibling)
                fwd.start()
                sends.append(fwd)
        for a in range(n):
            copy(a, 0, sibling, me).wait_recv()
            for j, chip in enumerate(chips):
                copy(a, 4 + j, (*chip, 1 - c), me).wait_recv()
        for cp in sends:
            cp.wait_send()
        for cp in started:
            cp.wait()

    outs = pl.pallas_call(
        body, name=name,
        in_specs=[_HBM] * n, out_specs=[_HBM] * n,
        out_shape=[jax.ShapeDtypeStruct((8,) + a.shape, a.dtype) for a in arrs],
        scratch_shapes=[pltpu.SemaphoreType.DMA((7 * n,)), pltpu.SemaphoreType.DMA((7 * n,)), pltpu.SemaphoreType.DMA((n,))],
    )(*arrs)
    return list(outs)


def sibling_exchange(arrs, name):
    n = len(arrs)

    def body(*refs):
        ins, got = refs[:n], refs[n:2 * n]
        send_sems, recv_sems = refs[2 * n:]
        x, y, c = _place()
        work = []
        for a in range(n):
            for ch in range(4):
                cp = pltpu.make_async_remote_copy(
                    src_ref=ins[a].at[ch, 1 - c], dst_ref=got[a].at[ch],
                    send_sem=send_sems.at[4 * a + ch], recv_sem=recv_sems.at[4 * a + ch],
                    device_id=(x, y, 1 - c), device_id_type=_MESH)
                cp.start()
                work.append(cp)
        for cp in work:
            cp.wait()

    outs = pl.pallas_call(
        body, name=name,
        in_specs=[_HBM] * n, out_specs=[_HBM] * n,
        out_shape=[jax.ShapeDtypeStruct((4,) + a.shape[2:], a.dtype) for a in arrs],
        scratch_shapes=[pltpu.SemaphoreType.DMA((4 * n,)), pltpu.SemaphoreType.DMA((4 * n,))],
    )(*arrs)
    return list(outs)


def _remote(src, dst, send_sem, recv_sem, to):
    return pltpu.make_async_remote_copy(src_ref=src, dst_ref=dst, send_sem=send_sem, recv_sem=recv_sem,
                                        device_id=to, device_id_type=_MESH)


def gather_ici(blocks):
    n = len(blocks)

    def copies(cins, couts, sems):
        send_sems, recv_sems, local_sems = sems
        x, y, c = _place()
        mine = 4 * x + 2 * y + c
        local, sends, recvs = [], [], []
        for a in range(n):
            local.append(pltpu.make_async_copy(cins[a], couts[a].at[mine], local_sems.at[a]))
            for j, (px, py) in enumerate([(1 - x, y), (x, 1 - y), (1 - x, 1 - y)]):
                k = 3 * a + j
                sends.append(_remote(cins[a], couts[a].at[mine], send_sems.at[k], recv_sems.at[k], (px, py, c)))
                recvs.append(_remote(cins[a], couts[a].at[4 * px + 2 * py + c], send_sems.at[k], recv_sems.at[k], (px, py, c)))
        return local, sends, recvs

    def start(cins, couts, sems):
        local, sends, _ = copies(cins, couts, sems)
        for cp in local + sends:
            cp.start()

    def finish(cins, couts, sems):
        local, sends, recvs = copies(cins, couts, sems)
        for cp in sends:
            cp.wait_send()
        for cp in recvs:
            cp.wait_recv()
        for cp in local:
            cp.wait()

    return Hosted(list(blocks), [jax.ShapeDtypeStruct((8,) + b.shape, b.dtype) for b in blocks],
                  [pltpu.SemaphoreType.DMA((3 * n,)), pltpu.SemaphoreType.DMA((3 * n,)), pltpu.SemaphoreType.DMA((n,))],
                  start, finish)


def fill_sibling(stks, name):
    n = len(stks)

    def body(*refs):
        ins, outs = refs[:n], refs[n:2 * n]
        send_sems, recv_sems = refs[2 * n:]
        x, y, c = _place()
        sends, recvs = [], []
        for a in range(n):
            for ch in range(4):
                k = 4 * a + ch
                sends.append(_remote(ins[a].at[2 * ch + c], outs[a].at[2 * ch + c], send_sems.at[k], recv_sems.at[k], (x, y, 1 - c)))
                recvs.append(_remote(ins[a].at[2 * ch + c], outs[a].at[2 * ch + 1 - c], send_sems.at[k], recv_sems.at[k], (x, y, 1 - c)))
        for cp in sends:
            cp.start()
        for cp in sends:
            cp.wait_send()
        for cp in recvs:
            cp.wait_recv()

    outs = pl.pallas_call(
        body, name=name,
        in_specs=[_HBM] * n, out_specs=[_HBM] * n,
        out_shape=[jax.ShapeDtypeStruct(s.shape, s.dtype) for s in stks],
        scratch_shapes=[pltpu.SemaphoreType.DMA((4 * n,)), pltpu.SemaphoreType.DMA((4 * n,))],
        input_output_aliases={a: a for a in range(n)},
    )(*stks)
    return list(outs)


def reduce_ici(arrs):
    n = len(arrs)

    def copies(cins, couts, sems):
        send_sems, recv_sems, local_sems = sems
        x, y, c = _place()
        my_chip = 2 * x + y
        local, sends, recvs = [], [], []
        for a in range(n):
            local.append(pltpu.make_async_copy(cins[a].at[my_chip], couts[a].at[my_chip], local_sems.at[a]))
            for j, (px, py) in enumerate([(1 - x, y), (x, 1 - y), (1 - x, 1 - y)]):
                k = 3 * a + j
                sends.append(_remote(cins[a].at[2 * px + py], couts[a].at[my_chip], send_sems.at[k], recv_sems.at[k], (px, py, c)))
                recvs.append(_remote(cins[a].at[my_chip], couts[a].at[2 * px + py], send_sems.at[k], recv_sems.at[k], (px, py, c)))
        return local, sends, recvs

    def start(cins, couts, sems):
        local, sends, _ = copies(cins, couts, sems)
        for cp in local + sends:
            cp.start()

    def finish(cins, couts, sems):
        local, sends, recvs = copies(cins, couts, sems)
        for cp in sends:
            cp.wait_send()
        for cp in recvs:
            cp.wait_recv()
        for cp in local:
            cp.wait()

    return Hosted(list(arrs), [jax.ShapeDtypeStruct(a.shape, a.dtype) for a in arrs],
                  [pltpu.SemaphoreType.DMA((3 * n,)), pltpu.SemaphoreType.DMA((3 * n,)), pltpu.SemaphoreType.DMA((n,))],
                  start, finish)


def sibling_hosted(arrs):
    n = len(arrs)

    def copies(cins, couts, sems):
        send_sems, recv_sems = sems
        x, y, c = _place()
        return [_remote(cins[a].at[ch, 1 - c], couts[a].at[ch], send_sems.at[4 * a + ch], recv_sems.at[4 * a + ch],
                        (x, y, 1 - c)) for a in range(n) for ch in range(4)]

    def start(cins, couts, sems):
        for cp in copies(cins, couts, sems):
            cp.start()

    def finish(cins, couts, sems):
        for cp in copies(cins, couts, sems):
            cp.wait()

    return Hosted(list(arrs), [jax.ShapeDtypeStruct((4,) + a.shape[2:], a.dtype) for a in arrs],
                  [pltpu.SemaphoreType.DMA((4 * n,)), pltpu.SemaphoreType.DMA((4 * n,))], start, finish)


def merge_hosted(first, second):
    ni, no, ns = len(first.ins), len(first.out_shapes), len(first.sems)

    def start(cins, couts, sems):
        first.start(cins[:ni], couts[:no], sems[:ns])
        second.start(cins[ni:], couts[no:], sems[ns:])

    def finish(cins, couts, sems):
        first.finish(cins[:ni], couts[:no], sems[:ns])
        second.finish(cins[ni:], couts[no:], sems[ns:])

    aliases = dict(first.aliases)
    aliases.update({ni + i: no + o for i, o in second.aliases.items()})
    both = Hosted(first.ins + second.ins, first.out_shapes + second.out_shapes, first.sems + second.sems, start, finish,
                  aliases)
    return both, lambda couts: (couts[:no], couts[no:])


def fill_hosted(stks):
    n = len(stks)

    def copies(cins, couts, sems):
        send_sems, recv_sems = sems
        x, y, c = _place()
        sends, recvs = [], []
        for a in range(n):
            for ch in range(4):
                k = 4 * a + ch
                sends.append(_remote(cins[a].at[2 * ch + c], couts[a].at[2 * ch + c], send_sems.at[k], recv_sems.at[k], (x, y, 1 - c)))
                recvs.append(_remote(cins[a].at[2 * ch + c], couts[a].at[2 * ch + 1 - c], send_sems.at[k], recv_sems.at[k], (x, y, 1 - c)))
        return sends, recvs

    def start(cins, couts, sems):
        for cp in copies(cins, couts, sems)[0]:
            cp.start()

    def finish(cins, couts, sems):
        sends, recvs = copies(cins, couts, sems)
        for cp in sends:
            cp.wait_send()
        for cp in recvs:
            cp.wait_recv()

    return Hosted(list(stks), [jax.ShapeDtypeStruct(s.shape, s.dtype) for s in stks],
                  [pltpu.SemaphoreType.DMA((4 * n,)), pltpu.SemaphoreType.DMA((4 * n,))], start, finish,
                  aliases={a: a for a in range(n)})


COL_SHARDED = ("w_in", "w_uq", "w_ukv", "w_branch_a", "w_branch_c", "w_up")
ROW_SHARDED = ("w_branch_b", "w_o", "w_down")
BIG = COL_SHARDED + ROW_SHARDED


def _full_from_stacked(name, st):
    if name in COL_SHARDED:
        _, L, K, n = st.shape
        return st.transpose(1, 2, 0, 3).reshape(L, K, 8 * n)
    if name in ROW_SHARDED:
        _, L, k, N = st.shape
        return st.transpose(1, 0, 2, 3).reshape(L, 8 * k, N)
    if name == "pool_w":
        _, L, G, pk, PG = st.shape
        return st.transpose(1, 2, 0, 3, 4).reshape(L, G, 8 * pk, PG)
    if name == "meta_tokens":
        _, M, n = st.shape
        return st.transpose(1, 0, 2).reshape(M, 8 * n)
    if name == "conv_w":
        _, L, W, n = st.shape
        return st.transpose(1, 2, 0, 3).reshape(L, W, 8 * n)
    raise ValueError(name)


def _shards_from_full(name, g):
    if name in COL_SHARDED:
        L, K, N = g.shape
        s = g.reshape(L, K, 8, N // 8).transpose(2, 0, 1, 3)
    else:
        L, K, N = g.shape
        s = g.reshape(L, 8, K // 8, N).transpose(1, 0, 2, 3)
    return s.reshape((4, 2) + s.shape[1:])


def _w_in_to_padded(w, dm):
    o3 = 3 * dm.DC + dm.QL + dm.KL
    o4 = o3 + dm.ROPE
    o5 = o4 + dm.DP
    pad = jnp.zeros(w.shape[:-1] + (256 - dm.ROPE,), w.dtype)
    return jnp.concatenate([w[..., o5:], w[..., :o3], w[..., o4:o5], w[..., o3:o4], pad], axis=-1)


def _w_in_from_padded(g, dm):
    o3 = 3 * dm.DC + dm.QL + dm.KL
    a = 3 * dm.D
    return jnp.concatenate([g[..., a:a + o3], g[..., dm.o_rope:dm.o_rope + dm.ROPE], g[..., dm.o_pool:dm.o_pool + dm.DP],
                            g[..., :a]], axis=-1)


def _pad_heads(w, dm):
    w = w.reshape(w.shape[:-1] + (dm.H, dm.QKH))
    w = jnp.pad(w, [(0, 0)] * (w.ndim - 1) + [(0, dm.HP - dm.QKH)])
    return w.reshape(w.shape[:-2] + (dm.H * dm.HP,))


def _unpad_heads(g, dm):
    g = g.reshape(g.shape[:-1] + (dm.H, dm.HP))[..., :dm.QKH]
    return g.reshape(g.shape[:-2] + (dm.H * dm.QKH,))


class _Weights:
    def __init__(self, w, dm, plan, fill_in):
        self.w, self.dm, self.plan, self.fill_in, self.full = w, dm, plan, fill_in, {}
        self.pending, self.split = {}, {}

    def blocks(self, items):
        return [self.w[n][l:l + 1].astype(BF16) for n, l in items]

    def put(self, items, stacked):
        for (n, l), st in zip(items, stacked):
            f = _full_from_stacked(n, st)[0]
            if n == "w_in":
                f = _w_in_to_padded(f, self.dm)
            if n == "w_uq":
                f = _pad_heads(f, self.dm)
            self.full[(n, l)] = f

    def comm(self, tag):
        items, waiting = self.plan.get(tag), self.pending.get(tag)
        gather = gather_ici(self.blocks(items)) if items else None
        fill = fill_hosted(waiting[1]) if waiting else None
        if gather is not None and fill is not None:
            both, self.split[tag] = merge_hosted(gather, fill)
            return both
        return gather if fill is None else fill

    def arrived(self, tag, couts):
        items, waiting = self.plan.get(tag), self.pending.pop(tag, None)
        if items and waiting:
            gathered, filled = self.split[tag](couts)
        else:
            gathered, filled = (couts, None) if items else (None, couts)
        if waiting:
            self.put(waiting[0], filled)
        if items:
            if tag in self.fill_in:
                self.pending[self.fill_in[tag]] = (items, gathered)
            else:
                self.put(items, fill_sibling(gathered, f"fill_{tag}"))

    def __call__(self, n, l):
        return self.full[(n, l)]


class _Reducer:
    def __init__(self, dm, core):
        self.dm, self.core, self.q = dm, core, {}

    def parts(self, items, g):
        parts = []
        for n, l in items:
            f = g[n]
            if n == "w_in":
                f = _w_in_from_padded(f, self.dm)
            if n == "w_uq":
                f = _unpad_heads(f, self.dm)
            parts.append(_shards_from_full(n, f[None]))
        return parts

    def add(self, items, parts, got):
        out = []
        for (n, l), a, b in zip(items, parts, got):
            C = a.shape[-1]
            out.append(add_sibling(a.reshape(4, 2, -1, C), b.reshape(4, -1, C), self.core,
                                   f"reduce_add_{n}_{l}").reshape(b.shape))
        return out

    def prepare(self, tag, items, g):
        parts = self.parts(items, g)
        return self.add(items, parts, sibling_exchange(parts, f"reduce_sibling_{tag}"))

    def put(self, items, summed):
        for key, q in zip(items, summed):
            self.q[key] = q


def _layer_fwd(xin, l, ws, G, tabs, dm):
    nm = lambda s: f"l{l}_{s}"
    D = dm.D

    def mm(tag, *args, **kw):
        comm = ws.comm(nm(tag))
        if comm is None:
            return matmul(*args, nm(tag), **kw)
        res, couts = matmul(*args, nm(tag), comm=comm, **kw)
        ws.arrived(nm(tag), couts)
        return res

    h = rms_fwd(xin, 0, D, G["attn_norm"], dm, nm("rms1"))
    proj = mm("proj", h, ws("w_in", l), "nn", (F32,))
    ya = mixer_a_fwd(proj, G["conv_w"], dm, nm("mixa"))
    ql = rms_fwd(proj, dm.o_ql // dm.QL, dm.QL, G["q_lat_norm"], dm, nm("rms_q"))
    kl = rms_fwd(proj, dm.o_kl // dm.KL, dm.KL, G["kv_lat_norm"], dm, nm("rms_kv"))
    q0 = mm("uq", ql, ws("w_uq", l), "nn", (F32,))
    kv0 = mm("ukv", kl, ws("w_ukv", l), "nn", (F32,))
    comm = ws.comm(nm("qkprep"))
    if comm is None:
        q_s, k_s, v_s = qk_prep_fwd(q0, kv0, proj, G["q_norm"], G["k_norm"], tabs[0], tabs[1], dm, nm("qkprep"))
    else:
        (q_s, k_s, v_s), couts = qk_prep_fwd(q0, kv0, proj, G["q_norm"], G["k_norm"], tabs[0], tabs[1], dm, nm("qkprep"),
                                            comm=comm)
        ws.arrived(nm("qkprep"), couts)
    comm = ws.comm(nm("attn"))
    if comm is None:
        o, ob, lse = attn_fwd(q_s, k_s, v_s, dm, nm("attn"))
    else:
        (o, ob, lse), couts = attn_fwd(q_s, k_s, v_s, dm, nm("attn"), comm=comm)
        ws.arrived(nm("attn"), couts)
    pooled, mixed, yc = pool_fwd(proj, ws("pool_w", l), G["pool_scale"], dm, nm("pool"))
    A = mm("br_a", ya, ws("w_branch_a", l), "nn", (F32,))
    B = mm("br_b", ob, ws("w_branch_b", l), "nn", (F32,))
    C = mm("br_c", yc, ws("w_branch_c", l), "nn", (F32,))
    merged = merge_fwd(proj, A, B, C, dm, nm("merge"))
    x1 = mm("wo", merged, ws("w_o", l), "nn", (F32,), extras=(xin,), epi=lambda acc, r: (acc + r,))
    h2 = rms_fwd(x1, 0, D, G["mlp_norm"], dm, nm("rms2"))
    up, act = mm("up", h2, ws("w_up", l), "nn", (F32, BF16), epi=lambda acc: (acc, jnp.square(jnp.maximum(acc, 0.0))))
    x2 = mm("down", act, ws("w_down", l), "nn", (F32,), extras=(x1,), epi=lambda acc, r: (acc + r,))
    saved = dict(xin=xin, h=h, proj=proj, ya=ya, ql=ql, kl=kl, q0=q0, kv0=kv0, q_s=q_s, k_s=k_s, v_s=v_s, o=o, ob=ob,
                 lse=lse, pooled=pooled, mixed=mixed, yc=yc, A=A, B=B, C=C, merged=merged, x1=x1, h2=h2, up=up, act=act)
    return x2, saved


def _layer_bwd(dx2, dx2b, S, l, ws, G, tabs, dm, pre_attn, late=None, first=None):
    nm = lambda s: f"l{l}_b_{s}"
    D, T = dm.D, dm.T
    g = {}
    d_up = matmul(dx2b, ws("w_down", l), "nt", (BF16,), nm("d_act"), extras=(S["up"],),
                  epi=lambda acc, up: (acc * (2.0 * jnp.maximum(up, 0.0)),), comm=first[0] if first else None)
    if first:
        d_up, couts = d_up
        first[1](couts)
    g["w_down"] = matmul(S["act"], dx2b, "tn", (BF16,), nm("g_down"))
    g["w_up"] = matmul(S["h2"], d_up, "tn", (BF16,), nm("g_up"))
    dh2 = matmul(d_up, ws("w_up", l), "nt", (F32,), nm("d_h2"))
    dx1, dx1b, g["mlp_norm"] = rms_bwd(S["x1"], 0, D, G["mlp_norm"], dh2, dx2, dm, nm("rms2"))
    dmerged = matmul(dx1b, ws("w_o", l), "nt", (F32,), nm("d_merged"))
    g["w_o"] = matmul(S["merged"], dx1b, "tn", (BF16,), nm("g_o"))
    dA, dB, dC, dl0, dl1, dl2 = merge_bwd(S["proj"], S["A"], S["B"], S["C"], dmerged, dm, nm("merge"))
    dya = matmul(dA, ws("w_branch_a", l), "nt", (F32,), nm("d_ya"))
    g["w_branch_a"] = matmul(S["ya"], dA, "tn", (BF16,), nm("g_a"))
    dyb = matmul(dB, ws("w_branch_b", l), "nt", (F32,), nm("d_yb"))
    g["w_branch_b"] = matmul(S["ob"], dB, "tn", (BF16,), nm("g_b"))
    dyc = matmul(dC, ws("w_branch_c", l), "nt", (F32,), nm("d_yc"))
    g["w_branch_c"] = matmul(S["yc"], dC, "tn", (BF16,), nm("g_c"))
    du, db, dc, g["conv_w"] = mixer_a_bwd(S["proj"], G["conv_w"], dya, dm, nm("mixa"))
    dpool, g["pool_w"], g["pool_scale"] = pool_bwd(dyc, S["mixed"], S["pooled"], ws("pool_w", l), G["pool_scale"], dm, nm("pool"))
    delta, dob = attn_delta(dyb, S["o"], dm, nm("delta"))
    nq = T // dm.tq
    comm_dq, after_dq = pre_attn(g)
    dq, couts = attn_bwd_dq(S["q_s"], S["k_s"], S["v_s"], dob, S["lse"], delta, dm, nm("attn_dq"), comm=comm_dq)
    comm_dkv, done_dkv = after_dq(couts)
    dkv = attn_bwd_dkv(S["q_s"], S["k_s"], S["v_s"], dob, S["lse"].reshape(dm.H, nq, 1, dm.tq),
                       delta.reshape(dm.H, nq, 1, dm.tq), dm, nm("attn_dkv"), comm=comm_dkv)
    if comm_dkv is not None:
        dkv, couts = dkv
        done_dkv(couts)
    dk, dv = dkv
    dq0, dkv0, dkr, g["q_norm"], g["k_norm"] = qk_prep_bwd(S["q0"], S["kv0"], S["proj"], G["q_norm"], G["k_norm"],
                                                            tabs[0], tabs[1], dq, dk, dv, dm, nm("qkprep"))
    dql = matmul(dq0, ws("w_uq", l), "nt", (F32,), nm("d_ql"))
    g["w_uq"] = matmul(S["ql"], dq0, "tn", (BF16,), nm("g_uq"))
    dkl = matmul(dkv0, ws("w_ukv", l), "nt", (F32,), nm("d_kl"))
    g["w_ukv"] = matmul(S["kl"], dkv0, "tn", (BF16,), nm("g_ukv"))
    _, dqlat, g["q_lat_norm"] = rms_bwd(S["proj"], dm.o_ql // dm.QL, dm.QL, G["q_lat_norm"], dql, None, dm, nm("rms_q"))
    _, dkvlat, g["kv_lat_norm"] = rms_bwd(S["proj"], dm.o_kl // dm.KL, dm.KL, G["kv_lat_norm"], dkl, None, dm, nm("rms_kv"))
    dproj = jnp.concatenate([dl0, dl1, dl2, du, db, dc, dqlat, dkvlat, dpool, dkr.astype(BF16),
                             jnp.zeros((T, 128), BF16)], axis=1)
    g["w_in"] = matmul(S["h"], dproj, "tn", (BF16,), nm("g_in"))
    comm_dh, done_dh = late(g) if late is not None else (None, None)
    dh = matmul(dproj, ws("w_in", l), "nt", (F32,), nm("d_h"), comm=comm_dh)
    if comm_dh is not None:
        dh, couts = dh
        done_dh(couts)
    dx, dxb, g["attn_norm"] = rms_bwd(S["xin"], 0, D, G["attn_norm"], dh, dx1, dm, nm("rms1"))
    return dx, dxb, g


WEIGHTS = ("meta_tokens", "attn_norm", "w_in", "conv_w", "q_lat_norm", "kv_lat_norm", "w_uq", "w_ukv", "q_norm", "k_norm",
           "pool_w", "pool_scale", "w_branch_a", "w_branch_b", "w_branch_c", "w_o", "mlp_norm", "w_up", "w_down")
SMALL = tuple(n for n in WEIGHTS if n not in BIG)


def kernel(x, meta_tokens, attn_norm, w_in, conv_w, q_lat_norm, kv_lat_norm, w_uq, w_ukv, q_norm, k_norm, pool_w, pool_scale, w_branch_a, w_branch_b, w_branch_c, w_o, mlp_norm, w_up, w_down, loss_target, m_meta_tokens, m_attn_norm, m_w_in, m_conv_w, m_q_lat_norm, m_kv_lat_norm, m_w_uq, m_w_ukv, m_q_norm, m_k_norm, m_pool_w, m_pool_scale, m_w_branch_a, m_w_branch_b, m_w_branch_c, m_w_o, m_mlp_norm, m_w_up, m_w_down, v_meta_tokens, v_attn_norm, v_w_in, v_conv_w, v_q_lat_norm, v_kv_lat_norm, v_w_uq, v_w_ukv, v_q_norm, v_k_norm, v_pool_w, v_pool_scale, v_w_branch_a, v_w_branch_b, v_w_branch_c, v_w_o, v_mlp_norm, v_w_up, v_w_down):
    w = dict(meta_tokens=meta_tokens, attn_norm=attn_norm, w_in=w_in, conv_w=conv_w, q_lat_norm=q_lat_norm,
             kv_lat_norm=kv_lat_norm, w_uq=w_uq, w_ukv=w_ukv, q_norm=q_norm, k_norm=k_norm, pool_w=pool_w,
             pool_scale=pool_scale, w_branch_a=w_branch_a, w_branch_b=w_branch_b, w_branch_c=w_branch_c, w_o=w_o,
             mlp_norm=mlp_norm, w_up=w_up, w_down=w_down)
    m = dict(meta_tokens=m_meta_tokens, attn_norm=m_attn_norm, w_in=m_w_in, conv_w=m_conv_w, q_lat_norm=m_q_lat_norm,
             kv_lat_norm=m_kv_lat_norm, w_uq=m_w_uq, w_ukv=m_w_ukv, q_norm=m_q_norm, k_norm=m_k_norm, pool_w=m_pool_w,
             pool_scale=m_pool_scale, w_branch_a=m_w_branch_a, w_branch_b=m_w_branch_b, w_branch_c=m_w_branch_c, w_o=m_w_o,
             mlp_norm=m_mlp_norm, w_up=m_w_up, w_down=m_w_down)
    v = dict(meta_tokens=v_meta_tokens, attn_norm=v_attn_norm, w_in=v_w_in, conv_w=v_conv_w, q_lat_norm=v_q_lat_norm,
             kv_lat_norm=v_kv_lat_norm, w_uq=v_w_uq, w_ukv=v_w_ukv, q_norm=v_q_norm, k_norm=v_k_norm, pool_w=v_pool_w,
             pool_scale=v_pool_scale, w_branch_a=v_w_branch_a, w_branch_b=v_w_branch_b, w_branch_c=v_w_branch_c, w_o=v_w_o,
             mlp_norm=v_mlp_norm, w_up=v_w_up, w_down=v_w_down)
    L = attn_norm.shape[0]
    assert L == 2, "the gather / reduce schedule below is written for two layers"
    seq, D = x.shape[1], x.shape[2]
    n_meta = meta_tokens.shape[0]
    dm = Dims(D, seq, n_meta)
    T = dm.T
    me = 4 * lax.axis_index("x") + 2 * lax.axis_index("y") + lax.axis_index("c")
    core = lax.axis_index("c").astype(jnp.int32).reshape(1)

    plan = {
        "l0_proj": [("w_uq", 0), ("w_ukv", 0), ("pool_w", 0), ("pool_w", 1), ("w_up", 0)],
        "l0_qkprep": [("w_branch_a", 0), ("w_branch_b", 0), ("w_branch_c", 0), ("w_o", 0)],
        "l0_attn": [("w_down", 0), ("w_in", 1)],
        "l0_up": [("w_up", 1)],
        "l0_down": [("w_down", 1)],
        "l1_proj": [("w_uq", 1), ("w_ukv", 1), ("w_branch_a", 1), ("w_branch_b", 1), ("w_branch_c", 1), ("w_o", 1)],
    }
    ws = _Weights(w, dm, plan, {"l0_qkprep": "l0_attn", "l0_attn": "l0_up", "l0_up": "l0_down", "l0_down": "l1_proj"})
    first = [("w_in", 0)]
    ws.put(first, all_gather(ws.blocks(first), "gather_first"))
    st_small = all_gather([w["meta_tokens"], w["conv_w"]], "gather_small")
    meta_full = _full_from_stacked("meta_tokens", st_small[0])
    conv_full = _full_from_stacked("conv_w", st_small[1])
    pad_gain = lambda gn: jnp.pad(gn, (0, dm.HP - dm.QKH))

    def gains(l):
        G = {n: w[n][l][None, :] for n in ("attn_norm", "q_lat_norm", "kv_lat_norm", "pool_scale", "mlp_norm")}
        G["q_norm"], G["k_norm"] = pad_gain(w["q_norm"][l])[None, :], pad_gain(w["k_norm"][l])[None, :]
        G["conv_w"] = conv_full[l]
        return G

    Gs = [gains(l) for l in range(L)]

    pos = jnp.arange(dm.T_real, dtype=F32)
    inv = 10000.0 ** (-jnp.arange(0, dm.ROPE, 2, dtype=F32) / dm.ROPE)
    ang = pos[:, None] * inv[None, :]
    zpad = jnp.zeros((dm.T_real, LANES - dm.ROPE), F32)
    rows = ((0, T - dm.T_real), (0, 0))
    tabs = (jnp.pad(jnp.concatenate([jnp.cos(ang), jnp.cos(ang), zpad], 1), rows),
            jnp.pad(jnp.concatenate([jnp.sin(ang), jnp.sin(ang), zpad], 1), rows))

    xs = jnp.concatenate([meta_full, x[0], jnp.zeros((T - dm.T_real, D), F32)], axis=0)
    target = jnp.pad(loss_target[0], ((n_meta, T - dm.T_real), (0, 0)))
    saved = []
    for l in range(L):
        xs, S = _layer_fwd(xs, l, ws, Gs[l], tabs, dm)
        saved.append(S)
    dx, dxb, loss_acc = loss_head(xs, target, dm, "loss_head")

    red = _Reducer(dm, core)
    early = lambda l: [(n, l) for n in ("w_down", "w_up", "w_o", "w_branch_a", "w_branch_b", "w_branch_c")]
    late = lambda l: [(n, l) for n in ("w_uq", "w_ukv", "w_in")]
    grads = [None] * L

    def pre_attn_1(g):
        parts = red.parts(early(1), g)

        def after(got):
            return reduce_ici(red.add(early(1), parts, got)), lambda couts: red.put(early(1), couts)

        return sibling_hosted(parts), after

    dx, dxb, grads[1] = _layer_bwd(dx, dxb, saved[1], 1, ws, Gs[1], tabs, dm, pre_attn_1)
    parts_late1 = red.parts(late(1), grads[1])
    chip_late1 = []

    def pre_attn_0(g):
        parts = red.parts(early(0), g)
        both, split = merge_hosted(reduce_ici(chip_late1[0]), sibling_hosted(parts))

        def after(couts):
            summed_late1, got = split(couts)
            red.put(late(1), summed_late1)
            return reduce_ici(red.add(early(0), parts, got)), lambda couts: red.put(early(0), couts)

        return both, after

    def late_0(g):
        return reduce_ici(red.prepare("l0", late(0), g)), lambda couts: red.put(late(0), couts)

    dx, dxb, grads[0] = _layer_bwd(
        dx, dxb, saved[0], 0, ws, Gs[0], tabs, dm, pre_attn_0, late_0,
        first=(sibling_hosted(parts_late1), lambda got: chip_late1.append(red.add(late(1), parts_late1, got))))
    grad_x = dx[n_meta:dm.T_real][None]
    summed = [jnp.concatenate([red.q[(n, l)] for l in range(L)], axis=1) for n in BIG]

    out_g, out_d, out_m, out_v = {}, {}, {}, {}

    def update(n, parts3):
        shp = w[n].shape
        C = shp[-1]
        res = adamw(w[n].reshape(-1, C), m[n].reshape(-1, C), v[n].reshape(-1, C), parts3, f"adamw_{n}")
        out_g[n], out_d[n], out_m[n], out_v[n] = [r.reshape(shp) for r in res]

    for n, q in zip(BIG, summed):
        update(n, q.reshape(4, -1, q.shape[-1]))

    small_full = {
        "meta_tokens": dx[:n_meta],
        "conv_w": jnp.stack([grads[l]["conv_w"] for l in range(L)]),
        "pool_w": jnp.stack([grads[l]["pool_w"] for l in range(L)]),
        "q_norm": jnp.stack([grads[l]["q_norm"][0, :dm.QKH] for l in range(L)]),
        "k_norm": jnp.stack([grads[l]["k_norm"][0, :dm.QKH] for l in range(L)]),
    }
    for n in ("attn_norm", "q_lat_norm", "kv_lat_norm", "pool_scale", "mlp_norm"):
        small_full[n] = jnp.stack([grads[l][n][0] for l in range(L)])
    flat = jnp.concatenate([small_full[n].reshape(-1) for n in SMALL] + [loss_acc[0, :1]])
    n_flat = flat.shape[0]
    rows_small = -(-n_flat // (8 * LANES)) * 8
    flat = jnp.pad(flat, (0, rows_small * LANES - n_flat)).reshape(rows_small, LANES)
    total = sum_parts(all_gather([flat], "gather_small_grads")[0], "sum_small").reshape(-1)
    off = 0
    for n in SMALL:
        size = math.prod(small_full[n].shape)
        gsum = total[off:off + size].reshape(small_full[n].shape)
        off += size
        if n in ("meta_tokens", "conv_w"):
            blk = w[n].shape[-1]
            gsum = lax.dynamic_slice_in_dim(gsum, me * blk, blk, axis=gsum.ndim - 1)
        elif n == "pool_w":
            blk = w[n].shape[2]
            gsum = lax.dynamic_slice_in_dim(gsum, me * blk, blk, axis=2)
        update(n, gsum.reshape(1, -1, gsum.shape[-1]))
    loss = total[off]

    return (loss, grad_x, *[out_g[n] for n in WEIGHTS], *[out_d[n] for n in WEIGHTS],
            *[out_m[n] for n in WEIGHTS], *[out_v[n] for n in WEIGHTS])
```

```python
import functools
import math

import jax
import jax.numpy as jnp
from jax import lax
from jax.experimental import pallas as pl
from jax.experimental.pallas import tpu as pltpu

F32 = jnp.float32
BF16 = jnp.bfloat16

VMEM_LIMIT_BYTES = 56 * 1024 * 1024
LANES = 128
EPS = 1e-6
HALO = 16

ADAM_LR = 0.001
ADAM_B1 = 0.9
ADAM_B2 = 0.999
ADAM_EPS = 1e-08
ADAM_WD = 0.01
ADAM_STEP = 10


def _params(sem):
    return pltpu.CompilerParams(dimension_semantics=sem, vmem_limit_bytes=VMEM_LIMIT_BYTES)


def _pick(n, prefs):
    for p in prefs:
        if p <= n and n % p == 0:
            return p
    return n


_MESH = pl.DeviceIdType.MESH
_HBM = pl.BlockSpec(memory_space=pltpu.HBM)


def _place():
    return lax.axis_index("x"), lax.axis_index("y"), lax.axis_index("c")


class Hosted:
    def __init__(self, ins, out_shapes, sems, start, finish, aliases=None):
        self.ins, self.out_shapes, self.sems, self.start, self.finish = ins, out_shapes, sems, start, finish
        self.aliases = aliases or {}


def _hosted_call(body, *, name, grid, in_specs, out_specs, out_shape, scratch_shapes, semantics, args, comm):
    n_in, n_out, n_scr = len(in_specs), len(out_specs), len(scratch_shapes)
    if comm is None:
        outs = pl.pallas_call(body, name=name, grid=grid, in_specs=in_specs, out_specs=out_specs, out_shape=out_shape,
                              scratch_shapes=scratch_shapes, compiler_params=_params(semantics))(*args)
        return list(outs), []
    ci, co = len(comm.ins), len(comm.out_shapes)

    def hosting(*refs):
        ins, cins = refs[:n_in], refs[n_in:n_in + ci]
        outs = refs[n_in + ci:n_in + ci + n_out]
        couts = refs[n_in + ci + n_out:n_in + ci + n_out + co]
        scr = refs[n_in + ci + n_out + co:n_in + ci + n_out + co + n_scr]
        csems = refs[n_in + ci + n_out + co + n_scr:]
        ids = [pl.program_id(d) for d in range(len(grid))]
        first = functools.reduce(jnp.logical_and, [i == 0 for i in ids])
        last = functools.reduce(jnp.logical_and, [i == g - 1 for i, g in zip(ids, grid)])

        @pl.when(first)
        def _():
            comm.start(cins, couts, csems)

        body(*ins, *outs, *scr)

        @pl.when(last)
        def _():
            comm.finish(cins, couts, csems)

    outs = pl.pallas_call(
        hosting, name=name, grid=grid,
        in_specs=list(in_specs) + [_HBM] * ci, out_specs=list(out_specs) + [_HBM] * co,
        out_shape=list(out_shape) + list(comm.out_shapes),
        scratch_shapes=list(scratch_shapes) + list(comm.sems),
        input_output_aliases={n_in + i: n_out + o for i, o in comm.aliases.items()},
        compiler_params=_params(("arbitrary",) * len(grid)),
    )(*args, *comm.ins)
    return list(outs[:n_out]), list(outs[n_out:])


MXU_FLOPS = 750e12
HBM_BYTES_PER_S = 3.0e12
ACC_RMW_BYTES_PER_S = 8e12
GRID_STEP_S = 0.4e-6
VMEM_COMPILER_RESERVE_BYTES = 8 * 1024 * 1024
MAX_TILE_ROWS, MAX_TILE_COLS = 2112, 2304


def _divisors(n, step):
    return [d for d in range(step, n + 1, step) if n % d == 0]


def _matmul_tiles(M, N, K, mode, out_sizes, n_extra):
    budget = VMEM_LIMIT_BYTES - VMEM_COMPILER_RESERVE_BYTES
    best = None
    for tk in _divisors(K, 16 if mode == "tn" else LANES):
        for tm in _divisors(M, LANES if mode == "tn" else 16):
            if tm > MAX_TILE_ROWS:
                continue
            for tn in _divisors(N, LANES):
                if tn > MAX_TILE_COLS:
                    continue
                need = 4 * tm * tk + 4 * tk * tn + 4 * tm * tn + 2 * tm * tn * sum(out_sizes) + 8 * n_extra * tm * tn
                if need > budget:
                    continue
                nk = K // tk
                steps = (M // tm) * (N // tn) * nk
                t_mxu = 2 * M * N * K / MXU_FLOPS + (nk > 1) * (8 * M * N * nk) / ACC_RMW_BYTES_PER_S
                t_hbm = (2 * M * K * (N // tn) + 2 * K * N * (M // tm) + M * N * (sum(out_sizes) + 4 * n_extra)) / HBM_BYTES_PER_S
                t = max(t_mxu, t_hbm) + steps * GRID_STEP_S
                if best is None or t < best[0]:
                    best = (t, tm, tn, tk)
    assert best is not None, (M, N, K, mode)
    return best[1:]


def matmul(a, b, mode, out_dtypes, name, extras=(), epi=None, tm=None, tn=None, tk=None, comm=None):
    if mode == "nn":
        (M, K), (K2, N) = a.shape, b.shape
    elif mode == "nt":
        (M, K), (N, K2) = a.shape, b.shape
    else:
        (K, M), (K2, N) = a.shape, b.shape
    assert K == K2, (a.shape, b.shape, mode)
    if not (tm and tn and tk):
        tm, tn, tk = _matmul_tiles(M, N, K, mode, [jnp.dtype(d).itemsize for d in out_dtypes], len(extras))
    nk = K // tk
    dims = {"nn": (((1,), (0,)), ((), ())), "nt": (((1,), (1,)), ((), ())), "tn": (((0,), (0,)), ((), ()))}[mode]
    n_extra, n_out = len(extras), len(out_dtypes)

    def body(*refs):
        a_ref, b_ref = refs[0], refs[1]
        extra_refs = refs[2:2 + n_extra]
        out_refs = refs[2 + n_extra:2 + n_extra + n_out]

        def finish(acc):
            outs = (acc,) if epi is None else epi(acc, *[r[...] for r in extra_refs])
            for o_ref, o in zip(out_refs, outs):
                o_ref[...] = o.astype(o_ref.dtype)

        part = lax.dot_general(a_ref[...], b_ref[...], dims, preferred_element_type=F32)
        if nk == 1:
            finish(part)
            return
        acc_ref = refs[-1]
        k = pl.program_id(2)

        @pl.when(k == 0)
        def _():
            acc_ref[...] = part

        @pl.when(k > 0)
        def _():
            acc_ref[...] += part

        @pl.when(k == nk - 1)
        def _():
            finish(acc_ref[...])

    a_spec = {"nn": pl.BlockSpec((tm, tk), lambda i, j, k: (i, k)),
              "nt": pl.BlockSpec((tm, tk), lambda i, j, k: (i, k)),
              "tn": pl.BlockSpec((tk, tm), lambda i, j, k: (k, i))}[mode]
    b_spec = {"nn": pl.BlockSpec((tk, tn), lambda i, j, k: (k, j)),
              "nt": pl.BlockSpec((tn, tk), lambda i, j, k: (j, k)),
              "tn": pl.BlockSpec((tk, tn), lambda i, j, k: (k, j))}[mode]
    o_spec = pl.BlockSpec((tm, tn), lambda i, j, k: (i, j))
    outs, couts = _hosted_call(
        body, name=name, grid=(M // tm, N // tn, nk),
        in_specs=[a_spec, b_spec] + [o_spec] * n_extra,
        out_specs=[o_spec] * n_out,
        out_shape=[jax.ShapeDtypeStruct((M, N), d) for d in out_dtypes],
        scratch_shapes=[pltpu.VMEM((tm, tn), F32)] if nk > 1 else [],
        semantics=("parallel", "parallel", "arbitrary"), args=(a, b, *extras), comm=comm)
    res = outs[0] if n_out == 1 else outs
    return res if comm is None else (res, couts)


class Dims:
    def __init__(self, d_model, seq, n_meta):
        self.D = d_model
        self.n_meta = n_meta
        self.T_real = seq + n_meta
        self.T = -(-self.T_real // LANES) * LANES
        self.H = d_model // 128
        self.DC = d_model // 2
        self.DP = d_model // 2
        self.PG = self.DP // 4
        self.QL = 512
        self.KL = 512
        self.ROPE = 64
        self.NOPE = 128
        self.QKH = 192
        self.HP = 256
        self.DFF = 4 * d_model
        self.o_gate = 0
        self.o_u = 3 * d_model
        self.o_b = self.o_u + self.DC
        self.o_c = self.o_b + self.DC
        self.o_ql = self.o_c + self.DC
        self.o_kl = self.o_ql + self.QL
        self.o_pool = self.o_kl + self.KL
        self.o_rope = self.o_pool + self.DP
        self.NIN = self.o_rope + 256
        self.tr = _pick(self.T, (384, 256, 128))
        self.tq = _pick(self.T, (384, 256, 128))


def _row_ids(i, tr):
    return i * tr + lax.broadcasted_iota(jnp.int32, (tr, 1), 0)


def rms_fwd(x, col_block, width, g, dm, name):
    tr = dm.tr

    def body(x_ref, g_ref, y_ref):
        xv = x_ref[...]
        r = lax.rsqrt(jnp.mean(xv * xv, axis=-1, keepdims=True) + EPS)
        y_ref[...] = (xv * r * g_ref[...]).astype(y_ref.dtype)

    return pl.pallas_call(
        body, name=name, grid=(dm.T // tr,),
        in_specs=[pl.BlockSpec((tr, width), lambda i: (i, col_block)), pl.BlockSpec((1, width), lambda i: (0, 0))],
        out_specs=pl.BlockSpec((tr, width), lambda i: (i, 0)),
        out_shape=jax.ShapeDtypeStruct((dm.T, width), BF16),
        compiler_params=_params(("parallel",)),
    )(x, g.reshape(1, width))


def rms_bwd(x, col_block, width, g, dy, dres, dm, name):
    tr = dm.tr
    has_res = dres is not None

    def body(*refs):
        if has_res:
            x_ref, g_ref, dy_ref, dres_ref, dx_ref, dxb_ref, dg_ref = refs
        else:
            x_ref, g_ref, dy_ref, dx_ref, dxb_ref, dg_ref = refs
        xv, dyv = x_ref[...], dy_ref[...]
        r = lax.rsqrt(jnp.mean(xv * xv, axis=-1, keepdims=True) + EPS)
        gdy = dyv * g_ref[...]
        dx = r * gdy - xv * (r * r * r) * jnp.mean(xv * gdy, axis=-1, keepdims=True)
        if has_res:
            dx = dx + dres_ref[...]
        dx_ref[...] = dx
        dxb_ref[...] = dx.astype(BF16)

        @pl.when(pl.program_id(0) == 0)
        def _():
            dg_ref[...] = jnp.zeros_like(dg_ref)

        dg_ref[...] += jnp.sum(dyv * xv * r, axis=0, keepdims=True)

    row = pl.BlockSpec((tr, width), lambda i: (i, 0))
    in_specs = [pl.BlockSpec((tr, width), lambda i: (i, col_block)), pl.BlockSpec((1, width), lambda i: (0, 0)), row]
    args = [x, g.reshape(1, width), dy]
    if has_res:
        in_specs.append(row)
        args.append(dres)
    return pl.pallas_call(
        body, name=name, grid=(dm.T // tr,),
        in_specs=in_specs,
        out_specs=[row, row, pl.BlockSpec((1, width), lambda i: (0, 0))],
        out_shape=[jax.ShapeDtypeStruct((dm.T, width), F32), jax.ShapeDtypeStruct((dm.T, width), BF16),
                   jax.ShapeDtypeStruct((1, width), F32)],
        compiler_params=_params(("arbitrary",)),
    )(*args)


def _fill_halo_buf(buf, src_fn, T, R, width):
    zeros = jnp.zeros((HALO, width), F32)
    buf[pl.ds(0, HALO), :] = zeros
    buf[pl.ds(HALO + T, HALO), :] = zeros

    def fill(r, c):
        r0 = pl.multiple_of(r * R, 8)
        buf[pl.ds(r0 + HALO, R), :] = src_fn(r0)
        return c

    lax.fori_loop(0, T // R, fill, 0)


def _back(win, sh):
    return pltpu.roll(win, sh, 0)


def _fwd(win, sh):
    return pltpu.roll(win, win.shape[0] - sh, 0)


def mixer_a_fwd(proj, conv_w, dm, name):
    T, cw = dm.T, 128
    R = dm.tr
    nb = dm.DC // cw

    def body(u_ref, b_ref, c_ref, w_ref, ya_ref, buf):
        _fill_halo_buf(buf, lambda r0: c_ref[pl.ds(r0, R), :] * u_ref[pl.ds(r0, R), :], T, R, cw)
        w0, w1, w2 = w_ref[0:1, :], w_ref[1:2, :], w_ref[2:3, :]

        def chunk(r, c):
            r0 = pl.multiple_of(r * R, 8)
            win = buf[pl.ds(r0, R + HALO), :]
            cv = w2 * win + w1 * _back(win, 1) + w0 * _back(win, 2)
            ya_ref[pl.ds(r0, R), :] = (b_ref[pl.ds(r0, R), :] * cv[HALO:, :]).astype(BF16)
            return c

        lax.fori_loop(0, T // R, chunk, 0)

    col = lambda off: pl.BlockSpec((T, cw), lambda j: (0, off // cw + j))
    return pl.pallas_call(
        body, name=name, grid=(nb,),
        in_specs=[col(dm.o_u), col(dm.o_b), col(dm.o_c), pl.BlockSpec((3, cw), lambda j: (0, j))],
        out_specs=pl.BlockSpec((T, cw), lambda j: (0, j)),
        out_shape=jax.ShapeDtypeStruct((T, dm.DC), BF16),
        scratch_shapes=[pltpu.VMEM((T + 2 * HALO, cw), F32)],
        compiler_params=_params(("parallel",)),
    )(proj, proj, proj, conv_w)


def mixer_a_bwd(proj, conv_w, dya, dm, name):
    T, cw = dm.T, 128
    R = dm.tr
    nb = dm.DC // cw

    def body(u_ref, b_ref, c_ref, w_ref, dya_ref, du_ref, db_ref, dc_ref, dw_ref, sbuf, gbuf):
        _fill_halo_buf(sbuf, lambda r0: c_ref[pl.ds(r0, R), :] * u_ref[pl.ds(r0, R), :], T, R, cw)
        _fill_halo_buf(gbuf, lambda r0: dya_ref[pl.ds(r0, R), :] * b_ref[pl.ds(r0, R), :], T, R, cw)
        w0, w1, w2 = w_ref[0:1, :], w_ref[1:2, :], w_ref[2:3, :]

        def chunk(r, acc):
            a0, a1, a2 = acc
            r0 = pl.multiple_of(r * R, 8)
            swin = sbuf[pl.ds(r0, R + HALO), :]
            s0, s1, s2 = swin[HALO:, :], _back(swin, 1)[HALO:, :], _back(swin, 2)[HALO:, :]
            gwin = gbuf[pl.ds(r0 + HALO, R + HALO), :]
            g0, g1, g2 = gwin[:R, :], _fwd(gwin, 1)[:R, :], _fwd(gwin, 2)[:R, :]
            cv = w2 * s0 + w1 * s1 + w0 * s2
            ds = w2 * g0 + w1 * g1 + w0 * g2
            db_ref[pl.ds(r0, R), :] = (dya_ref[pl.ds(r0, R), :] * cv).astype(BF16)
            du_ref[pl.ds(r0, R), :] = (ds * c_ref[pl.ds(r0, R), :]).astype(BF16)
            dc_ref[pl.ds(r0, R), :] = (ds * u_ref[pl.ds(r0, R), :]).astype(BF16)
            a2 = a2 + jnp.sum(g0 * s0, axis=0, keepdims=True)
            a1 = a1 + jnp.sum(g0 * s1, axis=0, keepdims=True)
            a0 = a0 + jnp.sum(g0 * s2, axis=0, keepdims=True)
            return a0, a1, a2

        z = jnp.zeros((1, cw), F32)
        a0, a1, a2 = lax.fori_loop(0, T // R, chunk, (z, z, z))
        dw_ref[0:1, :] = a0
        dw_ref[1:2, :] = a1
        dw_ref[2:3, :] = a2

    col = lambda off: pl.BlockSpec((T, cw), lambda j: (0, off // cw + j))
    own = pl.BlockSpec((T, cw), lambda j: (0, j))
    o = jax.ShapeDtypeStruct((T, dm.DC), BF16)
    return pl.pallas_call(
        body, name=name, grid=(nb,),
        in_specs=[col(dm.o_u), col(dm.o_b), col(dm.o_c), pl.BlockSpec((3, cw), lambda j: (0, j)), own],
        out_specs=[own, own, own, pl.BlockSpec((3, cw), lambda j: (0, j))],
        out_shape=[o, o, o, jax.ShapeDtypeStruct((3, dm.DC), F32)],
        scratch_shapes=[pltpu.VMEM((T + 2 * HALO, cw), F32), pltpu.VMEM((T + 2 * HALO, cw), F32)],
        compiler_params=_params(("parallel",)),
    )(proj, proj, proj, conv_w, dya)


def _rope(x, C, S):
    return x * C + (pltpu.roll(x, 32, 1) - pltpu.roll(x, 96, 1)) * S


def _rope_t(dy, C, S):
    return dy * C + (pltpu.roll(dy, 96, 1) - pltpu.roll(dy, 32, 1)) * S


HP = 8


def qk_prep_fwd(q0, kv0, proj, qn, kn, C, S, dm, name, comm=None):
    T, H, tr = dm.T, dm.H, dm.tr
    inv = 1.0 / dm.QKH

    def body(q0_ref, kv_ref, kr_ref, qn_ref, kn_ref, c_ref, s_ref, q_ref, k_ref, v_ref):
        Cv, Sv = c_ref[...], s_ref[...]
        kb = kr_ref[...]
        kb2 = jnp.sum(kb * kb, -1, keepdims=True)
        for a in range(HP):
            lo, mid, hi = 256 * a, 256 * a + 128, 256 * (a + 1)
            qa, qb = q0_ref[:, lo:mid], q0_ref[:, mid:hi]
            r = lax.rsqrt((jnp.sum(qa * qa, -1, keepdims=True) + jnp.sum(qb * qb, -1, keepdims=True)) * inv + EPS)
            q_ref[:, lo:mid] = (qa * r * qn_ref[:, :128]).astype(BF16)
            q_ref[:, mid:hi] = _rope(qb * r * qn_ref[:, 128:], Cv, Sv).astype(BF16)
            ka = kv_ref[:, lo:mid]
            r = lax.rsqrt((jnp.sum(ka * ka, -1, keepdims=True) + kb2) * inv + EPS)
            k_ref[:, lo:mid] = (ka * r * kn_ref[:, :128]).astype(BF16)
            k_ref[:, mid:hi] = _rope(kb * r * kn_ref[:, 128:], Cv, Sv).astype(BF16)
            v_ref[:, 128 * a:128 * (a + 1)] = kv_ref[:, mid:hi].astype(BF16)

    head = pl.BlockSpec((tr, 256 * HP), lambda i, h: (i, h))
    gain = pl.BlockSpec((1, 256), lambda i, h: (0, 0))
    tab = pl.BlockSpec((tr, 128), lambda i, h: (i, 0))
    outs, couts = _hosted_call(
        body, name=name, grid=(T // tr, H // HP),
        in_specs=[head, head, pl.BlockSpec((tr, 128), lambda i, h: (i, dm.o_rope // 128)), gain, gain, tab, tab],
        out_specs=[head, head, pl.BlockSpec((tr, 128 * HP), lambda i, h: (i, h))],
        out_shape=[jax.ShapeDtypeStruct((T, H * 256), BF16), jax.ShapeDtypeStruct((T, H * 256), BF16),
                   jax.ShapeDtypeStruct((T, H * 128), BF16)],
        scratch_shapes=[], semantics=("parallel", "parallel"), args=(q0, kv0, proj, qn, kn, C, S), comm=comm)
    return outs if comm is None else (outs, couts)


def qk_prep_bwd(q0, kv0, proj, qn, kn, C, S, dq, dk, dv, dm, name):
    T, H, tr = dm.T, dm.H, dm.tr
    inv = 1.0 / dm.QKH

    def body(q0_ref, kv_ref, kr_ref, qn_ref, kn_ref, c_ref, s_ref, dq_ref, dk_ref, dv_ref,
             dq0_ref, dkv_ref, dkr_ref, dqn_ref, dkn_ref):
        i, h = pl.program_id(0), pl.program_id(1)
        Cv, Sv = c_ref[...], s_ref[...]

        def norm_bwd(xa, xb, ga, gb, dya, dyb):
            r = lax.rsqrt((jnp.sum(xa * xa, -1, keepdims=True) + jnp.sum(xb * xb, -1, keepdims=True)) * inv + EPS)
            dzb = _rope_t(dyb, Cv, Sv)
            gda, gdb = ga * dya, gb * dzb
            dot = (jnp.sum(xa * gda, -1, keepdims=True) + jnp.sum(xb * gdb, -1, keepdims=True)) * inv
            r3 = r * r * r
            dxa = r * gda - xa * r3 * dot
            dxb = r * gdb - xb * r3 * dot
            dga = jnp.sum(dya * xa * r, axis=0, keepdims=True)
            dgb = jnp.sum(dzb * xb * r, axis=0, keepdims=True)
            return dxa, dxb, dga, dgb

        @pl.when((i == 0) & (h == 0))
        def _():
            dqn_ref[...] = jnp.zeros_like(dqn_ref)
            dkn_ref[...] = jnp.zeros_like(dkn_ref)

        @pl.when(h == 0)
        def _():
            dkr_ref[...] = jnp.zeros_like(dkr_ref)

        for a in range(HP):
            lo, mid, hi = 256 * a, 256 * a + 128, 256 * (a + 1)
            dxa, dxb, dga, dgb = norm_bwd(q0_ref[:, lo:mid], q0_ref[:, mid:hi], qn_ref[:, :128], qn_ref[:, 128:],
                                          dq_ref[:, lo:mid], dq_ref[:, mid:hi])
            dq0_ref[:, lo:mid] = dxa.astype(BF16)
            dq0_ref[:, mid:hi] = dxb.astype(BF16)
            dqn_ref[:, :128] += dga
            dqn_ref[:, 128:] += dgb
            dxa, dxb, dga, dgb = norm_bwd(kv_ref[:, lo:mid], kr_ref[...], kn_ref[:, :128], kn_ref[:, 128:],
                                          dk_ref[:, lo:mid], dk_ref[:, mid:hi])
            dkv_ref[:, lo:mid] = dxa.astype(BF16)
            dkv_ref[:, mid:hi] = dv_ref[:, 128 * a:128 * (a + 1)].astype(BF16)
            dkn_ref[:, :128] += dga
            dkn_ref[:, 128:] += dgb
            dkr_ref[...] += dxb

    head = pl.BlockSpec((tr, 256 * HP), lambda i, h: (i, h))
    gain = pl.BlockSpec((1, 256), lambda i, h: (0, 0))
    tab = pl.BlockSpec((tr, 128), lambda i, h: (i, 0))
    return pl.pallas_call(
        body, name=name, grid=(T // tr, H // HP),
        in_specs=[head, head, pl.BlockSpec((tr, 128), lambda i, h: (i, dm.o_rope // 128)), gain, gain, tab, tab,
                  head, head, pl.BlockSpec((tr, 128 * HP), lambda i, h: (i, h))],
        out_specs=[head, head, tab, gain, gain],
        out_shape=[jax.ShapeDtypeStruct((T, H * 256), BF16), jax.ShapeDtypeStruct((T, H * 256), BF16),
                   jax.ShapeDtypeStruct((T, 128), F32), jax.ShapeDtypeStruct((1, 256), F32),
                   jax.ShapeDtypeStruct((1, 256), F32)],
        compiler_params=_params(("arbitrary", "arbitrary")),
    )(q0, kv0, proj, qn, kn, C, S, dq, dk, dv)


_NT = (((1,), (1,)), ((), ()))


def _causal_mask(t):
    return lax.broadcasted_iota(jnp.int32, (t, t), 0) >= lax.broadcasted_iota(jnp.int32, (t, t), 1)


def _causal_mask_t(t):
    return lax.broadcasted_iota(jnp.int32, (t, t), 0) <= lax.broadcasted_iota(jnp.int32, (t, t), 1)


HB = 2


def attn_fwd(q, k, v, dm, name, comm=None):
    T, H, tq = dm.T, dm.H, dm.tq
    scale = dm.QKH ** -0.5

    def body(q_ref, k_ref, v_ref, o_ref, ob_ref, lse_ref):
        qi = pl.program_id(1)

        def step(j0, w, carry, masked):
            ss = [lax.dot_general(q_ref[:, 256 * a:256 * (a + 1)], k_ref[pl.ds(j0, w), 256 * a:256 * (a + 1)], _NT,
                                  preferred_element_type=F32) for a in range(HB)]
            out = []
            for a in range(HB):
                m, l, acc = carry[a]
                s = ss[a] * scale
                if masked:
                    s = jnp.where(_causal_mask(tq), s, -jnp.inf)
                m_new = jnp.maximum(m, jnp.max(s, -1, keepdims=True))
                alpha = jnp.exp(m - m_new)
                p = jnp.exp(s - m_new)
                l = alpha * l + jnp.sum(p, -1, keepdims=True)
                acc = alpha * acc + jnp.dot(p.astype(BF16), v_ref[pl.ds(j0, w), 128 * a:128 * (a + 1)],
                                            preferred_element_type=F32)
                out.append((m_new, l, acc))
            return tuple(out)

        one = (jnp.full((tq, 1), -jnp.inf, F32), jnp.zeros((tq, 1), F32), jnp.zeros((tq, 128), F32))
        carry = lax.fori_loop(0, jnp.right_shift(qi, 1), lambda t, c: step(pl.multiple_of(t * 2 * tq, tq), 2 * tq, c, False), (one,) * HB)
        carry = lax.cond(jnp.bitwise_and(qi, 1) == 1, lambda c: step(pl.multiple_of((qi - 1) * tq, tq), tq, c, False), lambda c: c, carry)
        carry = step(pl.multiple_of(qi * tq, tq), tq, carry, True)
        for a in range(HB):
            m, l, acc = carry[a]
            o = acc / l
            o_ref[:, 128 * a:128 * (a + 1)] = o
            ob_ref[:, 128 * a:128 * (a + 1)] = o.astype(BF16)
            lse_ref[a] = m + jnp.log(l)

    outs, couts = _hosted_call(
        body, name=name, grid=(H // HB, T // tq),
        in_specs=[pl.BlockSpec((tq, 256 * HB), lambda h, i: (i, h)), pl.BlockSpec((T, 256 * HB), lambda h, i: (0, h)),
                  pl.BlockSpec((T, 128 * HB), lambda h, i: (0, h))],
        out_specs=[pl.BlockSpec((tq, 128 * HB), lambda h, i: (i, h)), pl.BlockSpec((tq, 128 * HB), lambda h, i: (i, h)),
                   pl.BlockSpec((HB, tq, 1), lambda h, i: (h, i, 0))],
        out_shape=[jax.ShapeDtypeStruct((T, H * 128), F32), jax.ShapeDtypeStruct((T, H * 128), BF16),
                   jax.ShapeDtypeStruct((H, T, 1), F32)],
        scratch_shapes=[], semantics=("parallel", "parallel"), args=(q, k, v), comm=comm)
    return outs if comm is None else (outs, couts)


def attn_delta(do, o, dm, name):
    T, H, tr = dm.T, dm.H, dm.tr

    def body(do_ref, o_ref, delta_ref, dob_ref):
        for a in range(H):
            d = do_ref[:, 128 * a:128 * (a + 1)]
            delta_ref[a] = jnp.sum(d * o_ref[:, 128 * a:128 * (a + 1)], -1, keepdims=True)
            dob_ref[:, 128 * a:128 * (a + 1)] = d.astype(BF16)

    blk = pl.BlockSpec((tr, 128 * H), lambda i: (i, 0))
    return pl.pallas_call(
        body, name=name, grid=(T // tr,),
        in_specs=[blk, blk],
        out_specs=[pl.BlockSpec((H, tr, 1), lambda i: (0, i, 0)), blk],
        out_shape=[jax.ShapeDtypeStruct((H, T, 1), F32), jax.ShapeDtypeStruct((T, H * 128), BF16)],
        compiler_params=_params(("parallel",)),
    )(do, o)


def attn_bwd_dq(q, k, v, do, lse, delta, dm, name, comm=None):
    T, H, tq = dm.T, dm.H, dm.tq
    scale = dm.QKH ** -0.5

    def body(q_ref, k_ref, v_ref, do_ref, lse_ref, delta_ref, dq_ref):
        qi = pl.program_id(1)

        def step(j0, w, dqs, masked):
            hk = lambda a: slice(256 * a, 256 * (a + 1))
            hv = lambda a: slice(128 * a, 128 * (a + 1))
            ss = [lax.dot_general(q_ref[:, hk(a)], k_ref[pl.ds(j0, w), hk(a)], _NT, preferred_element_type=F32)
                  for a in range(HB)]
            dps = [lax.dot_general(do_ref[:, hv(a)], v_ref[pl.ds(j0, w), hv(a)], _NT, preferred_element_type=F32)
                   for a in range(HB)]
            out = []
            for a in range(HB):
                p = jnp.exp(ss[a] * scale - lse_ref[a])
                if masked:
                    p = jnp.where(_causal_mask(tq), p, 0.0)
                ds = p * (dps[a] - delta_ref[a]) * scale
                out.append(dqs[a] + jnp.dot(ds.astype(BF16), k_ref[pl.ds(j0, w), hk(a)], preferred_element_type=F32))
            return tuple(out)

        dqs = lax.fori_loop(0, jnp.right_shift(qi, 1), lambda t, c: step(pl.multiple_of(t * 2 * tq, tq), 2 * tq, c, False),
                            (jnp.zeros((tq, 256), F32),) * HB)
        dqs = lax.cond(jnp.bitwise_and(qi, 1) == 1, lambda c: step(pl.multiple_of((qi - 1) * tq, tq), tq, c, False),
                       lambda c: c, dqs)
        dqs = step(pl.multiple_of(qi * tq, tq), tq, dqs, True)
        for a in range(HB):
            dq_ref[:, 256 * a:256 * (a + 1)] = dqs[a]

    stat = pl.BlockSpec((HB, tq, 1), lambda h, i: (h, i, 0))
    outs, couts = _hosted_call(
        body, name=name, grid=(H // HB, T // tq),
        in_specs=[pl.BlockSpec((tq, 256 * HB), lambda h, i: (i, h)), pl.BlockSpec((T, 256 * HB), lambda h, i: (0, h)),
                  pl.BlockSpec((T, 128 * HB), lambda h, i: (0, h)), pl.BlockSpec((tq, 128 * HB), lambda h, i: (i, h)),
                  stat, stat],
        out_specs=[pl.BlockSpec((tq, 256 * HB), lambda h, i: (i, h))],
        out_shape=[jax.ShapeDtypeStruct((T, H * 256), F32)],
        scratch_shapes=[], semantics=("parallel", "parallel"), args=(q, k, v, do, lse, delta), comm=comm)
    return outs[0] if comm is None else (outs[0], couts)


def attn_bwd_dkv(q, k, v, do, lse_rows, delta_rows, dm, name, comm=None):
    T, H, tq = dm.T, dm.H, dm.tq
    nq = T // tq
    scale = dm.QKH ** -0.5

    def body(q_ref, k_ref, v_ref, do_ref, lse_ref, delta_ref, dk_ref, dv_ref):
        kj = pl.program_id(1)

        def step(i, n, carry, masked):
            i0 = pl.multiple_of(i * tq, tq)
            w = n * tq
            hk = lambda a: slice(256 * a, 256 * (a + 1))
            hv = lambda a: slice(128 * a, 128 * (a + 1))
            row = lambda ref, a: ref[a, i] if n == 1 else jnp.concatenate([ref[a, i], ref[a, i + 1]], axis=1)
            sts = [lax.dot_general(k_ref[:, hk(a)], q_ref[pl.ds(i0, w), hk(a)], _NT, preferred_element_type=F32)
                   for a in range(HB)]
            dpts = [lax.dot_general(v_ref[:, hv(a)], do_ref[pl.ds(i0, w), hv(a)], _NT, preferred_element_type=F32)
                    for a in range(HB)]
            out = []
            for a in range(HB):
                dk, dv = carry[a]
                pt = jnp.exp(sts[a] * scale - row(lse_ref, a))
                if masked:
                    pt = jnp.where(_causal_mask_t(tq), pt, 0.0)
                dv = dv + jnp.dot(pt.astype(BF16), do_ref[pl.ds(i0, w), hv(a)], preferred_element_type=F32)
                dst = pt * (dpts[a] - row(delta_ref, a)) * scale
                dk = dk + jnp.dot(dst.astype(BF16), q_ref[pl.ds(i0, w), hk(a)], preferred_element_type=F32)
                out.append((dk, dv))
            return tuple(out)

        carry = step(kj, 1, ((jnp.zeros((tq, 256), F32), jnp.zeros((tq, 128), F32)),) * HB, True)
        rest = nq - 1 - kj
        carry = lax.fori_loop(0, jnp.right_shift(rest, 1), lambda t, c: step(kj + 1 + 2 * t, 2, c, False), carry)
        carry = lax.cond(jnp.bitwise_and(rest, 1) == 1, lambda c: step(nq - 1, 1, c, False), lambda c: c, carry)
        for a in range(HB):
            dk_ref[:, 256 * a:256 * (a + 1)] = carry[a][0]
            dv_ref[:, 128 * a:128 * (a + 1)] = carry[a][1]

    rows = pl.BlockSpec((HB, nq, 1, tq), lambda h, j: (h, 0, 0, 0))
    outs, couts = _hosted_call(
        body, name=name, grid=(H // HB, nq),
        in_specs=[pl.BlockSpec((T, 256 * HB), lambda h, j: (0, h)), pl.BlockSpec((tq, 256 * HB), lambda h, j: (j, h)),
                  pl.BlockSpec((tq, 128 * HB), lambda h, j: (j, h)), pl.BlockSpec((T, 128 * HB), lambda h, j: (0, h)),
                  rows, rows],
        out_specs=[pl.BlockSpec((tq, 256 * HB), lambda h, j: (j, h)), pl.BlockSpec((tq, 128 * HB), lambda h, j: (j, h))],
        out_shape=[jax.ShapeDtypeStruct((T, H * 256), F32), jax.ShapeDtypeStruct((T, H * 128), F32)],
        scratch_shapes=[], semantics=("parallel", "parallel"), args=(q, k, v, do, lse_rows, delta_rows), comm=comm)
    return outs if comm is None else (outs, couts)


def _window_sum(win, g, shift):
    s1 = win + shift(win, 1)
    s2 = s1 + shift(s1, 2)
    s3 = s2 + shift(s2, 4)
    s4 = s3 + shift(s3, 8)
    return jnp.where(g == 0, s1, jnp.where(g == 1, s2, jnp.where(g == 2, s3, s4)))


def _count(r0, R, g, T_unused=None):
    t = r0 + lax.broadcasted_iota(jnp.int32, (R, 1), 0)
    return jnp.minimum(t + 1, jnp.left_shift(2, g)).astype(F32)


def pool_fwd(proj, pw, ps, dm, name):
    T, PG, R = dm.T, dm.PG, dm.tr

    def body(x_ref, pw_ref, ps_ref, pooled_ref, mixed_ref, yc_ref, buf):
        g = pl.program_id(0)
        _fill_halo_buf(buf, lambda r0: x_ref[pl.ds(r0, R), :], T, R, PG)

        def chunk(r, c):
            r0 = pl.multiple_of(r * R, 8)
            win = buf[pl.ds(r0, R + HALO), :]
            ws = _window_sum(win, g, _back)[HALO:, :]
            pooled = (ws / _count(r0, R, g) - win[HALO:, :]).astype(BF16)
            pooled_ref[pl.ds(r0, R), :] = pooled
            mixed = jnp.dot(pooled, pw_ref[...], preferred_element_type=F32)
            mixed_ref[pl.ds(r0, R), :] = mixed
            yc_ref[pl.ds(r0, R), :] = (mixed * ps_ref[...]).astype(BF16)
            return c

        lax.fori_loop(0, T // R, chunk, 0)

    own = pl.BlockSpec((T, PG), lambda g: (0, g))
    return pl.pallas_call(
        body, name=name, grid=(4,),
        in_specs=[pl.BlockSpec((T, PG), lambda g: (0, dm.o_pool // PG + g)), pl.BlockSpec((None, PG, PG), lambda g: (g, 0, 0)),
                  pl.BlockSpec((1, PG), lambda g: (0, g))],
        out_specs=[own, own, own],
        out_shape=[jax.ShapeDtypeStruct((T, dm.DP), BF16), jax.ShapeDtypeStruct((T, dm.DP), F32),
                   jax.ShapeDtypeStruct((T, dm.DP), BF16)],
        scratch_shapes=[pltpu.VMEM((T + 2 * HALO, PG), F32)],
        compiler_params=_params(("parallel",)),
    )(proj, pw, ps)


def pool_bwd(dyc, mixed, pooled, pw, ps, dm, name):
    T, PG, R = dm.T, dm.PG, dm.tr
    _TN = (((0,), (0,)), ((), ()))

    def body(dyc_ref, mixed_ref, pooled_ref, pw_ref, ps_ref, dx_ref, dpw_ref, dps_ref, qbuf, dpbuf):
        g = pl.program_id(0)
        zeros = jnp.zeros((HALO, PG), F32)
        qbuf[pl.ds(0, HALO), :] = zeros
        qbuf[pl.ds(HALO + T, HALO), :] = zeros
        dpw_ref[...] = jnp.zeros_like(dpw_ref)

        def first(r, dps):
            r0 = pl.multiple_of(r * R, 8)
            dyc = dyc_ref[pl.ds(r0, R), :]
            dps = dps + jnp.sum(dyc * mixed_ref[pl.ds(r0, R), :], axis=0, keepdims=True)
            dmb = (dyc * ps_ref[...]).astype(BF16)
            dpw_ref[...] += lax.dot_general(pooled_ref[pl.ds(r0, R), :], dmb, _TN, preferred_element_type=F32)
            dp = lax.dot_general(dmb, pw_ref[...], _NT, preferred_element_type=F32)
            dpbuf[pl.ds(r0, R), :] = dp
            qbuf[pl.ds(r0 + HALO, R), :] = dp / _count(r0, R, g)
            return dps

        dps_ref[...] = lax.fori_loop(0, T // R, first, jnp.zeros((1, PG), F32))

        def second(r, c):
            r0 = pl.multiple_of(r * R, 8)
            win = qbuf[pl.ds(r0 + HALO, R + HALO), :]
            ws = _window_sum(win, g, _fwd)[:R, :]
            dx_ref[pl.ds(r0, R), :] = (ws - dpbuf[pl.ds(r0, R), :]).astype(BF16)
            return c

        lax.fori_loop(0, T // R, second, 0)

    own = pl.BlockSpec((T, PG), lambda g: (0, g))
    return pl.pallas_call(
        body, name=name, grid=(4,),
        in_specs=[own, own, own, pl.BlockSpec((None, PG, PG), lambda g: (g, 0, 0)), pl.BlockSpec((1, PG), lambda g: (0, g))],
        out_specs=[own, pl.BlockSpec((None, PG, PG), lambda g: (g, 0, 0)), pl.BlockSpec((1, PG), lambda g: (0, g))],
        out_shape=[jax.ShapeDtypeStruct((T, dm.DP), BF16), jax.ShapeDtypeStruct((4, PG, PG), F32),
                   jax.ShapeDtypeStruct((1, dm.DP), F32)],
        scratch_shapes=[pltpu.VMEM((T + 2 * HALO, PG), F32), pltpu.VMEM((T, PG), F32)],
        compiler_params=_params(("parallel",)),
    )(dyc, mixed, pooled, pw, ps)


def _sigmoid(x):
    return 1.0 / (1.0 + jnp.exp(-x))


def merge_fwd(proj, A, B, C, dm, name):
    T, D, tr, tc = dm.T, dm.D, dm.tr, 1024
    nc = D // tc

    def body(g0, g1, g2, a, b, c, out):
        out[...] = (_sigmoid(g0[...]) * a[...] + _sigmoid(g1[...]) * b[...] + _sigmoid(g2[...]) * c[...]).astype(BF16)

    gate = lambda k: pl.BlockSpec((tr, tc), lambda i, j: (i, k * nc + j))
    own = pl.BlockSpec((tr, tc), lambda i, j: (i, j))
    return pl.pallas_call(
        body, name=name, grid=(T // tr, nc),
        in_specs=[gate(0), gate(1), gate(2), own, own, own],
        out_specs=own,
        out_shape=jax.ShapeDtypeStruct((T, D), BF16),
        compiler_params=_params(("parallel", "parallel")),
    )(proj, proj, proj, A, B, C)


def merge_bwd(proj, A, B, C, dmerged, dm, name):
    T, D, tr, tc = dm.T, dm.D, dm.tr, 1024
    nc = D // tc

    def body(g0, g1, g2, a, b, c, dmr, da, db, dc, dl0, dl1, dl2):
        d = dmr[...]
        for g_ref, y_ref, dy_ref, dl_ref in ((g0, a, da, dl0), (g1, b, db, dl1), (g2, c, dc, dl2)):
            s = _sigmoid(g_ref[...])
            dy_ref[...] = (d * s).astype(BF16)
            dl_ref[...] = (d * y_ref[...] * s * (1.0 - s)).astype(BF16)

    gate = lambda k: pl.BlockSpec((tr, tc), lambda i, j: (i, k * nc + j))
    own = pl.BlockSpec((tr, tc), lambda i, j: (i, j))
    o = jax.ShapeDtypeStruct((T, D), BF16)
    return pl.pallas_call(
        body, name=name, grid=(T // tr, nc),
        in_specs=[gate(0), gate(1), gate(2), own, own, own, own],
        out_specs=[own] * 6,
        out_shape=[o] * 6,
        compiler_params=_params(("parallel", "parallel")),
    )(proj, proj, proj, A, B, C, dmerged)


def loss_head(y, target, dm, name):
    T, D, tr = dm.T, dm.D, dm.tr

    def body(y_ref, t_ref, dy_ref, dyb_ref, loss_ref):
        i = pl.program_id(0)
        t = _row_ids(i, tr)
        real = (t >= dm.n_meta) & (t < dm.T_real)
        err = jnp.where(real, y_ref[...] - t_ref[...], 0.0)
        dy = err * (1.0 / D)
        dy_ref[...] = dy
        dyb_ref[...] = dy.astype(BF16)

        @pl.when(i == 0)
        def _():
            loss_ref[...] = jnp.zeros_like(loss_ref)

        loss_ref[...] += 0.5 * jnp.sum(jnp.sum(err * err, axis=-1, keepdims=True) * (1.0 / D))

    row = pl.BlockSpec((tr, D), lambda i: (i, 0))
    return pl.pallas_call(
        body, name=name, grid=(T // tr,),
        in_specs=[row, row],
        out_specs=[row, row, pl.BlockSpec((8, LANES), lambda i: (0, 0))],
        out_shape=[jax.ShapeDtypeStruct((T, D), F32), jax.ShapeDtypeStruct((T, D), BF16),
                   jax.ShapeDtypeStruct((8, LANES), F32)],
        compiler_params=_params(("arbitrary",)),
    )(y, target)


def adamw(w, m, v, parts, name):
    R, C = w.shape
    P = parts.shape[0]
    br = R
    for cand in (512, 256, 128, 64, 32, 16, 8):
        if R % cand == 0 and cand * C * 4 <= (1 << 20):
            br = cand
            break
    if R * C * 4 <= (1 << 20):
        br = R

    def body(w_ref, m_ref, v_ref, p_ref, g_ref, d_ref, nm_ref, nv_ref):
        g = p_ref[0].astype(F32)
        for k in range(1, P):
            g = g + p_ref[k].astype(F32)
        mm = ADAM_B1 * m_ref[...] + (1.0 - ADAM_B1) * g
        vv = ADAM_B2 * v_ref[...] + (1.0 - ADAM_B2) * (g * g)
        m_hat = mm / (1.0 - ADAM_B1 ** ADAM_STEP)
        v_hat = vv / (1.0 - ADAM_B2 ** ADAM_STEP)
        g_ref[...] = g
        d_ref[...] = -ADAM_LR * (m_hat / (jnp.sqrt(v_hat) + ADAM_EPS) + ADAM_WD * w_ref[...])
        nm_ref[...] = mm
        nv_ref[...] = vv

    blk = pl.BlockSpec((br, C), lambda i: (i, 0))
    o = jax.ShapeDtypeStruct((R, C), F32)
    return pl.pallas_call(
        body, name=name, grid=(R // br,),
        in_specs=[blk, blk, blk, pl.BlockSpec((P, br, C), lambda i: (0, i, 0))],
        out_specs=[blk] * 4,
        out_shape=[o] * 4,
        compiler_params=_params(("parallel",)),
    )(w, m, v, parts)


def sum_parts(parts, name):
    P, R, C = parts.shape

    def body(p_ref, o_ref):
        acc = p_ref[0]
        for k in range(1, P):
            acc = acc + p_ref[k]
        o_ref[...] = acc

    return pl.pallas_call(
        body, name=name, grid=(1,),
        in_specs=[pl.BlockSpec((P, R, C), lambda i: (0, 0, 0))],
        out_specs=pl.BlockSpec((R, C), lambda i: (0, 0)),
        out_shape=jax.ShapeDtypeStruct((R, C), F32),
        compiler_params=_params(("arbitrary",)),
    )(parts)


def add_sibling(parts, got, core, name):
    _, _, R, C = parts.shape
    br = _pick(R, (1024, 512, 256, 128, 64, 32, 16))

    def body(c_ref, a_ref, b_ref, o_ref):
        o_ref[...] = (a_ref[...].astype(F32) + b_ref[...].astype(F32)).astype(BF16)

    blk = pl.BlockSpec((None, br, C), lambda ch, i, c: (ch, i, 0))
    return pl.pallas_call(
        body, name=name,
        grid_spec=pltpu.PrefetchScalarGridSpec(
            num_scalar_prefetch=1, grid=(4, R // br),
            in_specs=[pl.BlockSpec((None, None, br, C), lambda ch, i, c: (ch, c[0], i, 0)), blk],
            out_specs=blk),
        out_shape=jax.ShapeDtypeStruct((4, R, C), BF16), compiler_params=_params(("parallel", "parallel")),
    )(core, parts, got)


def all_gather(arrs, name):
    n = len(arrs)

    def body(*refs):
        ins, outs = refs[:n], refs[n:2 * n]
        send_sems, recv_sems, local_sems = refs[2 * n:]
        x, y, c = _place()
        me, sibling = (x, y, c), (x, y, 1 - c)
        chips = [(1 - x, y), (x, 1 - y), (1 - x, 1 - y)]

        def copy(a, k, block, to, src=None):
            px, py, pc = block
            dst = outs[a].at[4 * px + 2 * py + pc]
            return pltpu.make_async_remote_copy(
                src_ref=dst if src is None else src, dst_ref=dst,
                send_sem=send_sems.at[7 * a + k], recv_sem=recv_sems.at[7 * a + k],
                device_id=to, device_id_type=_MESH)

        started = []
        for a in range(n):
            mine = pltpu.make_async_copy(ins[a], outs[a].at[4 * x + 2 * y + c], local_sems.at[a])
            mine.start()
            started.append(mine)
        sends = []
        for a in range(n):
            sends.append(copy(a, 0, me, sibling, src=ins[a]))
            for j, chip in enumerate(chips):
                sends.append(copy(a, 1 + j, me, (*chip, c), src=ins[a]))
        for cp in sends:
            cp.start()
        for j, chip in enumerate(chips):
            for a in range(n):
                copy(a, 1 + j, (*chip, c), me).wait_recv()
                fwd = copy(a, 4 + j, (*chip, c), sibling)
                fwd.start()
                sends.append(fwd)
        for a in range(n):
            copy(a, 0, sibling, me).wait_recv()
            for j, chip in enumerate(chips):
                copy(a, 4 + j, (*chip, 1 - c), me).wait_recv()
        for cp in sends:
            cp.wait_send()
        for cp in started:
            cp.wait()

    outs = pl.pallas_call(
        body, name=name,
        in_specs=[_HBM] * n, out_specs=[_HBM] * n,
        out_shape=[jax.ShapeDtypeStruct((8,) + a.shape, a.dtype) for a in arrs],
        scratch_shapes=[pltpu.SemaphoreType.DMA((7 * n,)), pltpu.SemaphoreType.DMA((7 * n,)), pltpu.SemaphoreType.DMA((n,))],
    )(*arrs)
    return list(outs)


def sibling_exchange(arrs, name):
    n = len(arrs)

    def body(*refs):
        ins, got = refs[:n], refs[n:2 * n]
        send_sems, recv_sems = refs[2 * n:]
        x, y, c = _place()
        work = []
        for a in range(n):
            for ch in range(4):
                cp = pltpu.make_async_remote_copy(
                    src_ref=ins[a].at[ch, 1 - c], dst_ref=got[a].at[ch],
                    send_sem=send_sems.at[4 * a + ch], recv_sem=recv_sems.at[4 * a + ch],
                    device_id=(x, y, 1 - c), device_id_type=_MESH)
                cp.start()
                work.append(cp)
        for cp in work:
            cp.wait()

    outs = pl.pallas_call(
        body, name=name,
        in_specs=[_HBM] * n, out_specs=[_HBM] * n,
        out_shape=[jax.ShapeDtypeStruct((4,) + a.shape[2:], a.dtype) for a in arrs],
        scratch_shapes=[pltpu.SemaphoreType.DMA((4 * n,)), pltpu.SemaphoreType.DMA((4 * n,))],
    )(*arrs)
    return list(outs)


def _remote(src, dst, send_sem, recv_sem, to):
    return pltpu.make_async_remote_copy(src_ref=src, dst_ref=dst, send_sem=send_sem, recv_sem=recv_sem,
                                        device_id=to, device_id_type=_MESH)


def gather_ici(blocks):
    n = len(blocks)

    def copies(cins, couts, sems):
        send_sems, recv_sems, local_sems = sems
        x, y, c = _place()
        mine = 4 * x + 2 * y + c
        local, sends, recvs = [], [], []
        for a in range(n):
            local.append(pltpu.make_async_copy(cins[a], couts[a].at[mine], local_sems.at[a]))
            for j, (px, py) in enumerate([(1 - x, y), (x, 1 - y), (1 - x, 1 - y)]):
                k = 3 * a + j
                sends.append(_remote(cins[a], couts[a].at[mine], send_sems.at[k], recv_sems.at[k], (px, py, c)))
                recvs.append(_remote(cins[a], couts[a].at[4 * px + 2 * py + c], send_sems.at[k], recv_sems.at[k], (px, py, c)))
        return local, sends, recvs

    def start(cins, couts, sems):
        local, sends, _ = copies(cins, couts, sems)
        for cp in local + sends:
            cp.start()

    def finish(cins, couts, sems):
        local, sends, recvs = copies(cins, couts, sems)
        for cp in sends:
            cp.wait_send()
        for cp in recvs:
            cp.wait_recv()
        for cp in local:
            cp.wait()

    return Hosted(list(blocks), [jax.ShapeDtypeStruct((8,) + b.shape, b.dtype) for b in blocks],
                  [pltpu.SemaphoreType.DMA((3 * n,)), pltpu.SemaphoreType.DMA((3 * n,)), pltpu.SemaphoreType.DMA((n,))],
                  start, finish)


def fill_sibling(stks, name):
    n = len(stks)

    def body(*refs):
        ins, outs = refs[:n], refs[n:2 * n]
        send_sems, recv_sems = refs[2 * n:]
        x, y, c = _place()
        sends, recvs = [], []
        for a in range(n):
            for ch in range(4):
                k = 4 * a + ch
                sends.append(_remote(ins[a].at[2 * ch + c], outs[a].at[2 * ch + c], send_sems.at[k], recv_sems.at[k], (x, y, 1 - c)))
                recvs.append(_remote(ins[a].at[2 * ch + c], outs[a].at[2 * ch + 1 - c], send_sems.at[k], recv_sems.at[k], (x, y, 1 - c)))
        for cp in sends:
            cp.start()
        for cp in sends:
            cp.wait_send()
        for cp in recvs:
            cp.wait_recv()

    outs = pl.pallas_call(
        body, name=name,
        in_specs=[_HBM] * n, out_specs=[_HBM] * n,
        out_shape=[jax.ShapeDtypeStruct(s.shape, s.dtype) for s in stks],
        scratch_shapes=[pltpu.SemaphoreType.DMA((4 * n,)), pltpu.SemaphoreType.DMA((4 * n,))],
        input_output_aliases={a: a for a in range(n)},
    )(*stks)
    return list(outs)


def reduce_ici(arrs):
    n = len(arrs)

    def copies(cins, couts, sems):
        send_sems, recv_sems, local_sems = sems
        x, y, c = _place()
        my_chip = 2 * x + y
        local, sends, recvs = [], [], []
        for a in range(n):
            local.append(pltpu.make_async_copy(cins[a].at[my_chip], couts[a].at[my_chip], local_sems.at[a]))
            for j, (px, py) in enumerate([(1 - x, y), (x, 1 - y), (1 - x, 1 - y)]):
                k = 3 * a + j
                sends.append(_remote(cins[a].at[2 * px + py], couts[a].at[my_chip], send_sems.at[k], recv_sems.at[k], (px, py, c)))
                recvs.append(_remote(cins[a].at[my_chip], couts[a].at[2 * px + py], send_sems.at[k], recv_sems.at[k], (px, py, c)))
        return local, sends, recvs

    def start(cins, couts, sems):
        local, sends, _ = copies(cins, couts, sems)
        for cp in local + sends:
            cp.start()

    def finish(cins, couts, sems):
        local, sends, recvs = copies(cins, couts, sems)
        for cp in sends:
            cp.wait_send()
        for cp in recvs:
            cp.wait_recv()
        for cp in local:
            cp.wait()

    return Hosted(list(arrs), [jax.ShapeDtypeStruct(a.shape, a.dtype) for a in arrs],
                  [pltpu.SemaphoreType.DMA((3 * n,)), pltpu.SemaphoreType.DMA((3 * n,)), pltpu.SemaphoreType.DMA((n,))],
                  start, finish)


def sibling_hosted(arrs):
    n = len(arrs)

    def copies(cins, couts, sems):
        send_sems, recv_sems = sems
        x, y, c = _place()
        return [_remote(cins[a].at[ch, 1 - c], couts[a].at[ch], send_sems.at[4 * a + ch], recv_sems.at[4 * a + ch],
                        (x, y, 1 - c)) for a in range(n) for ch in range(4)]

    def start(cins, couts, sems):
        for cp in copies(cins, couts, sems):
            cp.start()

    def finish(cins, couts, sems):
        for cp in copies(cins, couts, sems):
            cp.wait()

    return Hosted(list(arrs), [jax.ShapeDtypeStruct((4,) + a.shape[2:], a.dtype) for a in arrs],
                  [pltpu.SemaphoreType.DMA((4 * n,)), pltpu.SemaphoreType.DMA((4 * n,))], start, finish)


def merge_hosted(first, second):
    ni, no, ns = len(first.ins), len(first.out_shapes), len(first.sems)

    def start(cins, couts, sems):
        first.start(cins[:ni], couts[:no], sems[:ns])
        second.start(cins[ni:], couts[no:], sems[ns:])

    def finish(cins, couts, sems):
        first.finish(cins[:ni], couts[:no], sems[:ns])
        second.finish(cins[ni:], couts[no:], sems[ns:])

    aliases = dict(first.aliases)
    aliases.update({ni + i: no + o for i, o in second.aliases.items()})
    both = Hosted(first.ins + second.ins, first.out_shapes + second.out_shapes, first.sems + second.sems, start, finish,
                  aliases)
    return both, lambda couts: (couts[:no], couts[no:])


def fill_hosted(stks):
    n = len(stks)

    def copies(cins, couts, sems):
        send_sems, recv_sems = sems
        x, y, c = _place()
        sends, recvs = [], []
        for a in range(n):
            for ch in range(4):
                k = 4 * a + ch
                sends.append(_remote(cins[a].at[2 * ch + c], couts[a].at[2 * ch + c], send_sems.at[k], recv_sems.at[k], (x, y, 1 - c)))
                recvs.append(_remote(cins[a].at[2 * ch + c], couts[a].at[2 * ch + 1 - c], send_sems.at[k], recv_sems.at[k], (x, y, 1 - c)))
        return sends, recvs

    def start(cins, couts, sems):
        for cp in copies(cins, couts, sems)[0]:
            cp.start()

    def finish(cins, couts, sems):
        sends, recvs = copies(cins, couts, sems)
        for cp in sends:
            cp.wait_send()
        for cp in recvs:
            cp.wait_recv()

    return Hosted(list(stks), [jax.ShapeDtypeStruct(s.shape, s.dtype) for s in stks],
                  [pltpu.SemaphoreType.DMA((4 * n,)), pltpu.SemaphoreType.DMA((4 * n,))], start, finish,
                  aliases={a: a for a in range(n)})


COL_SHARDED = ("w_in", "w_uq", "w_ukv", "w_branch_a", "w_branch_c", "w_up")
ROW_SHARDED = ("w_branch_b", "w_o", "w_down")
BIG = COL_SHARDED + ROW_SHARDED


def _full_from_stacked(name, st):
    if name in COL_SHARDED:
        _, L, K, n = st.shape
        return st.transpose(1, 2, 0, 3).reshape(L, K, 8 * n)
    if name in ROW_SHARDED:
        _, L, k, N = st.shape
        return st.transpose(1, 0, 2, 3).reshape(L, 8 * k, N)
    if name == "pool_w":
        _, L, G, pk, PG = st.shape
        return st.transpose(1, 2, 0, 3, 4).reshape(L, G, 8 * pk, PG)
    if name == "meta_tokens":
        _, M, n = st.shape
        return st.transpose(1, 0, 2).reshape(M, 8 * n)
    if name == "conv_w":
        _, L, W, n = st.shape
        return st.transpose(1, 2, 0, 3).reshape(L, W, 8 * n)
    raise ValueError(name)


def _shards_from_full(name, g):
    if name in COL_SHARDED:
        L, K, N = g.shape
        s = g.reshape(L, K, 8, N // 8).transpose(2, 0, 1, 3)
    else:
        L, K, N = g.shape
        s = g.reshape(L, 8, K // 8, N).transpose(1, 0, 2, 3)
    return s.reshape((4, 2) + s.shape[1:])


def _w_in_to_padded(w, dm):
    o3 = 3 * dm.DC + dm.QL + dm.KL
    o4 = o3 + dm.ROPE
    o5 = o4 + dm.DP
    pad = jnp.zeros(w.shape[:-1] + (256 - dm.ROPE,), w.dtype)
    return jnp.concatenate([w[..., o5:], w[..., :o3], w[..., o4:o5], w[..., o3:o4], pad], axis=-1)


def _w_in_from_padded(g, dm):
    o3 = 3 * dm.DC + dm.QL + dm.KL
    a = 3 * dm.D
    return jnp.concatenate([g[..., a:a + o3], g[..., dm.o_rope:dm.o_rope + dm.ROPE], g[..., dm.o_pool:dm.o_pool + dm.DP],
                            g[..., :a]], axis=-1)


def _pad_heads(w, dm):
    w = w.reshape(w.shape[:-1] + (dm.H, dm.QKH))
    w = jnp.pad(w, [(0, 0)] * (w.ndim - 1) + [(0, dm.HP - dm.QKH)])
    return w.reshape(w.shape[:-2] + (dm.H * dm.HP,))


def _unpad_heads(g, dm):
    g = g.reshape(g.shape[:-1] + (dm.H, dm.HP))[..., :dm.QKH]
    return g.reshape(g.shape[:-2] + (dm.H * dm.QKH,))


class _Weights:
    def __init__(self, w, dm, plan, fill_in):
        self.w, self.dm, self.plan, self.fill_in, self.full = w, dm, plan, fill_in, {}
        self.pending, self.split = {}, {}

    def blocks(self, items):
        return [self.w[n][l:l + 1].astype(BF16) for n, l in items]

    def put(self, items, stacked):
        for (n, l), st in zip(items, stacked):
            f = _full_from_stacked(n, st)[0]
            if n == "w_in":
                f = _w_in_to_padded(f, self.dm)
            if n == "w_uq":
                f = _pad_heads(f, self.dm)
            self.full[(n, l)] = f

    def comm(self, tag):
        items, waiting = self.plan.get(tag), self.pending.get(tag)
        gather = gather_ici(self.blocks(items)) if items else None
        fill = fill_hosted(waiting[1]) if waiting else None
        if gather is not None and fill is not None:
            both, self.split[tag] = merge_hosted(gather, fill)
            return both
        return gather if fill is None else fill

    def arrived(self, tag, couts):
        items, waiting = self.plan.get(tag), self.pending.pop(tag, None)
        if items and waiting:
            gathered, filled = self.split[tag](couts)
        else:
            gathered, filled = (couts, None) if items else (None, couts)
        if waiting:
            self.put(waiting[0], filled)
        if items:
            if tag in self.fill_in:
                self.pending[self.fill_in[tag]] = (items, gathered)
            else:
                self.put(items, fill_sibling(gathered, f"fill_{tag}"))

    def __call__(self, n, l):
        return self.full[(n, l)]


class _Reducer:
    def __init__(self, dm, core):
        self.dm, self.core, self.q = dm, core, {}

    def parts(self, items, g):
        parts = []
        for n, l in items:
            f = g[n]
            if n == "w_in":
                f = _w_in_from_padded(f, self.dm)
            if n == "w_uq":
                f = _unpad_heads(f, self.dm)
            parts.append(_shards_from_full(n, f[None]))
        return parts

    def add(self, items, parts, got):
        out = []
        for (n, l), a, b in zip(items, parts, got):
            C = a.shape[-1]
            out.append(add_sibling(a.reshape(4, 2, -1, C), b.reshape(4, -1, C), self.core,
                                   f"reduce_add_{n}_{l}").reshape(b.shape))
        return out

    def prepare(self, tag, items, g):
        parts = self.parts(items, g)
        return self.add(items, parts, sibling_exchange(parts, f"reduce_sibling_{tag}"))

    def put(self, items, summed):
        for key, q in zip(items, summed):
            self.q[key] = q


def _layer_fwd(xin, l, ws, G, tabs, dm):
    nm = lambda s: f"l{l}_{s}"
    D = dm.D

    def mm(tag, *args, **kw):
        comm = ws.comm(nm(tag))
        if comm is None:
            return matmul(*args, nm(tag), **kw)
        res, couts = matmul(*args, nm(tag), comm=comm, **kw)
        ws.arrived(nm(tag), couts)
        return res

    h = rms_fwd(xin, 0, D, G["attn_norm"], dm, nm("rms1"))
    proj = mm("proj", h, ws("w_in", l), "nn", (F32,))
    ya = mixer_a_fwd(proj, G["conv_w"], dm, nm("mixa"))
    ql = rms_fwd(proj, dm.o_ql // dm.QL, dm.QL, G["q_lat_norm"], dm, nm("rms_q"))
    kl = rms_fwd(proj, dm.o_kl // dm.KL, dm.KL, G["kv_lat_norm"], dm, nm("rms_kv"))
    q0 = mm("uq", ql, ws("w_uq", l), "nn", (F32,))
    kv0 = mm("ukv", kl, ws("w_ukv", l), "nn", (F32,))
    comm = ws.comm(nm("qkprep"))
    if comm is None:
        q_s, k_s, v_s = qk_prep_fwd(q0, kv0, proj, G["q_norm"], G["k_norm"], tabs[0], tabs[1], dm, nm("qkprep"))
    else:
        (q_s, k_s, v_s), couts = qk_prep_fwd(q0, kv0, proj, G["q_norm"], G["k_norm"], tabs[0], tabs[1], dm, nm("qkprep"),
                                            comm=comm)
        ws.arrived(nm("qkprep"), couts)
    comm = ws.comm(nm("attn"))
    if comm is None:
        o, ob, lse = attn_fwd(q_s, k_s, v_s, dm, nm("attn"))
    else:
        (o, ob, lse), couts = attn_fwd(q_s, k_s, v_s, dm, nm("attn"), comm=comm)
        ws.arrived(nm("attn"), couts)
    pooled, mixed, yc = pool_fwd(proj, ws("pool_w", l), G["pool_scale"], dm, nm("pool"))
    A = mm("br_a", ya, ws("w_branch_a", l), "nn", (F32,))
    B = mm("br_b", ob, ws("w_branch_b", l), "nn", (F32,))
    C = mm("br_c", yc, ws("w_branch_c", l), "nn", (F32,))
    merged = merge_fwd(proj, A, B, C, dm, nm("merge"))
    x1 = mm("wo", merged, ws("w_o", l), "nn", (F32,), extras=(xin,), epi=lambda acc, r: (acc + r,))
    h2 = rms_fwd(x1, 0, D, G["mlp_norm"], dm, nm("rms2"))
    up, act = mm("up", h2, ws("w_up", l), "nn", (F32, BF16), epi=lambda acc: (acc, jnp.square(jnp.maximum(acc, 0.0))))
    x2 = mm("down", act, ws("w_down", l), "nn", (F32,), extras=(x1,), epi=lambda acc, r: (acc + r,))
    saved = dict(xin=xin, h=h, proj=proj, ya=ya, ql=ql, kl=kl, q0=q0, kv0=kv0, q_s=q_s, k_s=k_s, v_s=v_s, o=o, ob=ob,
                 lse=lse, pooled=pooled, mixed=mixed, yc=yc, A=A, B=B, C=C, merged=merged, x1=x1, h2=h2, up=up, act=act)
    return x2, saved


def _layer_bwd(dx2, dx2b, S, l, ws, G, tabs, dm, pre_attn, late=None, first=None):
    nm = lambda s: f"l{l}_b_{s}"
    D, T = dm.D, dm.T
    g = {}
    d_up = matmul(dx2b, ws("w_down", l), "nt", (BF16,), nm("d_act"), extras=(S["up"],),
                  epi=lambda acc, up: (acc * (2.0 * jnp.maximum(up, 0.0)),), comm=first[0] if first else None)
    if first:
        d_up, couts = d_up
        first[1](couts)
    g["w_down"] = matmul(S["act"], dx2b, "tn", (BF16,), nm("g_down"))
    g["w_up"] = matmul(S["h2"], d_up, "tn", (BF16,), nm("g_up"))
    dh2 = matmul(d_up, ws("w_up", l), "nt", (F32,), nm("d_h2"))
    dx1, dx1b, g["mlp_norm"] = rms_bwd(S["x1"], 0, D, G["mlp_norm"], dh2, dx2, dm, nm("rms2"))
    dmerged = matmul(dx1b, ws("w_o", l), "nt", (F32,), nm("d_merged"))
    g["w_o"] = matmul(S["merged"], dx1b, "tn", (BF16,), nm("g_o"))
    dA, dB, dC, dl0, dl1, dl2 = merge_bwd(S["proj"], S["A"], S["B"], S["C"], dmerged, dm, nm("merge"))
    dya = matmul(dA, ws("w_branch_a", l), "nt", (F32,), nm("d_ya"))
    g["w_branch_a"] = matmul(S["ya"], dA, "tn", (BF16,), nm("g_a"))
    dyb = matmul(dB, ws("w_branch_b", l), "nt", (F32,), nm("d_yb"))
    g["w_branch_b"] = matmul(S["ob"], dB, "tn", (BF16,), nm("g_b"))
    dyc = matmul(dC, ws("w_branch_c", l), "nt", (F32,), nm("d_yc"))
    g["w_branch_c"] = matmul(S["yc"], dC, "tn", (BF16,), nm("g_c"))
    du, db, dc, g["conv_w"] = mixer_a_bwd(S["proj"], G["conv_w"], dya, dm, nm("mixa"))
    dpool, g["pool_w"], g["pool_scale"] = pool_bwd(dyc, S["mixed"], S["pooled"], ws("pool_w", l), G["pool_scale"], dm, nm("pool"))
    delta, dob = attn_delta(dyb, S["o"], dm, nm("delta"))
    nq = T // dm.tq
    comm_dq, after_dq = pre_attn(g)
    dq, couts = attn_bwd_dq(S["q_s"], S["k_s"], S["v_s"], dob, S["lse"], delta, dm, nm("attn_dq"), comm=comm_dq)
    comm_dkv, done_dkv = after_dq(couts)
    dkv = attn_bwd_dkv(S["q_s"], S["k_s"], S["v_s"], dob, S["lse"].reshape(dm.H, nq, 1, dm.tq),
                       delta.reshape(dm.H, nq, 1, dm.tq), dm, nm("attn_dkv"), comm=comm_dkv)
    if comm_dkv is not None:
        dkv, couts = dkv
        done_dkv(couts)
    dk, dv = dkv
    dq0, dkv0, dkr, g["q_norm"], g["k_norm"] = qk_prep_bwd(S["q0"], S["kv0"], S["proj"], G["q_norm"], G["k_norm"],
                                                            tabs[0], tabs[1], dq, dk, dv, dm, nm("qkprep"))
    dql = matmul(dq0, ws("w_uq", l), "nt", (F32,), nm("d_ql"))
    g["w_uq"] = matmul(S["ql"], dq0, "tn", (BF16,), nm("g_uq"))
    dkl = matmul(dkv0, ws("w_ukv", l), "nt", (F32,), nm("d_kl"))
    g["w_ukv"] = matmul(S["kl"], dkv0, "tn", (BF16,), nm("g_ukv"))
    _, dqlat, g["q_lat_norm"] = rms_bwd(S["proj"], dm.o_ql // dm.QL, dm.QL, G["q_lat_norm"], dql, None, dm, nm("rms_q"))
    _, dkvlat, g["kv_lat_norm"] = rms_bwd(S["proj"], dm.o_kl // dm.KL, dm.KL, G["kv_lat_norm"], dkl, None, dm, nm("rms_kv"))
    dproj = jnp.concatenate([dl0, dl1, dl2, du, db, dc, dqlat, dkvlat, dpool, dkr.astype(BF16),
                             jnp.zeros((T, 128), BF16)], axis=1)
    g["w_in"] = matmul(S["h"], dproj, "tn", (BF16,), nm("g_in"))
    comm_dh, done_dh = late(g) if late is not None else (None, None)
    dh = matmul(dproj, ws("w_in", l), "nt", (F32,), nm("d_h"), comm=comm_dh)
    if comm_dh is not None:
        dh, couts = dh
        done_dh(couts)
    dx, dxb, g["attn_norm"] = rms_bwd(S["xin"], 0, D, G["attn_norm"], dh, dx1, dm, nm("rms1"))
    return dx, dxb, g


WEIGHTS = ("meta_tokens", "attn_norm", "w_in", "conv_w", "q_lat_norm", "kv_lat_norm", "w_uq", "w_ukv", "q_norm", "k_norm",
           "pool_w", "pool_scale", "w_branch_a", "w_branch_b", "w_branch_c", "w_o", "mlp_norm", "w_up", "w_down")
SMALL = tuple(n for n in WEIGHTS if n not in BIG)


def kernel(x, meta_tokens, attn_norm, w_in, conv_w, q_lat_norm, kv_lat_norm, w_uq, w_ukv, q_norm, k_norm, pool_w, pool_scale, w_branch_a, w_branch_b, w_branch_c, w_o, mlp_norm, w_up, w_down, loss_target, m_meta_tokens, m_attn_norm, m_w_in, m_conv_w, m_q_lat_norm, m_kv_lat_norm, m_w_uq, m_w_ukv, m_q_norm, m_k_norm, m_pool_w, m_pool_scale, m_w_branch_a, m_w_branch_b, m_w_branch_c, m_w_o, m_mlp_norm, m_w_up, m_w_down, v_meta_tokens, v_attn_norm, v_w_in, v_conv_w, v_q_lat_norm, v_kv_lat_norm, v_w_uq, v_w_ukv, v_q_norm, v_k_norm, v_pool_w, v_pool_scale, v_w_branch_a, v_w_branch_b, v_w_branch_c, v_w_o, v_mlp_norm, v_w_up, v_w_down):
    w = dict(meta_tokens=meta_tokens, attn_norm=attn_norm, w_in=w_in, conv_w=conv_w, q_lat_norm=q_lat_norm,
             kv_lat_norm=kv_lat_norm, w_uq=w_uq, w_ukv=w_ukv, q_norm=q_norm, k_norm=k_norm, pool_w=pool_w,
             pool_scale=pool_scale, w_branch_a=w_branch_a, w_branch_b=w_branch_b, w_branch_c=w_branch_c, w_o=w_o,
             mlp_norm=mlp_norm, w_up=w_up, w_down=w_down)
    m = dict(meta_tokens=m_meta_tokens, attn_norm=m_attn_norm, w_in=m_w_in, conv_w=m_conv_w, q_lat_norm=m_q_lat_norm,
             kv_lat_norm=m_kv_lat_norm, w_uq=m_w_uq, w_ukv=m_w_ukv, q_norm=m_q_norm, k_norm=m_k_norm, pool_w=m_pool_w,
             pool_scale=m_pool_scale, w_branch_a=m_w_branch_a, w_branch_b=m_w_branch_b, w_branch_c=m_w_branch_c, w_o=m_w_o,
             mlp_norm=m_mlp_norm, w_up=m_w_up, w_down=m_w_down)
    v = dict(meta_tokens=v_meta_tokens, attn_norm=v_attn_norm, w_in=v_w_in, conv_w=v_conv_w, q_lat_norm=v_q_lat_norm,
             kv_lat_norm=v_kv_lat_norm, w_uq=v_w_uq, w_ukv=v_w_ukv, q_norm=v_q_norm, k_norm=v_k_norm, pool_w=v_pool_w,
             pool_scale=v_pool_scale, w_branch_a=v_w_branch_a, w_branch_b=v_w_branch_b, w_branch_c=v_w_branch_c, w_o=v_w_o,
             mlp_norm=v_mlp_norm, w_up=v_w_up, w_down=v_w_down)
    L = attn_norm.shape[0]
    assert L == 2, "the gather / reduce schedule below is written for two layers"
    seq, D = x.shape[1], x.shape[2]
    n_meta = meta_tokens.shape[0]
    dm = Dims(D, seq, n_meta)
    T = dm.T
    me = 4 * lax.axis_index("x") + 2 * lax.axis_index("y") + lax.axis_index("c")
    core = lax.axis_index("c").astype(jnp.int32).reshape(1)

    plan = {
        "l0_proj": [("w_uq", 0), ("w_ukv", 0), ("pool_w", 0), ("pool_w", 1), ("w_up", 0)],
        "l0_qkprep": [("w_branch_a", 0), ("w_branch_b", 0), ("w_branch_c", 0), ("w_o", 0)],
        "l0_attn": [("w_down", 0), ("w_in", 1)],
        "l0_up": [("w_up", 1)],
        "l0_down": [("w_down", 1)],
        "l1_proj": [("w_uq", 1), ("w_ukv", 1), ("w_branch_a", 1), ("w_branch_b", 1), ("w_branch_c", 1), ("w_o", 1)],
    }
    ws = _Weights(w, dm, plan, {"l0_qkprep": "l0_attn", "l0_attn": "l0_up", "l0_up": "l0_down", "l0_down": "l1_proj"})
    first = [("w_in", 0)]
    ws.put(first, all_gather(ws.blocks(first), "gather_first"))
    st_small = all_gather([w["meta_tokens"], w["conv_w"]], "gather_small")
    meta_full = _full_from_stacked("meta_tokens", st_small[0])
    conv_full = _full_from_stacked("conv_w", st_small[1])
    pad_gain = lambda gn: jnp.pad(gn, (0, dm.HP - dm.QKH))

    def gains(l):
        G = {n: w[n][l][None, :] for n in ("attn_norm", "q_lat_norm", "kv_lat_norm", "pool_scale", "mlp_norm")}
        G["q_norm"], G["k_norm"] = pad_gain(w["q_norm"][l])[None, :], pad_gain(w["k_norm"][l])[None, :]
        G["conv_w"] = conv_full[l]
        return G

    Gs = [gains(l) for l in range(L)]

    pos = jnp.arange(dm.T_real, dtype=F32)
    inv = 10000.0 ** (-jnp.arange(0, dm.ROPE, 2, dtype=F32) / dm.ROPE)
    ang = pos[:, None] * inv[None, :]
    zpad = jnp.zeros((dm.T_real, LANES - dm.ROPE), F32)
    rows = ((0, T - dm.T_real), (0, 0))
    tabs = (jnp.pad(jnp.concatenate([jnp.cos(ang), jnp.cos(ang), zpad], 1), rows),
            jnp.pad(jnp.concatenate([jnp.sin(ang), jnp.sin(ang), zpad], 1), rows))

    xs = jnp.concatenate([meta_full, x[0], jnp.zeros((T - dm.T_real, D), F32)], axis=0)
    target = jnp.pad(loss_target[0], ((n_meta, T - dm.T_real), (0, 0)))
    saved = []
    for l in range(L):
        xs, S = _layer_fwd(xs, l, ws, Gs[l], tabs, dm)
        saved.append(S)
    dx, dxb, loss_acc = loss_head(xs, target, dm, "loss_head")

    red = _Reducer(dm, core)
    early = lambda l: [(n, l) for n in ("w_down", "w_up", "w_o", "w_branch_a", "w_branch_b", "w_branch_c")]
    late = lambda l: [(n, l) for n in ("w_uq", "w_ukv", "w_in")]
    grads = [None] * L

    def pre_attn_1(g):
        parts = red.parts(early(1), g)

        def after(got):
            return reduce_ici(red.add(early(1), parts, got)), lambda couts: red.put(early(1), couts)

        return sibling_hosted(parts), after

    dx, dxb, grads[1] = _layer_bwd(dx, dxb, saved[1], 1, ws, Gs[1], tabs, dm, pre_attn_1)
    parts_late1 = red.parts(late(1), grads[1])
    chip_late1 = []

    def pre_attn_0(g):
        parts = red.parts(early(0), g)
        both, split = merge_hosted(reduce_ici(chip_late1[0]), sibling_hosted(parts))

        def after(couts):
            summed_late1, got = split(couts)
            red.put(late(1), summed_late1)
            return reduce_ici(red.add(early(0), parts, got)), lambda couts: red.put(early(0), couts)

        return both, after

    def late_0(g):
        return reduce_ici(red.prepare("l0", late(0), g)), lambda couts: red.put(late(0), couts)

    dx, dxb, grads[0] = _layer_bwd(
        dx, dxb, saved[0], 0, ws, Gs[0], tabs, dm, pre_attn_0, late_0,
        first=(sibling_hosted(parts_late1), lambda got: chip_late1.append(red.add(late(1), parts_late1, got))))
    grad_x = dx[n_meta:dm.T_real][None]
    summed = [jnp.concatenate([red.q[(n, l)] for l in range(L)], axis=1) for n in BIG]

    out_g, out_d, out_m, out_v = {}, {}, {}, {}

    def update(n, parts3):
        shp = w[n].shape
        C = shp[-1]
        res = adamw(w[n].reshape(-1, C), m[n].reshape(-1, C), v[n].reshape(-1, C), parts3, f"adamw_{n}")
        out_g[n], out_d[n], out_m[n], out_v[n] = [r.reshape(shp) for r in res]

    for n, q in zip(BIG, summed):
        update(n, q.reshape(4, -1, q.shape[-1]))

    small_full = {
        "meta_tokens": dx[:n_meta],
        "conv_w": jnp.stack([grads[l]["conv_w"] for l in range(L)]),
        "pool_w": jnp.stack([grads[l]["pool_w"] for l in range(L)]),
        "q_norm": jnp.stack([grads[l]["q_norm"][0, :dm.QKH] for l in range(L)]),
        "k_norm": jnp.stack([grads[l]["k_norm"][0, :dm.QKH] for l in range(L)]),
    }
    for n in ("attn_norm", "q_lat_norm", "kv_lat_norm", "pool_scale", "mlp_norm"):
        small_full[n] = jnp.stack([grads[l][n][0] for l in range(L)])
    flat = jnp.concatenate([small_full[n].reshape(-1) for n in SMALL] + [loss_acc[0, :1]])
    n_flat = flat.shape[0]
    rows_small = -(-n_flat // (8 * LANES)) * 8
    flat = jnp.pad(flat, (0, rows_small * LANES - n_flat)).reshape(rows_small, LANES)
    total = sum_parts(all_gather([flat], "gather_small_grads")[0], "sum_small").reshape(-1)
    off = 0
    for n in SMALL:
        size = math.prod(small_full[n].shape)
        gsum = total[off:off + size].reshape(small_full[n].shape)
        off += size
        if n in ("meta_tokens", "conv_w"):
            blk = w[n].shape[-1]
            gsum = lax.dynamic_slice_in_dim(gsum, me * blk, blk, axis=gsum.ndim - 1)
        elif n == "pool_w":
            blk = w[n].shape[2]
            gsum = lax.dynamic_slice_in_dim(gsum, me * blk, blk, axis=2)
        update(n, gsum.reshape(1, -1, gsum.shape[-1]))
    loss = total[off]

    return (loss, grad_x, *[out_g[n] for n in WEIGHTS], *[out_d[n] for n in WEIGHTS],
            *[out_m[n] for n in WEIGHTS], *[out_v[n] for n in WEIGHTS])
```

```python
import functools
import math

import jax
import jax.numpy as jnp
from jax import lax
from jax.experimental import pallas as pl
from jax.experimental.pallas import tpu as pltpu

F32 = jnp.float32
BF16 = jnp.bfloat16

VMEM_LIMIT_BYTES = 56 * 1024 * 1024
LANES = 128
EPS = 1e-6
HALO = 16

ADAM_LR = 0.001
ADAM_B1 = 0.9
ADAM_B2 = 0.999
ADAM_EPS = 1e-08
ADAM_WD = 0.01
ADAM_STEP = 10


def _params(sem):
    return pltpu.CompilerParams(dimension_semantics=sem, vmem_limit_bytes=VMEM_LIMIT_BYTES)


def _pick(n, prefs):
    for p in prefs:
        if p <= n and n % p == 0:
            return p
    return n


_MESH = pl.DeviceIdType.MESH
_HBM = pl.BlockSpec(memory_space=pltpu.HBM)


def _place():
    return lax.axis_index("x"), lax.axis_index("y"), lax.axis_index("c")


class Hosted:
    def __init__(self, ins, out_shapes, sems, start, finish, aliases=None):
        self.ins, self.out_shapes, self.sems, self.start, self.finish = ins, out_shapes, sems, start, finish
        self.aliases = aliases or {}


def _hosted_call(body, *, name, grid, in_specs, out_specs, out_shape, scratch_shapes, semantics, args, comm):
    n_in, n_out, n_scr = len(in_specs), len(out_specs), len(scratch_shapes)
    if comm is None:
        outs = pl.pallas_call(body, name=name, grid=grid, in_specs=in_specs, out_specs=out_specs, out_shape=out_shape,
                              scratch_shapes=scratch_shapes, compiler_params=_params(semantics))(*args)
        return list(outs), []
    ci, co = len(comm.ins), len(comm.out_shapes)

    def hosting(*refs):
        ins, cins = refs[:n_in], refs[n_in:n_in + ci]
        outs = refs[n_in + ci:n_in + ci + n_out]
        couts = refs[n_in + ci + n_out:n_in + ci + n_out + co]
        scr = refs[n_in + ci + n_out + co:n_in + ci + n_out + co + n_scr]
        csems = refs[n_in + ci + n_out + co + n_scr:]
        ids = [pl.program_id(d) for d in range(len(grid))]
        first = functools.reduce(jnp.logical_and, [i == 0 for i in ids])
        last = functools.reduce(jnp.logical_and, [i == g - 1 for i, g in zip(ids, grid)])

        @pl.when(first)
        def _():
            comm.start(cins, couts, csems)

        body(*ins, *outs, *scr)

        @pl.when(last)
        def _():
            comm.finish(cins, couts, csems)

    outs = pl.pallas_call(
        hosting, name=name, grid=grid,
        in_specs=list(in_specs) + [_HBM] * ci, out_specs=list(out_specs) + [_HBM] * co,
        out_shape=list(out_shape) + list(comm.out_shapes),
        scratch_shapes=list(scratch_shapes) + list(comm.sems),
        input_output_aliases={n_in + i: n_out + o for i, o in comm.aliases.items()},
        compiler_params=_params(("arbitrary",) * len(grid)),
    )(*args, *comm.ins)
    return list(outs[:n_out]), list(outs[n_out:])


MXU_FLOPS = 750e12
HBM_BYTES_PER_S = 3.0e12
ACC_RMW_BYTES_PER_S = 8e12
GRID_STEP_S = 0.4e-6
VMEM_COMPILER_RESERVE_BYTES = 8 * 1024 * 1024
MAX_TILE_ROWS, MAX_TILE_COLS = 2112, 2304


def _divisors(n, step):
    return [d for d in range(step, n + 1, step) if n % d == 0]


def _matmul_tiles(M, N, K, mode, out_sizes, n_extra):
    budget = VMEM_LIMIT_BYTES - VMEM_COMPILER_RESERVE_BYTES
    best = None
    for tk in _divisors(K, 16 if mode == "tn" else LANES):
        for tm in _divisors(M, LANES if mode == "tn" else 16):
            if tm > MAX_TILE_ROWS:
                continue
            for tn in _divisors(N, LANES):
                if tn > MAX_TILE_COLS:
                    continue
                need = 4 * tm * tk + 4 * tk * tn + 4 * tm * tn + 2 * tm * tn * sum(out_sizes) + 8 * n_extra * tm * tn
                if need > budget:
                    continue
                nk = K // tk
                steps = (M // tm) * (N // tn) * nk
                t_mxu = 2 * M * N * K / MXU_FLOPS + (nk > 1) * (8 * M * N * nk) / ACC_RMW_BYTES_PER_S
                t_hbm = (2 * M * K * (N // tn) + 2 * K * N * (M // tm) + M * N * (sum(out_sizes) + 4 * n_extra)) / HBM_BYTES_PER_S
                t = max(t_mxu, t_hbm) + steps * GRID_STEP_S
                if best is None or t < best[0]:
                    best = (t, tm, tn, tk)
    assert best is not None, (M, N, K, mode)
    return best[1:]


def matmul(a, b, mode, out_dtypes, name, extras=(), epi=None, tm=None, tn=None, tk=None, comm=None):
    if mode == "nn":
        (M, K), (K2, N) = a.shape, b.shape
    elif mode == "nt":
        (M, K), (N, K2) = a.shape, b.shape
    else:
        (K, M), (K2, N) = a.shape, b.shape
    assert K == K2, (a.shape, b.shape, mode)
    if not (tm and tn and tk):
        tm, tn, tk = _matmul_tiles(M, N, K, mode, [jnp.dtype(d).itemsize for d in out_dtypes], len(extras))
    nk = K // tk
    dims = {"nn": (((1,), (0,)), ((), ())), "nt": (((1,), (1,)), ((), ())), "tn": (((0,), (0,)), ((), ()))}[mode]
    n_extra, n_out = len(extras), len(out_dtypes)

    def body(*refs):
        a_ref, b_ref = refs[0], refs[1]
        extra_refs = refs[2:2 + n_extra]
        out_refs = refs[2 + n_extra:2 + n_extra + n_out]

        def finish(acc):
            outs = (acc,) if epi is None else epi(acc, *[r[...] for r in extra_refs])
            for o_ref, o in zip(out_refs, outs):
                o_ref[...] = o.astype(o_ref.dtype)

        part = lax.dot_general(a_ref[...], b_ref[...], dims, preferred_element_type=F32)
        if nk == 1:
            finish(part)
            return
        acc_ref = refs[-1]
        k = pl.program_id(2)

        @pl.when(k == 0)
        def _():
            acc_ref[...] = part

        @pl.when(k > 0)
        def _():
            acc_ref[...] += part

        @pl.when(k == nk - 1)
        def _():
            finish(acc_ref[...])

    a_spec = {"nn": pl.BlockSpec((tm, tk), lambda i, j, k: (i, k)),
              "nt": pl.BlockSpec((tm, tk), lambda i, j, k: (i, k)),
              "tn": pl.BlockSpec((tk, tm), lambda i, j, k: (k, i))}[mode]
    b_spec = {"nn": pl.BlockSpec((tk, tn), lambda i, j, k: (k, j)),
              "nt": pl.BlockSpec((tn, tk), lambda i, j, k: (j, k)),
              "tn": pl.BlockSpec((tk, tn), lambda i, j, k: (k, j))}[mode]
    o_spec = pl.BlockSpec((tm, tn), lambda i, j, k: (i, j))
    outs, couts = _hosted_call(
        body, name=name, grid=(M // tm, N // tn, nk),
        in_specs=[a_spec, b_spec] + [o_spec] * n_extra,
        out_specs=[o_spec] * n_out,
        out_shape=[jax.ShapeDtypeStruct((M, N), d) for d in out_dtypes],
        scratch_shapes=[pltpu.VMEM((tm, tn), F32)] if nk > 1 else [],
        semantics=("parallel", "parallel", "arbitrary"), args=(a, b, *extras), comm=comm)
    res = outs[0] if n_out == 1 else outs
    return res if comm is None else (res, couts)


class Dims:
    def __init__(self, d_model, seq, n_meta):
        self.D = d_model
        self.n_meta = n_meta
        self.T_real = seq + n_meta
        self.T = -(-self.T_real // LANES) * LANES
        self.H = d_model // 128
        self.DC = d_model // 2
        self.DP = d_model // 2
        self.PG = self.DP // 4
        self.QL = 512
        self.KL = 512
        self.ROPE = 64
        self.NOPE = 128
        self.QKH = 192
        self.HP = 256
        self.DFF = 4 * d_model
        self.o_gate = 0
        self.o_u = 3 * d_model
        self.o_b = self.o_u + self.DC
        self.o_c = self.o_b + self.DC
        self.o_ql = self.o_c + self.DC
        self.o_kl = self.o_ql + self.QL
        self.o_pool = self.o_kl + self.KL
        self.o_rope = self.o_pool + self.DP
        self.NIN = self.o_rope + 256
        self.tr = _pick(self.T, (384, 256, 128))
        self.tq = _pick(self.T, (384, 256, 128))


def _row_ids(i, tr):
    return i * tr + lax.broadcasted_iota(jnp.int32, (tr, 1), 0)


def rms_fwd(x, col_block, width, g, dm, name):
    tr = dm.tr

    def body(x_ref, g_ref, y_ref):
        xv = x_ref[...]
        r = lax.rsqrt(jnp.mean(xv * xv, axis=-1, keepdims=True) + EPS)
        y_ref[...] = (xv * r * g_ref[...]).astype(y_ref.dtype)

    return pl.pallas_call(
        body, name=name, grid=(dm.T // tr,),
        in_specs=[pl.BlockSpec((tr, width), lambda i: (i, col_block)), pl.BlockSpec((1, width), lambda i: (0, 0))],
        out_specs=pl.BlockSpec((tr, width), lambda i: (i, 0)),
        out_shape=jax.ShapeDtypeStruct((dm.T, width), BF16),
        compiler_params=_params(("parallel",)),
    )(x, g.reshape(1, width))


def rms_bwd(x, col_block, width, g, dy, dres, dm, name):
    tr = dm.tr
    has_res = dres is not None

    def body(*refs):
        if has_res:
            x_ref, g_ref, dy_ref, dres_ref, dx_ref, dxb_ref, dg_ref = refs
        else:
            x_ref, g_ref, dy_ref, dx_ref, dxb_ref, dg_ref = refs
        xv, dyv = x_ref[...], dy_ref[...]
        r = lax.rsqrt(jnp.mean(xv * xv, axis=-1, keepdims=True) + EPS)
        gdy = dyv * g_ref[...]
        dx = r * gdy - xv * (r * r * r) * jnp.mean(xv * gdy, axis=-1, keepdims=True)
        if has_res:
            dx = dx + dres_ref[...]
        dx_ref[...] = dx
        dxb_ref[...] = dx.astype(BF16)

        @pl.when(pl.program_id(0) == 0)
        def _():
            dg_ref[...] = jnp.zeros_like(dg_ref)

        dg_ref[...] += jnp.sum(dyv * xv * r, axis=0, keepdims=True)

    row = pl.BlockSpec((tr, width), lambda i: (i, 0))
    in_specs = [pl.BlockSpec((tr, width), lambda i: (i, col_block)), pl.BlockSpec((1, width), lambda i: (0, 0)), row]
    args = [x, g.reshape(1, width), dy]
    if has_res:
        in_specs.append(row)
        args.append(dres)
    return pl.pallas_call(
        body, name=name, grid=(dm.T // tr,),
        in_specs=in_specs,
        out_specs=[row, row, pl.BlockSpec((1, width), lambda i: (0, 0))],
        out_shape=[jax.ShapeDtypeStruct((dm.T, width), F32), jax.ShapeDtypeStruct((dm.T, width), BF16),
                   jax.ShapeDtypeStruct((1, width), F32)],
        compiler_params=_params(("arbitrary",)),
    )(*args)


def _fill_halo_buf(buf, src_fn, T, R, width):
    zeros = jnp.zeros((HALO, width), F32)
    buf[pl.ds(0, HALO), :] = zeros
    buf[pl.ds(HALO + T, HALO), :] = zeros

    def fill(r, c):
        r0 = pl.multiple_of(r * R, 8)
        buf[pl.ds(r0 + HALO, R), :] = src_fn(r0)
        return c

    lax.fori_loop(0, T // R, fill, 0)


def _back(win, sh):
    return pltpu.roll(win, sh, 0)


def _fwd(win, sh):
    return pltpu.roll(win, win.shape[0] - sh, 0)


def mixer_a_fwd(proj, conv_w, dm, name):
    T, cw = dm.T, 128
    R = dm.tr
    nb = dm.DC // cw

    def body(u_ref, b_ref, c_ref, w_ref, ya_ref, buf):
        _fill_halo_buf(buf, lambda r0: c_ref[pl.ds(r0, R), :] * u_ref[pl.ds(r0, R), :], T, R, cw)
        w0, w1, w2 = w_ref[0:1, :], w_ref[1:2, :], w_ref[2:3, :]

        def chunk(r, c):
            r0 = pl.multiple_of(r * R, 8)
            win = buf[pl.ds(r0, R + HALO), :]
            cv = w2 * win + w1 * _back(win, 1) + w0 * _back(win, 2)
            ya_ref[pl.ds(r0, R), :] = (b_ref[pl.ds(r0, R), :] * cv[HALO:, :]).astype(BF16)
            return c

        lax.fori_loop(0, T // R, chunk, 0)

    col = lambda off: pl.BlockSpec((T, cw), lambda j: (0, off // cw + j))
    return pl.pallas_call(
        body, name=name, grid=(nb,),
        in_specs=[col(dm.o_u), col(dm.o_b), col(dm.o_c), pl.BlockSpec((3, cw), lambda j: (0, j))],
        out_specs=pl.BlockSpec((T, cw), lambda j: (0, j)),
        out_shape=jax.ShapeDtypeStruct((T, dm.DC), BF16),
        scratch_shapes=[pltpu.VMEM((T + 2 * HALO, cw), F32)],
        compiler_params=_params(("parallel",)),
    )(proj, proj, proj, conv_w)


def mixer_a_bwd(proj, conv_w, dya, dm, name):
    T, cw = dm.T, 128
    R = dm.tr
    nb = dm.DC // cw

    def body(u_ref, b_ref, c_ref, w_ref, dya_ref, du_ref, db_ref, dc_ref, dw_ref, sbuf, gbuf):
        _fill_halo_buf(sbuf, lambda r0: c_ref[pl.ds(r0, R), :] * u_ref[pl.ds(r0, R), :], T, R, cw)
        _fill_halo_buf(gbuf, lambda r0: dya_ref[pl.ds(r0, R), :] * b_ref[pl.ds(r0, R), :], T, R, cw)
        w0, w1, w2 = w_ref[0:1, :], w_ref[1:2, :], w_ref[2:3, :]

        def chunk(r, acc):
            a0, a1, a2 = acc
            r0 = pl.multiple_of(r * R, 8)
            swin = sbuf[pl.ds(r0, R + HALO), :]
            s0, s1, s2 = swin[HALO:, :], _back(swin, 1)[HALO:, :], _back(swin, 2)[HALO:, :]
            gwin = gbuf[pl.ds(r0 + HALO, R + HALO), :]
            g0, g1, g2 = gwin[:R, :], _fwd(gwin, 1)[:R, :], _fwd(gwin, 2)[:R, :]
            cv = w2 * s0 + w1 * s1 + w0 * s2
            ds = w2 * g0 + w1 * g1 + w0 * g2
            db_ref[pl.ds(r0, R), :] = (dya_ref[pl.ds(r0, R), :] * cv).astype(BF16)
            du_ref[pl.ds(r0, R), :] = (ds * c_ref[pl.ds(r0, R), :]).astype(BF16)
            dc_ref[pl.ds(r0, R), :] = (ds * u_ref[pl.ds(r0, R), :]).astype(BF16)
            a2 = a2 + jnp.sum(g0 * s0, axis=0, keepdims=True)
            a1 = a1 + jnp.sum(g0 * s1, axis=0, keepdims=True)
            a0 = a0 + jnp.sum(g0 * s2, axis=0, keepdims=True)
            return a0, a1, a2

        z = jnp.zeros((1, cw), F32)
        a0, a1, a2 = lax.fori_loop(0, T // R, chunk, (z, z, z))
        dw_ref[0:1, :] = a0
        dw_ref[1:2, :] = a1
        dw_ref[2:3, :] = a2

    col = lambda off: pl.BlockSpec((T, cw), lambda j: (0, off // cw + j))
    own = pl.BlockSpec((T, cw), lambda j: (0, j))
    o = jax.ShapeDtypeStruct((T, dm.DC), BF16)
    return pl.pallas_call(
        body, name=name, grid=(nb,),
        in_specs=[col(dm.o_u), col(dm.o_b), col(dm.o_c), pl.BlockSpec((3, cw), lambda j: (0, j)), own],
        out_specs=[own, own, own, pl.BlockSpec((3, cw), lambda j: (0, j))],
        out_shape=[o, o, o, jax.ShapeDtypeStruct((3, dm.DC), F32)],
        scratch_shapes=[pltpu.VMEM((T + 2 * HALO, cw), F32), pltpu.VMEM((T + 2 * HALO, cw), F32)],
        compiler_params=_params(("parallel",)),
    )(proj, proj, proj, conv_w, dya)


def _rope(x, C, S):
    return x * C + (pltpu.roll(x, 32, 1) - pltpu.roll(x, 96, 1)) * S


def _rope_t(dy, C, S):
    return dy * C + (pltpu.roll(dy, 96, 1) - pltpu.roll(dy, 32, 1)) * S


HP = 8


def qk_prep_fwd(q0, kv0, proj, qn, kn, C, S, dm, name, comm=None):
    T, H, tr = dm.T, dm.H, dm.tr
    inv = 1.0 / dm.QKH

    def body(q0_ref, kv_ref, kr_ref, qn_ref, kn_ref, c_ref, s_ref, q_ref, k_ref, v_ref):
        Cv, Sv = c_ref[...], s_ref[...]
        kb = kr_ref[...]
        kb2 = jnp.sum(kb * kb, -1, keepdims=True)
        for a in range(HP):
            lo, mid, hi = 256 * a, 256 * a + 128, 256 * (a + 1)
            qa, qb = q0_ref[:, lo:mid], q0_ref[:, mid:hi]
            r = lax.rsqrt((jnp.sum(qa * qa, -1, keepdims=True) + jnp.sum(qb * qb, -1, keepdims=True)) * inv + EPS)
            q_ref[:, lo:mid] = (qa * r * qn_ref[:, :128]).astype(BF16)
            q_ref[:, mid:hi] = _rope(qb * r * qn_ref[:, 128:], Cv, Sv).astype(BF16)
            ka = kv_ref[:, lo:mid]
            r = lax.rsqrt((jnp.sum(ka * ka, -1, keepdims=True) + kb2) * inv + EPS)
            k_ref[:, lo:mid] = (ka * r * kn_ref[:, :128]).astype(BF16)
            k_ref[:, mid:hi] = _rope(kb * r * kn_ref[:, 128:], Cv, Sv).astype(BF16)
            v_ref[:, 128 * a:128 * (a + 1)] = kv_ref[:, mid:hi].astype(BF16)

    head = pl.BlockSpec((tr, 256 * HP), lambda i, h: (i, h))
    gain = pl.BlockSpec((1, 256), lambda i, h: (0, 0))
    tab = pl.BlockSpec((tr, 128), lambda i, h: (i, 0))
    outs, couts = _hosted_call(
        body, name=name, grid=(T // tr, H // HP),
        in_specs=[head, head, pl.BlockSpec((tr, 128), lambda i, h: (i, dm.o_rope // 128)), gain, gain, tab, tab],
        out_specs=[head, head, pl.BlockSpec((tr, 128 * HP), lambda i, h: (i, h))],
        out_shape=[jax.ShapeDtypeStruct((T, H * 256), BF16), jax.ShapeDtypeStruct((T, H * 256), BF16),
                   jax.ShapeDtypeStruct((T, H * 128), BF16)],
        scratch_shapes=[], semantics=("parallel", "parallel"), args=(q0, kv0, proj, qn, kn, C, S), comm=comm)
    return outs if comm is None else (outs, couts)


def qk_prep_bwd(q0, kv0, proj, qn, kn, C, S, dq, dk, dv, dm, name):
    T, H, tr = dm.T, dm.H, dm.tr
    inv = 1.0 / dm.QKH

    def body(q0_ref, kv_ref, kr_ref, qn_ref, kn_ref, c_ref, s_ref, dq_ref, dk_ref, dv_ref,
             dq0_ref, dkv_ref, dkr_ref, dqn_ref, dkn_ref):
        i, h = pl.program_id(0), pl.program_id(1)
        Cv, Sv = c_ref[...], s_ref[...]

        def norm_bwd(xa, xb, ga, gb, dya, dyb):
            r = lax.rsqrt((jnp.sum(xa * xa, -1, keepdims=True) + jnp.sum(xb * xb, -1, keepdims=True)) * inv + EPS)
            dzb = _rope_t(dyb, Cv, Sv)
            gda, gdb = ga * dya, gb * dzb
            dot = (jnp.sum(xa * gda, -1, keepdims=True) + jnp.sum(xb * gdb, -1, keepdims=True)) * inv
            r3 = r * r * r
            dxa = r * gda - xa * r3 * dot
            dxb = r * gdb - xb * r3 * dot
            dga = jnp.sum(dya * xa * r, axis=0, keepdims=True)
            dgb = jnp.sum(dzb * xb * r, axis=0, keepdims=True)
            return dxa, dxb, dga, dgb

        @pl.when((i == 0) & (h == 0))
        def _():
            dqn_ref[...] = jnp.zeros_like(dqn_ref)
            dkn_ref[...] = jnp.zeros_like(dkn_ref)

        @pl.when(h == 0)
        def _():
            dkr_ref[...] = jnp.zeros_like(dkr_ref)

        for a in range(HP):
            lo, mid, hi = 256 * a, 256 * a + 128, 256 * (a + 1)
            dxa, dxb, dga, dgb = norm_bwd(q0_ref[:, lo:mid], q0_ref[:, mid:hi], qn_ref[:, :128], qn_ref[:, 128:],
                                          dq_ref[:, lo:mid], dq_ref[:, mid:hi])
            dq0_ref[:, lo:mid] = dxa.astype(BF16)
            dq0_ref[:, mid:hi] = dxb.astype(BF16)
            dqn_ref[:, :128] += dga
            dqn_ref[:, 128:] += dgb
            dxa, dxb, dga, dgb = norm_bwd(kv_ref[:, lo:mid], kr_ref[...], kn_ref[:, :128], kn_ref[:, 128:],
                                          dk_ref[:, lo:mid], dk_ref[:, mid:hi])
            dkv_ref[:, lo:mid] = dxa.astype(BF16)
            dkv_ref[:, mid:hi] = dv_ref[:, 128 * a:128 * (a + 1)].astype(BF16)
            dkn_ref[:, :128] += dga
            dkn_ref[:, 128:] += dgb
            dkr_ref[...] += dxb

    head = pl.BlockSpec((tr, 256 * HP), lambda i, h: (i, h))
    gain = pl.BlockSpec((1, 256), lambda i, h: (0, 0))
    tab = pl.BlockSpec((tr, 128), lambda i, h: (i, 0))
    return pl.pallas_call(
        body, name=name, grid=(T // tr, H // HP),
        in_specs=[head, head, pl.BlockSpec((tr, 128), lambda i, h: (i, dm.o_rope // 128)), gain, gain, tab, tab,
                  head, head, pl.BlockSpec((tr, 128 * HP), lambda i, h: (i, h))],
        out_specs=[head, head, tab, gain, gain],
        out_shape=[jax.ShapeDtypeStruct((T, H * 256), BF16), jax.ShapeDtypeStruct((T, H * 256), BF16),
                   jax.ShapeDtypeStruct((T, 128), F32), jax.ShapeDtypeStruct((1, 256), F32),
                   jax.ShapeDtypeStruct((1, 256), F32)],
        compiler_params=_params(("arbitrary", "arbitrary")),
    )(q0, kv0, proj, qn, kn, C, S, dq, dk, dv)


_NT = (((1,), (1,)), ((), ()))


def _causal_mask(t):
    return lax.broadcasted_iota(jnp.int32, (t, t), 0) >= lax.broadcasted_iota(jnp.int32, (t, t), 1)


def _causal_mask_t(t):
    return lax.broadcasted_iota(jnp.int32, (t, t), 0) <= lax.broadcasted_iota(jnp.int32, (t, t), 1)


HB = 2


def attn_fwd(q, k, v, dm, name, comm=None):
    T, H, tq = dm.T, dm.H, dm.tq
    scale = dm.QKH ** -0.5

    def body(q_ref, k_ref, v_ref, o_ref, ob_ref, lse_ref):
        qi = pl.program_id(1)

        def step(j0, w, carry, masked):
            ss = [lax.dot_general(q_ref[:, 256 * a:256 * (a + 1)], k_ref[pl.ds(j0, w), 256 * a:256 * (a + 1)], _NT,
                                  preferred_element_type=F32) for a in range(HB)]
            out = []
            for a in range(HB):
                m, l, acc = carry[a]
                s = ss[a] * scale
                if masked:
                    s = jnp.where(_causal_mask(tq), s, -jnp.inf)
                m_new = jnp.maximum(m, jnp.max(s, -1, keepdims=True))
                alpha = jnp.exp(m - m_new)
                p = jnp.exp(s - m_new)
                l = alpha * l + jnp.sum(p, -1, keepdims=True)
                acc = alpha * acc + jnp.dot(p.astype(BF16), v_ref[pl.ds(j0, w), 128 * a:128 * (a + 1)],
                                            preferred_element_type=F32)
                out.append((m_new, l, acc))
            return tuple(out)

        one = (jnp.full((tq, 1), -jnp.inf, F32), jnp.zeros((tq, 1), F32), jnp.zeros((tq, 128), F32))
        carry = lax.fori_loop(0, jnp.right_shift(qi, 1), lambda t, c: step(pl.multiple_of(t * 2 * tq, tq), 2 * tq, c, False), (one,) * HB)
        carry = lax.cond(jnp.bitwise_and(qi, 1) == 1, lambda c: step(pl.multiple_of((qi - 1) * tq, tq), tq, c, False), lambda c: c, carry)
        carry = step(pl.multiple_of(qi * tq, tq), tq, carry, True)
        for a in range(HB):
            m, l, acc = carry[a]
            o = acc / l
            o_ref[:, 128 * a:128 * (a + 1)] = o
            ob_ref[:, 128 * a:128 * (a + 1)] = o.astype(BF16)
            lse_ref[a] = m + jnp.log(l)

    outs, couts = _hosted_call(
        body, name=name, grid=(H // HB, T // tq),
        in_specs=[pl.BlockSpec((tq, 256 * HB), lambda h, i: (i, h)), pl.BlockSpec((T, 256 * HB), lambda h, i: (0, h)),
                  pl.BlockSpec((T, 128 * HB), lambda h, i: (0, h))],
        out_specs=[pl.BlockSpec((tq, 128 * HB), lambda h, i: (i, h)), pl.BlockSpec((tq, 128 * HB), lambda h, i: (i, h)),
                   pl.BlockSpec((HB, tq, 1), lambda h, i: (h, i, 0))],
        out_shape=[jax.ShapeDtypeStruct((T, H * 128), F32), jax.ShapeDtypeStruct((T, H * 128), BF16),
                   jax.ShapeDtypeStruct((H, T, 1), F32)],
        scratch_shapes=[], semantics=("parallel", "parallel"), args=(q, k, v), comm=comm)
    return outs if comm is None else (outs, couts)


def attn_delta(do, o, dm, name):
    T, H, tr = dm.T, dm.H, dm.tr

    def body(do_ref, o_ref, delta_ref, dob_ref):
        for a in range(H):
            d = do_ref[:, 128 * a:128 * (a + 1)]
            delta_ref[a] = jnp.sum(d * o_ref[:, 128 * a:128 * (a + 1)], -1, keepdims=True)
            dob_ref[:, 128 * a:128 * (a + 1)] = d.astype(BF16)

    blk = pl.BlockSpec((tr, 128 * H), lambda i: (i, 0))
    return pl.pallas_call(
        body, name=name, grid=(T // tr,),
        in_specs=[blk, blk],
        out_specs=[pl.BlockSpec((H, tr, 1), lambda i: (0, i, 0)), blk],
        out_shape=[jax.ShapeDtypeStruct((H, T, 1), F32), jax.ShapeDtypeStruct((T, H * 128), BF16)],
        compiler_params=_params(("parallel",)),
    )(do, o)


def attn_bwd_dq(q, k, v, do, lse, delta, dm, name, comm=None):
    T, H, tq = dm.T, dm.H, dm.tq
    scale = dm.QKH ** -0.5

    def body(q_ref, k_ref, v_ref, do_ref, lse_ref, delta_ref, dq_ref):
        qi = pl.program_id(1)

        def step(j0, w, dqs, masked):
            hk = lambda a: slice(256 * a, 256 * (a + 1))
            hv = lambda a: slice(128 * a, 128 * (a + 1))
            ss = [lax.dot_general(q_ref[:, hk(a)], k_ref[pl.ds(j0, w), hk(a)], _NT, preferred_element_type=F32)
                  for a in range(HB)]
            dps = [lax.dot_general(do_ref[:, hv(a)], v_ref[pl.ds(j0, w), hv(a)], _NT, preferred_element_type=F32)
                   for a in range(HB)]
            out = []
            for a in range(HB):
                p = jnp.exp(ss[a] * scale - lse_ref[a])
                if masked:
                    p = jnp.where(_causal_mask(tq), p, 0.0)
                ds = p * (dps[a] - delta_ref[a]) * scale
                out.append(dqs[a] + jnp.dot(ds.astype(BF16), k_ref[pl.ds(j0, w), hk(a)], preferred_element_type=F32))
            return tuple(out)

        dqs = lax.fori_loop(0, jnp.right_shift(qi, 1), lambda t, c: step(pl.multiple_of(t * 2 * tq, tq), 2 * tq, c, False),
                            (jnp.zeros((tq, 256), F32),) * HB)
        dqs = lax.cond(jnp.bitwise_and(qi, 1) == 1, lambda c: step(pl.multiple_of((qi - 1) * tq, tq), tq, c, False),
                       lambda c: c, dqs)
        dqs = step(pl.multiple_of(qi * tq, tq), tq, dqs, True)
        for a in range(HB):
            dq_ref[:, 256 * a:256 * (a + 1)] = dqs[a]

    stat = pl.BlockSpec((HB, tq, 1), lambda h, i: (h, i, 0))
    outs, couts = _hosted_call(
        body, name=name, grid=(H // HB, T // tq),
        in_specs=[pl.BlockSpec((tq, 256 * HB), lambda h, i: (i, h)), pl.BlockSpec((T, 256 * HB), lambda h, i: (0, h)),
                  pl.BlockSpec((T, 128 * HB), lambda h, i: (0, h)), pl.BlockSpec((tq, 128 * HB), lambda h, i: (i, h)),
                  stat, stat],
        out_specs=[pl.BlockSpec((tq, 256 * HB), lambda h, i: (i, h))],
        out_shape=[jax.ShapeDtypeStruct((T, H * 256), F32)],
        scratch_shapes=[], semantics=("parallel", "parallel"), args=(q, k, v, do, lse, delta), comm=comm)
    return outs[0] if comm is None else (outs[0], couts)


def attn_bwd_dkv(q, k, v, do, lse_rows, delta_rows, dm, name, comm=None):
    T, H, tq = dm.T, dm.H, dm.tq
    nq = T // tq
    scale = dm.QKH ** -0.5

    def body(q_ref, k_ref, v_ref, do_ref, lse_ref, delta_ref, dk_ref, dv_ref):
        kj = pl.program_id(1)

        def step(i, n, carry, masked):
            i0 = pl.multiple_of(i * tq, tq)
            w = n * tq
            hk = lambda a: slice(256 * a, 256 * (a + 1))
            hv = lambda a: slice(128 * a, 128 * (a + 1))
            row = lambda ref, a: ref[a, i] if n == 1 else jnp.concatenate([ref[a, i], ref[a, i + 1]], axis=1)
            sts = [lax.dot_general(k_ref[:, hk(a)], q_ref[pl.ds(i0, w), hk(a)], _NT, preferred_element_type=F32)
                   for a in range(HB)]
            dpts = [lax.dot_general(v_ref[:, hv(a)], do_ref[pl.ds(i0, w), hv(a)], _NT, preferred_element_type=F32)
                    for a in range(HB)]
            out = []
            for a in range(HB):
                dk, dv = carry[a]
                pt = jnp.exp(sts[a] * scale - row(lse_ref, a))
                if masked:
                    pt = jnp.where(_causal_mask_t(tq), pt, 0.0)
                dv = dv + jnp.dot(pt.astype(BF16), do_ref[pl.ds(i0, w), hv(a)], preferred_element_type=F32)
                dst = pt * (dpts[a] - row(delta_ref, a)) * scale
                dk = dk + jnp.dot(dst.astype(BF16), q_ref[pl.ds(i0, w), hk(a)], preferred_element_type=F32)
                out.append((dk, dv))
            return tuple(out)

        carry = step(kj, 1, ((jnp.zeros((tq, 256), F32), jnp.zeros((tq, 128), F32)),) * HB, True)
        rest = nq - 1 - kj
        carry = lax.fori_loop(0, jnp.right_shift(rest, 1), lambda t, c: step(kj + 1 + 2 * t, 2, c, False), carry)
        carry = lax.cond(jnp.bitwise_and(rest, 1) == 1, lambda c: step(nq - 1, 1, c, False), lambda c: c, carry)
        for a in range(HB):
            dk_ref[:, 256 * a:256 * (a + 1)] = carry[a][0]
            dv_ref[:, 128 * a:128 * (a + 1)] = carry[a][1]

    rows = pl.BlockSpec((HB, nq, 1, tq), lambda h, j: (h, 0, 0, 0))
    outs, couts = _hosted_call(
        body, name=name, grid=(H // HB, nq),
        in_specs=[pl.BlockSpec((T, 256 * HB), lambda h, j: (0, h)), pl.BlockSpec((tq, 256 * HB), lambda h, j: (j, h)),
                  pl.BlockSpec((tq, 128 * HB), lambda h, j: (j, h)), pl.BlockSpec((T, 128 * HB), lambda h, j: (0, h)),
                  rows, rows],
        out_specs=[pl.BlockSpec((tq, 256 * HB), lambda h, j: (j, h)), pl.BlockSpec((tq, 128 * HB), lambda h, j: (j, h))],
        out_shape=[jax.ShapeDtypeStruct((T, H * 256), F32), jax.ShapeDtypeStruct((T, H * 128), F32)],
        scratch_shapes=[], semantics=("parallel", "parallel"), args=(q, k, v, do, lse_rows, delta_rows), comm=comm)
    return outs if comm is None else (outs, couts)


def _window_sum(win, g, shift):
    s1 = win + shift(win, 1)
    s2 = s1 + shift(s1, 2)
    s3 = s2 + shift(s2, 4)
    s4 = s3 + shift(s3, 8)
    return jnp.where(g == 0, s1, jnp.where(g == 1, s2, jnp.where(g == 2, s3, s4)))


def _count(r0, R, g, T_unused=None):
    t = r0 + lax.broadcasted_iota(jnp.int32, (R, 1), 0)
    return jnp.minimum(t + 1, jnp.left_shift(2, g)).astype(F32)


def pool_fwd(proj, pw, ps, dm, name):
    T, PG, R = dm.T, dm.PG, dm.tr

    def body(x_ref, pw_ref, ps_ref, pooled_ref, mixed_ref, yc_ref, buf):
        g = pl.program_id(0)
        _fill_halo_buf(buf, lambda r0: x_ref[pl.ds(r0, R), :], T, R, PG)

        def chunk(r, c):
            r0 = pl.multiple_of(r * R, 8)
            win = buf[pl.ds(r0, R + HALO), :]
            ws = _window_sum(win, g, _back)[HALO:, :]
            pooled = (ws / _count(r0, R, g) - win[HALO:, :]).astype(BF16)
            pooled_ref[pl.ds(r0, R), :] = pooled
            mixed = jnp.dot(pooled, pw_ref[...], preferred_element_type=F32)
            mixed_ref[pl.ds(r0, R), :] = mixed
            yc_ref[pl.ds(r0, R), :] = (mixed * ps_ref[...]).astype(BF16)
            return c

        lax.fori_loop(0, T // R, chunk, 0)

    own = pl.BlockSpec((T, PG), lambda g: (0, g))
    return pl.pallas_call(
        body, name=name, grid=(4,),
        in_specs=[pl.BlockSpec((T, PG), lambda g: (0, dm.o_pool // PG + g)), pl.BlockSpec((None, PG, PG), lambda g: (g, 0, 0)),
                  pl.BlockSpec((1, PG), lambda g: (0, g))],
        out_specs=[own, own, own],
        out_shape=[jax.ShapeDtypeStruct((T, dm.DP), BF16), jax.ShapeDtypeStruct((T, dm.DP), F32),
                   jax.ShapeDtypeStruct((T, dm.DP), BF16)],
        scratch_shapes=[pltpu.VMEM((T + 2 * HALO, PG), F32)],
        compiler_params=_params(("parallel",)),
    )(proj, pw, ps)


def pool_bwd(dyc, mixed, pooled, pw, ps, dm, name):
    T, PG, R = dm.T, dm.PG, dm.tr
    _TN = (((0,), (0,)), ((), ()))

    def body(dyc_ref, mixed_ref, pooled_ref, pw_ref, ps_ref, dx_ref, dpw_ref, dps_ref, qbuf, dpbuf):
        g = pl.program_id(0)
        zeros = jnp.zeros((HALO, PG), F32)
        qbuf[pl.ds(0, HALO), :] = zeros
        qbuf[pl.ds(HALO + T, HALO), :] = zeros
        dpw_ref[...] = jnp.zeros_like(dpw_ref)

        def first(r, dps):
            r0 = pl.multiple_of(r * R, 8)
            dyc = dyc_ref[pl.ds(r0, R), :]
            dps = dps + jnp.sum(dyc * mixed_ref[pl.ds(r0, R), :], axis=0, keepdims=True)
            dmb = (dyc * ps_ref[...]).astype(BF16)
            dpw_ref[...] += lax.dot_general(pooled_ref[pl.ds(r0, R), :], dmb, _TN, preferred_element_type=F32)
            dp = lax.dot_general(dmb, pw_ref[...], _NT, preferred_element_type=F32)
            dpbuf[pl.ds(r0, R), :] = dp
            qbuf[pl.ds(r0 + HALO, R), :] = dp / _count(r0, R, g)
            return dps

        dps_ref[...] = lax.fori_loop(0, T // R, first, jnp.zeros((1, PG), F32))

        def second(r, c):
            r0 = pl.multiple_of(r * R, 8)
            win = qbuf[pl.ds(r0 + HALO, R + HALO), :]
            ws = _window_sum(win, g, _fwd)[:R, :]
            dx_ref[pl.ds(r0, R), :] = (ws - dpbuf[pl.ds(r0, R), :]).astype(BF16)
            return c

        lax.fori_loop(0, T // R, second, 0)

    own = pl.BlockSpec((T, PG), lambda g: (0, g))
    return pl.pallas_call(
        body, name=name, grid=(4,),
        in_specs=[own, own, own, pl.BlockSpec((None, PG, PG), lambda g: (g, 0, 0)), pl.BlockSpec((1, PG), lambda g: (0, g))],
        out_specs=[own, pl.BlockSpec((None, PG, PG), lambda g: (g, 0, 0)), pl.BlockSpec((1, PG), lambda g: (0, g))],
        out_shape=[jax.ShapeDtypeStruct((T, dm.DP), BF16), jax.ShapeDtypeStruct((4, PG, PG), F32),
                   jax.ShapeDtypeStruct((1, dm.DP), F32)],
        scratch_shapes=[pltpu.VMEM((T + 2 * HALO, PG), F32), pltpu.VMEM((T, PG), F32)],
        compiler_params=_params(("parallel",)),
    )(dyc, mixed, pooled, pw, ps)


def _sigmoid(x):
    return 1.0 / (1.0 + jnp.exp(-x))


def merge_fwd(proj, A, B, C, dm, name):
    T, D, tr, tc = dm.T, dm.D, dm.tr, 1024
    nc = D // tc

    def body(g0, g1, g2, a, b, c, out):
        out[...] = (_sigmoid(g0[...]) * a[...] + _sigmoid(g1[...]) * b[...] + _sigmoid(g2[...]) * c[...]).astype(BF16)

    gate = lambda k: pl.BlockSpec((tr, tc), lambda i, j: (i, k * nc + j))
    own = pl.BlockSpec((tr, tc), lambda i, j: (i, j))
    return pl.pallas_call(
        body, name=name, grid=(T // tr, nc),
        in_specs=[gate(0), gate(1), gate(2), own, own, own],
        out_specs=own,
        out_shape=jax.ShapeDtypeStruct((T, D), BF16),
        compiler_params=_params(("parallel", "parallel")),
    )(proj, proj, proj, A, B, C)


def merge_bwd(proj, A, B, C, dmerged, dm, name):
    T, D, tr, tc = dm.T, dm.D, dm.tr, 1024
    nc = D // tc

    def body(g0, g1, g2, a, b, c, dmr, da, db, dc, dl0, dl1, dl2):
        d = dmr[...]
        for g_ref, y_ref, dy_ref, dl_ref in ((g0, a, da, dl0), (g1, b, db, dl1), (g2, c, dc, dl2)):
            s = _sigmoid(g_ref[...])
            dy_ref[...] = (d * s).astype(BF16)
            dl_ref[...] = (d * y_ref[...] * s * (1.0 - s)).astype(BF16)

    gate = lambda k: pl.BlockSpec((tr, tc), lambda i, j: (i, k * nc + j))
    own = pl.BlockSpec((tr, tc), lambda i, j: (i, j))
    o = jax.ShapeDtypeStruct((T, D), BF16)
    return pl.pallas_call(
        body, name=name, grid=(T // tr, nc),
        in_specs=[gate(0), gate(1), gate(2), own, own, own, own],
        out_specs=[own] * 6,
        out_shape=[o] * 6,
        compiler_params=_params(("parallel", "parallel")),
    )(proj, proj, proj, A, B, C, dmerged)


def loss_head(y, target, dm, name):
    T, D, tr = dm.T, dm.D, dm.tr

    def body(y_ref, t_ref, dy_ref, dyb_ref, loss_ref):
        i = pl.program_id(0)
        t = _row_ids(i, tr)
        real = (t >= dm.n_meta) & (t < dm.T_real)
        err = jnp.where(real, y_ref[...] - t_ref[...], 0.0)
        dy = err * (1.0 / D)
        dy_ref[...] = dy
        dyb_ref[...] = dy.astype(BF16)

        @pl.when(i == 0)
        def _():
            loss_ref[...] = jnp.zeros_like(loss_ref)

        loss_ref[...] += 0.5 * jnp.sum(jnp.sum(err * err, axis=-1, keepdims=True) * (1.0 / D))

    row = pl.BlockSpec((tr, D), lambda i: (i, 0))
    return pl.pallas_call(
        body, name=name, grid=(T // tr,),
        in_specs=[row, row],
        out_specs=[row, row, pl.BlockSpec((8, LANES), lambda i: (0, 0))],
        out_shape=[jax.ShapeDtypeStruct((T, D), F32), jax.ShapeDtypeStruct((T, D), BF16),
                   jax.ShapeDtypeStruct((8, LANES), F32)],
        compiler_params=_params(("arbitrary",)),
    )(y, target)


def adamw(w, m, v, parts, name):
    R, C = w.shape
    P = parts.shape[0]
    br = R
    for cand in (512, 256, 128, 64, 32, 16, 8):
        if R % cand == 0 and cand * C * 4 <= (1 << 20):
            br = cand
            break
    if R * C * 4 <= (1 << 20):
        br = R

    def body(w_ref, m_ref, v_ref, p_ref, g_ref, d_ref, nm_ref, nv_ref):
        g = p_ref[0].astype(F32)
        for k in range(1, P):
            g = g + p_ref[k].astype(F32)
        mm = ADAM_B1 * m_ref[...] + (1.0 - ADAM_B1) * g
        vv = ADAM_B2 * v_ref[...] + (1.0 - ADAM_B2) * (g * g)
        m_hat = mm / (1.0 - ADAM_B1 ** ADAM_STEP)
        v_hat = vv / (1.0 - ADAM_B2 ** ADAM_STEP)
        g_ref[...] = g
        d_ref[...] = -ADAM_LR * (m_hat / (jnp.sqrt(v_hat) + ADAM_EPS) + ADAM_WD * w_ref[...])
        nm_ref[...] = mm
        nv_ref[...] = vv

    blk = pl.BlockSpec((br, C), lambda i: (i, 0))
    o = jax.ShapeDtypeStruct((R, C), F32)
    return pl.pallas_call(
        body, name=name, grid=(R // br,),
        in_specs=[blk, blk, blk, pl.BlockSpec((P, br, C), lambda i: (0, i, 0))],
        out_specs=[blk] * 4,
        out_shape=[o] * 4,
        compiler_params=_params(("parallel",)),
    )(w, m, v, parts)


def sum_parts(parts, name):
    P, R, C = parts.shape

    def body(p_ref, o_ref):
        acc = p_ref[0]
        for k in range(1, P):
            acc = acc + p_ref[k]
        o_ref[...] = acc

    return pl.pallas_call(
        body, name=name, grid=(1,),
        in_specs=[pl.BlockSpec((P, R, C), lambda i: (0, 0, 0))],
        out_specs=pl.BlockSpec((R, C), lambda i: (0, 0)),
        out_shape=jax.ShapeDtypeStruct((R, C), F32),
        compiler_params=_params(("arbitrary",)),
    )(parts)


def add_sibling(parts, got, core, name):
    _, _, R, C = parts.shape
    br = _pick(R, (1024, 512, 256, 128, 64, 32, 16))

    def body(c_ref, a_ref, b_ref, o_ref):
        o_ref[...] = (a_ref[...].astype(F32) + b_ref[...].astype(F32)).astype(BF16)

    blk = pl.BlockSpec((None, br, C), lambda ch, i, c: (ch, i, 0))
    return pl.pallas_call(
        body, name=name,
        grid_spec=pltpu.PrefetchScalarGridSpec(
            num_scalar_prefetch=1, grid=(4, R // br),
            in_specs=[pl.BlockSpec((None, None, br, C), lambda ch, i, c: (ch, c[0], i, 0)), blk],
            out_specs=blk),
        out_shape=jax.ShapeDtypeStruct((4, R, C), BF16), compiler_params=_params(("parallel", "parallel")),
    )(core, parts, got)


def all_gather(arrs, name):
    n = len(arrs)

    def body(*refs):
        ins, outs = refs[:n], refs[n:2 * n]
        send_sems, recv_sems, local_sems = refs[2 * n:]
        x, y, c = _place()
        me, sibling = (x, y, c), (x, y, 1 - c)
        chips = [(1 - x, y), (x, 1 - y), (1 - x, 1 - y)]

        def copy(a, k, block, to, src=None):
            px, py, pc = block
            dst = outs[a].at[4 * px + 2 * py + pc]
            return pltpu.make_async_remote_copy(
                src_ref=dst if src is None else src, dst_ref=dst,
                send_sem=send_sems.at[7 * a + k], recv_sem=recv_sems.at[7 * a + k],
                device_id=to, device_id_type=_MESH)

        started = []
        for a in range(n):
            mine = pltpu.make_async_copy(ins[a], outs[a].at[4 * x + 2 * y + c], local_sems.at[a])
            mine.start()
            started.append(mine)
        sends = []
        for a in range(n):
            sends.append(copy(a, 0, me, sibling, src=ins[a]))
            for j, chip in enumerate(chips):
                sends.append(copy(a, 1 + j, me, (*chip, c), src=ins[a]))
        for cp in sends:
            cp.start()
        for j, chip in enumerate(chips):
            for a in range(n):
                copy(a, 1 + j, (*chip, c), me).wait_recv()
                fwd = copy(a, 4 + j, (*chip, c), sibling)
                fwd.start()
                sends.append(fwd)
        for a in range(n):
            copy(a, 0, sibling, me).wait_recv()
            for j, chip in enumerate(chips):
                copy(a, 4 + j, (*chip, 1 - c), me).wait_recv()
        for cp in sends:
            cp.wait_send()
        for cp in started:
            cp.wait()

    outs = pl.pallas_call(
        body, name=name,
        in_specs=[_HBM] * n, out_specs=[_HBM] * n,
        out_shape=[jax.ShapeDtypeStruct((8,) + a.shape, a.dtype) for a in arrs],
        scratch_shapes=[pltpu.SemaphoreType.DMA((7 * n,)), pltpu.SemaphoreType.DMA((7 * n,)), pltpu.SemaphoreType.DMA((n,))],
    )(*arrs)
    return list(outs)


def sibling_exchange(arrs, name):
    n = len(arrs)

    def body(*refs):
        ins, got = refs[:n], refs[n:2 * n]
        send_sems, recv_sems = refs[2 * n:]
        x, y, c = _place()
        work = []
        for a in range(n):
            for ch in range(4):
                cp = pltpu.make_async_remote_copy(
                    src_ref=ins[a].at[ch, 1 - c], dst_ref=got[a].at[ch],
                    send_sem=send_sems.at[4 * a + ch], recv_sem=recv_sems.at[4 * a + ch],
                    device_id=(x, y, 1 - c), device_id_type=_MESH)
                cp.start()
                work.append(cp)
        for cp in work:
            cp.wait()

    outs = pl.pallas_call(
        body, name=name,
        in_specs=[_HBM] * n, out_specs=[_HBM] * n,
        out_shape=[jax.ShapeDtypeStruct((4,) + a.shape[2:], a.dtype) for a in arrs],
        scratch_shapes=[pltpu.SemaphoreType.DMA((4 * n,)), pltpu.SemaphoreType.DMA((4 * n,))],
    )(*arrs)
    return list(outs)


def _remote(src, dst, send_sem, recv_sem, to):
    return pltpu.make_async_remote_copy(src_ref=src, dst_ref=dst, send_sem=send_sem, recv_sem=recv_sem,
                                        device_id=to, device_id_type=_MESH)


def gather_ici(blocks):
    n = len(blocks)

    def copies(cins, couts, sems):
        send_sems, recv_sems, local_sems = sems
        x, y, c = _place()
        mine = 4 * x + 2 * y + c
        local, sends, recvs = [], [], []
        for a in range(n):
            local.append(pltpu.make_async_copy(cins[a], couts[a].at[mine], local_sems.at[a]))
            for j, (px, py) in enumerate([(1 - x, y), (x, 1 - y), (1 - x, 1 - y)]):
                k = 3 * a + j
                sends.append(_remote(cins[a], couts[a].at[mine], send_sems.at[k], recv_sems.at[k], (px, py, c)))
                recvs.append(_remote(cins[a], couts[a].at[4 * px + 2 * py + c], send_sems.at[k], recv_sems.at[k], (px, py, c)))
        return local, sends, recvs

    def start(cins, couts, sems):
        local, sends, _ = copies(cins, couts, sems)
        for cp in local + sends:
            cp.start()

    def finish(cins, couts, sems):
        local, sends, recvs = copies(cins, couts, sems)
        for cp in sends:
            cp.wait_send()
        for cp in recvs:
            cp.wait_recv()
        for cp in local:
            cp.wait()

    return Hosted(list(blocks), [jax.ShapeDtypeStruct((8,) + b.shape, b.dtype) for b in blocks],
                  [pltpu.SemaphoreType.DMA((3 * n,)), pltpu.SemaphoreType.DMA((3 * n,)), pltpu.SemaphoreType.DMA((n,))],
                  start, finish)


def fill_sibling(stks, name):
    n = len(stks)

    def body(*refs):
        ins, outs = refs[:n], refs[n:2 * n]
        send_sems, recv_sems = refs[2 * n:]
        x, y, c = _place()
        sends, recvs = [], []
        for a in range(n):
            for ch in range(4):
                k = 4 * a + ch
                sends.append(_remote(ins[a].at[2 * ch + c], outs[a].at[2 * ch + c], send_sems.at[k], recv_sems.at[k], (x, y, 1 - c)))
                recvs.append(_remote(ins[a].at[2 * ch + c], outs[a].at[2 * ch + 1 - c], send_sems.at[k], recv_sems.at[k], (x, y, 1 - c)))
        for cp in sends:
            cp.start()
        for cp in sends:
            cp.wait_send()
        for cp in recvs:
            cp.wait_recv()

    outs = pl.pallas_call(
        body, name=name,
        in_specs=[_HBM] * n, out_specs=[_HBM] * n,
        out_shape=[jax.ShapeDtypeStruct(s.shape, s.dtype) for s in stks],
        scratch_shapes=[pltpu.SemaphoreType.DMA((4 * n,)), pltpu.SemaphoreType.DMA((4 * n,))],
        input_output_aliases={a: a for a in range(n)},
    )(*stks)
    return list(outs)


def reduce_ici(arrs):
    n = len(arrs)

    def copies(cins, couts, sems):
        send_sems, recv_sems, local_sems = sems
        x, y, c = _place()
        my_chip = 2 * x + y
        local, sends, recvs = [], [], []
        for a in range(n):
            local.append(pltpu.make_async_copy(cins[a].at[my_chip], couts[a].at[my_chip], local_sems.at[a]))
            for j, (px, py) in enumerate([(1 - x, y), (x, 1 - y), (1 - x, 1 - y)]):
                k = 3 * a + j
                sends.append(_remote(cins[a].at[2 * px + py], couts[a].at[my_chip], send_sems.at[k], recv_sems.at[k], (px, py, c)))
                recvs.append(_remote(cins[a].at[my_chip], couts[a].at[2 * px + py], send_sems.at[k], recv_sems.at[k], (px, py, c)))
        return local, sends, recvs

    def start(cins, couts, sems):
        local, sends, _ = copies(cins, couts, sems)
        for cp in local + sends:
            cp.start()

    def finish(cins, couts, sems):
        local, sends, recvs = copies(cins, couts, sems)
        for cp in sends:
            cp.wait_send()
        for cp in recvs:
            cp.wait_recv()
        for cp in local:
            cp.wait()

    return Hosted(list(arrs), [jax.ShapeDtypeStruct(a.shape, a.dtype) for a in arrs],
                  [pltpu.SemaphoreType.DMA((3 * n,)), pltpu.SemaphoreType.DMA((3 * n,)), pltpu.SemaphoreType.DMA((n,))],
                  start, finish)


def sibling_hosted(arrs):
    n = len(arrs)

    def copies(cins, couts, sems):
        send_sems, recv_sems = sems
        x, y, c = _place()
        return [_remote(cins[a].at[ch, 1 - c], couts[a].at[ch], send_sems.at[4 * a + ch], recv_sems.at[4 * a + ch],
                        (x, y, 1 - c)) for a in range(n) for ch in range(4)]

    def start(cins, couts, sems):
        for cp in copies(cins, couts, sems):
            cp.start()

    def finish(cins, couts, sems):
        for cp in copies(cins, couts, sems):
            cp.wait()

    return Hosted(list(arrs), [jax.ShapeDtypeStruct((4,) + a.shape[2:], a.dtype) for a in arrs],
                  [pltpu.SemaphoreType.DMA((4 * n,)), pltpu.SemaphoreType.DMA((4 * n,))], start, finish)


def merge_hosted(first, second):
    ni, no, ns = len(first.ins), len(first.out_shapes), len(first.sems)

    def start(cins, couts, sems):
        first.start(cins[:ni], couts[:no], sems[:ns])
        second.start(cins[ni:], couts[no:], sems[ns:])

    def finish(cins, couts, sems):
        first.finish(cins[:ni], couts[:no], sems[:ns])
        second.finish(cins[ni:], couts[no:], sems[ns:])

    aliases = dict(first.aliases)
    aliases.update({ni + i: no + o for i, o in second.aliases.items()})
    both = Hosted(first.ins + second.ins, first.out_shapes + second.out_shapes, first.sems + second.sems, start, finish,
                  aliases)
    return both, lambda couts: (couts[:no], couts[no:])


def fill_hosted(stks):
    n = len(stks)

    def copies(cins, couts, sems):
        send_sems, recv_sems = sems
        x, y, c = _place()
        sends, recvs = [], []
        for a in range(n):
            for ch in range(4):
                k = 4 * a + ch
                sends.append(_remote(cins[a].at[2 * ch + c], couts[a].at[2 * ch + c], send_sems.at[k], recv_sems.at[k], (x, y, 1 - c)))
                recvs.append(_remote(cins[a].at[2 * ch + c], couts[a].at[2 * ch + 1 - c], send_sems.at[k], recv_sems.at[k], (x, y, 1 - c)))
        return sends, recvs

    def start(cins, couts, sems):
        for cp in copies(cins, couts, sems)[0]:
            cp.start()

    def finish(cins, couts, sems):
        sends, recvs = copies(cins, couts, sems)
        for cp in sends:
            cp.wait_send()
        for cp in recvs:
            cp.wait_recv()

    return Hosted(list(stks), [jax.ShapeDtypeStruct(s.shape, s.dtype) for s in stks],
                  [pltpu.SemaphoreType.DMA((4 * n,)), pltpu.SemaphoreType.DMA((4 * n,))], start, finish,
                  aliases={a: a for a in range(n)})


COL_SHARDED = ("w_in", "w_uq", "w_ukv", "w_branch_a", "w_branch_c", "w_up")
ROW_SHARDED = ("w_branch_b", "w_o", "w_down")
BIG = COL_SHARDED + ROW_SHARDED


def _full_from_stacked(name, st):
    if name in COL_SHARDED:
        _, L, K, n = st.shape
        return st.transpose(1, 2, 0, 3).reshape(L, K, 8 * n)
    if name in ROW_SHARDED:
        _, L, k, N = st.shape
        return st.transpose(1, 0, 2, 3).reshape(L, 8 * k, N)
    if name == "pool_w":
        _, L, G, pk, PG = st.shape
        return st.transpose(1, 2, 0, 3, 4).reshape(L, G, 8 * pk, PG)
    if name == "meta_tokens":
        _, M, n = st.shape
        return st.transpose(1, 0, 2).reshape(M, 8 * n)
    if name == "conv_w":
        _, L, W, n = st.shape
        return st.transpose(1, 2, 0, 3).reshape(L, W, 8 * n)
    raise ValueError(name)


def _shards_from_full(name, g):
    if name in COL_SHARDED:
        L, K, N = g.shape
        s = g.reshape(L, K, 8, N // 8).transpose(2, 0, 1, 3)
    else:
        L, K, N = g.shape
        s = g.reshape(L, 8, K // 8, N).transpose(1, 0, 2, 3)
    return s.reshape((4, 2) + s.shape[1:])


def _w_in_to_padded(w, dm):
    o3 = 3 * dm.DC + dm.QL + dm.KL
    o4 = o3 + dm.ROPE
    o5 = o4 + dm.DP
    pad = jnp.zeros(w.shape[:-1] + (256 - dm.ROPE,), w.dtype)
    return jnp.concatenate([w[..., o5:], w[..., :o3], w[..., o4:o5], w[..., o3:o4], pad], axis=-1)


def _w_in_from_padded(g, dm):
    o3 = 3 * dm.DC + dm.QL + dm.KL
    a = 3 * dm.D
    return jnp.concatenate([g[..., a:a + o3], g[..., dm.o_rope:dm.o_rope + dm.ROPE], g[..., dm.o_pool:dm.o_pool + dm.DP],
                            g[..., :a]], axis=-1)


def _pad_heads(w, dm):
    w = w.reshape(w.shape[:-1] + (dm.H, dm.QKH))
    w = jnp.pad(w, [(0, 0)] * (w.ndim - 1) + [(0, dm.HP - dm.QKH)])
    return w.reshape(w.shape[:-2] + (dm.H * dm.HP,))


def _unpad_heads(g, dm):
    g = g.reshape(g.shape[:-1] + (dm.H, dm.HP))[..., :dm.QKH]
    return g.reshape(g.shape[:-2] + (dm.H * dm.QKH,))


class _Weights:
    def __init__(self, w, dm, plan, fill_in):
        self.w, self.dm, self.plan, self.fill_in, self.full = w, dm, plan, fill_in, {}
        self.pending, self.split = {}, {}

    def blocks(self, items):
        return [self.w[n][l:l + 1].astype(BF16) for n, l in items]

    def put(self, items, stacked):
        for (n, l), st in zip(items, stacked):
            f = _full_from_stacked(n, st)[0]
            if n == "w_in":
                f = _w_in_to_padded(f, self.dm)
            if n == "w_uq":
                f = _pad_heads(f, self.dm)
            self.full[(n, l)] = f

    def comm(self, tag):
        items, waiting = self.plan.get(tag), self.pending.get(tag)
        gather = gather_ici(self.blocks(items)) if items else None
        fill = fill_hosted(waiting[1]) if waiting else None
        if gather is not None and fill is not None:
            both, self.split[tag] = merge_hosted(gather, fill)
            return both
        return gather if fill is None else fill

    def arrived(self, tag, couts):
        items, waiting = self.plan.get(tag), self.pending.pop(tag, None)
        if items and waiting:
            gathered, filled = self.split[tag](couts)
        else:
            gathered, filled = (couts, None) if items else (None, couts)
        if waiting:
            self.put(waiting[0], filled)
        if items:
            if tag in self.fill_in:
                self.pending[self.fill_in[tag]] = (items, gathered)
            else:
                self.put(items, fill_sibling(gathered, f"fill_{tag}"))

    def __call__(self, n, l):
        return self.full[(n, l)]


class _Reducer:
    def __init__(self, dm, core):
        self.dm, self.core, self.q = dm, core, {}

    def parts(self, items, g):
        parts = []
        for n, l in items:
            f = g[n]
            if n == "w_in":
                f = _w_in_from_padded(f, self.dm)
            if n == "w_uq":
                f = _unpad_heads(f, self.dm)
            parts.append(_shards_from_full(n, f[None]))
        return parts

    def add(self, items, parts, got):
        out = []
        for (n, l), a, b in zip(items, parts, got):
            C = a.shape[-1]
            out.append(add_sibling(a.reshape(4, 2, -1, C), b.reshape(4, -1, C), self.core,
                                   f"reduce_add_{n}_{l}").reshape(b.shape))
        return out

    def prepare(self, tag, items, g):
        parts = self.parts(items, g)
        return self.add(items, parts, sibling_exchange(parts, f"reduce_sibling_{tag}"))

    def put(self, items, summed):
        for key, q in zip(items, summed):
            self.q[key] = q


def _layer_fwd(xin, l, ws, G, tabs, dm):
    nm = lambda s: f"l{l}_{s}"
    D = dm.D

    def mm(tag, *args, **kw):
        comm = ws.comm(nm(tag))
        if comm is None:
            return matmul(*args, nm(tag), **kw)
        res, couts = matmul(*args, nm(tag), comm=comm, **kw)
        ws.arrived(nm(tag), couts)
        return res

    h = rms_fwd(xin, 0, D, G["attn_norm"], dm, nm("rms1"))
    proj = mm("proj", h, ws("w_in", l), "nn", (F32,))
    ya = mixer_a_fwd(proj, G["conv_w"], dm, nm("mixa"))
    ql = rms_fwd(proj, dm.o_ql // dm.QL, dm.QL, G["q_lat_norm"], dm, nm("rms_q"))
    kl = rms_fwd(proj, dm.o_kl // dm.KL, dm.KL, G["kv_lat_norm"], dm, nm("rms_kv"))
    q0 = mm("uq", ql, ws("w_uq", l), "nn", (F32,))
    kv0 = mm("ukv", kl, ws("w_ukv", l), "nn", (F32,))
    comm = ws.comm(nm("qkprep"))
    if comm is None:
        q_s, k_s, v_s = qk_prep_fwd(q0, kv0, proj, G["q_norm"], G["k_norm"], tabs[0], tabs[1], dm, nm("qkprep"))
    else:
        (q_s, k_s, v_s), couts = qk_prep_fwd(q0, kv0, proj, G["q_norm"], G["k_norm"], tabs[0], tabs[1], dm, nm("qkprep"),
                                            comm=comm)
        ws.arrived(nm("qkprep"), couts)
    comm = ws.comm(nm("attn"))
    if comm is None:
        o, ob, lse = attn_fwd(q_s, k_s, v_s, dm, nm("attn"))
    else:
        (o, ob, lse), couts = attn_fwd(q_s, k_s, v_s, dm, nm("attn"), comm=comm)
        ws.arrived(nm("attn"), couts)
    pooled, mixed, yc = pool_fwd(proj, ws("pool_w", l), G["pool_scale"], dm, nm("pool"))
    A = mm("br_a", ya, ws("w_branch_a", l), "nn", (F32,))
    B = mm("br_b", ob, ws("w_branch_b", l), "nn", (F32,))
    C = mm("br_c", yc, ws("w_branch_c", l), "nn", (F32,))
    merged = merge_fwd(proj, A, B, C, dm, nm("merge"))
    x1 = mm("wo", merged, ws("w_o", l), "nn", (F32,), extras=(xin,), epi=lambda acc, r: (acc + r,))
    h2 = rms_fwd(x1, 0, D, G["mlp_norm"], dm, nm("rms2"))
    up, act = mm("up", h2, ws("w_up", l), "nn", (F32, BF16), epi=lambda acc: (acc, jnp.square(jnp.maximum(acc, 0.0))))
    x2 = mm("down", act, ws("w_down", l), "nn", (F32,), extras=(x1,), epi=lambda acc, r: (acc + r,))
    saved = dict(xin=xin, h=h, proj=proj, ya=ya, ql=ql, kl=kl, q0=q0, kv0=kv0, q_s=q_s, k_s=k_s, v_s=v_s, o=o, ob=ob,
                 lse=lse, pooled=pooled, mixed=mixed, yc=yc, A=A, B=B, C=C, merged=merged, x1=x1, h2=h2, up=up, act=act)
    return x2, saved


def _layer_bwd(dx2, dx2b, S, l, ws, G, tabs, dm, pre_attn, late=None, first=None):
    nm = lambda s: f"l{l}_b_{s}"
    D, T = dm.D, dm.T
    g = {}
    d_up = matmul(dx2b, ws("w_down", l), "nt", (BF16,), nm("d_act"), extras=(S["up"],),
                  epi=lambda acc, up: (acc * (2.0 * jnp.maximum(up, 0.0)),), comm=first[0] if first else None)
    if first:
        d_up, couts = d_up
        first[1](couts)
    g["w_down"] = matmul(S["act"], dx2b, "tn", (BF16,), nm("g_down"))
    g["w_up"] = matmul(S["h2"], d_up, "tn", (BF16,), nm("g_up"))
    dh2 = matmul(d_up, ws("w_up", l), "nt", (F32,), nm("d_h2"))
    dx1, dx1b, g["mlp_norm"] = rms_bwd(S["x1"], 0, D, G["mlp_norm"], dh2, dx2, dm, nm("rms2"))
    dmerged = matmul(dx1b, ws("w_o", l), "nt", (F32,), nm("d_merged"))
    g["w_o"] = matmul(S["merged"], dx1b, "tn", (BF16,), nm("g_o"))
    dA, dB, dC, dl0, dl1, dl2 = merge_bwd(S["proj"], S["A"], S["B"], S["C"], dmerged, dm, nm("merge"))
    dya = matmul(dA, ws("w_branch_a", l), "nt", (F32,), nm("d_ya"))
    g["w_branch_a"] = matmul(S["ya"], dA, "tn", (BF16,), nm("g_a"))
    dyb = matmul(dB, ws("w_branch_b", l), "nt", (F32,), nm("d_yb"))
    g["w_branch_b"] = matmul(S["ob"], dB, "tn", (BF16,), nm("g_b"))
    dyc = matmul(dC, ws("w_branch_c", l), "nt", (F32,), nm("d_yc"))
    g["w_branch_c"] = matmul(S["yc"], dC, "tn", (BF16,), nm("g_c"))
    du, db, dc, g["conv_w"] = mixer_a_bwd(S["proj"], G["conv_w"], dya, dm, nm("mixa"))
    dpool, g["pool_w"], g["pool_scale"] = pool_bwd(dyc, S["mixed"], S["pooled"], ws("pool_w", l), G["pool_scale"], dm, nm("pool"))
    delta, dob = attn_delta(dyb, S["o"], dm, nm("delta"))
    nq = T // dm.tq
    comm_dq, after_dq = pre_attn(g)
    dq, couts = attn_bwd_dq(S["q_s"], S["k_s"], S["v_s"], dob, S["lse"], delta, dm, nm("attn_dq"), comm=comm_dq)
    comm_dkv, done_dkv = after_dq(couts)
    dkv = attn_bwd_dkv(S["q_s"], S["k_s"], S["v_s"], dob, S["lse"].reshape(dm.H, nq, 1, dm.tq),
                       delta.reshape(dm.H, nq, 1, dm.tq), dm, nm("attn_dkv"), comm=comm_dkv)
    if comm_dkv is not None:
        dkv, couts = dkv
        done_dkv(couts)
    dk, dv = dkv
    dq0, dkv0, dkr, g["q_norm"], g["k_norm"] = qk_prep_bwd(S["q0"], S["kv0"], S["proj"], G["q_norm"], G["k_norm"],
                                                            tabs[0], tabs[1], dq, dk, dv, dm, nm("qkprep"))
    dql = matmul(dq0, ws("w_uq", l), "nt", (F32,), nm("d_ql"))
    g["w_uq"] = matmul(S["ql"], dq0, "tn", (BF16,), nm("g_uq"))
    dkl = matmul(dkv0, ws("w_ukv", l), "nt", (F32,), nm("d_kl"))
    g["w_ukv"] = matmul(S["kl"], dkv0, "tn", (BF16,), nm("g_ukv"))
    _, dqlat, g["q_lat_norm"] = rms_bwd(S["proj"], dm.o_ql // dm.QL, dm.QL, G["q_lat_norm"], dql, None, dm, nm("rms_q"))
    _, dkvlat, g["kv_lat_norm"] = rms_bwd(S["proj"], dm.o_kl // dm.KL, dm.KL, G["kv_lat_norm"], dkl, None, dm, nm("rms_kv"))
    dproj = jnp.concatenate([dl0, dl1, dl2, du, db, dc, dqlat, dkvlat, dpool, dkr.astype(BF16),
                             jnp.zeros((T, 128), BF16)], axis=1)
    g["w_in"] = matmul(S["h"], dproj, "tn", (BF16,), nm("g_in"))
    comm_dh, done_dh = late(g) if late is not None else (None, None)
    dh = matmul(dproj, ws("w_in", l), "nt", (F32,), nm("d_h"), comm=comm_dh)
    if comm_dh is not None:
        dh, couts = dh
        done_dh(couts)
    dx, dxb, g["attn_norm"] = rms_bwd(S["xin"], 0, D, G["attn_norm"], dh, dx1, dm, nm("rms1"))
    return dx, dxb, g


WEIGHTS = ("meta_tokens", "attn_norm", "w_in", "conv_w", "q_lat_norm", "kv_lat_norm", "w_uq", "w_ukv", "q_norm", "k_norm",
           "pool_w", "pool_scale", "w_branch_a", "w_branch_b", "w_branch_c", "w_o", "mlp_norm", "w_up", "w_down")
SMALL = tuple(n for n in WEIGHTS if n not in BIG)


def kernel(x, meta_tokens, attn_norm, w_in, conv_w, q_lat_norm, kv_lat_norm, w_uq, w_ukv, q_norm, k_norm, pool_w, pool_scale, w_branch_a, w_branch_b, w_branch_c, w_o, mlp_norm, w_up, w_down, loss_target, m_meta_tokens, m_attn_norm, m_w_in, m_conv_w, m_q_lat_norm, m_kv_lat_norm, m_w_uq, m_w_ukv, m_q_norm, m_k_norm, m_pool_w, m_pool_scale, m_w_branch_a, m_w_branch_b, m_w_branch_c, m_w_o, m_mlp_norm, m_w_up, m_w_down, v_meta_tokens, v_attn_norm, v_w_in, v_conv_w, v_q_lat_norm, v_kv_lat_norm, v_w_uq, v_w_ukv, v_q_norm, v_k_norm, v_pool_w, v_pool_scale, v_w_branch_a, v_w_branch_b, v_w_branch_c, v_w_o, v_mlp_norm, v_w_up, v_w_down):
    w = dict(meta_tokens=meta_tokens, attn_norm=attn_norm, w_in=w_in, conv_w=conv_w, q_lat_norm=q_lat_norm,
             kv_lat_norm=kv_lat_norm, w_uq=w_uq, w_ukv=w_ukv, q_norm=q_norm, k_norm=k_norm, pool_w=pool_w,
             pool_scale=pool_scale, w_branch_a=w_branch_a, w_branch_b=w_branch_b, w_branch_c=w_branch_c, w_o=w_o,
             mlp_norm=mlp_norm, w_up=w_up, w_down=w_down)
    m = dict(meta_tokens=m_meta_tokens, attn_norm=m_attn_norm, w_in=m_w_in, conv_w=m_conv_w, q_lat_norm=m_q_lat_norm,
             kv_lat_norm=m_kv_lat_norm, w_uq=m_w_uq, w_ukv=m_w_ukv, q_norm=m_q_norm, k_norm=m_k_norm, pool_w=m_pool_w,
             pool_scale=m_pool_scale, w_branch_a=m_w_branch_a, w_branch_b=m_w_branch_b, w_branch_c=m_w_branch_c, w_o=m_w_o,
             mlp_norm=m_mlp_norm, w_up=m_w_up, w_down=m_w_down)
    v = dict(meta_tokens=v_meta_tokens, attn_norm=v_attn_norm, w_in=v_w_in, conv_w=v_conv_w, q_lat_norm=v_q_lat_norm,
             kv_lat_norm=v_kv_lat_norm, w_uq=v_w_uq, w_ukv=v_w_ukv, q_norm=v_q_norm, k_norm=v_k_norm, pool_w=v_pool_w,
             pool_scale=v_pool_scale, w_branch_a=v_w_branch_a, w_branch_b=v_w_branch_b, w_branch_c=v_w_branch_c, w_o=v_w_o,
             mlp_norm=v_mlp_norm, w_up=v_w_up, w_down=v_w_down)
    L = attn_norm.shape[0]
    assert L == 2, "the gather / reduce schedule below is written for two layers"
    seq, D = x.shape[1], x.shape[2]
    n_meta = meta_tokens.shape[0]
    dm = Dims(D, seq, n_meta)
    T = dm.T
    me = 4 * lax.axis_index("x") + 2 * lax.axis_index("y") + lax.axis_index("c")
    core = lax.axis_index("c").astype(jnp.int32).reshape(1)

    plan = {
        "l0_proj": [("w_uq", 0), ("w_ukv", 0), ("pool_w", 0), ("pool_w", 1), ("w_up", 0)],
        "l0_qkprep": [("w_branch_a", 0), ("w_branch_b", 0), ("w_branch_c", 0), ("w_o", 0)],
        "l0_attn": [("w_down", 0), ("w_in", 1)],
        "l0_up": [("w_up", 1)],
        "l1_proj": [("w_uq", 1), ("w_ukv", 1), ("w_branch_a", 1), ("w_branch_b", 1), ("w_branch_c", 1), ("w_o", 1)],
        "l1_attn": [("w_down", 1)],
    }
    ws = _Weights(w, dm, plan, {"l0_qkprep": "l0_attn", "l0_attn": "l0_up", "l0_up": "l0_down", "l1_attn": "l1_up"})
    first = [("w_in", 0)]
    ws.put(first, all_gather(ws.blocks(first), "gather_first"))
    st_small = all_gather([w["meta_tokens"], w["conv_w"]], "gather_small")
    meta_full = _full_from_stacked("meta_tokens", st_small[0])
    conv_full = _full_from_stacked("conv_w", st_small[1])
    pad_gain = lambda gn: jnp.pad(gn, (0, dm.HP - dm.QKH))

    def gains(l):
        G = {n: w[n][l][None, :] for n in ("attn_norm", "q_lat_norm", "kv_lat_norm", "pool_scale", "mlp_norm")}
        G["q_norm"], G["k_norm"] = pad_gain(w["q_norm"][l])[None, :], pad_gain(w["k_norm"][l])[None, :]
        G["conv_w"] = conv_full[l]
        return G

    Gs = [gains(l) for l in range(L)]

    pos = jnp.arange(dm.T_real, dtype=F32)
    inv = 10000.0 ** (-jnp.arange(0, dm.ROPE, 2, dtype=F32) / dm.ROPE)
    ang = pos[:, None] * inv[None, :]
    zpad = jnp.zeros((dm.T_real, LANES - dm.ROPE), F32)
    rows = ((0, T - dm.T_real), (0, 0))
    tabs = (jnp.pad(jnp.concatenate([jnp.cos(ang), jnp.cos(ang), zpad], 1), rows),
            jnp.pad(jnp.concatenate([jnp.sin(ang), jnp.sin(ang), zpad], 1), rows))

    xs = jnp.concatenate([meta_full, x[0], jnp.zeros((T - dm.T_real, D), F32)], axis=0)
    target = jnp.pad(loss_target[0], ((n_meta, T - dm.T_real), (0, 0)))
    saved = []
    for l in range(L):
        xs, S = _layer_fwd(xs, l, ws, Gs[l], tabs, dm)
        saved.append(S)
    dx, dxb, loss_acc = loss_head(xs, target, dm, "loss_head")

    red = _Reducer(dm, core)
    early = lambda l: [(n, l) for n in ("w_down", "w_up", "w_o", "w_branch_a", "w_branch_b", "w_branch_c")]
    late = lambda l: [(n, l) for n in ("w_uq", "w_ukv", "w_in")]
    grads = [None] * L

    def pre_attn_1(g):
        parts = red.parts(early(1), g)

        def after(got):
            return reduce_ici(red.add(early(1), parts, got)), lambda couts: red.put(early(1), couts)

        return sibling_hosted(parts), after

    dx, dxb, grads[1] = _layer_bwd(dx, dxb, saved[1], 1, ws, Gs[1], tabs, dm, pre_attn_1)
    parts_late1 = red.parts(late(1), grads[1])
    chip_late1 = []

    def pre_attn_0(g):
        parts = red.parts(early(0), g)
        both, split = merge_hosted(reduce_ici(chip_late1[0]), sibling_hosted(parts))

        def after(couts):
            summed_late1, got = split(couts)
            red.put(late(1), summed_late1)
            return reduce_ici(red.add(early(0), parts, got)), lambda couts: red.put(early(0), couts)

        return both, after

    def late_0(g):
        return reduce_ici(red.prepare("l0", late(0), g)), lambda couts: red.put(late(0), couts)

    dx, dxb, grads[0] = _layer_bwd(
        dx, dxb, saved[0], 0, ws, Gs[0], tabs, dm, pre_attn_0, late_0,
        first=(sibling_hosted(parts_late1), lambda got: chip_late1.append(red.add(late(1), parts_late1, got))))
    grad_x = dx[n_meta:dm.T_real][None]
    summed = [jnp.concatenate([red.q[(n, l)] for l in range(L)], axis=1) for n in BIG]

    out_g, out_d, out_m, out_v = {}, {}, {}, {}

    def update(n, parts3):
        shp = w[n].shape
        C = shp[-1]
        res = adamw(w[n].reshape(-1, C), m[n].reshape(-1, C), v[n].reshape(-1, C), parts3, f"adamw_{n}")
        out_g[n], out_d[n], out_m[n], out_v[n] = [r.reshape(shp) for r in res]

    for n, q in zip(BIG, summed):
        update(n, q.reshape(4, -1, q.shape[-1]))

    small_full = {
        "meta_tokens": dx[:n_meta],
        "conv_w": jnp.stack([grads[l]["conv_w"] for l in range(L)]),
        "pool_w": jnp.stack([grads[l]["pool_w"] for l in range(L)]),
        "q_norm": jnp.stack([grads[l]["q_norm"][0, :dm.QKH] for l in range(L)]),
        "k_norm": jnp.stack([grads[l]["k_norm"][0, :dm.QKH] for l in range(L)]),
    }
    for n in ("attn_norm", "q_lat_norm", "kv_lat_norm", "pool_scale", "mlp_norm"):
        small_full[n] = jnp.stack([grads[l][n][0] for l in range(L)])
    flat = jnp.concatenate([small_full[n].reshape(-1) for n in SMALL] + [loss_acc[0, :1]])
    n_flat = flat.shape[0]
    rows_small = -(-n_flat // (8 * LANES)) * 8
    flat = jnp.pad(flat, (0, rows_small * LANES - n_flat)).reshape(rows_small, LANES)
    total = sum_parts(all_gather([flat], "gather_small_grads")[0], "sum_small").reshape(-1)
    off = 0
    for n in SMALL:
        size = math.prod(small_full[n].shape)
        gsum = total[off:off + size].reshape(small_full[n].shape)
        off += size
        if n in ("meta_tokens", "conv_w"):
            blk = w[n].shape[-1]
            gsum = lax.dynamic_slice_in_dim(gsum, me * blk, blk, axis=gsum.ndim - 1)
        elif n == "pool_w":
            blk = w[n].shape[2]
            gsum = lax.dynamic_slice_in_dim(gsum, me * blk, blk, axis=2)
        update(n, gsum.reshape(1, -1, gsum.shape[-1]))
    loss = total[off]

    return (loss, grad_x, *[out_g[n] for n in WEIGHTS], *[out_d[n] for n in WEIGHTS],
            *[out_m[n] for n in WEIGHTS], *[out_v[n] for n in WEIGHTS])
```
